```python
import math
import jax, jax.numpy as jnp
from jax import lax
import numpy as np

D_MODEL = 1024
BATCH = 2
SEQ = 16384
DEPTH = 2

CTX_LEN = 256
GRID_W = 64
HEAD_DIM = 64
GROUP_WIDTH = D_MODEL // 4
QB = 128
ROPE_THETA = 10000.0
EPS = 1e-6
NEG = -1e30
F32 = jnp.float32

A_HEADS = GROUP_WIDTH // HEAD_DIM
A_QK = HEAD_DIM // 2
A_V = HEAD_DIM
B_HEADS = GROUP_WIDTH // HEAD_DIM
B_KV_HEADS = B_HEADS // 2
C_HEADS = GROUP_WIDTH // HEAD_DIM
C_Q_RANK = (3 * D_MODEL) // 16
C_KV_RANK = D_MODEL // 8
C_NOPE = HEAD_DIM
C_ROPE = HEAD_DIM // 2
C_V = HEAD_DIM
D_HEADS = GROUP_WIDTH // HEAD_DIM
D_KV_HEADS = D_HEADS // 2
WINDOW = 128
N_EXPERTS = 16
N_EXPERT_GROUPS = 4
EXPERTS_PER_GROUP = N_EXPERTS // N_EXPERT_GROUPS
TOP_K = 2
D_EXPERT = D_MODEL // 4
ALPHA = (2 * DEPTH) ** 0.25
BETA = (8 * DEPTH) ** -0.25

IN_SPLITS = (A_HEADS * 2 * A_QK, A_HEADS * 2 * A_QK, A_HEADS * A_V,
             B_HEADS * HEAD_DIM, B_KV_HEADS * HEAD_DIM, B_KV_HEADS * HEAD_DIM,
             C_Q_RANK, C_KV_RANK, C_ROPE,
             D_HEADS * HEAD_DIM, D_KV_HEADS * HEAD_DIM, D_KV_HEADS * HEAD_DIM)
IN_COLS = sum(IN_SPLITS)

kernel_name = 'hybrid_parallel_head_groups_dit_moe'


def layer_norm(x, g=None, b=None):
    xf = x.astype(F32)
    xc = xf - jnp.mean(xf, -1, keepdims=True)
    y = xc * lax.rsqrt(jnp.mean(xc * xc, -1, keepdims=True) + EPS)
    if g is not None:
        y = y * g.astype(F32) + b.astype(F32)
    return y.astype(x.dtype)


def rms_norm(x, g):
    xf = x.astype(F32)
    y = xf * lax.rsqrt(jnp.mean(xf * xf, -1, keepdims=True) + EPS) * g.astype(F32)
    return y.astype(x.dtype)


def modulate(x, shift, scale):
    return layer_norm(x) * (1 + scale) + shift


def axial_angles(row, col, rot_dim):
    n = rot_dim // 4
    inv = ROPE_THETA ** (-jnp.arange(n, dtype=F32) / n)
    return row.astype(F32)[:, None] * inv, col.astype(F32)[:, None] * inv


def rope_1d(x, ang):
    x1, x2 = jnp.split(x, 2, axis=-1)
    cos = jnp.cos(ang)[None, :, None, :]
    sin = jnp.sin(ang)[None, :, None, :]
    return jnp.concatenate([x1 * cos - x2 * sin, x2 * cos + x1 * sin], -1)


def rope_2d(x, angs):
    ang_r, ang_c = angs
    xf = x.astype(F32)
    half = x.shape[-1] // 2
    return jnp.concatenate([rope_1d(xf[..., :half], ang_r), rope_1d(xf[..., half:], ang_c)], -1).astype(x.dtype)


def sweep_query_blocks(block_fn, qs):
    b, n = qs[0].shape[:2]
    nb = n // QB
    blocks = tuple(jnp.moveaxis(q.reshape((b, nb, QB) + q.shape[2:]), 1, 0) for q in qs)
    out = jnp.moveaxis(lax.map(block_fn, blocks), 0, 1)
    return out.reshape((b, n) + out.shape[3:])


def gqa_core(q, k, v, scale, sink=None):
    s = jnp.einsum('bqgrd,bkgd->bgrqk', q.astype(F32), k.astype(F32)) * scale
    if sink is not None:
        sk = jnp.broadcast_to(sink.astype(F32)[None, :, :, None, None], s.shape[:-1] + (1,))
        p = jax.nn.softmax(jnp.concatenate([s, sk], -1), axis=-1)[..., :-1]
    else:
        p = jax.nn.softmax(s, axis=-1)
    return jnp.einsum('bgrqk,bkgd->bqgrd', p.astype(v.dtype), v)


def diff_core(q1, q2, k1, k2, v, lam, scale):
    p1 = jax.nn.softmax(jnp.einsum('bqhd,bkhd->bhqk', q1.astype(F32), k1.astype(F32)) * scale, axis=-1)
    p2 = jax.nn.softmax(jnp.einsum('bqhd,bkhd->bhqk', q2.astype(F32), k2.astype(F32)) * scale, axis=-1)
    return jnp.einsum('bhqk,bkhd->bqhd', (p1 - lam * p2).astype(v.dtype), v)


def mixer_diff(p_lat, p_ctx, angs, lq1, lk1, lq2, lk2, subln_g, layer_idx, need_ctx):
    lam_init = 0.8 - 0.6 * math.exp(-0.3 * layer_idx)
    lam = (jnp.exp(jnp.sum(lq1.astype(F32) * lk1.astype(F32)))
           - jnp.exp(jnp.sum(lq2.astype(F32) * lk2.astype(F32))) + lam_init)
    scale = A_QK ** -0.5

    def heads(q, k, v):
        b, n = q.shape[:2]
        q = q.reshape(b, n, A_HEADS, 2, A_QK)
        k = k.reshape(b, n, A_HEADS, 2, A_QK)
        return q[..., 0, :], q[..., 1, :], k[..., 0, :], k[..., 1, :], v.reshape(b, n, A_HEADS, A_V)

    def post(o):
        o = rms_norm(o, subln_g) * (1.0 - lam_init)
        return o.reshape(o.shape[0], o.shape[1], GROUP_WIDTH)

    q1, q2, k1, k2, v = heads(*p_lat)
    q1, q2, k1, k2 = [rope_2d(t, angs) for t in (q1, q2, k1, k2)]
    cq1, cq2, ck1, ck2, cv = heads(*p_ctx)
    k1a = jnp.concatenate([ck1, k1], axis=1)
    k2a = jnp.concatenate([ck2, k2], axis=1)
    va = jnp.concatenate([cv, v], axis=1)
    o_lat = sweep_query_blocks(lambda qb: diff_core(qb[0], qb[1], k1a, k2a, va, lam, scale), (q1, q2))
    o_ctx = post(diff_core(cq1, cq2, ck1, ck2, cv, lam, scale)) if need_ctx else None
    return post(o_lat), o_ctx


def mixer_gqa(p_lat, p_ctx, angs, qn_g, kn_g, need_ctx):
    rep = B_HEADS // B_KV_HEADS
    scale = HEAD_DIM ** -0.5

    def heads(q, k, v):
        b, n = q.shape[:2]
        q = rms_norm(q.reshape(b, n, B_HEADS, HEAD_DIM), qn_g)
        k = rms_norm(k.reshape(b, n, B_KV_HEADS, HEAD_DIM), kn_g)
        return q, k, v.reshape(b, n, B_KV_HEADS, HEAD_DIM)

    def group(q):
        return q.reshape(q.shape[0], q.shape[1], B_KV_HEADS, rep, HEAD_DIM)

    def flat(o):
        return o.reshape(o.shape[0], o.shape[1], GROUP_WIDTH)

    q, k, v = heads(*p_lat)
    q, k = rope_2d(q, angs), rope_2d(k, angs)
    cq, ck, cv = heads(*p_ctx)
    ka = jnp.concatenate([ck, k], axis=1)
    va = jnp.concatenate([cv, v], axis=1)
    o_lat = sweep_query_blocks(lambda qb: gqa_core(qb[0], ka, va, scale), (group(q),))
    o_ctx = flat(gqa_core(group(cq), ck, cv, scale)) if need_ctx else None
    return flat(o_lat), o_ctx


def mixer_mla(p_lat, p_ctx, angs, qn_g, kvn_g, w_uq, w_ukv, need_ctx):
    scale = (C_NOPE + C_ROPE) ** -0.5

    def heads(cq, ckv, kr, rotate):
        b, n = cq.shape[:2]
        q = (rms_norm(cq, qn_g) @ w_uq).reshape(b, n, C_HEADS, C_NOPE + C_ROPE)
        kv = (rms_norm(ckv, kvn_g) @ w_ukv).reshape(b, n, C_HEADS, C_NOPE + C_V)
        q_nope, q_rope = q[..., :C_NOPE], q[..., C_NOPE:]
        k_nope, v = kv[..., :C_NOPE], kv[..., C_NOPE:]
        k_rope = kr[:, :, None, :]
        if rotate:
            q_rope, k_rope = rope_2d(q_rope, angs), rope_2d(k_rope, angs)
        q = jnp.concatenate([q_nope, q_rope], -1)[:, :, :, None, :]
        k = jnp.concatenate([k_nope, jnp.broadcast_to(k_rope, k_nope.shape[:-1] + (C_ROPE,))], -1)
        return q, k, v

    def flat(o):
        return o.reshape(o.shape[0], o.shape[1], GROUP_WIDTH)

    q, k, v = heads(*p_lat, True)
    cq, ck, cv = heads(*p_ctx, False)
    ka = jnp.concatenate([ck, k], axis=1)
    va = jnp.concatenate([cv, v], axis=1)
    o_lat = sweep_query_blocks(lambda qb: gqa_core(qb[0], ka, va, scale), (q,))
    o_ctx = flat(gqa_core(cq, ck, cv, scale)) if need_ctx else None
    return flat(o_lat), o_ctx


def mixer_swa(p_lat, p_ctx, angs, sink, need_ctx):
    rep = D_HEADS // D_KV_HEADS
    scale = HEAD_DIM ** -0.5
    sink_gr = sink.reshape(D_KV_HEADS, rep)

    def heads(q, k, v):
        b, n = q.shape[:2]
        return (q.reshape(b, n, D_HEADS, HEAD_DIM), k.reshape(b, n, D_KV_HEADS, HEAD_DIM),
                v.reshape(b, n, D_KV_HEADS, HEAD_DIM))

    q, k, v = heads(*p_lat)
    q, k = rope_2d(q, angs), rope_2d(k, angs)
    cq, ck, cv = heads(*p_ctx)
    b, n = q.shape[:2]
    nb = n // QB
    qb = q.reshape(b, nb, QB, D_KV_HEADS, rep, HEAD_DIM)

    def band(t):
        tp = jnp.pad(t, ((0, 0), (WINDOW, WINDOW), (0, 0), (0, 0)))
        tb = tp.reshape(b, nb + 2, QB, D_KV_HEADS, t.shape[-1])
        return jnp.concatenate([tb[:, :-2], tb[:, 1:-1], tb[:, 2:]], axis=2)

    kband, vband = band(k), band(v)
    blk = jnp.arange(nb, dtype=jnp.int32)[:, None]
    tq = blk * QB + jnp.arange(QB, dtype=jnp.int32)[None, :]
    tk = (blk - 1) * QB + jnp.arange(3 * QB, dtype=jnp.int32)[None, :]
    valid = ((jnp.abs(tq[:, :, None] - tk[:, None, :]) <= WINDOW)
             & (tk[:, None, :] >= 0) & (tk[:, None, :] < n))
    s_band = jnp.einsum('bnqgrd,bnkgd->bngrqk', qb.astype(F32), kband.astype(F32)) * scale
    s_band = jnp.where(valid[None, :, None, None, :, :], s_band, NEG)
    s_ctx = jnp.einsum('bnqgrd,bcgd->bngrqc', qb.astype(F32), ck.astype(F32)) * scale
    s_sink = jnp.broadcast_to(sink_gr.astype(F32)[None, None, :, :, None, None], s_band.shape[:-1] + (1,))
    p = jax.nn.softmax(jnp.concatenate([s_ctx, s_band, s_sink], -1), axis=-1)
    c_len = ck.shape[1]
    o = (jnp.einsum('bngrqc,bcgd->bnqgrd', p[..., :c_len].astype(v.dtype), cv)
         + jnp.einsum('bngrqk,bnkgd->bnqgrd', p[..., c_len:c_len + 3 * QB].astype(v.dtype), vband))
    o_lat = o.reshape(b, n, GROUP_WIDTH)
    o_ctx = None
    if need_ctx:
        cqg = cq.reshape(cq.shape[0], cq.shape[1], D_KV_HEADS, rep, HEAD_DIM)
        o_ctx = gqa_core(cqg, ck, cv, scale, sink=sink_gr).reshape(cq.shape[0], cq.shape[1], GROUP_WIDTH)
    return o_lat, o_ctx


def moe(h, router_w, router_bias, w_gate, w_up, w_down):
    s = jax.nn.sigmoid(jnp.einsum('btd,de->bte', h.astype(F32), router_w.astype(F32)))
    sel = s + router_bias.astype(F32)
    grp = sel.reshape(sel.shape[:-1] + (N_EXPERT_GROUPS, EXPERTS_PER_GROUP))
    g_best = jnp.argmax(lax.top_k(grp, TOP_K)[0].sum(-1), axis=-1)
    expert_group = jnp.arange(N_EXPERTS, dtype=jnp.int32) // EXPERTS_PER_GROUP
    masked = jnp.where(expert_group == g_best[..., None], sel, -jnp.inf)
    _, idx = lax.top_k(masked, TOP_K)
    w = jnp.take_along_axis(s, idx, axis=-1)
    w = w / jnp.sum(w, -1, keepdims=True)
    gates = jnp.sum(jax.nn.one_hot(idx, N_EXPERTS, dtype=F32) * w[..., None], axis=-2)
    hg = jnp.einsum('btd,edf->btef', h, w_gate)
    hu = jnp.einsum('btd,edf->btef', h, w_up)
    act = jax.nn.silu(hg) * hu * gates[..., None].astype(h.dtype)
    return jnp.einsum('btef,efd->btd', act, w_down)


def setup_inputs(seed: int = 0) -> dict:
    key = jax.random.key(seed)
    ks = jax.random.split(key, 29)

    def nrm(k, shape, scale):
        return jax.random.normal(k, shape, F32) * scale

    def gain(k, shape):
        return 1.0 + 0.02 * jax.random.normal(k, shape, F32)

    return {
        'x': nrm(ks[0], (BATCH, SEQ, D_MODEL), 1.0),
        'c': nrm(ks[1], (BATCH, D_MODEL), 1.0),
        'ctx': nrm(ks[2], (BATCH, CTX_LEN, D_MODEL), 1.0),
        'c_ctx': nrm(ks[3], (D_MODEL,), 1.0),
        'w_ada': nrm(ks[4], (DEPTH, D_MODEL, 6 * D_MODEL), 0.5 * D_MODEL ** -0.5),
        'b_ada': nrm(ks[5], (DEPTH, 6 * D_MODEL), 0.02),
        'w_in': nrm(ks[6], (DEPTH, D_MODEL, IN_COLS), D_MODEL ** -0.5),
        'w_out': nrm(ks[7], (DEPTH, D_MODEL, D_MODEL), BETA * D_MODEL ** -0.5),
        'diff_lambda_q1': nrm(ks[8], (DEPTH, A_QK), 0.1),
        'diff_lambda_k1': nrm(ks[9], (DEPTH, A_QK), 0.1),
        'diff_lambda_q2': nrm(ks[10], (DEPTH, A_QK), 0.1),
        'diff_lambda_k2': nrm(ks[11], (DEPTH, A_QK), 0.1),
        'diff_subln_g': gain(ks[12], (DEPTH, A_V)),
        'gqa_q_norm_g': gain(ks[13], (DEPTH, HEAD_DIM)),
        'gqa_k_norm_g': gain(ks[14], (DEPTH, HEAD_DIM)),
        'mla_q_norm_g': gain(ks[15], (DEPTH, C_Q_RANK)),
        'mla_kv_norm_g': gain(ks[16], (DEPTH, C_KV_RANK)),
        'mla_w_uq': nrm(ks[17], (DEPTH, C_Q_RANK, C_HEADS * (C_NOPE + C_ROPE)), C_Q_RANK ** -0.5),
        'mla_w_ukv': nrm(ks[18], (DEPTH, C_KV_RANK, C_HEADS * (C_NOPE + C_V)), C_KV_RANK ** -0.5),
        'swa_sink': nrm(ks[19], (DEPTH, D_HEADS), 0.5),
        'ln1_g': gain(ks[20], (DEPTH, D_MODEL)),
        'ln1_b': nrm(ks[21], (DEPTH, D_MODEL), 0.02),
        'ln2_g': gain(ks[22], (DEPTH, D_MODEL)),
        'ln2_b': nrm(ks[23], (DEPTH, D_MODEL), 0.02),
        'router_w': nrm(ks[24], (D_MODEL, N_EXPERTS), D_MODEL ** -0.5),
        'router_bias': nrm(ks[25], (N_EXPERTS,), 0.01),
        'exp_w_gate': nrm(ks[26], (DEPTH, N_EXPERTS, D_MODEL, D_EXPERT), D_MODEL ** -0.5),
        'exp_w_up': nrm(ks[27], (DEPTH, N_EXPERTS, D_MODEL, D_EXPERT), D_MODEL ** -0.5),
        'exp_w_down': nrm(ks[28], (DEPTH, N_EXPERTS, D_EXPERT, D_MODEL), BETA * D_EXPERT ** -0.5),
    }


def reference(x, c, ctx, c_ctx, w_ada, b_ada, w_in, w_out,
              diff_lambda_q1, diff_lambda_k1, diff_lambda_q2, diff_lambda_k2, diff_subln_g,
              gqa_q_norm_g, gqa_k_norm_g, mla_q_norm_g, mla_kv_norm_g, mla_w_uq, mla_w_ukv,
              swa_sink, ln1_g, ln1_b, ln2_g, ln2_b, router_w, router_bias,
              exp_w_gate, exp_w_up, exp_w_down):
    n = x.shape[1]
    ROWS = n // GRID_W
    row = jnp.repeat(jnp.arange(ROWS, dtype=jnp.int32), GRID_W)
    col = jnp.tile(jnp.arange(GRID_W, dtype=jnp.int32), ROWS)
    angs32 = axial_angles(row, col, A_QK)
    angs64 = axial_angles(row, col, HEAD_DIM)
    split_idx = tuple(int(i) for i in np.cumsum(IN_SPLITS)[:-1])

    xl, xc = x, ctx
    for l in range(DEPTH):
        need_ctx = l < DEPTH - 1
        sh1, sc1, g1, sh2, sc2, g2 = [m[:, None, :] for m in
                                      jnp.split(jax.nn.silu(c) @ w_ada[l] + b_ada[l], 6, axis=-1)]
        csh1, csc1, cg1, csh2, csc2, cg2 = jnp.split(jax.nn.silu(c_ctx) @ w_ada[l] + b_ada[l], 6, axis=-1)

        pl = jnp.split(modulate(xl, sh1, sc1) @ w_in[l], split_idx, axis=-1)
        pc = jnp.split(modulate(xc, csh1, csc1) @ w_in[l], split_idx, axis=-1)
        oa_l, oa_c = mixer_diff(pl[0:3], pc[0:3], angs32, diff_lambda_q1[l], diff_lambda_k1[l],
                                diff_lambda_q2[l], diff_lambda_k2[l], diff_subln_g[l], l, need_ctx)
        ob_l, ob_c = mixer_gqa(pl[3:6], pc[3:6], angs64, gqa_q_norm_g[l], gqa_k_norm_g[l], need_ctx)
        oc_l, oc_c = mixer_mla(pl[6:9], pc[6:9], angs32, mla_q_norm_g[l], mla_kv_norm_g[l],
                               mla_w_uq[l], mla_w_ukv[l], need_ctx)
        od_l, od_c = mixer_swa(pl[9:12], pc[9:12], angs64, swa_sink[l], need_ctx)
        yl = jnp.concatenate([oa_l, ob_l, oc_l, od_l], axis=-1) @ w_out[l]
        xl = layer_norm(ALPHA * xl + g1 * yl, ln1_g[l], ln1_b[l])

        yl = moe(modulate(xl, sh2, sc2), router_w, router_bias, exp_w_gate[l], exp_w_up[l], exp_w_down[l])
        xl = layer_norm(ALPHA * xl + g2 * yl, ln2_g[l], ln2_b[l])

        if need_ctx:
            yc = jnp.concatenate([oa_c, ob_c, oc_c, od_c], axis=-1) @ w_out[l]
            xc = layer_norm(ALPHA * xc + cg1 * yc, ln1_g[l], ln1_b[l])
            yc = moe(modulate(xc, csh2, csc2), router_w, router_bias, exp_w_gate[l], exp_w_up[l], exp_w_down[l])
            xc = layer_norm(ALPHA * xc + cg2 * yc, ln2_g[l], ln2_b[l])
    return xl
```

```python
import functools
import math

import numpy as np
import jax
import jax.numpy as jnp
from jax import lax
from jax.experimental import pallas as pl
from jax.experimental.pallas import tpu as pltpu

F32 = jnp.float32
BF16 = jnp.bfloat16

D_MODEL = 1024
HEAD_DIM = 64
GROUP_WIDTH = 256
GRID_W = 64
ROPE_THETA = 10000.0
EPS = 1e-6
NEG = -1e30
LOG2E = 1.4426950408889634

A_HEADS = 4
A_QK = 32
B_HEADS = 4
B_KV_HEADS = 2
C_HEADS = 4
C_Q_RANK = 192
C_KV_RANK = 128
C_NOPE = 64
C_ROPE = 32
D_HEADS = 4
D_KV_HEADS = 2
WINDOW = 128
N_EXPERTS = 16
EXPERTS_PER_GROUP = 4
N_EXPERT_GROUPS = 4
D_EXPERT = 256

TQ = 256
TK = 256
SWA_T = 128
LANES = 128
ONES_ROWS = 16
VMEM_LIMIT = 56 * 1024 * 1024

_C_QA, _C_QAS, _C_KA, _C_KAS = 0, 256, 512, 768
_C_QB, _C_QBS, _C_KB, _C_KBS = 1024, 1280, 1536, 1664
_C_QD, _C_QDS, _C_KD, _C_KDS = 1792, 2048, 2304, 2432
_C_CQ, _C_CKV, _C_KR, _C_KRS = 2560, 2816, 2944, 3072
_C_TOTAL = 3200
_R_VA, _R_VB, _R_VD, _R_TOTAL = 0, 256, 384, 512


def _nt_dot(a, b):
    return lax.dot_general(a, b, (((1,), (1,)), ((), ())), preferred_element_type=F32)


def _split_bf16(a):
    hi = a.astype(BF16)
    lo = (a - hi.astype(F32)).astype(BF16)
    return hi, lo


def _layer_norm(x):
    mu = jnp.mean(x, axis=-1, keepdims=True)
    xc = x - mu
    var = jnp.mean(xc * xc, axis=-1, keepdims=True)
    return xc * lax.rsqrt(var + EPS)


def _tile_lanes(a, n):
    return jnp.concatenate([a] * n, axis=1)


def _ada_kernel(cc_ref, w_ref, b_ref, o_ref):
    cc = cc_ref[...]
    s = cc * jax.nn.sigmoid(cc)
    s_hi, s_lo = _split_bf16(s)
    w_hi, w_lo = _split_bf16(w_ref[...])
    acc = jnp.dot(s_hi, w_hi, preferred_element_type=F32)
    acc += jnp.dot(s_hi, w_lo, preferred_element_type=F32)
    acc += jnp.dot(s_lo, w_hi, preferred_element_type=F32)
    o_ref[...] = acc + b_ref[...]


def _ada(cc, w_ada, b_ada):
    depth, d, n6 = w_ada.shape
    bn = 1536
    return pl.pallas_call(
        _ada_kernel,
        grid=(depth, n6 // bn),
        in_specs=[
            pl.BlockSpec((8, d), lambda l, j: (0, 0)),
            pl.BlockSpec((None, d, bn), lambda l, j: (l, 0, j)),
            pl.BlockSpec((None, 1, bn), lambda l, j: (l, 0, j)),
        ],
        out_specs=pl.BlockSpec((None, 8, bn), lambda l, j: (l, 0, j)),
        out_shape=jax.ShapeDtypeStruct((depth, 8, n6), F32),
        compiler_params=pltpu.CompilerParams(
            dimension_semantics=("arbitrary", "arbitrary"), vmem_limit_bytes=VMEM_LIMIT),
        name="ada",
    )(cc, w_ada, b_ada.reshape(depth, 1, n6))


def _block_diag_ones(n, blk):
    sh = int(math.log2(blk))
    r = lax.broadcasted_iota(jnp.int32, (n, n), 0) >> sh
    c = lax.broadcasted_iota(jnp.int32, (n, n), 1) >> sh
    return jnp.where(r == c, 1.0, 0.0).astype(BF16)


def _proj_kernel(x_ref, mod_ref, wrm_ref, wt_ref, wuq_ref, wuqs_ref, wukn_ref, wuvt_ref,
                 gqb_ref, gqbs_ref, gkb_ref, gkbs_ref, gcq_ref, gckv_ref,
                 cos32_ref, sin32_ref, cos64_ref, sin64_ref, cosc_ref, sinc_ref,
                 qa_ref, ka_ref, vat_ref, qb_ref, kb_ref, vbt_ref,
                 qc_ref, kc_ref, vct_ref, qd_ref, kd_ref, vdt_ref):
    d = D_MODEL
    xn = _layer_norm(x_ref[...])
    h = (xn * (1.0 + mod_ref[:, d:2 * d]) + mod_ref[:, 0:d]).astype(BF16)

    def cols(lo, width):
        return jnp.dot(h, wrm_ref[:, lo:lo + width], preferred_element_type=F32)

    cos32, sin32 = _tile_lanes(cos32_ref[...], 2), _tile_lanes(sin32_ref[...], 2)
    cos64, sin64 = cos64_ref[...], sin64_ref[...]
    cos64w, sin64w = _tile_lanes(cos64, 2), _tile_lanes(sin64, 2)

    sa = (A_QK ** -0.5) * LOG2E
    qa_ref[...] = ((cols(_C_QA, 256) * cos32 + cols(_C_QAS, 256) * sin32) * sa).astype(BF16)
    ka_ref[...] = (cols(_C_KA, 256) * cos32 + cols(_C_KAS, 256) * sin32).astype(BF16)

    sb = (HEAD_DIM ** -0.5) * LOG2E
    q = cols(_C_QB, 256)
    ssq = jnp.dot((q * q).astype(BF16), _block_diag_ones(256, HEAD_DIM), preferred_element_type=F32)
    r = lax.rsqrt(ssq * (1.0 / HEAD_DIM) + EPS)
    qr = (q * gqb_ref[...]) * cos64w + (cols(_C_QBS, 256) * gqbs_ref[...]) * sin64w
    qb_ref[...] = (qr * (r * sb)).astype(BF16)
    k = cols(_C_KB, 128)
    ssq = jnp.dot((k * k).astype(BF16), _block_diag_ones(128, HEAD_DIM), preferred_element_type=F32)
    r = lax.rsqrt(ssq * (1.0 / HEAD_DIM) + EPS)
    kr = (k * gkb_ref[...]) * cos64 + (cols(_C_KBS, 128) * gkbs_ref[...]) * sin64
    kb_ref[...] = (kr * r).astype(BF16)

    qd_ref[...] = ((cols(_C_QD, 256) * cos64w + cols(_C_QDS, 256) * sin64w) * sb).astype(BF16)
    kd_ref[...] = (cols(_C_KD, 128) * cos64 + cols(_C_KDS, 128) * sin64).astype(BF16)

    sc = ((C_NOPE + C_ROPE) ** -0.5) * LOG2E
    cosc, sinc = cosc_ref[...], sinc_ref[...]
    cq = cols(_C_CQ, 256)
    ms = jnp.sum(cq * cq, axis=-1, keepdims=True) * (1.0 / C_Q_RANK)
    cqn = (cq * lax.rsqrt(ms + EPS) * gcq_ref[...]).astype(BF16)
    qn = jnp.dot(cqn, wuq_ref[...], preferred_element_type=F32)
    qs = jnp.dot(cqn, wuqs_ref[...], preferred_element_type=F32)
    qc_ref[...] = ((qn * _tile_lanes(cosc, 4) + qs * _tile_lanes(sinc, 4)) * sc).astype(BF16)
    ckv = cols(_C_CKV, 128)
    ms = jnp.mean(ckv * ckv, axis=-1, keepdims=True)
    ckvn = (ckv * lax.rsqrt(ms + EPS) * gckv_ref[...]).astype(BF16)
    kn = jnp.dot(ckvn, wukn_ref[...], preferred_element_type=F32)
    krr = cols(_C_KR, 128) * cosc + cols(_C_KRS, 128) * sinc
    kc_ref[...] = (kn + _tile_lanes(krr, 4)).astype(BF16)
    vct_ref[...] = _nt_dot(wuvt_ref[...], ckvn).astype(BF16)

    vt = _nt_dot(wt_ref[...], h)
    vat_ref[...] = vt[_R_VA:_R_VA + 256].astype(BF16)
    vbt_ref[...] = vt[_R_VB:_R_VB + 128].astype(BF16)
    vdt_ref[...] = vt[_R_VD:_R_VD + 128].astype(BF16)


def _proj(xa, modsel, lw, tabs):
    b, nt, d = xa.shape
    nc = nt // TQ

    def full(a):
        nd = a.ndim
        return pl.BlockSpec(a.shape, lambda bi, i, _n=nd: (0,) * _n)

    def tab(a):
        return pl.BlockSpec((TQ, LANES), lambda bi, i: (i, 0))

    def rm(w):
        return pl.BlockSpec((None, TQ, w), lambda bi, i: (bi, i, 0))

    def tr(w):
        return pl.BlockSpec((None, None, w, TQ), lambda bi, i: (bi, i, 0, 0))

    weights = [lw["wrm"], lw["wt"], lw["wuq"], lw["wuqs"], lw["wukn"], lw["wuvt"],
               lw["gqb"], lw["gqbs"], lw["gkb"], lw["gkbs"], lw["gcq"], lw["gckv"]]
    tables = [tabs["cos32"], tabs["sin32"], tabs["cos64"], tabs["sin64"], tabs["cosc"], tabs["sinc"]]
    in_specs = ([pl.BlockSpec((None, TQ, d), lambda bi, i: (bi, i, 0)),
                 pl.BlockSpec((None, None, 1, 6 * d), lambda bi, i: (bi, jnp.minimum(i, 1), 0, 0))]
                + [full(w) for w in weights] + [tab(t) for t in tables])
    widths = [("rm", 256), ("rm", 256), ("tr", 256), ("rm", 256), ("rm", 128), ("tr", 128),
              ("rm", 512), ("rm", 512), ("tr", 256), ("rm", 256), ("rm", 128), ("tr", 128)]
    out_specs, out_shape = [], []
    for kind, w in widths:
        if kind == "rm":
            out_specs.append(rm(w))
            out_shape.append(jax.ShapeDtypeStruct((b, nt, w), BF16))
        else:
            out_specs.append(tr(w))
            out_shape.append(jax.ShapeDtypeStruct((b, nc, w, TQ), BF16))
    return pl.pallas_call(
        _proj_kernel,
        grid=(b, nc),
        in_specs=in_specs,
        out_specs=out_specs,
        out_shape=out_shape,
        compiler_params=pltpu.CompilerParams(
            dimension_semantics=("arbitrary", "arbitrary"), vmem_limit_bytes=VMEM_LIMIT),
        name="proj",
    )(xa, modsel, *weights, *tables)


def _flash_loop(qi, nc, k_ref, vt_ref, qz_ref, m_ref, acc_ref, sm_cfg):
    m_ref[...] = jnp.full(m_ref.shape, NEG, F32)
    acc_ref[...] = jnp.zeros(acc_ref.shape, F32)
    ones = jnp.ones((ONES_ROWS, TK), BF16)

    def body(c, carry):
        kc = k_ref[pl.ds(pl.multiple_of(c * TK, TK), TK), :]
        vc = vt_ref[c]
        for j, (kl, kh, vl, vh) in enumerate(sm_cfg):
            s = jnp.dot(kc[:, kl:kh], qz_ref[j], preferred_element_type=F32)
            m_old = m_ref[j]
            m_new = jnp.maximum(m_old, jnp.max(s, axis=0, keepdims=True))
            alpha = jnp.exp2(m_old - m_new)
            p = jnp.exp2(s - m_new).astype(BF16)
            va = jnp.concatenate([vc[vl:vh, :], ones], axis=0)
            acc_ref[j] = alpha * acc_ref[j] + jnp.dot(va, p, preferred_element_type=F32)
            m_ref[j] = m_new
        return carry

    lax.fori_loop(0, jnp.where(qi == 0, 1, nc), body, 0)


def _normalized(acc_ref, j, dv):
    a = acc_ref[j]
    return a[0:dv] / a[dv:dv + 1]


def _attn_a_kernel(q_ref, k_ref, vt_ref, lq1_ref, lk1_ref, lq2_ref, lk2_ref, g_ref, o_ref,
                   qz_ref, m_ref, acc_ref, *, nc, lam_init):
    qi = pl.program_id(1)
    qt = q_ref[...].astype(F32).T
    row = lax.broadcasted_iota(jnp.int32, qt.shape, 0)
    for j in range(2 * A_HEADS):
        keep = (row >= A_QK * j) & (row < A_QK * (j + 1))
        qz_ref[j] = jnp.where(keep, qt, 0.0).astype(BF16)
    cfg = [(0, 256, HEAD_DIM * (j // 2), HEAD_DIM * (j // 2 + 1)) for j in range(2 * A_HEADS)]
    _flash_loop(qi, nc, k_ref, vt_ref, qz_ref, m_ref, acc_ref, cfg)

    lam = (jnp.exp(jnp.sum(lq1_ref[...] * lk1_ref[...], axis=-1, keepdims=True))
           - jnp.exp(jnp.sum(lq2_ref[...] * lk2_ref[...], axis=-1, keepdims=True)) + lam_init)
    outs = []
    for hd in range(A_HEADS):
        o = _normalized(acc_ref, 2 * hd, HEAD_DIM) - lam * _normalized(acc_ref, 2 * hd + 1, HEAD_DIM)
        ms = jnp.mean(o * o, axis=0, keepdims=True)
        outs.append(o * lax.rsqrt(ms + EPS) * g_ref[...] * (1.0 - lam_init))
    o_ref[...] = jnp.concatenate(outs, axis=0).T.astype(BF16)


def _attn_b_kernel(q_ref, k_ref, vt_ref, o_ref, qz_ref, m_ref, acc_ref, *, nc):
    qi = pl.program_id(1)
    qt = q_ref[...].astype(F32).T
    zeros = jnp.zeros((HEAD_DIM, TQ), F32)
    cfg = []
    for hd in range(B_HEADS):
        g = hd // (B_HEADS // B_KV_HEADS)
        qh = qt[HEAD_DIM * hd:HEAD_DIM * (hd + 1)]
        parts = [zeros] * B_KV_HEADS
        parts[g] = qh
        qz_ref[hd] = jnp.concatenate(parts, axis=0).astype(BF16)
        cfg.append((0, 128, HEAD_DIM * g, HEAD_DIM * (g + 1)))
    _flash_loop(qi, nc, k_ref, vt_ref, qz_ref, m_ref, acc_ref, cfg)
    outs = [_normalized(acc_ref, hd, HEAD_DIM) for hd in range(B_HEADS)]
    o_ref[...] = jnp.concatenate(outs, axis=0).T.astype(BF16)


def _attn_c_kernel(q_ref, k_ref, vt_ref, o_ref, qz_ref, m_ref, acc_ref, *, nc):
    qi = pl.program_id(1)
    qt = q_ref[...].astype(F32).T
    cfg = []
    for hd in range(C_HEADS):
        qz_ref[hd] = qt[LANES * hd:LANES * (hd + 1)].astype(BF16)
        cfg.append((LANES * hd, LANES * (hd + 1), HEAD_DIM * hd, HEAD_DIM * (hd + 1)))
    _flash_loop(qi, nc, k_ref, vt_ref, qz_ref, m_ref, acc_ref, cfg)
    outs = [_normalized(acc_ref, hd, HEAD_DIM) for hd in range(C_HEADS)]
    o_ref[...] = jnp.concatenate(outs, axis=0).T.astype(BF16)


def _attn_full(kind, q, k, vt, extras=(), lam_init=0.0):
    b, nt, wq = q.shape
    wk = k.shape[-1]
    nc, wv = vt.shape[1], vt.shape[2]
    if kind == "a":
        body, n_sm, dk = functools.partial(_attn_a_kernel, nc=nc, lam_init=lam_init), 2 * A_HEADS, 256
    elif kind == "b":
        body, n_sm, dk = functools.partial(_attn_b_kernel, nc=nc), B_HEADS, 128
    else:
        body, n_sm, dk = functools.partial(_attn_c_kernel, nc=nc), C_HEADS, 128
    resident = pl.Buffered(1)
    in_specs = [
        pl.BlockSpec((None, TQ, wq), lambda bi, i: (bi, i, 0)),
        pl.BlockSpec((None, nt, wk), lambda bi, i: (bi, 0, 0), pipeline_mode=resident),
        pl.BlockSpec((None, nc, wv, TK), lambda bi, i: (bi, 0, 0, 0), pipeline_mode=resident),
    ] + [pl.BlockSpec(e.shape, lambda bi, i: (0, 0)) for e in extras]
    return pl.pallas_call(
        body,
        grid=(b, nc),
        in_specs=in_specs,
        out_specs=pl.BlockSpec((None, TQ, GROUP_WIDTH), lambda bi, i: (bi, i, 0)),
        out_shape=jax.ShapeDtypeStruct((b, nt, GROUP_WIDTH), BF16),
        scratch_shapes=[
            pltpu.VMEM((n_sm, dk, TQ), BF16),
            pltpu.VMEM((n_sm, 1, TQ), F32),
            pltpu.VMEM((n_sm, HEAD_DIM + ONES_ROWS, TQ), F32),
        ],
        compiler_params=pltpu.CompilerParams(
            dimension_semantics=("arbitrary", "arbitrary"), vmem_limit_bytes=VMEM_LIMIT),
        name="attn_" + kind,
    )(q, k, vt, *extras)


def _attn_d_kernel(q_ref, kc_ref, kp_ref, ko_ref, kn_ref, vc_ref, vp_ref, vo_ref, vn_ref, sink_ref,
                   o_ref, *, nb, ctx_blocks):
    n = pl.program_id(1)
    t = SWA_T
    qt = q_ref[...].astype(F32).T
    zeros = jnp.zeros((HEAD_DIM, t), F32)
    ko_i = lax.broadcasted_iota(jnp.int32, (t, t), 0)
    qo_i = lax.broadcasted_iota(jnp.int32, (t, t), 1)
    band = n >= ctx_blocks
    ok_prev = jnp.logical_and(n >= ctx_blocks + 1, qo_i <= ko_i)
    ok_own = jnp.logical_and(band, ko_i >= 0)
    ok_next = jnp.logical_and(jnp.logical_and(band, n + 1 <= nb - 1), ko_i <= qo_i)
    kc, kp, ko, kn = kc_ref[...], kp_ref[...], ko_ref[...], kn_ref[...]
    vc, vp, vo, vn = vc_ref[...], vp_ref[...], vo_ref[...], vn_ref[...]
    outs = []
    for hd in range(D_HEADS):
        g = hd // (D_HEADS // D_KV_HEADS)
        parts = [zeros] * D_KV_HEADS
        parts[g] = qt[HEAD_DIM * hd:HEAD_DIM * (hd + 1)]
        qz = jnp.concatenate(parts, axis=0).astype(BF16)
        s_c = jnp.dot(kc, qz, preferred_element_type=F32)
        s_p = jnp.where(ok_prev, jnp.dot(kp, qz, preferred_element_type=F32), NEG)
        s_o = jnp.where(ok_own, jnp.dot(ko, qz, preferred_element_type=F32), NEG)
        s_n = jnp.where(ok_next, jnp.dot(kn, qz, preferred_element_type=F32), NEG)
        sk = sink_ref[0:1, hd:hd + 1] * LOG2E
        m = jnp.maximum(jnp.max(s_c, axis=0, keepdims=True), jnp.max(s_p, axis=0, keepdims=True))
        m = jnp.maximum(m, jnp.max(s_o, axis=0, keepdims=True))
        m = jnp.maximum(m, jnp.max(s_n, axis=0, keepdims=True))
        m = jnp.maximum(m, sk)
        p_c, p_p = jnp.exp2(s_c - m), jnp.exp2(s_p - m)
        p_o, p_n = jnp.exp2(s_o - m), jnp.exp2(s_n - m)
        l = (jnp.sum(p_c, axis=0, keepdims=True) + jnp.sum(p_p, axis=0, keepdims=True)
             + jnp.sum(p_o, axis=0, keepdims=True) + jnp.sum(p_n, axis=0, keepdims=True)
             + jnp.exp2(sk - m))
        rows = slice(HEAD_DIM * g, HEAD_DIM * (g + 1))
        o = (jnp.dot(vc[rows], p_c.astype(BF16), preferred_element_type=F32)
             + jnp.dot(vp[rows], p_p.astype(BF16), preferred_element_type=F32)
             + jnp.dot(vo[rows], p_o.astype(BF16), preferred_element_type=F32)
             + jnp.dot(vn[rows], p_n.astype(BF16), preferred_element_type=F32))
        outs.append(o / l)
    o_ref[...] = jnp.concatenate(outs, axis=0).T.astype(BF16)


def _attn_d(q, k, vt, sink, ctx_len):
    b, nt, wq = q.shape
    wk = k.shape[-1]
    wv = vt.shape[2]
    t = SWA_T
    nb = nt // t
    per = TQ // t
    ctx_blocks = ctx_len // t
    assert ctx_len == TQ

    def kspec(shift):
        return pl.BlockSpec((None, t, wk), lambda bi, i: (bi, jnp.clip(i + shift, 0, nb - 1), 0))

    def vspec(shift):
        def imap(bi, i):
            j = jnp.clip(i + shift, 0, nb - 1)
            return (bi, j // per, 0, j % per)
        return pl.BlockSpec((None, None, wv, t), imap)

    in_specs = [
        pl.BlockSpec((None, t, wq), lambda bi, i: (bi, i, 0)),
        pl.BlockSpec((None, ctx_len, wk), lambda bi, i: (bi, 0, 0)),
        kspec(-1), kspec(0), kspec(1),
        pl.BlockSpec((None, None, wv, ctx_len), lambda bi, i: (bi, 0, 0, 0)),
        vspec(-1), vspec(0), vspec(1),
        pl.BlockSpec(sink.shape, lambda bi, i: (0, 0)),
    ]
    return pl.pallas_call(
        functools.partial(_attn_d_kernel, nb=nb, ctx_blocks=ctx_blocks),
        grid=(b, nb),
        in_specs=in_specs,
        out_specs=pl.BlockSpec((None, t, GROUP_WIDTH), lambda bi, i: (bi, i, 0)),
        out_shape=jax.ShapeDtypeStruct((b, nt, GROUP_WIDTH), BF16),
        compiler_params=pltpu.CompilerParams(
            dimension_semantics=("arbitrary", "arbitrary"), vmem_limit_bytes=VMEM_LIMIT),
        name="attn_d",
    )(q, k, k, k, k, vt, vt, vt, vt, sink)


def _route(sel_rows, s_rows):
    epg = EXPERTS_PER_GROUP
    gscore = []
    for g in range(N_EXPERT_GROUPS):
        r = sel_rows[epg * g:epg * (g + 1)]
        pair = None
        for i in range(epg):
            for j in range(i + 1, epg):
                v = r[i] + r[j]
                pair = v if pair is None else jnp.maximum(pair, v)
        gscore.append(pair)
    best, best_g = gscore[0], jnp.zeros_like(gscore[0], dtype=jnp.int32)
    for g in range(1, N_EXPERT_GROUPS):
        better = gscore[g] > best
        best_g = jnp.where(better, g, best_g)
        best = jnp.where(better, gscore[g], best)
    w = []
    for e in range(N_EXPERTS):
        g = e // epg
        cnt = jnp.zeros_like(best_g)
        for e2 in range(epg * g, epg * (g + 1)):
            if e2 == e:
                continue
            beats = sel_rows[e2] > sel_rows[e]
            if e2 < e:
                beats = jnp.logical_or(beats, sel_rows[e2] == sel_rows[e])
            cnt = cnt + jnp.where(beats, 1, 0)
        chosen = jnp.logical_and(best_g == g, cnt < 2)
        w.append(jnp.where(chosen, s_rows[e], 0.0))
    tot = w[0]
    for e in range(1, N_EXPERTS):
        tot = tot + w[e]
    return [we / tot for we in w]


def _outproj_kernel(oa_ref, ob_ref, oc_ref, od_ref, x_ref, mod_ref, wout_ref, g_ref, b_ref,
                    rwt_ref, rb_ref, x1_ref, h2_ref, gates_ref, *, alpha):
    d = D_MODEL
    o = jnp.concatenate([oa_ref[...], ob_ref[...], oc_ref[...], od_ref[...]], axis=1)
    y = jnp.dot(o, wout_ref[...], preferred_element_type=F32)
    u = alpha * x_ref[...] + mod_ref[:, 2 * d:3 * d] * y
    x1 = _layer_norm(u) * g_ref[...] + b_ref[...]
    x1_ref[...] = x1
    h2 = _layer_norm(x1) * (1.0 + mod_ref[:, 4 * d:5 * d]) + mod_ref[:, 3 * d:4 * d]
    h2_ref[...] = h2.astype(BF16)

    h_hi, h_lo = _split_bf16(h2)
    w_hi, w_lo = _split_bf16(rwt_ref[...])
    logits = _nt_dot(w_hi, h_hi) + _nt_dot(w_hi, h_lo) + _nt_dot(w_lo, h_hi)
    s = jax.nn.sigmoid(logits)
    sel = s + rb_ref[...]
    s_rows = [s[e:e + 1] for e in range(N_EXPERTS)]
    sel_rows = [sel[e:e + 1] for e in range(N_EXPERTS)]
    gates = _route(sel_rows, s_rows)
    gt = jnp.concatenate(gates + [jnp.zeros((LANES - N_EXPERTS, TQ), F32)], axis=0)
    gates_ref[...] = gt.T


def _outproj(oa, ob, oc, od, xa, modsel, w_out, ln_g, ln_b, rwt, rb, alpha):
    b, nt, d = xa.shape
    nc = nt // TQ

    def rm(w):
        return pl.BlockSpec((None, TQ, w), lambda bi, i: (bi, i, 0))

    def full(a):
        nd = a.ndim
        return pl.BlockSpec(a.shape, lambda bi, i, _n=nd: (0,) * _n)

    in_specs = [rm(GROUP_WIDTH) for _ in range(4)] + [
        rm(d),
        pl.BlockSpec((None, None, 1, 6 * d), lambda bi, i: (bi, jnp.minimum(i, 1), 0, 0)),
        full(w_out), full(ln_g), full(ln_b), full(rwt), full(rb)]
    return pl.pallas_call(
        functools.partial(_outproj_kernel, alpha=alpha),
        grid=(b, nc),
        in_specs=in_specs,
        out_specs=[rm(d), rm(d), rm(LANES)],
        out_shape=[jax.ShapeDtypeStruct((b, nt, d), F32), jax.ShapeDtypeStruct((b, nt, d), BF16),
                   jax.ShapeDtypeStruct((b, nt, LANES), F32)],
        compiler_params=pltpu.CompilerParams(
            dimension_semantics=("arbitrary", "arbitrary"), vmem_limit_bytes=VMEM_LIMIT),
        name="outproj",
    )(oa, ob, oc, od, xa, modsel, w_out, ln_g, ln_b, rwt, rb)


def _moe_kernel(h_ref, gates_ref, x_ref, mod_ref, wg_ref, wu_ref, wd_ref, g_ref, b_ref, o_ref, *, alpha):
    d = D_MODEL
    h = h_ref[...]
    gates = gates_ref[...]
    y = jnp.zeros((TQ, d), F32)
    for g in range(N_EXPERT_GROUPS):
        acts = []
        for el in range(EXPERTS_PER_GROUP):
            e = EXPERTS_PER_GROUP * g + el
            hg = jnp.dot(h, wg_ref[e], preferred_element_type=F32)
            hu = jnp.dot(h, wu_ref[e], preferred_element_type=F32)
            a = hg * jax.nn.sigmoid(hg) * hu * gates[:, e:e + 1]
            acts.append(a.astype(BF16))
        y = y + jnp.dot(jnp.concatenate(acts, axis=1), wd_ref[g], preferred_element_type=F32)
    u = alpha * x_ref[...] + mod_ref[:, 5 * d:6 * d] * y
    o_ref[...] = _layer_norm(u) * g_ref[...] + b_ref[...]


def _moe(h2, gates, x1, modsel, wg, wu, wd, ln_g, ln_b, alpha):
    b, nt, d = x1.shape
    nc = nt // TQ
    resident = pl.Buffered(1)

    def rm(w):
        return pl.BlockSpec((None, TQ, w), lambda bi, i: (bi, i, 0))

    def res(a):
        nd = a.ndim
        return pl.BlockSpec(a.shape, lambda bi, i, _n=nd: (0,) * _n, pipeline_mode=resident)

    in_specs = [rm(d), rm(LANES), rm(d),
                pl.BlockSpec((None, None, 1, 6 * d), lambda bi, i: (bi, jnp.minimum(i, 1), 0, 0)),
                res(wg), res(wu), res(wd), res(ln_g), res(ln_b)]
    return pl.pallas_call(
        functools.partial(_moe_kernel, alpha=alpha),
        grid=(b, nc),
        in_specs=in_specs,
        out_specs=rm(d),
        out_shape=jax.ShapeDtypeStruct((b, nt, d), F32),
        compiler_params=pltpu.CompilerParams(
            dimension_semantics=("arbitrary", "arbitrary"), vmem_limit_bytes=VMEM_LIMIT),
        name="moe",
    )(h2, gates, x1, modsel, wg, wu, wd, ln_g, ln_b)


def _rope_perm(rot_dim):
    n = rot_dim // 4
    j = np.arange(rot_dim)
    return j ^ n, np.where((j // n) % 2 == 0, -1.0, 1.0).astype(np.float32)


def _swapped(w, rot_dim):
    perm, sign = _rope_perm(rot_dim)
    cols = w.shape[-1]
    idx = (np.arange(cols) // rot_dim) * rot_dim + perm[np.arange(cols) % rot_dim]
    sgn = sign[np.arange(cols) % rot_dim]
    return w[..., idx] * sgn


def _rope_tables(n_lat, ctx_len):
    t = jnp.arange(n_lat, dtype=jnp.int32)
    row, col = (t // GRID_W).astype(F32), (t % GRID_W).astype(F32)

    def pattern(rot_dim):
        n = rot_dim // 4
        inv = ROPE_THETA ** (-jnp.arange(n, dtype=F32) / n)
        ar, ac = row[:, None] * inv, col[:, None] * inv
        ang = jnp.concatenate([ar, ar, ac, ac], axis=-1)
        return jnp.cos(ang), jnp.sin(ang)

    def with_ctx(a, fill):
        return jnp.concatenate([jnp.full((ctx_len, a.shape[1]), fill, F32), a], axis=0)

    c32, s32 = pattern(A_QK)
    c64, s64 = pattern(HEAD_DIM)
    ones = jnp.ones((n_lat, C_NOPE), F32)
    zeros = jnp.zeros((n_lat, C_NOPE), F32)
    pad1 = jnp.ones((n_lat, LANES - C_NOPE - C_ROPE), F32)
    pad0 = jnp.zeros((n_lat, LANES - C_NOPE - C_ROPE), F32)
    return {
        "cos32": with_ctx(jnp.tile(c32, (1, LANES // A_QK)), 1.0),
        "sin32": with_ctx(jnp.tile(s32, (1, LANES // A_QK)), 0.0),
        "cos64": with_ctx(jnp.tile(c64, (1, LANES // HEAD_DIM)), 1.0),
        "sin64": with_ctx(jnp.tile(s64, (1, LANES // HEAD_DIM)), 0.0),
        "cosc": with_ctx(jnp.concatenate([ones, c32, pad1], axis=-1), 1.0),
        "sinc": with_ctx(jnp.concatenate([zeros, s32, pad0], axis=-1), 0.0),
    }


def _prep_layer_weights(w_in, gq, gk, gcq, gckv, w_uq, w_ukv):
    d = w_in.shape[0]
    splits = np.cumsum([256, 256, 256, 256, 128, 128, C_Q_RANK, C_KV_RANK, C_ROPE, 256, 128, 128])[:-1]
    (a_q, a_k, a_v, b_q, b_k, b_v, c_q, c_kv, c_kr, d_q, d_k, d_v) = jnp.split(w_in, splits, axis=1)
    z = lambda n: jnp.zeros((d, n), F32)
    kr4 = jnp.concatenate([z(C_NOPE), c_kr, z(LANES - C_NOPE - C_ROPE)], axis=1)
    kr4s = jnp.concatenate([z(C_NOPE), _swapped(c_kr, C_ROPE), z(LANES - C_NOPE - C_ROPE)], axis=1)
    wrm = jnp.concatenate([
        a_q, _swapped(a_q, A_QK), a_k, _swapped(a_k, A_QK),
        b_q, _swapped(b_q, HEAD_DIM), b_k, _swapped(b_k, HEAD_DIM),
        d_q, _swapped(d_q, HEAD_DIM), d_k, _swapped(d_k, HEAD_DIM),
        c_q, z(256 - C_Q_RANK), c_kv, kr4, kr4s], axis=1)
    wt = jnp.concatenate([a_v, b_v, d_v], axis=1).T

    uq = w_uq.reshape(C_Q_RANK, C_HEADS, C_NOPE + C_ROPE)
    uq_n, uq_r = uq[..., :C_NOPE], uq[..., C_NOPE:]
    zq = lambda n: jnp.zeros((C_Q_RANK, C_HEADS, n), F32)
    pad = LANES - C_NOPE - C_ROPE
    wuq = jnp.concatenate([uq_n, uq_r, zq(pad)], axis=-1).reshape(C_Q_RANK, C_HEADS * LANES)
    wuqs = jnp.concatenate([zq(C_NOPE), _swapped(uq_r, C_ROPE), zq(pad)], axis=-1).reshape(C_Q_RANK, C_HEADS * LANES)
    zrows = jnp.zeros((256 - C_Q_RANK, C_HEADS * LANES), F32)
    wuq, wuqs = jnp.concatenate([wuq, zrows], axis=0), jnp.concatenate([wuqs, zrows], axis=0)
    ukv = w_ukv.reshape(C_KV_RANK, C_HEADS, C_NOPE + HEAD_DIM)
    uk_n, u_v = ukv[..., :C_NOPE], ukv[..., C_NOPE:]
    wukn = jnp.concatenate([uk_n, jnp.zeros((C_KV_RANK, C_HEADS, LANES - C_NOPE), F32)], axis=-1)
    wukn = wukn.reshape(C_KV_RANK, C_HEADS * LANES)
    wuvt = u_v.reshape(C_KV_RANK, C_HEADS * HEAD_DIM).T

    perm64, _ = _rope_perm(HEAD_DIM)
    return {
        "wrm": wrm.astype(BF16), "wt": wt.astype(BF16),
        "wuq": wuq.astype(BF16), "wuqs": wuqs.astype(BF16),
        "wukn": wukn.astype(BF16), "wuvt": wuvt.astype(BF16),
        "gqb": jnp.tile(gq, B_HEADS)[None, :], "gqbs": jnp.tile(gq[perm64], B_HEADS)[None, :],
        "gkb": jnp.tile(gk, B_KV_HEADS)[None, :], "gkbs": jnp.tile(gk[perm64], B_KV_HEADS)[None, :],
        "gcq": jnp.concatenate([gcq, jnp.zeros((256 - C_Q_RANK,), F32)])[None, :],
        "gckv": gckv[None, :],
    }


def kernel(x, c, ctx, c_ctx, w_ada, b_ada, w_in, w_out, diff_lambda_q1, diff_lambda_k1, diff_lambda_q2,
           diff_lambda_k2, diff_subln_g, gqa_q_norm_g, gqa_k_norm_g, mla_q_norm_g, mla_kv_norm_g, mla_w_uq,
           mla_w_ukv, swa_sink, ln1_g, ln1_b, ln2_g, ln2_b, router_w, router_bias,
           exp_w_gate, exp_w_up, exp_w_down):
    b, n_lat, d = x.shape
    ctx_len = ctx.shape[1]
    depth = w_ada.shape[0]
    assert d == D_MODEL and ctx_len == TQ and n_lat % TQ == 0 and b + 1 <= 8
    alpha = (2 * depth) ** 0.25

    xa = jnp.concatenate([ctx, x], axis=1)
    tabs = _rope_tables(n_lat, ctx_len)

    cc = jnp.concatenate([c, c_ctx[None, :], jnp.zeros((8 - b - 1, d), F32)], axis=0)
    mods = _ada(cc, w_ada, b_ada)
    rwt = router_w.T
    rb = router_bias[:, None]

    for l in range(depth):
        lat = mods[l, :b]
        cx = jnp.broadcast_to(mods[l, b], lat.shape)
        modsel = jnp.stack([cx, lat], axis=1)[:, :, None, :]
        lw = _prep_layer_weights(w_in[l], gqa_q_norm_g[l], gqa_k_norm_g[l], mla_q_norm_g[l],
                                 mla_kv_norm_g[l], mla_w_uq[l], mla_w_ukv[l])
        qa, ka, vat, qb, kb, vbt, qc, kc, vct, qd, kd, vdt = _proj(xa, modsel, lw, tabs)

        lam_init = 0.8 - 0.6 * math.exp(-0.3 * l)
        extras = (diff_lambda_q1[l][None, :], diff_lambda_k1[l][None, :], diff_lambda_q2[l][None, :],
                  diff_lambda_k2[l][None, :], diff_subln_g[l][:, None])
        oa = _attn_full("a", qa, ka, vat, extras, lam_init)
        ob = _attn_full("b", qb, kb, vbt)
        oc = _attn_full("c", qc, kc, vct)
        od = _attn_d(qd, kd, vdt, swa_sink[l][None, :], ctx_len)

        x1, h2, gates = _outproj(oa, ob, oc, od, xa, modsel, w_out[l].astype(BF16),
                                 ln1_g[l][None, :], ln1_b[l][None, :], rwt, rb, alpha)
        wg = exp_w_gate[l].astype(BF16)
        wu = exp_w_up[l].astype(BF16)
        wd = exp_w_down[l].astype(BF16).reshape(N_EXPERT_GROUPS, EXPERTS_PER_GROUP * D_EXPERT, d)
        xa = _moe(h2, gates, x1, modsel, wg, wu, wd, ln2_g[l][None, :], ln2_b[l][None, :], alpha)
    return xa[:, ctx_len:, :]
```

```python
import functools
import math

import numpy as np
import jax
import jax.numpy as jnp
from jax import lax
from jax.experimental import pallas as pl
from jax.experimental.pallas import tpu as pltpu

F32 = jnp.float32
BF16 = jnp.bfloat16

D_MODEL = 1024
HEAD_DIM = 64
GROUP_WIDTH = 256
GRID_W = 64
ROPE_THETA = 10000.0
EPS = 1e-6
NEG = -1e30
LOG2E = 1.4426950408889634

A_HEADS = 4
A_QK = 32
B_HEADS = 4
B_KV_HEADS = 2
C_HEADS = 4
C_Q_RANK = 192
C_KV_RANK = 128
C_NOPE = 64
C_ROPE = 32
D_HEADS = 4
D_KV_HEADS = 2
WINDOW = 128
N_EXPERTS = 16
EXPERTS_PER_GROUP = 4
N_EXPERT_GROUPS = 4
D_EXPERT = 256

TQ = 256
TK = 256
TK_LAT = 1024
SWA_T = 128
LANES = 128
ONES_ROWS = 16
VMEM_LIMIT = 56 * 1024 * 1024

_C_QA, _C_QAS, _C_KA, _C_KAS = 0, 256, 512, 768
_C_QB, _C_QBS, _C_KB, _C_KBS = 1024, 1280, 1536, 1664
_C_QD, _C_QDS, _C_KD, _C_KDS = 1792, 2048, 2304, 2432
_C_CQ, _C_CKV, _C_KR, _C_KRS = 2560, 2816, 2944, 3072
_C_TOTAL = 3200
_R_VA, _R_VB, _R_VD, _R_TOTAL = 0, 256, 384, 512


def _nt_dot(a, b):
    return lax.dot_general(a, b, (((1,), (1,)), ((), ())), preferred_element_type=F32)


def _split_bf16(a):
    hi = a.astype(BF16)
    lo = (a - hi.astype(F32)).astype(BF16)
    return hi, lo


def _layer_norm(x):
    mu = jnp.mean(x, axis=-1, keepdims=True)
    xc = x - mu
    var = jnp.mean(xc * xc, axis=-1, keepdims=True)
    return xc * lax.rsqrt(var + EPS)


def _tile_lanes(a, n):
    return jnp.concatenate([a] * n, axis=1)


def _ada_kernel(cc_ref, w_ref, b_ref, o_ref):
    cc = cc_ref[...]
    s = cc * jax.nn.sigmoid(cc)
    s_hi, s_lo = _split_bf16(s)
    w_hi, w_lo = _split_bf16(w_ref[...])
    acc = jnp.dot(s_hi, w_hi, preferred_element_type=F32)
    acc += jnp.dot(s_hi, w_lo, preferred_element_type=F32)
    acc += jnp.dot(s_lo, w_hi, preferred_element_type=F32)
    o_ref[...] = acc + b_ref[...]


def _ada(cc, w_ada, b_ada):
    depth, d, n6 = w_ada.shape
    bn = 1536
    return pl.pallas_call(
        _ada_kernel,
        grid=(depth, n6 // bn),
        in_specs=[
            pl.BlockSpec((8, d), lambda l, j: (0, 0)),
            pl.BlockSpec((None, d, bn), lambda l, j: (l, 0, j)),
            pl.BlockSpec((None, 1, bn), lambda l, j: (l, 0, j)),
        ],
        out_specs=pl.BlockSpec((None, 8, bn), lambda l, j: (l, 0, j)),
        out_shape=jax.ShapeDtypeStruct((depth, 8, n6), F32),
        compiler_params=pltpu.CompilerParams(
            dimension_semantics=("arbitrary", "arbitrary"), vmem_limit_bytes=VMEM_LIMIT),
        name="ada",
    )(cc, w_ada, b_ada.reshape(depth, 1, n6))


def _block_diag_ones(n, blk):
    sh = int(math.log2(blk))
    r = lax.broadcasted_iota(jnp.int32, (n, n), 0) >> sh
    c = lax.broadcasted_iota(jnp.int32, (n, n), 1) >> sh
    return jnp.where(r == c, 1.0, 0.0).astype(BF16)


def _proj_kernel(x_ref, mod_ref, wrm_ref, wt_ref, wuq_ref, wuqs_ref, wukn_ref, wuvt_ref,
                 gqb_ref, gqbs_ref, gkb_ref, gkbs_ref, gcq_ref, gckv_ref,
                 cos32_ref, sin32_ref, cos64_ref, sin64_ref, cosc_ref, sinc_ref,
                 qa_ref, ka_ref, vat_ref, qb_ref, kb_ref, vbt_ref,
                 qc_ref, kc_ref, vct_ref, qd_ref, kd_ref, vdt_ref):
    d = D_MODEL
    xn = _layer_norm(x_ref[...])
    h = (xn * (1.0 + mod_ref[:, d:2 * d]) + mod_ref[:, 0:d]).astype(BF16)

    def cols(lo, width):
        return jnp.dot(h, wrm_ref[:, lo:lo + width], preferred_element_type=F32)

    cos32, sin32 = _tile_lanes(cos32_ref[...], 2), _tile_lanes(sin32_ref[...], 2)
    cos64, sin64 = cos64_ref[...], sin64_ref[...]
    cos64w, sin64w = _tile_lanes(cos64, 2), _tile_lanes(sin64, 2)

    sa = (A_QK ** -0.5) * LOG2E
    qa_ref[...] = ((cols(_C_QA, 256) * cos32 + cols(_C_QAS, 256) * sin32) * sa).astype(BF16)
    ka_ref[...] = (cols(_C_KA, 256) * cos32 + cols(_C_KAS, 256) * sin32).astype(BF16)

    sb = (HEAD_DIM ** -0.5) * LOG2E
    q = cols(_C_QB, 256)
    ssq = jnp.dot((q * q).astype(BF16), _block_diag_ones(256, HEAD_DIM), preferred_element_type=F32)
    r = lax.rsqrt(ssq * (1.0 / HEAD_DIM) + EPS)
    qr = (q * gqb_ref[...]) * cos64w + (cols(_C_QBS, 256) * gqbs_ref[...]) * sin64w
    qb_ref[...] = (qr * (r * sb)).astype(BF16)
    k = cols(_C_KB, 128)
    ssq = jnp.dot((k * k).astype(BF16), _block_diag_ones(128, HEAD_DIM), preferred_element_type=F32)
    r = lax.rsqrt(ssq * (1.0 / HEAD_DIM) + EPS)
    kr = (k * gkb_ref[...]) * cos64 + (cols(_C_KBS, 128) * gkbs_ref[...]) * sin64
    kb_ref[...] = (kr * r).astype(BF16)

    qd_ref[...] = ((cols(_C_QD, 256) * cos64w + cols(_C_QDS, 256) * sin64w) * sb).astype(BF16)
    kd_ref[...] = (cols(_C_KD, 128) * cos64 + cols(_C_KDS, 128) * sin64).astype(BF16)

    sc = ((C_NOPE + C_ROPE) ** -0.5) * LOG2E
    cosc, sinc = cosc_ref[...], sinc_ref[...]
    cq = cols(_C_CQ, 256)
    ms = jnp.sum(cq * cq, axis=-1, keepdims=True) * (1.0 / C_Q_RANK)
    cqn = (cq * lax.rsqrt(ms + EPS) * gcq_ref[...]).astype(BF16)
    qn = jnp.dot(cqn, wuq_ref[...], preferred_element_type=F32)
    qs = jnp.dot(cqn, wuqs_ref[...], preferred_element_type=F32)
    qc_ref[...] = ((qn * _tile_lanes(cosc, 4) + qs * _tile_lanes(sinc, 4)) * sc).astype(BF16)
    ckv = cols(_C_CKV, 128)
    ms = jnp.mean(ckv * ckv, axis=-1, keepdims=True)
    ckvn = (ckv * lax.rsqrt(ms + EPS) * gckv_ref[...]).astype(BF16)
    kn = jnp.dot(ckvn, wukn_ref[...], preferred_element_type=F32)
    krr = cols(_C_KR, 128) * cosc + cols(_C_KRS, 128) * sinc
    kc_ref[...] = (kn + _tile_lanes(krr, 4)).astype(BF16)
    vct_ref[...] = _nt_dot(wuvt_ref[...], ckvn).astype(BF16)

    vt = _nt_dot(wt_ref[...], h)
    vat_ref[...] = vt[_R_VA:_R_VA + 256].astype(BF16)
    vbt_ref[...] = vt[_R_VB:_R_VB + 128].astype(BF16)
    vdt_ref[...] = vt[_R_VD:_R_VD + 128].astype(BF16)


def _proj(xa, modsel, lw, tabs):
    b, nt, d = xa.shape
    nc = nt // TQ

    def full(a):
        nd = a.ndim
        return pl.BlockSpec(a.shape, lambda bi, i, _n=nd: (0,) * _n)

    def tab(a):
        return pl.BlockSpec((TQ, LANES), lambda bi, i: (i, 0))

    def rm(w):
        return pl.BlockSpec((None, TQ, w), lambda bi, i: (bi, i, 0))

    def tr(w):
        return pl.BlockSpec((None, None, w, TQ), lambda bi, i: (bi, i, 0, 0))

    weights = [lw["wrm"], lw["wt"], lw["wuq"], lw["wuqs"], lw["wukn"], lw["wuvt"],
               lw["gqb"], lw["gqbs"], lw["gkb"], lw["gkbs"], lw["gcq"], lw["gckv"]]
    tables = [tabs["cos32"], tabs["sin32"], tabs["cos64"], tabs["sin64"], tabs["cosc"], tabs["sinc"]]
    in_specs = ([pl.BlockSpec((None, TQ, d), lambda bi, i: (bi, i, 0)),
                 pl.BlockSpec((None, None, 1, 6 * d), lambda bi, i: (bi, jnp.minimum(i, 1), 0, 0))]
                + [full(w) for w in weights] + [tab(t) for t in tables])
    widths = [("rm", 256), ("rm", 256), ("tr", 256), ("rm", 256), ("rm", 128), ("tr", 128),
              ("rm", 512), ("rm", 512), ("tr", 256), ("rm", 256), ("rm", 128), ("tr", 128)]
    out_specs, out_shape = [], []
    for kind, w in widths:
        if kind == "rm":
            out_specs.append(rm(w))
            out_shape.append(jax.ShapeDtypeStruct((b, nt, w), BF16))
        else:
            out_specs.append(tr(w))
            out_shape.append(jax.ShapeDtypeStruct((b, nc, w, TQ), BF16))
    return pl.pallas_call(
        _proj_kernel,
        grid=(b, nc),
        in_specs=in_specs,
        out_specs=out_specs,
        out_shape=out_shape,
        compiler_params=pltpu.CompilerParams(
            dimension_semantics=("arbitrary", "arbitrary"), vmem_limit_bytes=VMEM_LIMIT),
        name="proj",
    )(xa, modsel, *weights, *tables)


def _flash_loop(qi, nc, k_ref, vt_ref, qz_ref, m_ref, acc_ref,k_lanes, pv_cfg):
    per = TK_LAT // TK
    n_lat = (nc - 1) // per
    same_lanes = all(kl == k_lanes[0] for kl in k_lanes)

    def scores(kc):
        if same_lanes:
            return jnp.dot(kc[:, k_lanes[0][0]:k_lanes[0][1]], qz_ref[...], preferred_element_type=F32)
        return jnp.concatenate(
            [jnp.dot(kc[:, lo:hi], qz_ref[:, TQ * j:TQ * (j + 1)], preferred_element_type=F32)
             for j, (lo, hi) in enumerate(k_lanes)], axis=1)

    def accumulate(vc, p_of, alpha):
        ones = jnp.ones((ONES_ROWS, vc.shape[1]), BF16)
        for vl, vh, j0, nj in pv_cfg:
            cols = slice(TQ * j0, TQ * (j0 + nj))
            va = jnp.concatenate([vc[vl:vh, :], ones], axis=0)
            pv = jnp.dot(va, p_of(cols), preferred_element_type=F32)
            acc_ref[:, cols] = pv if alpha is None else alpha[:, cols] * acc_ref[:, cols] + pv

    def lat_keys(c):
        return k_ref[pl.ds(pl.multiple_of(TK + c * TK_LAT, TK), TK_LAT), :]

    def lat_values(c):
        return jnp.concatenate([vt_ref[1 + c * per + i] for i in range(per)], axis=1)

    s = scores(k_ref[0:TK, :])
    m_new = jnp.max(s, axis=0, keepdims=True)
    p = jnp.exp2(s - m_new).astype(BF16)
    m_ref[...] = m_new
    accumulate(vt_ref[0], lambda cols: p[:, cols], None)

    def body(c, carry):
        s = scores(lat_keys(c))
        m_old = m_ref[...]
        m_new = jnp.maximum(m_old, jnp.max(s, axis=0, keepdims=True))
        alpha = jnp.exp2(m_old - m_new)
        p = jnp.exp2(s - m_new).astype(BF16)
        m_ref[...] = m_new
        accumulate(lat_values(c), lambda cols: p[:, cols], alpha)
        return carry

    lax.fori_loop(0, jnp.where(qi == 0, 0, n_lat), body, 0)


def _normalized(acc_ref, j, dv):
    a = acc_ref[:, TQ * j:TQ * (j + 1)]
    return a[0:dv] / a[dv:dv + 1]


def _attn_a_kernel(q_ref, k_ref, vt_ref, lq1_ref, lk1_ref, lq2_ref, lk2_ref, g_ref, o_ref,
                   qz_ref, m_ref, acc_ref,*, nc, lam_init):
    qi = pl.program_id(1)
    qt = q_ref[...].astype(F32).T
    row = lax.broadcasted_iota(jnp.int32, qt.shape, 0)
    for j in range(2 * A_HEADS):
        keep = (row >= A_QK * j) & (row < A_QK * (j + 1))
        qz_ref[:, TQ * j:TQ * (j + 1)] = jnp.where(keep, qt, 0.0).astype(BF16)
    k_lanes = [(0, 256)] * (2 * A_HEADS)
    pv_cfg = [(HEAD_DIM * hd, HEAD_DIM * (hd + 1), 2 * hd, 2) for hd in range(A_HEADS)]
    _flash_loop(qi, nc, k_ref, vt_ref, qz_ref, m_ref, acc_ref,k_lanes, pv_cfg)

    lam = (jnp.exp(jnp.sum(lq1_ref[...] * lk1_ref[...], axis=-1, keepdims=True))
           - jnp.exp(jnp.sum(lq2_ref[...] * lk2_ref[...], axis=-1, keepdims=True)) + lam_init)
    outs = []
    for hd in range(A_HEADS):
        o = _normalized(acc_ref, 2 * hd, HEAD_DIM) - lam * _normalized(acc_ref, 2 * hd + 1, HEAD_DIM)
        ms = jnp.mean(o * o, axis=0, keepdims=True)
        outs.append(o * lax.rsqrt(ms + EPS) * g_ref[...] * (1.0 - lam_init))
    o_ref[...] = jnp.concatenate(outs, axis=0).T.astype(BF16)


def _attn_b_kernel(q_ref, k_ref, vt_ref, o_ref, qz_ref, m_ref, acc_ref,*, nc):
    qi = pl.program_id(1)
    qt = q_ref[...].astype(F32).T
    zeros = jnp.zeros((HEAD_DIM, TQ), F32)
    rep = B_HEADS // B_KV_HEADS
    for hd in range(B_HEADS):
        parts = [zeros] * B_KV_HEADS
        parts[hd // rep] = qt[HEAD_DIM * hd:HEAD_DIM * (hd + 1)]
        qz_ref[:, TQ * hd:TQ * (hd + 1)] = jnp.concatenate(parts, axis=0).astype(BF16)
    k_lanes = [(0, 128)] * B_HEADS
    pv_cfg = [(HEAD_DIM * g, HEAD_DIM * (g + 1), rep * g, rep) for g in range(B_KV_HEADS)]
    _flash_loop(qi, nc, k_ref, vt_ref, qz_ref, m_ref, acc_ref,k_lanes, pv_cfg)
    outs = [_normalized(acc_ref, hd, HEAD_DIM) for hd in range(B_HEADS)]
    o_ref[...] = jnp.concatenate(outs, axis=0).T.astype(BF16)


def _attn_c_kernel(q_ref, k_ref, vt_ref, o_ref, qz_ref, m_ref, acc_ref,*, nc):
    qi = pl.program_id(1)
    qt = q_ref[...].astype(F32).T
    k_lanes, pv_cfg = [], []
    for hd in range(C_HEADS):
        qz_ref[:, TQ * hd:TQ * (hd + 1)] = qt[LANES * hd:LANES * (hd + 1)].astype(BF16)
        k_lanes.append((LANES * hd, LANES * (hd + 1)))
        pv_cfg.append((HEAD_DIM * hd, HEAD_DIM * (hd + 1), hd, 1))
    _flash_loop(qi, nc, k_ref, vt_ref, qz_ref, m_ref, acc_ref,k_lanes, pv_cfg)
    outs = [_normalized(acc_ref, hd, HEAD_DIM) for hd in range(C_HEADS)]
    o_ref[...] = jnp.concatenate(outs, axis=0).T.astype(BF16)


def _attn_full(kind, q, k, vt, extras=(), lam_init=0.0):
    b, nt, wq = q.shape
    wk = k.shape[-1]
    nc, wv = vt.shape[1], vt.shape[2]
    assert (nt - TK) % TK_LAT == 0
    if kind == "a":
        body, n_sm, dk = functools.partial(_attn_a_kernel, nc=nc, lam_init=lam_init), 2 * A_HEADS, 256
    elif kind == "b":
        body, n_sm, dk = functools.partial(_attn_b_kernel, nc=nc), B_HEADS, 128
    else:
        body, n_sm, dk = functools.partial(_attn_c_kernel, nc=nc), C_HEADS, 128
    resident = pl.Buffered(1)
    in_specs = [
        pl.BlockSpec((None, TQ, wq), lambda bi, i: (bi, i, 0)),
        pl.BlockSpec((None, nt, wk), lambda bi, i: (bi, 0, 0), pipeline_mode=resident),
        pl.BlockSpec((None, nc, wv, TK), lambda bi, i: (bi, 0, 0, 0), pipeline_mode=resident),
    ] + [pl.BlockSpec(e.shape, lambda bi, i: (0, 0)) for e in extras]
    return pl.pallas_call(
        body,
        grid=(b, nc),
        in_specs=in_specs,
        out_specs=pl.BlockSpec((None, TQ, GROUP_WIDTH), lambda bi, i: (bi, i, 0)),
        out_shape=jax.ShapeDtypeStruct((b, nt, GROUP_WIDTH), BF16),
        scratch_shapes=[
            pltpu.VMEM((dk, n_sm * TQ), BF16),
            pltpu.VMEM((1, n_sm * TQ), F32),
            pltpu.VMEM((HEAD_DIM + ONES_ROWS, n_sm * TQ), F32),
        ],
        compiler_params=pltpu.CompilerParams(
            dimension_semantics=("arbitrary", "arbitrary"), vmem_limit_bytes=VMEM_LIMIT),
        name="attn_" + kind,
    )(q, k, vt, *extras)


def _attn_d_kernel(q_ref, kc_ref, kp_ref, ko_ref, kn_ref, vc_ref, vp_ref, vo_ref, vn_ref, sink_ref,
                   o_ref, *, nb, ctx_blocks):
    n = pl.program_id(1)
    t = SWA_T
    qt = q_ref[...].astype(F32).T
    zeros = jnp.zeros((HEAD_DIM, t), F32)
    ko_i = lax.broadcasted_iota(jnp.int32, (t, t), 0)
    qo_i = lax.broadcasted_iota(jnp.int32, (t, t), 1)
    band = n >= ctx_blocks
    ok_prev = jnp.logical_and(n >= ctx_blocks + 1, qo_i <= ko_i)
    ok_own = jnp.logical_and(band, ko_i >= 0)
    ok_next = jnp.logical_and(jnp.logical_and(band, n + 1 <= nb - 1), ko_i <= qo_i)
    kc, kp, ko, kn = kc_ref[...], kp_ref[...], ko_ref[...], kn_ref[...]
    vc, vp, vo, vn = vc_ref[...], vp_ref[...], vo_ref[...], vn_ref[...]
    outs = []
    for hd in range(D_HEADS):
        g = hd // (D_HEADS // D_KV_HEADS)
        parts = [zeros] * D_KV_HEADS
        parts[g] = qt[HEAD_DIM * hd:HEAD_DIM * (hd + 1)]
        qz = jnp.concatenate(parts, axis=0).astype(BF16)
        s_c = jnp.dot(kc, qz, preferred_element_type=F32)
        s_p = jnp.where(ok_prev, jnp.dot(kp, qz, preferred_element_type=F32), NEG)
        s_o = jnp.where(ok_own, jnp.dot(ko, qz, preferred_element_type=F32), NEG)
        s_n = jnp.where(ok_next, jnp.dot(kn, qz, preferred_element_type=F32), NEG)
        sk = sink_ref[0:1, hd:hd + 1] * LOG2E
        m = jnp.maximum(jnp.max(s_c, axis=0, keepdims=True), jnp.max(s_p, axis=0, keepdims=True))
        m = jnp.maximum(m, jnp.max(s_o, axis=0, keepdims=True))
        m = jnp.maximum(m, jnp.max(s_n, axis=0, keepdims=True))
        m = jnp.maximum(m, sk)
        p_c, p_p = jnp.exp2(s_c - m), jnp.exp2(s_p - m)
        p_o, p_n = jnp.exp2(s_o - m), jnp.exp2(s_n - m)
        l = (jnp.sum(p_c, axis=0, keepdims=True) + jnp.sum(p_p, axis=0, keepdims=True)
             + jnp.sum(p_o, axis=0, keepdims=True) + jnp.sum(p_n, axis=0, keepdims=True)
             + jnp.exp2(sk - m))
        rows = slice(HEAD_DIM * g, HEAD_DIM * (g + 1))
        o = (jnp.dot(vc[rows], p_c.astype(BF16), preferred_element_type=F32)
             + jnp.dot(vp[rows], p_p.astype(BF16), preferred_element_type=F32)
             + jnp.dot(vo[rows], p_o.astype(BF16), preferred_element_type=F32)
             + jnp.dot(vn[rows], p_n.astype(BF16), preferred_element_type=F32))
        outs.append(o / l)
    o_ref[...] = jnp.concatenate(outs, axis=0).T.astype(BF16)


def _attn_d(q, k, vt, sink, ctx_len):
    b, nt, wq = q.shape
    wk = k.shape[-1]
    wv = vt.shape[2]
    t = SWA_T
    nb = nt // t
    per = TQ // t
    ctx_blocks = ctx_len // t
    assert ctx_len == TQ

    def kspec(shift):
        return pl.BlockSpec((None, t, wk), lambda bi, i: (bi, jnp.clip(i + shift, 0, nb - 1), 0))

    def vspec(shift):
        def imap(bi, i):
            j = jnp.clip(i + shift, 0, nb - 1)
            return (bi, j // per, 0, j % per)
        return pl.BlockSpec((None, None, wv, t), imap)

    in_specs = [
        pl.BlockSpec((None, t, wq), lambda bi, i: (bi, i, 0)),
        pl.BlockSpec((None, ctx_len, wk), lambda bi, i: (bi, 0, 0)),
        kspec(-1), kspec(0), kspec(1),
        pl.BlockSpec((None, None, wv, ctx_len), lambda bi, i: (bi, 0, 0, 0)),
        vspec(-1), vspec(0), vspec(1),
        pl.BlockSpec(sink.shape, lambda bi, i: (0, 0)),
    ]
    return pl.pallas_call(
        functools.partial(_attn_d_kernel, nb=nb, ctx_blocks=ctx_blocks),
        grid=(b, nb),
        in_specs=in_specs,
        out_specs=pl.BlockSpec((None, t, GROUP_WIDTH), lambda bi, i: (bi, i, 0)),
        out_shape=jax.ShapeDtypeStruct((b, nt, GROUP_WIDTH), BF16),
        compiler_params=pltpu.CompilerParams(
            dimension_semantics=("arbitrary", "arbitrary"), vmem_limit_bytes=VMEM_LIMIT),
        name="attn_d",
    )(q, k, k, k, k, vt, vt, vt, vt, sink)


def _route(sel_rows, s_rows):
    epg = EXPERTS_PER_GROUP
    gscore = []
    for g in range(N_EXPERT_GROUPS):
        r = sel_rows[epg * g:epg * (g + 1)]
        pair = None
        for i in range(epg):
            for j in range(i + 1, epg):
                v = r[i] + r[j]
                pair = v if pair is None else jnp.maximum(pair, v)
        gscore.append(pair)
    best, best_g = gscore[0], jnp.zeros_like(gscore[0], dtype=jnp.int32)
    for g in range(1, N_EXPERT_GROUPS):
        better = gscore[g] > best
        best_g = jnp.where(better, g, best_g)
        best = jnp.where(better, gscore[g], best)
    w = []
    for e in range(N_EXPERTS):
        g = e // epg
        cnt = jnp.zeros_like(best_g)
        for e2 in range(epg * g, epg * (g + 1)):
            if e2 == e:
                continue
            beats = sel_rows[e2] > sel_rows[e]
            if e2 < e:
                beats = jnp.logical_or(beats, sel_rows[e2] == sel_rows[e])
            cnt = cnt + jnp.where(beats, 1, 0)
        chosen = jnp.logical_and(best_g == g, cnt < 2)
        w.append(jnp.where(chosen, s_rows[e], 0.0))
    tot = w[0]
    for e in range(1, N_EXPERTS):
        tot = tot + w[e]
    return [we / tot for we in w]


def _outproj_kernel(oa_ref, ob_ref, oc_ref, od_ref, x_ref, mod_ref, wout_ref, g_ref, b_ref,
                    rwt_ref, rb_ref, x1_ref, h2_ref, gates_ref, *, alpha):
    d = D_MODEL
    o = jnp.concatenate([oa_ref[...], ob_ref[...], oc_ref[...], od_ref[...]], axis=1)
    y = jnp.dot(o, wout_ref[...], preferred_element_type=F32)
    u = alpha * x_ref[...] + mod_ref[:, 2 * d:3 * d] * y
    x1 = _layer_norm(u) * g_ref[...] + b_ref[...]
    x1_ref[...] = x1
    h2 = _layer_norm(x1) * (1.0 + mod_ref[:, 4 * d:5 * d]) + mod_ref[:, 3 * d:4 * d]
    h2_ref[...] = h2.astype(BF16)

    h_hi, h_lo = _split_bf16(h2)
    w_hi, w_lo = _split_bf16(rwt_ref[...])
    logits = _nt_dot(w_hi, h_hi) + _nt_dot(w_hi, h_lo) + _nt_dot(w_lo, h_hi)
    s = jax.nn.sigmoid(logits)
    sel = s + rb_ref[...]
    s_rows = [s[e:e + 1] for e in range(N_EXPERTS)]
    sel_rows = [sel[e:e + 1] for e in range(N_EXPERTS)]
    gates = _route(sel_rows, s_rows)
    gt = jnp.concatenate(gates + [jnp.zeros((LANES - N_EXPERTS, TQ), F32)], axis=0)
    gates_ref[...] = gt.T


def _outproj(oa, ob, oc, od, xa, modsel, w_out, ln_g, ln_b, rwt, rb, alpha):
    b, nt, d = xa.shape
    nc = nt // TQ

    def rm(w):
        return pl.BlockSpec((None, TQ, w), lambda bi, i: (bi, i, 0))

    def full(a):
        nd = a.ndim
        return pl.BlockSpec(a.shape, lambda bi, i, _n=nd: (0,) * _n)

    in_specs = [rm(GROUP_WIDTH) for _ in range(4)] + [
        rm(d),
        pl.BlockSpec((None, None, 1, 6 * d), lambda bi, i: (bi, jnp.minimum(i, 1), 0, 0)),
        full(w_out), full(ln_g), full(ln_b), full(rwt), full(rb)]
    return pl.pallas_call(
        functools.partial(_outproj_kernel, alpha=alpha),
        grid=(b, nc),
        in_specs=in_specs,
        out_specs=[rm(d), rm(d), rm(LANES)],
        out_shape=[jax.ShapeDtypeStruct((b, nt, d), F32), jax.ShapeDtypeStruct((b, nt, d), BF16),
                   jax.ShapeDtypeStruct((b, nt, LANES), F32)],
        compiler_params=pltpu.CompilerParams(
            dimension_semantics=("arbitrary", "arbitrary"), vmem_limit_bytes=VMEM_LIMIT),
        name="outproj",
    )(oa, ob, oc, od, xa, modsel, w_out, ln_g, ln_b, rwt, rb)


def _moe_kernel(h_ref, gates_ref, x_ref, mod_ref, wg_ref, wu_ref, wd_ref, g_ref, b_ref, o_ref, *, alpha):
    d = D_MODEL
    h = h_ref[...]
    gates = gates_ref[...]
    y = jnp.zeros((TQ, d), F32)
    for g in range(N_EXPERT_GROUPS):
        acts = []
        for el in range(EXPERTS_PER_GROUP):
            e = EXPERTS_PER_GROUP * g + el
            hg = jnp.dot(h, wg_ref[e], preferred_element_type=F32)
            hu = jnp.dot(h, wu_ref[e], preferred_element_type=F32)
            a = hg * jax.nn.sigmoid(hg) * hu * gates[:, e:e + 1]
            acts.append(a.astype(BF16))
        y = y + jnp.dot(jnp.concatenate(acts, axis=1), wd_ref[g], preferred_element_type=F32)
    u = alpha * x_ref[...] + mod_ref[:, 5 * d:6 * d] * y
    o_ref[...] = _layer_norm(u) * g_ref[...] + b_ref[...]


def _moe(h2, gates, x1, modsel, wg, wu, wd, ln_g, ln_b, alpha):
    b, nt, d = x1.shape
    nc = nt // TQ
    resident = pl.Buffered(1)

    def rm(w):
        return pl.BlockSpec((None, TQ, w), lambda bi, i: (bi, i, 0))

    def res(a):
        nd = a.ndim
        return pl.BlockSpec(a.shape, lambda bi, i, _n=nd: (0,) * _n, pipeline_mode=resident)

    in_specs = [rm(d), rm(LANES), rm(d),
                pl.BlockSpec((None, None, 1, 6 * d), lambda bi, i: (bi, jnp.minimum(i, 1), 0, 0)),
                res(wg), res(wu), res(wd), res(ln_g), res(ln_b)]
    return pl.pallas_call(
        functools.partial(_moe_kernel, alpha=alpha),
        grid=(b, nc),
        in_specs=in_specs,
        out_specs=rm(d),
        out_shape=jax.ShapeDtypeStruct((b, nt, d), F32),
        compiler_params=pltpu.CompilerParams(
            dimension_semantics=("arbitrary", "arbitrary"), vmem_limit_bytes=VMEM_LIMIT),
        name="moe",
    )(h2, gates, x1, modsel, wg, wu, wd, ln_g, ln_b)


def _rope_perm(rot_dim):
    n = rot_dim // 4
    j = np.arange(rot_dim)
    return j ^ n, np.where((j // n) % 2 == 0, -1.0, 1.0).astype(np.float32)


def _swapped(w, rot_dim):
    perm, sign = _rope_perm(rot_dim)
    cols = w.shape[-1]
    idx = (np.arange(cols) // rot_dim) * rot_dim + perm[np.arange(cols) % rot_dim]
    sgn = sign[np.arange(cols) % rot_dim]
    return w[..., idx] * sgn


def _rope_tables(n_lat, ctx_len):
    t = jnp.arange(n_lat, dtype=jnp.int32)
    row, col = (t // GRID_W).astype(F32), (t % GRID_W).astype(F32)

    def pattern(rot_dim):
        n = rot_dim // 4
        inv = ROPE_THETA ** (-jnp.arange(n, dtype=F32) / n)
        ar, ac = row[:, None] * inv, col[:, None] * inv
        ang = jnp.concatenate([ar, ar, ac, ac], axis=-1)
        return jnp.cos(ang), jnp.sin(ang)

    def with_ctx(a, fill):
        return jnp.concatenate([jnp.full((ctx_len, a.shape[1]), fill, F32), a], axis=0)

    c32, s32 = pattern(A_QK)
    c64, s64 = pattern(HEAD_DIM)
    ones = jnp.ones((n_lat, C_NOPE), F32)
    zeros = jnp.zeros((n_lat, C_NOPE), F32)
    pad1 = jnp.ones((n_lat, LANES - C_NOPE - C_ROPE), F32)
    pad0 = jnp.zeros((n_lat, LANES - C_NOPE - C_ROPE), F32)
    return {
        "cos32": with_ctx(jnp.tile(c32, (1, LANES // A_QK)), 1.0),
        "sin32": with_ctx(jnp.tile(s32, (1, LANES // A_QK)), 0.0),
        "cos64": with_ctx(jnp.tile(c64, (1, LANES // HEAD_DIM)), 1.0),
        "sin64": with_ctx(jnp.tile(s64, (1, LANES // HEAD_DIM)), 0.0),
        "cosc": with_ctx(jnp.concatenate([ones, c32, pad1], axis=-1), 1.0),
        "sinc": with_ctx(jnp.concatenate([zeros, s32, pad0], axis=-1), 0.0),
    }


def _prep_layer_weights(w_in, gq, gk, gcq, gckv, w_uq, w_ukv):
    d = w_in.shape[0]
    splits = np.cumsum([256, 256, 256, 256, 128, 128, C_Q_RANK, C_KV_RANK, C_ROPE, 256, 128, 128])[:-1]
    (a_q, a_k, a_v, b_q, b_k, b_v, c_q, c_kv, c_kr, d_q, d_k, d_v) = jnp.split(w_in, splits, axis=1)
    z = lambda n: jnp.zeros((d, n), F32)
    kr4 = jnp.concatenate([z(C_NOPE), c_kr, z(LANES - C_NOPE - C_ROPE)], axis=1)
    kr4s = jnp.concatenate([z(C_NOPE), _swapped(c_kr, C_ROPE), z(LANES - C_NOPE - C_ROPE)], axis=1)
    wrm = jnp.concatenate([
        a_q, _swapped(a_q, A_QK), a_k, _swapped(a_k, A_QK),
        b_q, _swapped(b_q, HEAD_DIM), b_k, _swapped(b_k, HEAD_DIM),
        d_q, _swapped(d_q, HEAD_DIM), d_k, _swapped(d_k, HEAD_DIM),
        c_q, z(256 - C_Q_RANK), c_kv, kr4, kr4s], axis=1)
    wt = jnp.concatenate([a_v, b_v, d_v], axis=1).T

    uq = w_uq.reshape(C_Q_RANK, C_HEADS, C_NOPE + C_ROPE)
    uq_n, uq_r = uq[..., :C_NOPE], uq[..., C_NOPE:]
    zq = lambda n: jnp.zeros((C_Q_RANK, C_HEADS, n), F32)
    pad = LANES - C_NOPE - C_ROPE
    wuq = jnp.concatenate([uq_n, uq_r, zq(pad)], axis=-1).reshape(C_Q_RANK, C_HEADS * LANES)
    wuqs = jnp.concatenate([zq(C_NOPE), _swapped(uq_r, C_ROPE), zq(pad)], axis=-1).reshape(C_Q_RANK, C_HEADS * LANES)
    zrows = jnp.zeros((256 - C_Q_RANK, C_HEADS * LANES), F32)
    wuq, wuqs = jnp.concatenate([wuq, zrows], axis=0), jnp.concatenate([wuqs, zrows], axis=0)
    ukv = w_ukv.reshape(C_KV_RANK, C_HEADS, C_NOPE + HEAD_DIM)
    uk_n, u_v = ukv[..., :C_NOPE], ukv[..., C_NOPE:]
    wukn = jnp.concatenate([uk_n, jnp.zeros((C_KV_RANK, C_HEADS, LANES - C_NOPE), F32)], axis=-1)
    wukn = wukn.reshape(C_KV_RANK, C_HEADS * LANES)
    wuvt = u_v.reshape(C_KV_RANK, C_HEADS * HEAD_DIM).T

    perm64, _ = _rope_perm(HEAD_DIM)
    return {
        "wrm": wrm.astype(BF16), "wt": wt.astype(BF16),
        "wuq": wuq.astype(BF16), "wuqs": wuqs.astype(BF16),
        "wukn": wukn.astype(BF16), "wuvt": wuvt.astype(BF16),
        "gqb": jnp.tile(gq, B_HEADS)[None, :], "gqbs": jnp.tile(gq[perm64], B_HEADS)[None, :],
        "gkb": jnp.tile(gk, B_KV_HEADS)[None, :], "gkbs": jnp.tile(gk[perm64], B_KV_HEADS)[None, :],
        "gcq": jnp.concatenate([gcq, jnp.zeros((256 - C_Q_RANK,), F32)])[None, :],
        "gckv": gckv[None, :],
    }


def kernel(x, c, ctx, c_ctx, w_ada, b_ada, w_in, w_out, diff_lambda_q1, diff_lambda_k1, diff_lambda_q2,
           diff_lambda_k2, diff_subln_g, gqa_q_norm_g, gqa_k_norm_g, mla_q_norm_g, mla_kv_norm_g, mla_w_uq,
           mla_w_ukv, swa_sink, ln1_g, ln1_b, ln2_g, ln2_b, router_w, router_bias,
           exp_w_gate, exp_w_up, exp_w_down):
    b, n_lat, d = x.shape
    ctx_len = ctx.shape[1]
    depth = w_ada.shape[0]
    assert d == D_MODEL and ctx_len == TQ and n_lat % TQ == 0 and b + 1 <= 8
    alpha = (2 * depth) ** 0.25

    xa = jnp.concatenate([ctx, x], axis=1)
    tabs = _rope_tables(n_lat, ctx_len)

    cc = jnp.concatenate([c, c_ctx[None, :], jnp.zeros((8 - b - 1, d), F32)], axis=0)
    mods = _ada(cc, w_ada, b_ada)
    rwt = router_w.T
    rb = router_bias[:, None]

    for l in range(depth):
        lat = mods[l, :b]
        cx = jnp.broadcast_to(mods[l, b], lat.shape)
        modsel = jnp.stack([cx, lat], axis=1)[:, :, None, :]
        lw = _prep_layer_weights(w_in[l], gqa_q_norm_g[l], gqa_k_norm_g[l], mla_q_norm_g[l],
                                 mla_kv_norm_g[l], mla_w_uq[l], mla_w_ukv[l])
        qa, ka, vat, qb, kb, vbt, qc, kc, vct, qd, kd, vdt = _proj(xa, modsel, lw, tabs)

        lam_init = 0.8 - 0.6 * math.exp(-0.3 * l)
        extras = (diff_lambda_q1[l][None, :], diff_lambda_k1[l][None, :], diff_lambda_q2[l][None, :],
                  diff_lambda_k2[l][None, :], diff_subln_g[l][:, None])
        oa = _attn_full("a", qa, ka, vat, extras, lam_init)
        ob = _attn_full("b", qb, kb, vbt)
        oc = _attn_full("c", qc, kc, vct)
        od = _attn_d(qd, kd, vdt, swa_sink[l][None, :], ctx_len)

        x1, h2, gates = _outproj(oa, ob, oc, od, xa, modsel, w_out[l].astype(BF16),
                                 ln1_g[l][None, :], ln1_b[l][None, :], rwt, rb, alpha)
        wg = exp_w_gate[l].astype(BF16)
        wu = exp_w_up[l].astype(BF16)
        wd = exp_w_down[l].astype(BF16).reshape(N_EXPERT_GROUPS, EXPERTS_PER_GROUP * D_EXPERT, d)
        xa = _moe(h2, gates, x1, modsel, wg, wu, wd, ln2_g[l][None, :], ln2_b[l][None, :], alpha)
    return xa[:, ctx_len:, :]
```

```python
import functools
import math

import numpy as np
import jax
import jax.numpy as jnp
from jax import lax
from jax.experimental import pallas as pl
from jax.experimental.pallas import tpu as pltpu

F32 = jnp.float32
BF16 = jnp.bfloat16

D_MODEL = 1024
HEAD_DIM = 64
GROUP_WIDTH = 256
GRID_W = 64
ROPE_THETA = 10000.0
EPS = 1e-6
NEG = -1e30
LOG2E = 1.4426950408889634

A_HEADS = 4
A_QK = 32
B_HEADS = 4
B_KV_HEADS = 2
C_HEADS = 4
C_Q_RANK = 192
C_KV_RANK = 128
C_NOPE = 64
C_ROPE = 32
D_HEADS = 4
D_KV_HEADS = 2
WINDOW = 128
N_EXPERTS = 16
EXPERTS_PER_GROUP = 4
N_EXPERT_GROUPS = 4
D_EXPERT = 256

TQ = 256
TK = 256
TK_LAT = 1024
SWA_T = 128
LANES = 128
ONES_ROWS = 16
VMEM_LIMIT = 56 * 1024 * 1024

_C_QA, _C_QAS, _C_KA, _C_KAS = 0, 256, 512, 768
_C_QB, _C_QBS, _C_KB, _C_KBS = 1024, 1280, 1536, 1664
_C_QD, _C_QDS, _C_KD, _C_KDS = 1792, 2048, 2304, 2432
_C_CQ, _C_CKV, _C_KR, _C_KRS = 2560, 2816, 2944, 3072
_C_TOTAL = 3200
_R_VA, _R_VB, _R_VD, _R_TOTAL = 0, 256, 384, 512


def _nt_dot(a, b):
    return lax.dot_general(a, b, (((1,), (1,)), ((), ())), preferred_element_type=F32)


def _split_bf16(a):
    hi = a.astype(BF16)
    lo = (a - hi.astype(F32)).astype(BF16)
    return hi, lo


def _layer_norm(x):
    mu = jnp.mean(x, axis=-1, keepdims=True)
    xc = x - mu
    var = jnp.mean(xc * xc, axis=-1, keepdims=True)
    return xc * lax.rsqrt(var + EPS)


def _tile_lanes(a, n):
    return jnp.concatenate([a] * n, axis=1)


def _ada_kernel(cc_ref, w_ref, b_ref, o_ref):
    cc = cc_ref[...]
    s = cc * jax.nn.sigmoid(cc)
    s_hi, s_lo = _split_bf16(s)
    w_hi, w_lo = _split_bf16(w_ref[...])
    acc = jnp.dot(s_hi, w_hi, preferred_element_type=F32)
    acc += jnp.dot(s_hi, w_lo, preferred_element_type=F32)
    acc += jnp.dot(s_lo, w_hi, preferred_element_type=F32)
    o_ref[...] = acc + b_ref[...]


def _ada(cc, w_ada, b_ada):
    depth, d, n6 = w_ada.shape
    bn = 1536
    return pl.pallas_call(
        _ada_kernel,
        grid=(depth, n6 // bn),
        in_specs=[
            pl.BlockSpec((8, d), lambda l, j: (0, 0)),
            pl.BlockSpec((None, d, bn), lambda l, j: (l, 0, j)),
            pl.BlockSpec((None, 1, bn), lambda l, j: (l, 0, j)),
        ],
        out_specs=pl.BlockSpec((None, 8, bn), lambda l, j: (l, 0, j)),
        out_shape=jax.ShapeDtypeStruct((depth, 8, n6), F32),
        compiler_params=pltpu.CompilerParams(
            dimension_semantics=("arbitrary", "arbitrary"), vmem_limit_bytes=VMEM_LIMIT),
        name="ada",
    )(cc, w_ada, b_ada.reshape(depth, 1, n6))


def _block_diag_ones(n, blk):
    sh = int(math.log2(blk))
    r = lax.broadcasted_iota(jnp.int32, (n, n), 0) >> sh
    c = lax.broadcasted_iota(jnp.int32, (n, n), 1) >> sh
    return jnp.where(r == c, 1.0, 0.0).astype(BF16)


def _proj_kernel(x_ref, mod_ref, wrm_ref, wt_ref, wuq_ref, wuqs_ref, wukn_ref, wuvt_ref,
                 gqb_ref, gqbs_ref, gkb_ref, gkbs_ref, gcq_ref, gckv_ref,
                 cos32_ref, sin32_ref, cos64_ref, sin64_ref, cosc_ref, sinc_ref,
                 qa_ref, ka_ref, vat_ref, qb_ref, kb_ref, vbt_ref,
                 qc_ref, kc_ref, vct_ref, qd_ref, kd_ref, vdt_ref):
    d = D_MODEL
    xn = _layer_norm(x_ref[...])
    h = (xn * (1.0 + mod_ref[:, d:2 * d]) + mod_ref[:, 0:d]).astype(BF16)

    def cols(lo, width):
        return jnp.dot(h, wrm_ref[:, lo:lo + width], preferred_element_type=F32)

    cos32, sin32 = _tile_lanes(cos32_ref[...], 2), _tile_lanes(sin32_ref[...], 2)
    cos64, sin64 = cos64_ref[...], sin64_ref[...]
    cos64w, sin64w = _tile_lanes(cos64, 2), _tile_lanes(sin64, 2)

    sa = (A_QK ** -0.5) * LOG2E
    qa_ref[...] = ((cols(_C_QA, 256) * cos32 + cols(_C_QAS, 256) * sin32) * sa).astype(BF16)
    ka_ref[...] = (cols(_C_KA, 256) * cos32 + cols(_C_KAS, 256) * sin32).astype(BF16)

    sb = (HEAD_DIM ** -0.5) * LOG2E
    q = cols(_C_QB, 256)
    ssq = jnp.dot((q * q).astype(BF16), _block_diag_ones(256, HEAD_DIM), preferred_element_type=F32)
    r = lax.rsqrt(ssq * (1.0 / HEAD_DIM) + EPS)
    qr = (q * gqb_ref[...]) * cos64w + (cols(_C_QBS, 256) * gqbs_ref[...]) * sin64w
    qb_ref[...] = (qr * (r * sb)).astype(BF16)
    k = cols(_C_KB, 128)
    ssq = jnp.dot((k * k).astype(BF16), _block_diag_ones(128, HEAD_DIM), preferred_element_type=F32)
    r = lax.rsqrt(ssq * (1.0 / HEAD_DIM) + EPS)
    kr = (k * gkb_ref[...]) * cos64 + (cols(_C_KBS, 128) * gkbs_ref[...]) * sin64
    kb_ref[...] = (kr * r).astype(BF16)

    qd_ref[...] = ((cols(_C_QD, 256) * cos64w + cols(_C_QDS, 256) * sin64w) * sb).astype(BF16)
    kd_ref[...] = (cols(_C_KD, 128) * cos64 + cols(_C_KDS, 128) * sin64).astype(BF16)

    sc = ((C_NOPE + C_ROPE) ** -0.5) * LOG2E
    cosc, sinc = cosc_ref[...], sinc_ref[...]
    cq = cols(_C_CQ, 256)
    ms = jnp.sum(cq * cq, axis=-1, keepdims=True) * (1.0 / C_Q_RANK)
    cqn = (cq * lax.rsqrt(ms + EPS) * gcq_ref[...]).astype(BF16)
    qn = jnp.dot(cqn, wuq_ref[...], preferred_element_type=F32)
    qs = jnp.dot(cqn, wuqs_ref[...], preferred_element_type=F32)
    qc_ref[...] = ((qn * _tile_lanes(cosc, 4) + qs * _tile_lanes(sinc, 4)) * sc).astype(BF16)
    ckv = cols(_C_CKV, 128)
    ms = jnp.mean(ckv * ckv, axis=-1, keepdims=True)
    ckvn = (ckv * lax.rsqrt(ms + EPS) * gckv_ref[...]).astype(BF16)
    kn = jnp.dot(ckvn, wukn_ref[...], preferred_element_type=F32)
    krr = cols(_C_KR, 128) * cosc + cols(_C_KRS, 128) * sinc
    kc_ref[...] = (kn + _tile_lanes(krr, 4)).astype(BF16)
    vct_ref[...] = _nt_dot(wuvt_ref[...], ckvn).astype(BF16)

    vt = _nt_dot(wt_ref[...], h)
    vat_ref[...] = vt[_R_VA:_R_VA + 256].astype(BF16)
    vbt_ref[...] = vt[_R_VB:_R_VB + 128].astype(BF16)
    vdt_ref[...] = vt[_R_VD:_R_VD + 128].astype(BF16)


def _proj(xa, modsel, lw, tabs):
    b, nt, d = xa.shape
    nc = nt // TQ

    def full(a):
        nd = a.ndim
        return pl.BlockSpec(a.shape, lambda bi, i, _n=nd: (0,) * _n)

    def tab(a):
        return pl.BlockSpec((TQ, LANES), lambda bi, i: (i, 0))

    def rm(w):
        return pl.BlockSpec((None, TQ, w), lambda bi, i: (bi, i, 0))

    def tr(w):
        return pl.BlockSpec((None, None, w, TQ), lambda bi, i: (bi, i, 0, 0))

    weights = [lw["wrm"], lw["wt"], lw["wuq"], lw["wuqs"], lw["wukn"], lw["wuvt"],
               lw["gqb"], lw["gqbs"], lw["gkb"], lw["gkbs"], lw["gcq"], lw["gckv"]]
    tables = [tabs["cos32"], tabs["sin32"], tabs["cos64"], tabs["sin64"], tabs["cosc"], tabs["sinc"]]
    in_specs = ([pl.BlockSpec((None, TQ, d), lambda bi, i: (bi, i, 0)),
                 pl.BlockSpec((None, None, 1, 6 * d), lambda bi, i: (bi, jnp.minimum(i, 1), 0, 0))]
                + [full(w) for w in weights] + [tab(t) for t in tables])
    widths = [("rm", 256), ("rm", 256), ("tr", 256), ("rm", 256), ("rm", 128), ("tr", 128),
              ("rm", 512), ("rm", 512), ("tr", 256), ("rm", 256), ("rm", 128), ("tr", 128)]
    out_specs, out_shape = [], []
    for kind, w in widths:
        if kind == "rm":
            out_specs.append(rm(w))
            out_shape.append(jax.ShapeDtypeStruct((b, nt, w), BF16))
        else:
            out_specs.append(tr(w))
            out_shape.append(jax.ShapeDtypeStruct((b, nc, w, TQ), BF16))
    return pl.pallas_call(
        _proj_kernel,
        grid=(b, nc),
        in_specs=in_specs,
        out_specs=out_specs,
        out_shape=out_shape,
        compiler_params=pltpu.CompilerParams(
            dimension_semantics=("arbitrary", "arbitrary"), vmem_limit_bytes=VMEM_LIMIT),
        name="proj",
    )(xa, modsel, *weights, *tables)


def _flash_loop(qi, nc, k_ref, vt_ref, qz_ref, m_ref, alpha_ref, acc_ref, s0_ref, s1_ref,k_lanes, pv_cfg):
    per = TK_LAT // TK
    n_lat = (nc - 1) // per
    same_lanes = all(kl == k_lanes[0] for kl in k_lanes)

    def scores(kc):
        if same_lanes:
            return jnp.dot(kc[:, k_lanes[0][0]:k_lanes[0][1]], qz_ref[...], preferred_element_type=F32)
        return jnp.concatenate(
            [jnp.dot(kc[:, lo:hi], qz_ref[:, TQ * j:TQ * (j + 1)], preferred_element_type=F32)
             for j, (lo, hi) in enumerate(k_lanes)], axis=1)

    def accumulate(vc, p_of, alpha):
        ones = jnp.ones((ONES_ROWS, vc.shape[1]), BF16)
        for vl, vh, j0, nj in pv_cfg:
            cols = slice(TQ * j0, TQ * (j0 + nj))
            va = jnp.concatenate([vc[vl:vh, :], ones], axis=0)
            pv = jnp.dot(va, p_of(cols), preferred_element_type=F32)
            acc_ref[:, cols] = pv if alpha is None else alpha[:, cols] * acc_ref[:, cols] + pv

    def lat_keys(c):
        return k_ref[pl.ds(pl.multiple_of(TK + c * TK_LAT, TK), TK_LAT), :]

    def lat_values(c):
        return jnp.concatenate([vt_ref[1 + c * per + i] for i in range(per)], axis=1)

    s = scores(k_ref[0:TK, :])
    m_new = jnp.max(s, axis=0, keepdims=True)
    p = jnp.exp2(s - m_new).astype(BF16)
    m_ref[...] = m_new
    accumulate(vt_ref[0], lambda cols: p[:, cols], None)

    def stage_scores(c, s_ref):
        s = scores(lat_keys(c))
        s_ref[...] = s
        m_old = m_ref[...]
        m_new = jnp.maximum(m_old, jnp.max(s, axis=0, keepdims=True))
        alpha_ref[...] = jnp.exp2(m_old - m_new)
        m_ref[...] = m_new

    def stage_values(c, s_ref, m_cur, alpha):
        p = jnp.exp2(s_ref[...] - m_cur).astype(BF16)
        accumulate(lat_values(c), lambda cols: p[:, cols], alpha)

    @pl.when(qi > 0)
    def _():
        stage_scores(0, s0_ref)

        def body(i, carry):
            c = 2 * i
            m_cur, alpha = m_ref[...], alpha_ref[...]
            stage_scores(c + 1, s1_ref)
            stage_values(c, s0_ref, m_cur, alpha)
            m_cur, alpha = m_ref[...], alpha_ref[...]
            stage_scores(c + 2, s0_ref)
            stage_values(c + 1, s1_ref, m_cur, alpha)
            return carry

        lax.fori_loop(0, n_lat // 2 - 1, body, 0)
        m_cur, alpha = m_ref[...], alpha_ref[...]
        stage_scores(n_lat - 1, s1_ref)
        stage_values(n_lat - 2, s0_ref, m_cur, alpha)
        stage_values(n_lat - 1, s1_ref, m_ref[...], alpha_ref[...])


def _normalized(acc_ref, j, dv):
    a = acc_ref[:, TQ * j:TQ * (j + 1)]
    return a[0:dv] / a[dv:dv + 1]


def _attn_a_kernel(q_ref, k_ref, vt_ref, lq1_ref, lk1_ref, lq2_ref, lk2_ref, g_ref, o_ref,
                   qz_ref, m_ref, alpha_ref, acc_ref, s0_ref, s1_ref,*, nc, lam_init):
    qi = pl.program_id(1)
    qt = q_ref[...].astype(F32).T
    row = lax.broadcasted_iota(jnp.int32, qt.shape, 0)
    for j in range(2 * A_HEADS):
        keep = (row >= A_QK * j) & (row < A_QK * (j + 1))
        qz_ref[:, TQ * j:TQ * (j + 1)] = jnp.where(keep, qt, 0.0).astype(BF16)
    k_lanes = [(0, 256)] * (2 * A_HEADS)
    pv_cfg = [(HEAD_DIM * hd, HEAD_DIM * (hd + 1), 2 * hd, 2) for hd in range(A_HEADS)]
    _flash_loop(qi, nc, k_ref, vt_ref, qz_ref, m_ref, alpha_ref, acc_ref, s0_ref, s1_ref,k_lanes, pv_cfg)

    lam = (jnp.exp(jnp.sum(lq1_ref[...] * lk1_ref[...], axis=-1, keepdims=True))
           - jnp.exp(jnp.sum(lq2_ref[...] * lk2_ref[...], axis=-1, keepdims=True)) + lam_init)
    outs = []
    for hd in range(A_HEADS):
        o = _normalized(acc_ref, 2 * hd, HEAD_DIM) - lam * _normalized(acc_ref, 2 * hd + 1, HEAD_DIM)
        ms = jnp.mean(o * o, axis=0, keepdims=True)
        outs.append(o * lax.rsqrt(ms + EPS) * g_ref[...] * (1.0 - lam_init))
    o_ref[...] = jnp.concatenate(outs, axis=0).T.astype(BF16)


def _attn_b_kernel(q_ref, k_ref, vt_ref, o_ref, qz_ref, m_ref, alpha_ref, acc_ref, s0_ref, s1_ref,*, nc):
    qi = pl.program_id(1)
    qt = q_ref[...].astype(F32).T
    zeros = jnp.zeros((HEAD_DIM, TQ), F32)
    rep = B_HEADS // B_KV_HEADS
    for hd in range(B_HEADS):
        parts = [zeros] * B_KV_HEADS
        parts[hd // rep] = qt[HEAD_DIM * hd:HEAD_DIM * (hd + 1)]
        qz_ref[:, TQ * hd:TQ * (hd + 1)] = jnp.concatenate(parts, axis=0).astype(BF16)
    k_lanes = [(0, 128)] * B_HEADS
    pv_cfg = [(HEAD_DIM * g, HEAD_DIM * (g + 1), rep * g, rep) for g in range(B_KV_HEADS)]
    _flash_loop(qi, nc, k_ref, vt_ref, qz_ref, m_ref, alpha_ref, acc_ref, s0_ref, s1_ref,k_lanes, pv_cfg)
    outs = [_normalized(acc_ref, hd, HEAD_DIM) for hd in range(B_HEADS)]
    o_ref[...] = jnp.concatenate(outs, axis=0).T.astype(BF16)


def _attn_c_kernel(q_ref, k_ref, vt_ref, o_ref, qz_ref, m_ref, alpha_ref, acc_ref, s0_ref, s1_ref,*, nc):
    qi = pl.program_id(1)
    qt = q_ref[...].astype(F32).T
    k_lanes, pv_cfg = [], []
    for hd in range(C_HEADS):
        qz_ref[:, TQ * hd:TQ * (hd + 1)] = qt[LANES * hd:LANES * (hd + 1)].astype(BF16)
        k_lanes.append((LANES * hd, LANES * (hd + 1)))
        pv_cfg.append((HEAD_DIM * hd, HEAD_DIM * (hd + 1), hd, 1))
    _flash_loop(qi, nc, k_ref, vt_ref, qz_ref, m_ref, alpha_ref, acc_ref, s0_ref, s1_ref,k_lanes, pv_cfg)
    outs = [_normalized(acc_ref, hd, HEAD_DIM) for hd in range(C_HEADS)]
    o_ref[...] = jnp.concatenate(outs, axis=0).T.astype(BF16)


def _attn_full(kind, q, k, vt, extras=(), lam_init=0.0):
    b, nt, wq = q.shape
    wk = k.shape[-1]
    nc, wv = vt.shape[1], vt.shape[2]
    assert (nt - TK) % TK_LAT == 0
    if kind == "a":
        body, n_sm, dk = functools.partial(_attn_a_kernel, nc=nc, lam_init=lam_init), 2 * A_HEADS, 256
    elif kind == "b":
        body, n_sm, dk = functools.partial(_attn_b_kernel, nc=nc), B_HEADS, 128
    else:
        body, n_sm, dk = functools.partial(_attn_c_kernel, nc=nc), C_HEADS, 128
    resident = pl.Buffered(1)
    in_specs = [
        pl.BlockSpec((None, TQ, wq), lambda bi, i: (bi, i, 0)),
        pl.BlockSpec((None, nt, wk), lambda bi, i: (bi, 0, 0), pipeline_mode=resident),
        pl.BlockSpec((None, nc, wv, TK), lambda bi, i: (bi, 0, 0, 0), pipeline_mode=resident),
    ] + [pl.BlockSpec(e.shape, lambda bi, i: (0, 0)) for e in extras]
    return pl.pallas_call(
        body,
        grid=(b, nc),
        in_specs=in_specs,
        out_specs=pl.BlockSpec((None, TQ, GROUP_WIDTH), lambda bi, i: (bi, i, 0)),
        out_shape=jax.ShapeDtypeStruct((b, nt, GROUP_WIDTH), BF16),
        scratch_shapes=[
            pltpu.VMEM((dk, n_sm * TQ), BF16),
            pltpu.VMEM((1, n_sm * TQ), F32),
            pltpu.VMEM((1, n_sm * TQ), F32),
            pltpu.VMEM((HEAD_DIM + ONES_ROWS, n_sm * TQ), F32),
            pltpu.VMEM((TK_LAT, n_sm * TQ), F32),
            pltpu.VMEM((TK_LAT, n_sm * TQ), F32),
        ],
        compiler_params=pltpu.CompilerParams(
            dimension_semantics=("arbitrary", "arbitrary"), vmem_limit_bytes=VMEM_LIMIT),
        name="attn_" + kind,
    )(q, k, vt, *extras)


def _attn_d_kernel(q_ref, kc_ref, kp_ref, ko_ref, kn_ref, vc_ref, vp_ref, vo_ref, vn_ref, sink_ref,
                   o_ref, *, nb, ctx_blocks):
    n = pl.program_id(1)
    t = SWA_T
    qt = q_ref[...].astype(F32).T
    zeros = jnp.zeros((HEAD_DIM, t), F32)
    ko_i = lax.broadcasted_iota(jnp.int32, (t, t), 0)
    qo_i = lax.broadcasted_iota(jnp.int32, (t, t), 1)
    band = n >= ctx_blocks
    ok_prev = jnp.logical_and(n >= ctx_blocks + 1, qo_i <= ko_i)
    ok_own = jnp.logical_and(band, ko_i >= 0)
    ok_next = jnp.logical_and(jnp.logical_and(band, n + 1 <= nb - 1), ko_i <= qo_i)
    kc, kp, ko, kn = kc_ref[...], kp_ref[...], ko_ref[...], kn_ref[...]
    vc, vp, vo, vn = vc_ref[...], vp_ref[...], vo_ref[...], vn_ref[...]
    outs = []
    for hd in range(D_HEADS):
        g = hd // (D_HEADS // D_KV_HEADS)
        parts = [zeros] * D_KV_HEADS
        parts[g] = qt[HEAD_DIM * hd:HEAD_DIM * (hd + 1)]
        qz = jnp.concatenate(parts, axis=0).astype(BF16)
        s_c = jnp.dot(kc, qz, preferred_element_type=F32)
        s_p = jnp.where(ok_prev, jnp.dot(kp, qz, preferred_element_type=F32), NEG)
        s_o = jnp.where(ok_own, jnp.dot(ko, qz, preferred_element_type=F32), NEG)
        s_n = jnp.where(ok_next, jnp.dot(kn, qz, preferred_element_type=F32), NEG)
        sk = sink_ref[0:1, hd:hd + 1] * LOG2E
        m = jnp.maximum(jnp.max(s_c, axis=0, keepdims=True), jnp.max(s_p, axis=0, keepdims=True))
        m = jnp.maximum(m, jnp.max(s_o, axis=0, keepdims=True))
        m = jnp.maximum(m, jnp.max(s_n, axis=0, keepdims=True))
        m = jnp.maximum(m, sk)
        p_c, p_p = jnp.exp2(s_c - m), jnp.exp2(s_p - m)
        p_o, p_n = jnp.exp2(s_o - m), jnp.exp2(s_n - m)
        l = (jnp.sum(p_c, axis=0, keepdims=True) + jnp.sum(p_p, axis=0, keepdims=True)
             + jnp.sum(p_o, axis=0, keepdims=True) + jnp.sum(p_n, axis=0, keepdims=True)
             + jnp.exp2(sk - m))
        rows = slice(HEAD_DIM * g, HEAD_DIM * (g + 1))
        o = (jnp.dot(vc[rows], p_c.astype(BF16), preferred_element_type=F32)
             + jnp.dot(vp[rows], p_p.astype(BF16), preferred_element_type=F32)
             + jnp.dot(vo[rows], p_o.astype(BF16), preferred_element_type=F32)
             + jnp.dot(vn[rows], p_n.astype(BF16), preferred_element_type=F32))
        outs.append(o / l)
    o_ref[...] = jnp.concatenate(outs, axis=0).T.astype(BF16)


def _attn_d(q, k, vt, sink, ctx_len):
    b, nt, wq = q.shape
    wk = k.shape[-1]
    wv = vt.shape[2]
    t = SWA_T
    nb = nt // t
    per = TQ // t
    ctx_blocks = ctx_len // t
    assert ctx_len == TQ

    def kspec(shift):
        return pl.BlockSpec((None, t, wk), lambda bi, i: (bi, jnp.clip(i + shift, 0, nb - 1), 0))

    def vspec(shift):
        def imap(bi, i):
            j = jnp.clip(i + shift, 0, nb - 1)
            return (bi, j // per, 0, j % per)
        return pl.BlockSpec((None, None, wv, t), imap)

    in_specs = [
        pl.BlockSpec((None, t, wq), lambda bi, i: (bi, i, 0)),
        pl.BlockSpec((None, ctx_len, wk), lambda bi, i: (bi, 0, 0)),
        kspec(-1), kspec(0), kspec(1),
        pl.BlockSpec((None, None, wv, ctx_len), lambda bi, i: (bi, 0, 0, 0)),
        vspec(-1), vspec(0), vspec(1),
        pl.BlockSpec(sink.shape, lambda bi, i: (0, 0)),
    ]
    return pl.pallas_call(
        functools.partial(_attn_d_kernel, nb=nb, ctx_blocks=ctx_blocks),
        grid=(b, nb),
        in_specs=in_specs,
        out_specs=pl.BlockSpec((None, t, GROUP_WIDTH), lambda bi, i: (bi, i, 0)),
        out_shape=jax.ShapeDtypeStruct((b, nt, GROUP_WIDTH), BF16),
        compiler_params=pltpu.CompilerParams(
            dimension_semantics=("arbitrary", "arbitrary"), vmem_limit_bytes=VMEM_LIMIT),
        name="attn_d",
    )(q, k, k, k, k, vt, vt, vt, vt, sink)


def _route(sel_rows, s_rows):
    epg = EXPERTS_PER_GROUP
    gscore = []
    for g in range(N_EXPERT_GROUPS):
        r = sel_rows[epg * g:epg * (g + 1)]
        pair = None
        for i in range(epg):
            for j in range(i + 1, epg):
                v = r[i] + r[j]
                pair = v if pair is None else jnp.maximum(pair, v)
        gscore.append(pair)
    best, best_g = gscore[0], jnp.zeros_like(gscore[0], dtype=jnp.int32)
    for g in range(1, N_EXPERT_GROUPS):
        better = gscore[g] > best
        best_g = jnp.where(better, g, best_g)
        best = jnp.where(better, gscore[g], best)
    w = []
    for e in range(N_EXPERTS):
        g = e // epg
        cnt = jnp.zeros_like(best_g)
        for e2 in range(epg * g, epg * (g + 1)):
            if e2 == e:
                continue
            beats = sel_rows[e2] > sel_rows[e]
            if e2 < e:
                beats = jnp.logical_or(beats, sel_rows[e2] == sel_rows[e])
            cnt = cnt + jnp.where(beats, 1, 0)
        chosen = jnp.logical_and(best_g == g, cnt < 2)
        w.append(jnp.where(chosen, s_rows[e], 0.0))
    tot = w[0]
    for e in range(1, N_EXPERTS):
        tot = tot + w[e]
    return [we / tot for we in w]


def _outproj_kernel(oa_ref, ob_ref, oc_ref, od_ref, x_ref, mod_ref, wout_ref, g_ref, b_ref,
                    rwt_ref, rb_ref, x1_ref, h2_ref, gates_ref, *, alpha):
    d = D_MODEL
    o = jnp.concatenate([oa_ref[...], ob_ref[...], oc_ref[...], od_ref[...]], axis=1)
    y = jnp.dot(o, wout_ref[...], preferred_element_type=F32)
    u = alpha * x_ref[...] + mod_ref[:, 2 * d:3 * d] * y
    x1 = _layer_norm(u) * g_ref[...] + b_ref[...]
    x1_ref[...] = x1
    h2 = _layer_norm(x1) * (1.0 + mod_ref[:, 4 * d:5 * d]) + mod_ref[:, 3 * d:4 * d]
    h2_ref[...] = h2.astype(BF16)

    h_hi, h_lo = _split_bf16(h2)
    w_hi, w_lo = _split_bf16(rwt_ref[...])
    logits = _nt_dot(w_hi, h_hi) + _nt_dot(w_hi, h_lo) + _nt_dot(w_lo, h_hi)
    s = jax.nn.sigmoid(logits)
    sel = s + rb_ref[...]
    s_rows = [s[e:e + 1] for e in range(N_EXPERTS)]
    sel_rows = [sel[e:e + 1] for e in range(N_EXPERTS)]
    gates = _route(sel_rows, s_rows)
    gt = jnp.concatenate(gates + [jnp.zeros((LANES - N_EXPERTS, TQ), F32)], axis=0)
    gates_ref[...] = gt.T


def _outproj(oa, ob, oc, od, xa, modsel, w_out, ln_g, ln_b, rwt, rb, alpha):
    b, nt, d = xa.shape
    nc = nt // TQ

    def rm(w):
        return pl.BlockSpec((None, TQ, w), lambda bi, i: (bi, i, 0))

    def full(a):
        nd = a.ndim
        return pl.BlockSpec(a.shape, lambda bi, i, _n=nd: (0,) * _n)

    in_specs = [rm(GROUP_WIDTH) for _ in range(4)] + [
        rm(d),
        pl.BlockSpec((None, None, 1, 6 * d), lambda bi, i: (bi, jnp.minimum(i, 1), 0, 0)),
        full(w_out), full(ln_g), full(ln_b), full(rwt), full(rb)]
    return pl.pallas_call(
        functools.partial(_outproj_kernel, alpha=alpha),
        grid=(b, nc),
        in_specs=in_specs,
        out_specs=[rm(d), rm(d), rm(LANES)],
        out_shape=[jax.ShapeDtypeStruct((b, nt, d), F32), jax.ShapeDtypeStruct((b, nt, d), BF16),
                   jax.ShapeDtypeStruct((b, nt, LANES), F32)],
        compiler_params=pltpu.CompilerParams(
            dimension_semantics=("arbitrary", "arbitrary"), vmem_limit_bytes=VMEM_LIMIT),
        name="outproj",
    )(oa, ob, oc, od, xa, modsel, w_out, ln_g, ln_b, rwt, rb)


def _moe_kernel(h_ref, gates_ref, x_ref, mod_ref, wg_ref, wu_ref, wd_ref, g_ref, b_ref, o_ref, *, alpha):
    d = D_MODEL
    h = h_ref[...]
    gates = gates_ref[...]
    y = jnp.zeros((TQ, d), F32)
    for g in range(N_EXPERT_GROUPS):
        acts = []
        for el in range(EXPERTS_PER_GROUP):
            e = EXPERTS_PER_GROUP * g + el
            hg = jnp.dot(h, wg_ref[e], preferred_element_type=F32)
            hu = jnp.dot(h, wu_ref[e], preferred_element_type=F32)
            a = hg * jax.nn.sigmoid(hg) * hu * gates[:, e:e + 1]
            acts.append(a.astype(BF16))
        y = y + jnp.dot(jnp.concatenate(acts, axis=1), wd_ref[g], preferred_element_type=F32)
    u = alpha * x_ref[...] + mod_ref[:, 5 * d:6 * d] * y
    o_ref[...] = _layer_norm(u) * g_ref[...] + b_ref[...]


def _moe(h2, gates, x1, modsel, wg, wu, wd, ln_g, ln_b, alpha):
    b, nt, d = x1.shape
    nc = nt // TQ
    resident = pl.Buffered(1)

    def rm(w):
        return pl.BlockSpec((None, TQ, w), lambda bi, i: (bi, i, 0))

    def res(a):
        nd = a.ndim
        return pl.BlockSpec(a.shape, lambda bi, i, _n=nd: (0,) * _n, pipeline_mode=resident)

    in_specs = [rm(d), rm(LANES), rm(d),
                pl.BlockSpec((None, None, 1, 6 * d), lambda bi, i: (bi, jnp.minimum(i, 1), 0, 0)),
                res(wg), res(wu), res(wd), res(ln_g), res(ln_b)]
    return pl.pallas_call(
        functools.partial(_moe_kernel, alpha=alpha),
        grid=(b, nc),
        in_specs=in_specs,
        out_specs=rm(d),
        out_shape=jax.ShapeDtypeStruct((b, nt, d), F32),
        compiler_params=pltpu.CompilerParams(
            dimension_semantics=("arbitrary", "arbitrary"), vmem_limit_bytes=VMEM_LIMIT),
        name="moe",
    )(h2, gates, x1, modsel, wg, wu, wd, ln_g, ln_b)


def _rope_perm(rot_dim):
    n = rot_dim // 4
    j = np.arange(rot_dim)
    return j ^ n, np.where((j // n) % 2 == 0, -1.0, 1.0).astype(np.float32)


def _swapped(w, rot_dim):
    perm, sign = _rope_perm(rot_dim)
    cols = w.shape[-1]
    idx = (np.arange(cols) // rot_dim) * rot_dim + perm[np.arange(cols) % rot_dim]
    sgn = sign[np.arange(cols) % rot_dim]
    return w[..., idx] * sgn


def _rope_tables(n_lat, ctx_len):
    t = jnp.arange(n_lat, dtype=jnp.int32)
    row, col = (t // GRID_W).astype(F32), (t % GRID_W).astype(F32)

    def pattern(rot_dim):
        n = rot_dim // 4
        inv = ROPE_THETA ** (-jnp.arange(n, dtype=F32) / n)
        ar, ac = row[:, None] * inv, col[:, None] * inv
        ang = jnp.concatenate([ar, ar, ac, ac], axis=-1)
        return jnp.cos(ang), jnp.sin(ang)

    def with_ctx(a, fill):
        return jnp.concatenate([jnp.full((ctx_len, a.shape[1]), fill, F32), a], axis=0)

    c32, s32 = pattern(A_QK)
    c64, s64 = pattern(HEAD_DIM)
    ones = jnp.ones((n_lat, C_NOPE), F32)
    zeros = jnp.zeros((n_lat, C_NOPE), F32)
    pad1 = jnp.ones((n_lat, LANES - C_NOPE - C_ROPE), F32)
    pad0 = jnp.zeros((n_lat, LANES - C_NOPE - C_ROPE), F32)
    return {
        "cos32": with_ctx(jnp.tile(c32, (1, LANES // A_QK)), 1.0),
        "sin32": with_ctx(jnp.tile(s32, (1, LANES // A_QK)), 0.0),
        "cos64": with_ctx(jnp.tile(c64, (1, LANES // HEAD_DIM)), 1.0),
        "sin64": with_ctx(jnp.tile(s64, (1, LANES // HEAD_DIM)), 0.0),
        "cosc": with_ctx(jnp.concatenate([ones, c32, pad1], axis=-1), 1.0),
        "sinc": with_ctx(jnp.concatenate([zeros, s32, pad0], axis=-1), 0.0),
    }


def _prep_layer_weights(w_in, gq, gk, gcq, gckv, w_uq, w_ukv):
    d = w_in.shape[0]
    splits = np.cumsum([256, 256, 256, 256, 128, 128, C_Q_RANK, C_KV_RANK, C_ROPE, 256, 128, 128])[:-1]
    (a_q, a_k, a_v, b_q, b_k, b_v, c_q, c_kv, c_kr, d_q, d_k, d_v) = jnp.split(w_in, splits, axis=1)
    z = lambda n: jnp.zeros((d, n), F32)
    kr4 = jnp.concatenate([z(C_NOPE), c_kr, z(LANES - C_NOPE - C_ROPE)], axis=1)
    kr4s = jnp.concatenate([z(C_NOPE), _swapped(c_kr, C_ROPE), z(LANES - C_NOPE - C_ROPE)], axis=1)
    wrm = jnp.concatenate([
        a_q, _swapped(a_q, A_QK), a_k, _swapped(a_k, A_QK),
        b_q, _swapped(b_q, HEAD_DIM), b_k, _swapped(b_k, HEAD_DIM),
        d_q, _swapped(d_q, HEAD_DIM), d_k, _swapped(d_k, HEAD_DIM),
        c_q, z(256 - C_Q_RANK), c_kv, kr4, kr4s], axis=1)
    wt = jnp.concatenate([a_v, b_v, d_v], axis=1).T

    uq = w_uq.reshape(C_Q_RANK, C_HEADS, C_NOPE + C_ROPE)
    uq_n, uq_r = uq[..., :C_NOPE], uq[..., C_NOPE:]
    zq = lambda n: jnp.zeros((C_Q_RANK, C_HEADS, n), F32)
    pad = LANES - C_NOPE - C_ROPE
    wuq = jnp.concatenate([uq_n, uq_r, zq(pad)], axis=-1).reshape(C_Q_RANK, C_HEADS * LANES)
    wuqs = jnp.concatenate([zq(C_NOPE), _swapped(uq_r, C_ROPE), zq(pad)], axis=-1).reshape(C_Q_RANK, C_HEADS * LANES)
    zrows = jnp.zeros((256 - C_Q_RANK, C_HEADS * LANES), F32)
    wuq, wuqs = jnp.concatenate([wuq, zrows], axis=0), jnp.concatenate([wuqs, zrows], axis=0)
    ukv = w_ukv.reshape(C_KV_RANK, C_HEADS, C_NOPE + HEAD_DIM)
    uk_n, u_v = ukv[..., :C_NOPE], ukv[..., C_NOPE:]
    wukn = jnp.concatenate([uk_n, jnp.zeros((C_KV_RANK, C_HEADS, LANES - C_NOPE), F32)], axis=-1)
    wukn = wukn.reshape(C_KV_RANK, C_HEADS * LANES)
    wuvt = u_v.reshape(C_KV_RANK, C_HEADS * HEAD_DIM).T

    perm64, _ = _rope_perm(HEAD_DIM)
    return {
        "wrm": wrm.astype(BF16), "wt": wt.astype(BF16),
        "wuq": wuq.astype(BF16), "wuqs": wuqs.astype(BF16),
        "wukn": wukn.astype(BF16), "wuvt": wuvt.astype(BF16),
        "gqb": jnp.tile(gq, B_HEADS)[None, :], "gqbs": jnp.tile(gq[perm64], B_HEADS)[None, :],
        "gkb": jnp.tile(gk, B_KV_HEADS)[None, :], "gkbs": jnp.tile(gk[perm64], B_KV_HEADS)[None, :],
        "gcq": jnp.concatenate([gcq, jnp.zeros((256 - C_Q_RANK,), F32)])[None, :],
        "gckv": gckv[None, :],
    }


def kernel(x, c, ctx, c_ctx, w_ada, b_ada, w_in, w_out, diff_lambda_q1, diff_lambda_k1, diff_lambda_q2,
           diff_lambda_k2, diff_subln_g, gqa_q_norm_g, gqa_k_norm_g, mla_q_norm_g, mla_kv_norm_g, mla_w_uq,
           mla_w_ukv, swa_sink, ln1_g, ln1_b, ln2_g, ln2_b, router_w, router_bias,
           exp_w_gate, exp_w_up, exp_w_down):
    b, n_lat, d = x.shape
    ctx_len = ctx.shape[1]
    depth = w_ada.shape[0]
    assert d == D_MODEL and ctx_len == TQ and n_lat % TQ == 0 and b + 1 <= 8
    alpha = (2 * depth) ** 0.25

    xa = jnp.concatenate([ctx, x], axis=1)
    tabs = _rope_tables(n_lat, ctx_len)

    cc = jnp.concatenate([c, c_ctx[None, :], jnp.zeros((8 - b - 1, d), F32)], axis=0)
    mods = _ada(cc, w_ada, b_ada)
    rwt = router_w.T
    rb = router_bias[:, None]

    for l in range(depth):
        lat = mods[l, :b]
        cx = jnp.broadcast_to(mods[l, b], lat.shape)
        modsel = jnp.stack([cx, lat], axis=1)[:, :, None, :]
        lw = _prep_layer_weights(w_in[l], gqa_q_norm_g[l], gqa_k_norm_g[l], mla_q_norm_g[l],
                                 mla_kv_norm_g[l], mla_w_uq[l], mla_w_ukv[l])
        qa, ka, vat, qb, kb, vbt, qc, kc, vct, qd, kd, vdt = _proj(xa, modsel, lw, tabs)

        lam_init = 0.8 - 0.6 * math.exp(-0.3 * l)
        extras = (diff_lambda_q1[l][None, :], diff_lambda_k1[l][None, :], diff_lambda_q2[l][None, :],
                  diff_lambda_k2[l][None, :], diff_subln_g[l][:, None])
        oa = _attn_full("a", qa, ka, vat, extras, lam_init)
        ob = _attn_full("b", qb, kb, vbt)
        oc = _attn_full("c", qc, kc, vct)
        od = _attn_d(qd, kd, vdt, swa_sink[l][None, :], ctx_len)

        x1, h2, gates = _outproj(oa, ob, oc, od, xa, modsel, w_out[l].astype(BF16),
                                 ln1_g[l][None, :], ln1_b[l][None, :], rwt, rb, alpha)
        wg = exp_w_gate[l].astype(BF16)
        wu = exp_w_up[l].astype(BF16)
        wd = exp_w_down[l].astype(BF16).reshape(N_EXPERT_GROUPS, EXPERTS_PER_GROUP * D_EXPERT, d)
        xa = _moe(h2, gates, x1, modsel, wg, wu, wd, ln2_g[l][None, :], ln2_b[l][None, :], alpha)
    return xa[:, ctx_len:, :]
```

```python
import functools
import math

import numpy as np
import jax
import jax.numpy as jnp
from jax import lax
from jax.experimental import pallas as pl
from jax.experimental.pallas import tpu as pltpu

F32 = jnp.float32
BF16 = jnp.bfloat16

D_MODEL = 1024
HEAD_DIM = 64
GROUP_WIDTH = 256
GRID_W = 64
ROPE_THETA = 10000.0
EPS = 1e-6
NEG = -1e30
LOG2E = 1.4426950408889634

A_HEADS = 4
A_QK = 32
B_HEADS = 4
B_KV_HEADS = 2
C_HEADS = 4
C_Q_RANK = 192
C_KV_RANK = 128
C_NOPE = 64
C_ROPE = 32
D_HEADS = 4
D_KV_HEADS = 2
WINDOW = 128
N_EXPERTS = 16
EXPERTS_PER_GROUP = 4
N_EXPERT_GROUPS = 4
D_EXPERT = 256

TQ = 256
TK = 256
TK_LAT = 1024
SWA_T = 128
LANES = 128
ONES_ROWS = 16
VMEM_LIMIT = 56 * 1024 * 1024

_C_QA, _C_QAS, _C_KA, _C_KAS = 0, 256, 512, 768
_C_QB, _C_QBS, _C_KB, _C_KBS = 1024, 1280, 1536, 1664
_C_QD, _C_QDS, _C_KD, _C_KDS = 1792, 2048, 2304, 2432
_C_CQ, _C_CKV, _C_KR, _C_KRS = 2560, 2816, 2944, 3072
_C_TOTAL = 3200
_R_VA, _R_VB, _R_VD, _R_TOTAL = 0, 256, 384, 512


def _nt_dot(a, b):
    return lax.dot_general(a, b, (((1,), (1,)), ((), ())), preferred_element_type=F32)


def _split_bf16(a):
    hi = a.astype(BF16)
    lo = (a - hi.astype(F32)).astype(BF16)
    return hi, lo


def _layer_norm(x):
    mu = jnp.mean(x, axis=-1, keepdims=True)
    xc = x - mu
    var = jnp.mean(xc * xc, axis=-1, keepdims=True)
    return xc * lax.rsqrt(var + EPS)


def _tile_lanes(a, n):
    return jnp.concatenate([a] * n, axis=1)


def _ada_kernel(cc_ref, w_ref, b_ref, o_ref):
    cc = cc_ref[...]
    s = cc * jax.nn.sigmoid(cc)
    s_hi, s_lo = _split_bf16(s)
    w_hi, w_lo = _split_bf16(w_ref[...])
    acc = jnp.dot(s_hi, w_hi, preferred_element_type=F32)
    acc += jnp.dot(s_hi, w_lo, preferred_element_type=F32)
    acc += jnp.dot(s_lo, w_hi, preferred_element_type=F32)
    o_ref[...] = acc + b_ref[...]


def _ada(cc, w_ada, b_ada):
    depth, d, n6 = w_ada.shape
    bn = 1536
    return pl.pallas_call(
        _ada_kernel,
        grid=(depth, n6 // bn),
        in_specs=[
            pl.BlockSpec((8, d), lambda l, j: (0, 0)),
            pl.BlockSpec((None, d, bn), lambda l, j: (l, 0, j)),
            pl.BlockSpec((None, 1, bn), lambda l, j: (l, 0, j)),
        ],
        out_specs=pl.BlockSpec((None, 8, bn), lambda l, j: (l, 0, j)),
        out_shape=jax.ShapeDtypeStruct((depth, 8, n6), F32),
        compiler_params=pltpu.CompilerParams(
            dimension_semantics=("arbitrary", "arbitrary"), vmem_limit_bytes=VMEM_LIMIT),
        name="ada",
    )(cc, w_ada, b_ada.reshape(depth, 1, n6))


def _block_diag_ones(n, blk):
    sh = int(math.log2(blk))
    r = lax.broadcasted_iota(jnp.int32, (n, n), 0) >> sh
    c = lax.broadcasted_iota(jnp.int32, (n, n), 1) >> sh
    return jnp.where(r == c, 1.0, 0.0).astype(BF16)


def _proj_kernel(x_ref, mod_ref, wrm_ref, wt_ref, wuq_ref, wuqs_ref, wukn_ref, wuvt_ref,
                 gqb_ref, gqbs_ref, gkb_ref, gkbs_ref, gcq_ref, gckv_ref,
                 cos32_ref, sin32_ref, cos64_ref, sin64_ref, cosc_ref, sinc_ref,
                 qa_ref, ka_ref, vat_ref, qb_ref, kb_ref, vbt_ref,
                 qc_ref, kc_ref, vct_ref, qd_ref, kd_ref, vdt_ref):
    d = D_MODEL
    xn = _layer_norm(x_ref[...])
    h = (xn * (1.0 + mod_ref[:, d:2 * d]) + mod_ref[:, 0:d]).astype(BF16)

    def cols(lo, width):
        return jnp.dot(h, wrm_ref[:, lo:lo + width], preferred_element_type=F32)

    cos32, sin32 = _tile_lanes(cos32_ref[...], 2), _tile_lanes(sin32_ref[...], 2)
    cos64, sin64 = cos64_ref[...], sin64_ref[...]
    cos64w, sin64w = _tile_lanes(cos64, 2), _tile_lanes(sin64, 2)

    sa = (A_QK ** -0.5) * LOG2E
    qa_ref[...] = ((cols(_C_QA, 256) * cos32 + cols(_C_QAS, 256) * sin32) * sa).astype(BF16)
    ka_ref[...] = (cols(_C_KA, 256) * cos32 + cols(_C_KAS, 256) * sin32).astype(BF16)

    sb = (HEAD_DIM ** -0.5) * LOG2E
    q = cols(_C_QB, 256)
    ssq = jnp.dot((q * q).astype(BF16), _block_diag_ones(256, HEAD_DIM), preferred_element_type=F32)
    r = lax.rsqrt(ssq * (1.0 / HEAD_DIM) + EPS)
    qr = (q * gqb_ref[...]) * cos64w + (cols(_C_QBS, 256) * gqbs_ref[...]) * sin64w
    qb_ref[...] = (qr * (r * sb)).astype(BF16)
    k = cols(_C_KB, 128)
    ssq = jnp.dot((k * k).astype(BF16), _block_diag_ones(128, HEAD_DIM), preferred_element_type=F32)
    r = lax.rsqrt(ssq * (1.0 / HEAD_DIM) + EPS)
    kr = (k * gkb_ref[...]) * cos64 + (cols(_C_KBS, 128) * gkbs_ref[...]) * sin64
    kb_ref[...] = (kr * r).astype(BF16)

    qd_ref[...] = ((cols(_C_QD, 256) * cos64w + cols(_C_QDS, 256) * sin64w) * sb).astype(BF16)
    kd_ref[...] = (cols(_C_KD, 128) * cos64 + cols(_C_KDS, 128) * sin64).astype(BF16)

    sc = ((C_NOPE + C_ROPE) ** -0.5) * LOG2E
    cosc, sinc = cosc_ref[...], sinc_ref[...]
    cq = cols(_C_CQ, 256)
    ms = jnp.sum(cq * cq, axis=-1, keepdims=True) * (1.0 / C_Q_RANK)
    cqn = (cq * lax.rsqrt(ms + EPS) * gcq_ref[...]).astype(BF16)
    qn = jnp.dot(cqn, wuq_ref[...], preferred_element_type=F32)
    qs = jnp.dot(cqn, wuqs_ref[...], preferred_element_type=F32)
    qc_ref[...] = ((qn * _tile_lanes(cosc, 4) + qs * _tile_lanes(sinc, 4)) * sc).astype(BF16)
    ckv = cols(_C_CKV, 128)
    ms = jnp.mean(ckv * ckv, axis=-1, keepdims=True)
    ckvn = (ckv * lax.rsqrt(ms + EPS) * gckv_ref[...]).astype(BF16)
    kn = jnp.dot(ckvn, wukn_ref[...], preferred_element_type=F32)
    krr = cols(_C_KR, 128) * cosc + cols(_C_KRS, 128) * sinc
    kc_ref[...] = (kn + _tile_lanes(krr, 4)).astype(BF16)
    vct_ref[...] = _nt_dot(wuvt_ref[...], ckvn).astype(BF16)

    vt = _nt_dot(wt_ref[...], h)
    vat_ref[...] = vt[_R_VA:_R_VA + 256].astype(BF16)
    vbt_ref[...] = vt[_R_VB:_R_VB + 128].astype(BF16)
    vdt_ref[...] = vt[_R_VD:_R_VD + 128].astype(BF16)


def _proj(xa, modsel, lw, tabs):
    b, nt, d = xa.shape
    nc = nt // TQ

    def full(a):
        nd = a.ndim
        return pl.BlockSpec(a.shape, lambda bi, i, _n=nd: (0,) * _n)

    def tab(a):
        return pl.BlockSpec((TQ, LANES), lambda bi, i: (i, 0))

    def rm(w):
        return pl.BlockSpec((None, TQ, w), lambda bi, i: (bi, i, 0))

    def tr(w):
        return pl.BlockSpec((None, None, w, TQ), lambda bi, i: (bi, i, 0, 0))

    weights = [lw["wrm"], lw["wt"], lw["wuq"], lw["wuqs"], lw["wukn"], lw["wuvt"],
               lw["gqb"], lw["gqbs"], lw["gkb"], lw["gkbs"], lw["gcq"], lw["gckv"]]
    tables = [tabs["cos32"], tabs["sin32"], tabs["cos64"], tabs["sin64"], tabs["cosc"], tabs["sinc"]]
    in_specs = ([pl.BlockSpec((None, TQ, d), lambda bi, i: (bi, i, 0)),
                 pl.BlockSpec((None, None, 1, 6 * d), lambda bi, i: (bi, jnp.minimum(i, 1), 0, 0))]
                + [full(w) for w in weights] + [tab(t) for t in tables])
    widths = [("rm", 256), ("rm", 256), ("tr", 256), ("rm", 256), ("rm", 128), ("tr", 128),
              ("rm", 512), ("rm", 512), ("tr", 256), ("rm", 256), ("rm", 128), ("tr", 128)]
    out_specs, out_shape = [], []
    for kind, w in widths:
        if kind == "rm":
            out_specs.append(rm(w))
            out_shape.append(jax.ShapeDtypeStruct((b, nt, w), BF16))
        else:
            out_specs.append(tr(w))
            out_shape.append(jax.ShapeDtypeStruct((b, nc, w, TQ), BF16))
    return pl.pallas_call(
        _proj_kernel,
        grid=(b, nc),
        in_specs=in_specs,
        out_specs=out_specs,
        out_shape=out_shape,
        compiler_params=pltpu.CompilerParams(
            dimension_semantics=("arbitrary", "arbitrary"), vmem_limit_bytes=VMEM_LIMIT),
        name="proj",
    )(xa, modsel, *weights, *tables)


def _flash_loop(qi, nc, k_ref, vt_ref, qz_ref, m_ref, alpha_ref, acc_ref, s0_ref, s1_ref,k_lanes, pv_cfg):
    per = TK_LAT // TK
    n_lat = (nc - 1) // per
    assert n_lat % 2 == 0 and n_lat >= 2
    same_lanes = all(kl == k_lanes[0] for kl in k_lanes)

    def scores(kc):
        if same_lanes:
            return jnp.dot(kc[:, k_lanes[0][0]:k_lanes[0][1]], qz_ref[...], preferred_element_type=F32)
        return jnp.concatenate(
            [jnp.dot(kc[:, lo:hi], qz_ref[:, TQ * j:TQ * (j + 1)], preferred_element_type=F32)
             for j, (lo, hi) in enumerate(k_lanes)], axis=1)

    ctx_chunk = (1, 0)

    def lat_chunk(c):
        return (per, 1 + c * per)

    def stats_update(mx):
        m_old = m_ref[...]
        m_new = jnp.maximum(m_old, mx)
        alpha_ref[...] = jnp.exp2(m_old - m_new)
        m_ref[...] = m_new

    def pipelined(cur, cur_ref, nxt, nxt_ref):
        m_cur, alpha = m_ref[...], alpha_ref[...]
        ones = jnp.ones((ONES_ROWS, TK), BF16)
        mx, pv = None, [None] * len(pv_cfg)
        n_cur = cur[0] if cur is not None else 0
        n_nxt = nxt[0] if nxt is not None else 0
        for r in range(max(n_cur, n_nxt)):
            rows = slice(TK * r, TK * (r + 1))
            if r < n_nxt:
                kc = k_ref[pl.ds(pl.multiple_of((nxt[1] + r) * TK, TK), TK), :]
                s = scores(kc)
                nxt_ref[rows, :] = s
                smx = jnp.max(s, axis=0, keepdims=True)
                mx = smx if mx is None else jnp.maximum(mx, smx)
            if r < n_cur:
                p = jnp.exp2(cur_ref[rows, :] - m_cur).astype(BF16)
                vc = vt_ref[cur[1] + r]
                for g, (vl, vh, j0, nj) in enumerate(pv_cfg):
                    va = jnp.concatenate([vc[vl:vh, :], ones], axis=0)
                    d = jnp.dot(va, p[:, TQ * j0:TQ * (j0 + nj)], preferred_element_type=F32)
                    pv[g] = d if pv[g] is None else pv[g] + d
        if cur is not None:
            for g, (vl, vh, j0, nj) in enumerate(pv_cfg):
                cols = slice(TQ * j0, TQ * (j0 + nj))
                acc_ref[:, cols] = alpha[:, cols] * acc_ref[:, cols] + pv[g]
        if nxt is not None:
            stats_update(mx)

    m_ref[...] = jnp.full(m_ref.shape, NEG, F32)
    acc_ref[...] = jnp.zeros(acc_ref.shape, F32)

    @pl.when(qi == 0)
    def _():
        pipelined(None, None, ctx_chunk, s0_ref)
        pipelined(ctx_chunk, s0_ref, None, None)

    @pl.when(qi > 0)
    def _():
        pipelined(None, None, lat_chunk(0), s0_ref)

        def body(i, carry):
            c = 2 * i
            pipelined(lat_chunk(c), s0_ref, lat_chunk(c + 1), s1_ref)
            pipelined(lat_chunk(c + 1), s1_ref, lat_chunk(c + 2), s0_ref)
            return carry

        lax.fori_loop(0, n_lat // 2 - 1, body, 0)
        pipelined(lat_chunk(n_lat - 2), s0_ref, lat_chunk(n_lat - 1), s1_ref)
        pipelined(lat_chunk(n_lat - 1), s1_ref, ctx_chunk, s0_ref)
        pipelined(ctx_chunk, s0_ref, None, None)


def _normalized(acc_ref, j, dv):
    a = acc_ref[:, TQ * j:TQ * (j + 1)]
    return a[0:dv] / a[dv:dv + 1]


def _attn_a_kernel(q_ref, k_ref, vt_ref, lq1_ref, lk1_ref, lq2_ref, lk2_ref, g_ref, o_ref,
                   qz_ref, m_ref, alpha_ref, acc_ref, s0_ref, s1_ref,*, nc, lam_init):
    qi = pl.program_id(1)
    qt = q_ref[...].astype(F32).T
    row = lax.broadcasted_iota(jnp.int32, qt.shape, 0)
    for j in range(2 * A_HEADS):
        keep = (row >= A_QK * j) & (row < A_QK * (j + 1))
        qz_ref[:, TQ * j:TQ * (j + 1)] = jnp.where(keep, qt, 0.0).astype(BF16)
    k_lanes = [(0, 256)] * (2 * A_HEADS)
    pv_cfg = [(HEAD_DIM * hd, HEAD_DIM * (hd + 1), 2 * hd, 2) for hd in range(A_HEADS)]
    _flash_loop(qi, nc, k_ref, vt_ref, qz_ref, m_ref, alpha_ref, acc_ref, s0_ref, s1_ref,k_lanes, pv_cfg)

    lam = (jnp.exp(jnp.sum(lq1_ref[...] * lk1_ref[...], axis=-1, keepdims=True))
           - jnp.exp(jnp.sum(lq2_ref[...] * lk2_ref[...], axis=-1, keepdims=True)) + lam_init)
    outs = []
    for hd in range(A_HEADS):
        o = _normalized(acc_ref, 2 * hd, HEAD_DIM) - lam * _normalized(acc_ref, 2 * hd + 1, HEAD_DIM)
        ms = jnp.mean(o * o, axis=0, keepdims=True)
        outs.append(o * lax.rsqrt(ms + EPS) * g_ref[...] * (1.0 - lam_init))
    o_ref[...] = jnp.concatenate(outs, axis=0).T.astype(BF16)


def _attn_b_kernel(q_ref, k_ref, vt_ref, o_ref, qz_ref, m_ref, alpha_ref, acc_ref, s0_ref, s1_ref,*, nc):
    qi = pl.program_id(1)
    qt = q_ref[...].astype(F32).T
    zeros = jnp.zeros((HEAD_DIM, TQ), F32)
    rep = B_HEADS // B_KV_HEADS
    for hd in range(B_HEADS):
        parts = [zeros] * B_KV_HEADS
        parts[hd // rep] = qt[HEAD_DIM * hd:HEAD_DIM * (hd + 1)]
        qz_ref[:, TQ * hd:TQ * (hd + 1)] = jnp.concatenate(parts, axis=0).astype(BF16)
    k_lanes = [(0, 128)] * B_HEADS
    pv_cfg = [(HEAD_DIM * g, HEAD_DIM * (g + 1), rep * g, rep) for g in range(B_KV_HEADS)]
    _flash_loop(qi, nc, k_ref, vt_ref, qz_ref, m_ref, alpha_ref, acc_ref, s0_ref, s1_ref,k_lanes, pv_cfg)
    outs = [_normalized(acc_ref, hd, HEAD_DIM) for hd in range(B_HEADS)]
    o_ref[...] = jnp.concatenate(outs, axis=0).T.astype(BF16)


def _attn_c_kernel(q_ref, k_ref, vt_ref, o_ref, qz_ref, m_ref, alpha_ref, acc_ref, s0_ref, s1_ref,*, nc):
    qi = pl.program_id(1)
    qt = q_ref[...].astype(F32).T
    k_lanes, pv_cfg = [], []
    for hd in range(C_HEADS):
        qz_ref[:, TQ * hd:TQ * (hd + 1)] = qt[LANES * hd:LANES * (hd + 1)].astype(BF16)
        k_lanes.append((LANES * hd, LANES * (hd + 1)))
        pv_cfg.append((HEAD_DIM * hd, HEAD_DIM * (hd + 1), hd, 1))
    _flash_loop(qi, nc, k_ref, vt_ref, qz_ref, m_ref, alpha_ref, acc_ref, s0_ref, s1_ref,k_lanes, pv_cfg)
    outs = [_normalized(acc_ref, hd, HEAD_DIM) for hd in range(C_HEADS)]
    o_ref[...] = jnp.concatenate(outs, axis=0).T.astype(BF16)


def _attn_full(kind, q, k, vt, extras=(), lam_init=0.0):
    b, nt, wq = q.shape
    wk = k.shape[-1]
    nc, wv = vt.shape[1], vt.shape[2]
    assert (nt - TK) % TK_LAT == 0
    if kind == "a":
        body, n_sm, dk = functools.partial(_attn_a_kernel, nc=nc, lam_init=lam_init), 2 * A_HEADS, 256
    elif kind == "b":
        body, n_sm, dk = functools.partial(_attn_b_kernel, nc=nc), B_HEADS, 128
    else:
        body, n_sm, dk = functools.partial(_attn_c_kernel, nc=nc), C_HEADS, 128
    resident = pl.Buffered(1)
    in_specs = [
        pl.BlockSpec((None, TQ, wq), lambda bi, i: (bi, i, 0)),
        pl.BlockSpec((None, nt, wk), lambda bi, i: (bi, 0, 0), pipeline_mode=resident),
        pl.BlockSpec((None, nc, wv, TK), lambda bi, i: (bi, 0, 0, 0), pipeline_mode=resident),
    ] + [pl.BlockSpec(e.shape, lambda bi, i: (0, 0)) for e in extras]
    return pl.pallas_call(
        body,
        grid=(b, nc),
        in_specs=in_specs,
        out_specs=pl.BlockSpec((None, TQ, GROUP_WIDTH), lambda bi, i: (bi, i, 0)),
        out_shape=jax.ShapeDtypeStruct((b, nt, GROUP_WIDTH), BF16),
        scratch_shapes=[
            pltpu.VMEM((dk, n_sm * TQ), BF16),
            pltpu.VMEM((1, n_sm * TQ), F32),
            pltpu.VMEM((1, n_sm * TQ), F32),
            pltpu.VMEM((HEAD_DIM + ONES_ROWS, n_sm * TQ), F32),
            pltpu.VMEM((TK_LAT, n_sm * TQ), F32),
            pltpu.VMEM((TK_LAT, n_sm * TQ), F32),
        ],
        compiler_params=pltpu.CompilerParams(
            dimension_semantics=("arbitrary", "arbitrary"), vmem_limit_bytes=VMEM_LIMIT),
        name="attn_" + kind,
    )(q, k, vt, *extras)


def _attn_d_kernel(q_ref, kc_ref, kp_ref, ko_ref, kn_ref, vc_ref, vp_ref, vo_ref, vn_ref, sink_ref,
                   o_ref, *, nb, ctx_blocks):
    n = pl.program_id(1)
    t = SWA_T
    qt = q_ref[...].astype(F32).T
    zeros = jnp.zeros((HEAD_DIM, t), F32)
    ko_i = lax.broadcasted_iota(jnp.int32, (t, t), 0)
    qo_i = lax.broadcasted_iota(jnp.int32, (t, t), 1)
    band = n >= ctx_blocks
    ok_prev = jnp.logical_and(n >= ctx_blocks + 1, qo_i <= ko_i)
    ok_own = jnp.logical_and(band, ko_i >= 0)
    ok_next = jnp.logical_and(jnp.logical_and(band, n + 1 <= nb - 1), ko_i <= qo_i)
    kc, kp, ko, kn = kc_ref[...], kp_ref[...], ko_ref[...], kn_ref[...]
    vc, vp, vo, vn = vc_ref[...], vp_ref[...], vo_ref[...], vn_ref[...]
    outs = []
    for hd in range(D_HEADS):
        g = hd // (D_HEADS // D_KV_HEADS)
        parts = [zeros] * D_KV_HEADS
        parts[g] = qt[HEAD_DIM * hd:HEAD_DIM * (hd + 1)]
        qz = jnp.concatenate(parts, axis=0).astype(BF16)
        s_c = jnp.dot(kc, qz, preferred_element_type=F32)
        s_p = jnp.where(ok_prev, jnp.dot(kp, qz, preferred_element_type=F32), NEG)
        s_o = jnp.where(ok_own, jnp.dot(ko, qz, preferred_element_type=F32), NEG)
        s_n = jnp.where(ok_next, jnp.dot(kn, qz, preferred_element_type=F32), NEG)
        sk = sink_ref[0:1, hd:hd + 1] * LOG2E
        m = jnp.maximum(jnp.max(s_c, axis=0, keepdims=True), jnp.max(s_p, axis=0, keepdims=True))
        m = jnp.maximum(m, jnp.max(s_o, axis=0, keepdims=True))
        m = jnp.maximum(m, jnp.max(s_n, axis=0, keepdims=True))
        m = jnp.maximum(m, sk)
        p_c, p_p = jnp.exp2(s_c - m), jnp.exp2(s_p - m)
        p_o, p_n = jnp.exp2(s_o - m), jnp.exp2(s_n - m)
        l = (jnp.sum(p_c, axis=0, keepdims=True) + jnp.sum(p_p, axis=0, keepdims=True)
             + jnp.sum(p_o, axis=0, keepdims=True) + jnp.sum(p_n, axis=0, keepdims=True)
             + jnp.exp2(sk - m))
        rows = slice(HEAD_DIM * g, HEAD_DIM * (g + 1))
        o = (jnp.dot(vc[rows], p_c.astype(BF16), preferred_element_type=F32)
             + jnp.dot(vp[rows], p_p.astype(BF16), preferred_element_type=F32)
             + jnp.dot(vo[rows], p_o.astype(BF16), preferred_element_type=F32)
             + jnp.dot(vn[rows], p_n.astype(BF16), preferred_element_type=F32))
        outs.append(o / l)
    o_ref[...] = jnp.concatenate(outs, axis=0).T.astype(BF16)


def _attn_d(q, k, vt, sink, ctx_len):
    b, nt, wq = q.shape
    wk = k.shape[-1]
    wv = vt.shape[2]
    t = SWA_T
    nb = nt // t
    per = TQ // t
    ctx_blocks = ctx_len // t
    assert ctx_len == TQ

    def kspec(shift):
        return pl.BlockSpec((None, t, wk), lambda bi, i: (bi, jnp.clip(i + shift, 0, nb - 1), 0))

    def vspec(shift):
        def imap(bi, i):
            j = jnp.clip(i + shift, 0, nb - 1)
            return (bi, j // per, 0, j % per)
        return pl.BlockSpec((None, None, wv, t), imap)

    in_specs = [
        pl.BlockSpec((None, t, wq), lambda bi, i: (bi, i, 0)),
        pl.BlockSpec((None, ctx_len, wk), lambda bi, i: (bi, 0, 0)),
        kspec(-1), kspec(0), kspec(1),
        pl.BlockSpec((None, None, wv, ctx_len), lambda bi, i: (bi, 0, 0, 0)),
        vspec(-1), vspec(0), vspec(1),
        pl.BlockSpec(sink.shape, lambda bi, i: (0, 0)),
    ]
    return pl.pallas_call(
        functools.partial(_attn_d_kernel, nb=nb, ctx_blocks=ctx_blocks),
        grid=(b, nb),
        in_specs=in_specs,
        out_specs=pl.BlockSpec((None, t, GROUP_WIDTH), lambda bi, i: (bi, i, 0)),
        out_shape=jax.ShapeDtypeStruct((b, nt, GROUP_WIDTH), BF16),
        compiler_params=pltpu.CompilerParams(
            dimension_semantics=("arbitrary", "arbitrary"), vmem_limit_bytes=VMEM_LIMIT),
        name="attn_d",
    )(q, k, k, k, k, vt, vt, vt, vt, sink)


def _route(sel_rows, s_rows):
    epg = EXPERTS_PER_GROUP
    gscore = []
    for g in range(N_EXPERT_GROUPS):
        r = sel_rows[epg * g:epg * (g + 1)]
        pair = None
        for i in range(epg):
            for j in range(i + 1, epg):
                v = r[i] + r[j]
                pair = v if pair is None else jnp.maximum(pair, v)
        gscore.append(pair)
    best, best_g = gscore[0], jnp.zeros_like(gscore[0], dtype=jnp.int32)
    for g in range(1, N_EXPERT_GROUPS):
        better = gscore[g] > best
        best_g = jnp.where(better, g, best_g)
        best = jnp.where(better, gscore[g], best)
    w = []
    for e in range(N_EXPERTS):
        g = e // epg
        cnt = jnp.zeros_like(best_g)
        for e2 in range(epg * g, epg * (g + 1)):
            if e2 == e:
                continue
            beats = sel_rows[e2] > sel_rows[e]
            if e2 < e:
                beats = jnp.logical_or(beats, sel_rows[e2] == sel_rows[e])
            cnt = cnt + jnp.where(beats, 1, 0)
        chosen = jnp.logical_and(best_g == g, cnt < 2)
        w.append(jnp.where(chosen, s_rows[e], 0.0))
    tot = w[0]
    for e in range(1, N_EXPERTS):
        tot = tot + w[e]
    return [we / tot for we in w]


def _outproj_kernel(oa_ref, ob_ref, oc_ref, od_ref, x_ref, mod_ref, wout_ref, g_ref, b_ref,
                    rwt_ref, rb_ref, x1_ref, h2_ref, gates_ref, *, alpha):
    d = D_MODEL
    o = jnp.concatenate([oa_ref[...], ob_ref[...], oc_ref[...], od_ref[...]], axis=1)
    y = jnp.dot(o, wout_ref[...], preferred_element_type=F32)
    u = alpha * x_ref[...] + mod_ref[:, 2 * d:3 * d] * y
    x1 = _layer_norm(u) * g_ref[...] + b_ref[...]
    x1_ref[...] = x1
    h2 = _layer_norm(x1) * (1.0 + mod_ref[:, 4 * d:5 * d]) + mod_ref[:, 3 * d:4 * d]
    h2_ref[...] = h2.astype(BF16)

    h_hi, h_lo = _split_bf16(h2)
    w_hi, w_lo = _split_bf16(rwt_ref[...])
    logits = _nt_dot(w_hi, h_hi) + _nt_dot(w_hi, h_lo) + _nt_dot(w_lo, h_hi)
    s = jax.nn.sigmoid(logits)
    sel = s + rb_ref[...]
    s_rows = [s[e:e + 1] for e in range(N_EXPERTS)]
    sel_rows = [sel[e:e + 1] for e in range(N_EXPERTS)]
    gates = _route(sel_rows, s_rows)
    gt = jnp.concatenate(gates + [jnp.zeros((LANES - N_EXPERTS, TQ), F32)], axis=0)
    gates_ref[...] = gt.T


def _outproj(oa, ob, oc, od, xa, modsel, w_out, ln_g, ln_b, rwt, rb, alpha):
    b, nt, d = xa.shape
    nc = nt // TQ

    def rm(w):
        return pl.BlockSpec((None, TQ, w), lambda bi, i: (bi, i, 0))

    def full(a):
        nd = a.ndim
        return pl.BlockSpec(a.shape, lambda bi, i, _n=nd: (0,) * _n)

    in_specs = [rm(GROUP_WIDTH) for _ in range(4)] + [
        rm(d),
        pl.BlockSpec((None, None, 1, 6 * d), lambda bi, i: (bi, jnp.minimum(i, 1), 0, 0)),
        full(w_out), full(ln_g), full(ln_b), full(rwt), full(rb)]
    return pl.pallas_call(
        functools.partial(_outproj_kernel, alpha=alpha),
        grid=(b, nc),
        in_specs=in_specs,
        out_specs=[rm(d), rm(d), rm(LANES)],
        out_shape=[jax.ShapeDtypeStruct((b, nt, d), F32), jax.ShapeDtypeStruct((b, nt, d), BF16),
                   jax.ShapeDtypeStruct((b, nt, LANES), F32)],
        compiler_params=pltpu.CompilerParams(
            dimension_semantics=("arbitrary", "arbitrary"), vmem_limit_bytes=VMEM_LIMIT),
        name="outproj",
    )(oa, ob, oc, od, xa, modsel, w_out, ln_g, ln_b, rwt, rb)


def _moe_kernel(h_ref, gates_ref, x_ref, mod_ref, wg_ref, wu_ref, wd_ref, g_ref, b_ref, o_ref, *, alpha):
    d = D_MODEL
    h = h_ref[...]
    gates = gates_ref[...]
    y = jnp.zeros((TQ, d), F32)
    for g in range(N_EXPERT_GROUPS):
        acts = []
        for el in range(EXPERTS_PER_GROUP):
            e = EXPERTS_PER_GROUP * g + el
            hg = jnp.dot(h, wg_ref[e], preferred_element_type=F32)
            hu = jnp.dot(h, wu_ref[e], preferred_element_type=F32)
            a = hg * jax.nn.sigmoid(hg) * hu * gates[:, e:e + 1]
            acts.append(a.astype(BF16))
        y = y + jnp.dot(jnp.concatenate(acts, axis=1), wd_ref[g], preferred_element_type=F32)
    u = alpha * x_ref[...] + mod_ref[:, 5 * d:6 * d] * y
    o_ref[...] = _layer_norm(u) * g_ref[...] + b_ref[...]


def _moe(h2, gates, x1, modsel, wg, wu, wd, ln_g, ln_b, alpha):
    b, nt, d = x1.shape
    nc = nt // TQ
    resident = pl.Buffered(1)

    def rm(w):
        return pl.BlockSpec((None, TQ, w), lambda bi, i: (bi, i, 0))

    def res(a):
        nd = a.ndim
        return pl.BlockSpec(a.shape, lambda bi, i, _n=nd: (0,) * _n, pipeline_mode=resident)

    in_specs = [rm(d), rm(LANES), rm(d),
                pl.BlockSpec((None, None, 1, 6 * d), lambda bi, i: (bi, jnp.minimum(i, 1), 0, 0)),
                res(wg), res(wu), res(wd), res(ln_g), res(ln_b)]
    return pl.pallas_call(
        functools.partial(_moe_kernel, alpha=alpha),
        grid=(b, nc),
        in_specs=in_specs,
        out_specs=rm(d),
        out_shape=jax.ShapeDtypeStruct((b, nt, d), F32),
        compiler_params=pltpu.CompilerParams(
            dimension_semantics=("arbitrary", "arbitrary"), vmem_limit_bytes=VMEM_LIMIT),
        name="moe",
    )(h2, gates, x1, modsel, wg, wu, wd, ln_g, ln_b)


def _rope_perm(rot_dim):
    n = rot_dim // 4
    j = np.arange(rot_dim)
    return j ^ n, np.where((j // n) % 2 == 0, -1.0, 1.0).astype(np.float32)


def _swapped(w, rot_dim):
    perm, sign = _rope_perm(rot_dim)
    cols = w.shape[-1]
    idx = (np.arange(cols) // rot_dim) * rot_dim + perm[np.arange(cols) % rot_dim]
    sgn = sign[np.arange(cols) % rot_dim]
    return w[..., idx] * sgn


def _rope_tables(n_lat, ctx_len):
    t = jnp.arange(n_lat, dtype=jnp.int32)
    row, col = (t // GRID_W).astype(F32), (t % GRID_W).astype(F32)

    def pattern(rot_dim):
        n = rot_dim // 4
        inv = ROPE_THETA ** (-jnp.arange(n, dtype=F32) / n)
        ar, ac = row[:, None] * inv, col[:, None] * inv
        ang = jnp.concatenate([ar, ar, ac, ac], axis=-1)
        return jnp.cos(ang), jnp.sin(ang)

    def with_ctx(a, fill):
        return jnp.concatenate([jnp.full((ctx_len, a.shape[1]), fill, F32), a], axis=0)

    c32, s32 = pattern(A_QK)
    c64, s64 = pattern(HEAD_DIM)
    ones = jnp.ones((n_lat, C_NOPE), F32)
    zeros = jnp.zeros((n_lat, C_NOPE), F32)
    pad1 = jnp.ones((n_lat, LANES - C_NOPE - C_ROPE), F32)
    pad0 = jnp.zeros((n_lat, LANES - C_NOPE - C_ROPE), F32)
    return {
        "cos32": with_ctx(jnp.tile(c32, (1, LANES // A_QK)), 1.0),
        "sin32": with_ctx(jnp.tile(s32, (1, LANES // A_QK)), 0.0),
        "cos64": with_ctx(jnp.tile(c64, (1, LANES // HEAD_DIM)), 1.0),
        "sin64": with_ctx(jnp.tile(s64, (1, LANES // HEAD_DIM)), 0.0),
        "cosc": with_ctx(jnp.concatenate([ones, c32, pad1], axis=-1), 1.0),
        "sinc": with_ctx(jnp.concatenate([zeros, s32, pad0], axis=-1), 0.0),
    }


def _prep_layer_weights(w_in, gq, gk, gcq, gckv, w_uq, w_ukv):
    d = w_in.shape[0]
    splits = np.cumsum([256, 256, 256, 256, 128, 128, C_Q_RANK, C_KV_RANK, C_ROPE, 256, 128, 128])[:-1]
    (a_q, a_k, a_v, b_q, b_k, b_v, c_q, c_kv, c_kr, d_q, d_k, d_v) = jnp.split(w_in, splits, axis=1)
    z = lambda n: jnp.zeros((d, n), F32)
    kr4 = jnp.concatenate([z(C_NOPE), c_kr, z(LANES - C_NOPE - C_ROPE)], axis=1)
    kr4s = jnp.concatenate([z(C_NOPE), _swapped(c_kr, C_ROPE), z(LANES - C_NOPE - C_ROPE)], axis=1)
    wrm = jnp.concatenate([
        a_q, _swapped(a_q, A_QK), a_k, _swapped(a_k, A_QK),
        b_q, _swapped(b_q, HEAD_DIM), b_k, _swapped(b_k, HEAD_DIM),
        d_q, _swapped(d_q, HEAD_DIM), d_k, _swapped(d_k, HEAD_DIM),
        c_q, z(256 - C_Q_RANK), c_kv, kr4, kr4s], axis=1)
    wt = jnp.concatenate([a_v, b_v, d_v], axis=1).T

    uq = w_uq.reshape(C_Q_RANK, C_HEADS, C_NOPE + C_ROPE)
    uq_n, uq_r = uq[..., :C_NOPE], uq[..., C_NOPE:]
    zq = lambda n: jnp.zeros((C_Q_RANK, C_HEADS, n), F32)
    pad = LANES - C_NOPE - C_ROPE
    wuq = jnp.concatenate([uq_n, uq_r, zq(pad)], axis=-1).reshape(C_Q_RANK, C_HEADS * LANES)
    wuqs = jnp.concatenate([zq(C_NOPE), _swapped(uq_r, C_ROPE), zq(pad)], axis=-1).reshape(C_Q_RANK, C_HEADS * LANES)
    zrows = jnp.zeros((256 - C_Q_RANK, C_HEADS * LANES), F32)
    wuq, wuqs = jnp.concatenate([wuq, zrows], axis=0), jnp.concatenate([wuqs, zrows], axis=0)
    ukv = w_ukv.reshape(C_KV_RANK, C_HEADS, C_NOPE + HEAD_DIM)
    uk_n, u_v = ukv[..., :C_NOPE], ukv[..., C_NOPE:]
    wukn = jnp.concatenate([uk_n, jnp.zeros((C_KV_RANK, C_HEADS, LANES - C_NOPE), F32)], axis=-1)
    wukn = wukn.reshape(C_KV_RANK, C_HEADS * LANES)
    wuvt = u_v.reshape(C_KV_RANK, C_HEADS * HEAD_DIM).T

    perm64, _ = _rope_perm(HEAD_DIM)
    return {
        "wrm": wrm.astype(BF16), "wt": wt.astype(BF16),
        "wuq": wuq.astype(BF16), "wuqs": wuqs.astype(BF16),
        "wukn": wukn.astype(BF16), "wuvt": wuvt.astype(BF16),
        "gqb": jnp.tile(gq, B_HEADS)[None, :], "gqbs": jnp.tile(gq[perm64], B_HEADS)[None, :],
        "gkb": jnp.tile(gk, B_KV_HEADS)[None, :], "gkbs": jnp.tile(gk[perm64], B_KV_HEADS)[None, :],
        "gcq": jnp.concatenate([gcq, jnp.zeros((256 - C_Q_RANK,), F32)])[None, :],
        "gckv": gckv[None, :],
    }


def kernel(x, c, ctx, c_ctx, w_ada, b_ada, w_in, w_out, diff_lambda_q1, diff_lambda_k1, diff_lambda_q2,
           diff_lambda_k2, diff_subln_g, gqa_q_norm_g, gqa_k_norm_g, mla_q_norm_g, mla_kv_norm_g, mla_w_uq,
           mla_w_ukv, swa_sink, ln1_g, ln1_b, ln2_g, ln2_b, router_w, router_bias,
           exp_w_gate, exp_w_up, exp_w_down):
    b, n_lat, d = x.shape
    ctx_len = ctx.shape[1]
    depth = w_ada.shape[0]
    assert d == D_MODEL and ctx_len == TQ and n_lat % TQ == 0 and b + 1 <= 8
    alpha = (2 * depth) ** 0.25

    xa = jnp.concatenate([ctx, x], axis=1)
    tabs = _rope_tables(n_lat, ctx_len)

    cc = jnp.concatenate([c, c_ctx[None, :], jnp.zeros((8 - b - 1, d), F32)], axis=0)
    mods = _ada(cc, w_ada, b_ada)
    rwt = router_w.T
    rb = router_bias[:, None]

    for l in range(depth):
        lat = mods[l, :b]
        cx = jnp.broadcast_to(mods[l, b], lat.shape)
        modsel = jnp.stack([cx, lat], axis=1)[:, :, None, :]
        lw = _prep_layer_weights(w_in[l], gqa_q_norm_g[l], gqa_k_norm_g[l], mla_q_norm_g[l],
                                 mla_kv_norm_g[l], mla_w_uq[l], mla_w_ukv[l])
        qa, ka, vat, qb, kb, vbt, qc, kc, vct, qd, kd, vdt = _proj(xa, modsel, lw, tabs)

        lam_init = 0.8 - 0.6 * math.exp(-0.3 * l)
        extras = (diff_lambda_q1[l][None, :], diff_lambda_k1[l][None, :], diff_lambda_q2[l][None, :],
                  diff_lambda_k2[l][None, :], diff_subln_g[l][:, None])
        oa = _attn_full("a", qa, ka, vat, extras, lam_init)
        ob = _attn_full("b", qb, kb, vbt)
        oc = _attn_full("c", qc, kc, vct)
        od = _attn_d(qd, kd, vdt, swa_sink[l][None, :], ctx_len)

        x1, h2, gates = _outproj(oa, ob, oc, od, xa, modsel, w_out[l].astype(BF16),
                                 ln1_g[l][None, :], ln1_b[l][None, :], rwt, rb, alpha)
        wg = exp_w_gate[l].astype(BF16)
        wu = exp_w_up[l].astype(BF16)
        wd = exp_w_down[l].astype(BF16).reshape(N_EXPERT_GROUPS, EXPERTS_PER_GROUP * D_EXPERT, d)
        xa = _moe(h2, gates, x1, modsel, wg, wu, wd, ln2_g[l][None, :], ln2_b[l][None, :], alpha)
    return xa[:, ctx_len:, :]
```

```python
import functools
import math

import numpy as np
import jax
import jax.numpy as jnp
from jax import lax
from jax.experimental import pallas as pl
from jax.experimental.pallas import tpu as pltpu

F32 = jnp.float32
BF16 = jnp.bfloat16

D_MODEL = 1024
HEAD_DIM = 64
GROUP_WIDTH = 256
GRID_W = 64
ROPE_THETA = 10000.0
EPS = 1e-6
NEG = -1e30
LOG2E = 1.4426950408889634

A_HEADS = 4
A_QK = 32
B_HEADS = 4
B_KV_HEADS = 2
C_HEADS = 4
C_Q_RANK = 192
C_KV_RANK = 128
C_NOPE = 64
C_ROPE = 32
D_HEADS = 4
D_KV_HEADS = 2
WINDOW = 128
N_EXPERTS = 16
EXPERTS_PER_GROUP = 4
N_EXPERT_GROUPS = 4
D_EXPERT = 256

TQ = 256
TK = 256
TK_LAT = 1024
SWA_T = 128
LANES = 128
ONES_ROWS = 16
VMEM_LIMIT = 56 * 1024 * 1024

_C_QA, _C_QAS, _C_KA, _C_KAS = 0, 256, 512, 768
_C_QB, _C_QBS, _C_KB, _C_KBS = 1024, 1280, 1536, 1664
_C_QD, _C_QDS, _C_KD, _C_KDS = 1792, 2048, 2304, 2432
_C_CQ, _C_CKV, _C_KR, _C_KRS = 2560, 2816, 2944, 3072
_C_TOTAL = 3200
_R_VA, _R_VB, _R_VD, _R_TOTAL = 0, 256, 384, 512


def _nt_dot(a, b):
    return lax.dot_general(a, b, (((1,), (1,)), ((), ())), preferred_element_type=F32)


def _split_bf16(a):
    hi = a.astype(BF16)
    lo = (a - hi.astype(F32)).astype(BF16)
    return hi, lo


def _layer_norm(x):
    mu = jnp.mean(x, axis=-1, keepdims=True)
    xc = x - mu
    var = jnp.mean(xc * xc, axis=-1, keepdims=True)
    return xc * lax.rsqrt(var + EPS)


def _tile_lanes(a, n):
    return jnp.concatenate([a] * n, axis=1)


def _ada_kernel(cc_ref, w_ref, b_ref, o_ref):
    cc = cc_ref[...]
    s = cc * jax.nn.sigmoid(cc)
    s_hi, s_lo = _split_bf16(s)
    w_hi, w_lo = _split_bf16(w_ref[...])
    acc = jnp.dot(s_hi, w_hi, preferred_element_type=F32)
    acc += jnp.dot(s_hi, w_lo, preferred_element_type=F32)
    acc += jnp.dot(s_lo, w_hi, preferred_element_type=F32)
    o_ref[...] = acc + b_ref[...]


def _ada(cc, w_ada, b_ada):
    depth, d, n6 = w_ada.shape
    bn = 1536
    return pl.pallas_call(
        _ada_kernel,
        grid=(depth, n6 // bn),
        in_specs=[
            pl.BlockSpec((8, d), lambda l, j: (0, 0)),
            pl.BlockSpec((None, d, bn), lambda l, j: (l, 0, j)),
            pl.BlockSpec((None, 1, bn), lambda l, j: (l, 0, j)),
        ],
        out_specs=pl.BlockSpec((None, 8, bn), lambda l, j: (l, 0, j)),
        out_shape=jax.ShapeDtypeStruct((depth, 8, n6), F32),
        compiler_params=pltpu.CompilerParams(
            dimension_semantics=("arbitrary", "arbitrary"), vmem_limit_bytes=VMEM_LIMIT),
        name="ada",
    )(cc, w_ada, b_ada.reshape(depth, 1, n6))


def _block_diag_ones(n, blk):
    sh = int(math.log2(blk))
    r = lax.broadcasted_iota(jnp.int32, (n, n), 0) >> sh
    c = lax.broadcasted_iota(jnp.int32, (n, n), 1) >> sh
    return jnp.where(r == c, 1.0, 0.0).astype(BF16)


def _proj_kernel(x_ref, mod_ref, wrm_ref, wt_ref, wuq_ref, wuqs_ref, wukn_ref, wuvt_ref,
                 gqb_ref, gqbs_ref, gkb_ref, gkbs_ref, gcq_ref, gckv_ref,
                 cos32_ref, sin32_ref, cos64_ref, sin64_ref, cosc_ref, sinc_ref,
                 qa_ref, ka_ref, vat_ref, qb_ref, kb_ref, vbt_ref,
                 qc_ref, kc_ref, vct_ref, qd_ref, kd_ref, vdt_ref):
    d = D_MODEL
    xn = _layer_norm(x_ref[...])
    h = (xn * (1.0 + mod_ref[:, d:2 * d]) + mod_ref[:, 0:d]).astype(BF16)

    def cols(lo, width):
        return jnp.dot(h, wrm_ref[:, lo:lo + width], preferred_element_type=F32)

    cos32, sin32 = _tile_lanes(cos32_ref[...], 2), _tile_lanes(sin32_ref[...], 2)
    cos64, sin64 = cos64_ref[...], sin64_ref[...]
    cos64w, sin64w = _tile_lanes(cos64, 2), _tile_lanes(sin64, 2)

    sa = (A_QK ** -0.5) * LOG2E
    qa_ref[...] = ((cols(_C_QA, 256) * cos32 + cols(_C_QAS, 256) * sin32) * sa).astype(BF16)
    ka_ref[...] = (cols(_C_KA, 256) * cos32 + cols(_C_KAS, 256) * sin32).astype(BF16)

    sb = (HEAD_DIM ** -0.5) * LOG2E
    q = cols(_C_QB, 256)
    ssq = jnp.dot((q * q).astype(BF16), _block_diag_ones(256, HEAD_DIM), preferred_element_type=F32)
    r = lax.rsqrt(ssq * (1.0 / HEAD_DIM) + EPS)
    qr = (q * gqb_ref[...]) * cos64w + (cols(_C_QBS, 256) * gqbs_ref[...]) * sin64w
    qb_ref[...] = (qr * (r * sb)).astype(BF16)
    k = cols(_C_KB, 128)
    ssq = jnp.dot((k * k).astype(BF16), _block_diag_ones(128, HEAD_DIM), preferred_element_type=F32)
    r = lax.rsqrt(ssq * (1.0 / HEAD_DIM) + EPS)
    kr = (k * gkb_ref[...]) * cos64 + (cols(_C_KBS, 128) * gkbs_ref[...]) * sin64
    kb_ref[...] = (kr * r).astype(BF16)

    qd_ref[...] = ((cols(_C_QD, 256) * cos64w + cols(_C_QDS, 256) * sin64w) * sb).astype(BF16)
    kd_ref[...] = (cols(_C_KD, 128) * cos64 + cols(_C_KDS, 128) * sin64).astype(BF16)

    sc = ((C_NOPE + C_ROPE) ** -0.5) * LOG2E
    cosc, sinc = cosc_ref[...], sinc_ref[...]
    cq = cols(_C_CQ, 256)
    ms = jnp.sum(cq * cq, axis=-1, keepdims=True) * (1.0 / C_Q_RANK)
    cqn = (cq * lax.rsqrt(ms + EPS) * gcq_ref[...]).astype(BF16)
    qn = jnp.dot(cqn, wuq_ref[...], preferred_element_type=F32)
    qs = jnp.dot(cqn, wuqs_ref[...], preferred_element_type=F32)
    qc_ref[...] = ((qn * _tile_lanes(cosc, 4) + qs * _tile_lanes(sinc, 4)) * sc).astype(BF16)
    ckv = cols(_C_CKV, 128)
    ms = jnp.mean(ckv * ckv, axis=-1, keepdims=True)
    ckvn = (ckv * lax.rsqrt(ms + EPS) * gckv_ref[...]).astype(BF16)
    kn = jnp.dot(ckvn, wukn_ref[...], preferred_element_type=F32)
    krr = cols(_C_KR, 128) * cosc + cols(_C_KRS, 128) * sinc
    kc_ref[...] = (kn + _tile_lanes(krr, 4)).astype(BF16)
    vct_ref[...] = _nt_dot(wuvt_ref[...], ckvn).astype(BF16)

    vt = _nt_dot(wt_ref[...], h)
    vat_ref[...] = vt[_R_VA:_R_VA + 256].astype(BF16)
    vbt_ref[...] = vt[_R_VB:_R_VB + 128].astype(BF16)
    vdt_ref[...] = vt[_R_VD:_R_VD + 128].astype(BF16)


def _proj(xa, modsel, lw, tabs):
    b, nt, d = xa.shape
    nc = nt // TQ

    def full(a):
        nd = a.ndim
        return pl.BlockSpec(a.shape, lambda bi, i, _n=nd: (0,) * _n)

    def tab(a):
        return pl.BlockSpec((TQ, LANES), lambda bi, i: (i, 0))

    def rm(w):
        return pl.BlockSpec((None, TQ, w), lambda bi, i: (bi, i, 0))

    def tr(w):
        return pl.BlockSpec((None, None, w, TQ), lambda bi, i: (bi, i, 0, 0))

    weights = [lw["wrm"], lw["wt"], lw["wuq"], lw["wuqs"], lw["wukn"], lw["wuvt"],
               lw["gqb"], lw["gqbs"], lw["gkb"], lw["gkbs"], lw["gcq"], lw["gckv"]]
    tables = [tabs["cos32"], tabs["sin32"], tabs["cos64"], tabs["sin64"], tabs["cosc"], tabs["sinc"]]
    in_specs = ([pl.BlockSpec((None, TQ, d), lambda bi, i: (bi, i, 0)),
                 pl.BlockSpec((None, None, 1, 6 * d), lambda bi, i: (bi, jnp.minimum(i, 1), 0, 0))]
                + [full(w) for w in weights] + [tab(t) for t in tables])
    widths = [("rm", 256), ("rm", 256), ("tr", 256), ("rm", 256), ("rm", 128), ("tr", 128),
              ("rm", 512), ("rm", 512), ("tr", 256), ("rm", 256), ("rm", 128), ("tr", 128)]
    out_specs, out_shape = [], []
    for kind, w in widths:
        if kind == "rm":
            out_specs.append(rm(w))
            out_shape.append(jax.ShapeDtypeStruct((b, nt, w), BF16))
        else:
            out_specs.append(tr(w))
            out_shape.append(jax.ShapeDtypeStruct((b, nc, w, TQ), BF16))
    return pl.pallas_call(
        _proj_kernel,
        grid=(b, nc),
        in_specs=in_specs,
        out_specs=out_specs,
        out_shape=out_shape,
        compiler_params=pltpu.CompilerParams(
            dimension_semantics=("arbitrary", "arbitrary"), vmem_limit_bytes=VMEM_LIMIT),
        name="proj",
    )(xa, modsel, *weights, *tables)


def _flash_loop(qi, nc, k_ref, vt_ref, qz_ref, m_ref, alpha_ref, acc_ref, s0_ref, s1_ref, k_lanes, pv_cfg,
                slab_rows):
    per = TK_LAT // TK
    n_lat = (nc - 1) // per
    assert n_lat % 2 == 0 and n_lat >= 2
    same_lanes = all(kl == k_lanes[0] for kl in k_lanes)

    def scores(kc):
        if same_lanes:
            return jnp.dot(kc[:, k_lanes[0][0]:k_lanes[0][1]], qz_ref[...], preferred_element_type=F32)
        return jnp.concatenate(
            [jnp.dot(kc[:, lo:hi], qz_ref[:, TQ * j:TQ * (j + 1)], preferred_element_type=F32)
             for j, (lo, hi) in enumerate(k_lanes)], axis=1)

    def small_chunk(pieces, first):
        if slab_rows >= pieces * TK:
            return (1, pieces, first)
        return (pieces * TK // slab_rows, slab_rows // TK, first)

    ctx_chunk = small_chunk(1, 0)

    def lat_chunk(c):
        return small_chunk(per, 1 + c * per)

    def stats_update(mx):
        m_old = m_ref[...]
        m_new = jnp.maximum(m_old, mx)
        alpha_ref[...] = jnp.exp2(m_old - m_new)
        m_ref[...] = m_new

    def pipelined(cur, cur_ref, nxt, nxt_ref):
        m_cur, alpha = m_ref[...], alpha_ref[...]
        mx, pv = None, [None] * len(pv_cfg)
        n_cur = cur[0] if cur is not None else 0
        n_nxt = nxt[0] if nxt is not None else 0
        for r in range(max(n_cur, n_nxt)):
            if r < n_nxt:
                rows = TK * nxt[1]
                kc = k_ref[pl.ds(pl.multiple_of((nxt[2] + r * nxt[1]) * TK, TK), rows), :]
                s = scores(kc)
                nxt_ref[rows * r:rows * (r + 1), :] = s
                smx = jnp.max(s, axis=0, keepdims=True)
                mx = smx if mx is None else jnp.maximum(mx, smx)
            if r < n_cur:
                rows = TK * cur[1]
                p = jnp.exp2(cur_ref[rows * r:rows * (r + 1), :] - m_cur).astype(BF16)
                vc = jnp.concatenate([vt_ref[cur[2] + r * cur[1] + i] for i in range(cur[1])], axis=1)
                ones = jnp.ones((ONES_ROWS, rows), BF16)
                for g, (vl, vh, j0, nj) in enumerate(pv_cfg):
                    va = jnp.concatenate([vc[vl:vh, :], ones], axis=0)
                    d = jnp.dot(va, p[:, TQ * j0:TQ * (j0 + nj)], preferred_element_type=F32)
                    pv[g] = d if pv[g] is None else pv[g] + d
        if cur is not None:
            for g, (vl, vh, j0, nj) in enumerate(pv_cfg):
                cols = slice(TQ * j0, TQ * (j0 + nj))
                acc_ref[:, cols] = alpha[:, cols] * acc_ref[:, cols] + pv[g]
        if nxt is not None:
            stats_update(mx)

    m_ref[...] = jnp.full(m_ref.shape, NEG, F32)
    acc_ref[...] = jnp.zeros(acc_ref.shape, F32)

    @pl.when(qi == 0)
    def _():
        pipelined(None, None, ctx_chunk, s0_ref)
        pipelined(ctx_chunk, s0_ref, None, None)

    @pl.when(qi > 0)
    def _():
        pipelined(None, None, lat_chunk(0), s0_ref)

        def body(i, carry):
            c = 2 * i
            pipelined(lat_chunk(c), s0_ref, lat_chunk(c + 1), s1_ref)
            pipelined(lat_chunk(c + 1), s1_ref, lat_chunk(c + 2), s0_ref)
            return carry

        lax.fori_loop(0, n_lat // 2 - 1, body, 0)
        pipelined(lat_chunk(n_lat - 2), s0_ref, lat_chunk(n_lat - 1), s1_ref)
        pipelined(lat_chunk(n_lat - 1), s1_ref, ctx_chunk, s0_ref)
        pipelined(ctx_chunk, s0_ref, None, None)


def _normalized(acc_ref, j, dv):
    a = acc_ref[:, TQ * j:TQ * (j + 1)]
    return a[0:dv] / a[dv:dv + 1]


def _attn_a_kernel(q_ref, k_ref, vt_ref, lq1_ref, lk1_ref, lq2_ref, lk2_ref, g_ref, o_ref,
                   qz_ref, m_ref, alpha_ref, acc_ref, s0_ref, s1_ref,*, nc, lam_init):
    qi = pl.program_id(1)
    qt = q_ref[...].astype(F32).T
    row = lax.broadcasted_iota(jnp.int32, qt.shape, 0)
    for j in range(2 * A_HEADS):
        keep = (row >= A_QK * j) & (row < A_QK * (j + 1))
        qz_ref[:, TQ * j:TQ * (j + 1)] = jnp.where(keep, qt, 0.0).astype(BF16)
    k_lanes = [(0, 256)] * (2 * A_HEADS)
    pv_cfg = [(HEAD_DIM * hd, HEAD_DIM * (hd + 1), 2 * hd, 2) for hd in range(A_HEADS)]
    _flash_loop(qi, nc, k_ref, vt_ref, qz_ref, m_ref, alpha_ref, acc_ref, s0_ref, s1_ref, k_lanes, pv_cfg,
                slab_rows=1024)

    lam = (jnp.exp(jnp.sum(lq1_ref[...] * lk1_ref[...], axis=-1, keepdims=True))
           - jnp.exp(jnp.sum(lq2_ref[...] * lk2_ref[...], axis=-1, keepdims=True)) + lam_init)
    outs = []
    for hd in range(A_HEADS):
        o = _normalized(acc_ref, 2 * hd, HEAD_DIM) - lam * _normalized(acc_ref, 2 * hd + 1, HEAD_DIM)
        ms = jnp.mean(o * o, axis=0, keepdims=True)
        outs.append(o * lax.rsqrt(ms + EPS) * g_ref[...] * (1.0 - lam_init))
    o_ref[...] = jnp.concatenate(outs, axis=0).T.astype(BF16)


def _attn_b_kernel(q_ref, k_ref, vt_ref, o_ref, qz_ref, m_ref, alpha_ref, acc_ref, s0_ref, s1_ref,*, nc):
    qi = pl.program_id(1)
    qt = q_ref[...].astype(F32).T
    zeros = jnp.zeros((HEAD_DIM, TQ), F32)
    rep = B_HEADS // B_KV_HEADS
    for hd in range(B_HEADS):
        parts = [zeros] * B_KV_HEADS
        parts[hd // rep] = qt[HEAD_DIM * hd:HEAD_DIM * (hd + 1)]
        qz_ref[:, TQ * hd:TQ * (hd + 1)] = jnp.concatenate(parts, axis=0).astype(BF16)
    k_lanes = [(0, 128)] * B_HEADS
    pv_cfg = [(HEAD_DIM * g, HEAD_DIM * (g + 1), rep * g, rep) for g in range(B_KV_HEADS)]
    _flash_loop(qi, nc, k_ref, vt_ref, qz_ref, m_ref, alpha_ref, acc_ref, s0_ref, s1_ref, k_lanes, pv_cfg,
                slab_rows=256)
    outs = [_normalized(acc_ref, hd, HEAD_DIM) for hd in range(B_HEADS)]
    o_ref[...] = jnp.concatenate(outs, axis=0).T.astype(BF16)


def _attn_c_kernel(q_ref, k_ref, vt_ref, o_ref, qz_ref, m_ref, alpha_ref, acc_ref, s0_ref, s1_ref,*, nc):
    qi = pl.program_id(1)
    qt = q_ref[...].astype(F32).T
    k_lanes, pv_cfg = [], []
    for hd in range(C_HEADS):
        qz_ref[:, TQ * hd:TQ * (hd + 1)] = qt[LANES * hd:LANES * (hd + 1)].astype(BF16)
        k_lanes.append((LANES * hd, LANES * (hd + 1)))
        pv_cfg.append((HEAD_DIM * hd, HEAD_DIM * (hd + 1), hd, 1))
    _flash_loop(qi, nc, k_ref, vt_ref, qz_ref, m_ref, alpha_ref, acc_ref, s0_ref, s1_ref, k_lanes, pv_cfg,
                slab_rows=256)
    outs = [_normalized(acc_ref, hd, HEAD_DIM) for hd in range(C_HEADS)]
    o_ref[...] = jnp.concatenate(outs, axis=0).T.astype(BF16)


def _attn_full(kind, q, k, vt, extras=(), lam_init=0.0):
    b, nt, wq = q.shape
    wk = k.shape[-1]
    nc, wv = vt.shape[1], vt.shape[2]
    assert (nt - TK) % TK_LAT == 0
    if kind == "a":
        body, n_sm, dk = functools.partial(_attn_a_kernel, nc=nc, lam_init=lam_init), 2 * A_HEADS, 256
    elif kind == "b":
        body, n_sm, dk = functools.partial(_attn_b_kernel, nc=nc), B_HEADS, 128
    else:
        body, n_sm, dk = functools.partial(_attn_c_kernel, nc=nc), C_HEADS, 128
    resident = pl.Buffered(1)
    in_specs = [
        pl.BlockSpec((None, TQ, wq), lambda bi, i: (bi, i, 0)),
        pl.BlockSpec((None, nt, wk), lambda bi, i: (bi, 0, 0), pipeline_mode=resident),
        pl.BlockSpec((None, nc, wv, TK), lambda bi, i: (bi, 0, 0, 0), pipeline_mode=resident),
    ] + [pl.BlockSpec(e.shape, lambda bi, i: (0, 0)) for e in extras]
    return pl.pallas_call(
        body,
        grid=(b, nc),
        in_specs=in_specs,
        out_specs=pl.BlockSpec((None, TQ, GROUP_WIDTH), lambda bi, i: (bi, i, 0)),
        out_shape=jax.ShapeDtypeStruct((b, nt, GROUP_WIDTH), BF16),
        scratch_shapes=[
            pltpu.VMEM((dk, n_sm * TQ), BF16),
            pltpu.VMEM((1, n_sm * TQ), F32),
            pltpu.VMEM((1, n_sm * TQ), F32),
            pltpu.VMEM((HEAD_DIM + ONES_ROWS, n_sm * TQ), F32),
            pltpu.VMEM((TK_LAT, n_sm * TQ), F32),
            pltpu.VMEM((TK_LAT, n_sm * TQ), F32),
        ],
        compiler_params=pltpu.CompilerParams(
            dimension_semantics=("arbitrary", "arbitrary"), vmem_limit_bytes=VMEM_LIMIT),
        name="attn_" + kind,
    )(q, k, vt, *extras)


def _attn_d_kernel(q_ref, kc_ref, kp_ref, ko_ref, kn_ref, vc_ref, vp_ref, vo_ref, vn_ref, sink_ref,
                   o_ref, *, nb, ctx_blocks):
    n = pl.program_id(1)
    t = SWA_T
    rep = D_HEADS // D_KV_HEADS
    qt = q_ref[...].astype(F32).T
    zeros = jnp.zeros((HEAD_DIM, t), F32)
    cols = []
    for hd in range(D_HEADS):
        parts = [zeros] * D_KV_HEADS
        parts[hd // rep] = qt[HEAD_DIM * hd:HEAD_DIM * (hd + 1)]
        cols.append(jnp.concatenate(parts, axis=0))
    qz = jnp.concatenate(cols, axis=1).astype(BF16)
    w = D_HEADS * t
    ko_i = lax.broadcasted_iota(jnp.int32, (t, w), 0)
    qo_i = lax.broadcasted_iota(jnp.int32, (t, w), 1) & (t - 1)
    band = n >= ctx_blocks
    ok_prev = jnp.logical_and(n >= ctx_blocks + 1, qo_i <= ko_i)
    ok_own = jnp.logical_and(band, ko_i >= 0)
    ok_next = jnp.logical_and(jnp.logical_and(band, n + 1 <= nb - 1), ko_i <= qo_i)
    s_c = jnp.dot(kc_ref[...], qz, preferred_element_type=F32)
    s_p = jnp.where(ok_prev, jnp.dot(kp_ref[...], qz, preferred_element_type=F32), NEG)
    s_o = jnp.where(ok_own, jnp.dot(ko_ref[...], qz, preferred_element_type=F32), NEG)
    s_n = jnp.where(ok_next, jnp.dot(kn_ref[...], qz, preferred_element_type=F32), NEG)
    sk = jnp.concatenate([jnp.broadcast_to(sink_ref[0:1, hd:hd + 1] * LOG2E, (1, t))
                          for hd in range(D_HEADS)], axis=1)
    m = jnp.maximum(jnp.max(s_c, axis=0, keepdims=True), jnp.max(s_p, axis=0, keepdims=True))
    m = jnp.maximum(m, jnp.max(s_o, axis=0, keepdims=True))
    m = jnp.maximum(m, jnp.max(s_n, axis=0, keepdims=True))
    m = jnp.maximum(m, sk)
    pieces = [(vc_ref[...], jnp.exp2(s_c - m).astype(BF16)), (vp_ref[...], jnp.exp2(s_p - m).astype(BF16)),
              (vo_ref[...], jnp.exp2(s_o - m).astype(BF16)), (vn_ref[...], jnp.exp2(s_n - m).astype(BF16))]
    l_sink = jnp.exp2(sk - m)
    outs = []
    for g in range(D_KV_HEADS):
        lanes = slice(rep * t * g, rep * t * (g + 1))
        o = None
        for v, p in pieces:
            va = jnp.concatenate([v[HEAD_DIM * g:HEAD_DIM * (g + 1)],
                                  jnp.ones((ONES_ROWS, v.shape[1]), BF16)], axis=0)
            d = jnp.dot(va, p[:, lanes], preferred_element_type=F32)
            o = d if o is None else o + d
        o = o[0:HEAD_DIM] / (o[HEAD_DIM:HEAD_DIM + 1] + l_sink[:, lanes])
        outs += [o[:, t * r:t * (r + 1)] for r in range(rep)]
    o_ref[...] = jnp.concatenate(outs, axis=0).T.astype(BF16)


def _attn_d(q, k, vt, sink, ctx_len):
    b, nt, wq = q.shape
    wk = k.shape[-1]
    wv = vt.shape[2]
    t = SWA_T
    nb = nt // t
    per = TQ // t
    ctx_blocks = ctx_len // t
    assert ctx_len == TQ

    def kspec(shift):
        return pl.BlockSpec((None, t, wk), lambda bi, i: (bi, jnp.clip(i + shift, 0, nb - 1), 0))

    def vspec(shift):
        def imap(bi, i):
            j = jnp.clip(i + shift, 0, nb - 1)
            return (bi, j // per, 0, j % per)
        return pl.BlockSpec((None, None, wv, t), imap)

    in_specs = [
        pl.BlockSpec((None, t, wq), lambda bi, i: (bi, i, 0)),
        pl.BlockSpec((None, ctx_len, wk), lambda bi, i: (bi, 0, 0)),
        kspec(-1), kspec(0), kspec(1),
        pl.BlockSpec((None, None, wv, ctx_len), lambda bi, i: (bi, 0, 0, 0)),
        vspec(-1), vspec(0), vspec(1),
        pl.BlockSpec(sink.shape, lambda bi, i: (0, 0)),
    ]
    return pl.pallas_call(
        functools.partial(_attn_d_kernel, nb=nb, ctx_blocks=ctx_blocks),
        grid=(b, nb),
        in_specs=in_specs,
        out_specs=pl.BlockSpec((None, t, GROUP_WIDTH), lambda bi, i: (bi, i, 0)),
        out_shape=jax.ShapeDtypeStruct((b, nt, GROUP_WIDTH), BF16),
        compiler_params=pltpu.CompilerParams(
            dimension_semantics=("arbitrary", "arbitrary"), vmem_limit_bytes=VMEM_LIMIT),
        name="attn_d",
    )(q, k, k, k, k, vt, vt, vt, vt, sink)


def _route(sel_rows, s_rows):
    epg = EXPERTS_PER_GROUP
    gscore = []
    for g in range(N_EXPERT_GROUPS):
        r = sel_rows[epg * g:epg * (g + 1)]
        pair = None
        for i in range(epg):
            for j in range(i + 1, epg):
                v = r[i] + r[j]
                pair = v if pair is None else jnp.maximum(pair, v)
        gscore.append(pair)
    best, best_g = gscore[0], jnp.zeros_like(gscore[0], dtype=jnp.int32)
    for g in range(1, N_EXPERT_GROUPS):
        better = gscore[g] > best
        best_g = jnp.where(better, g, best_g)
        best = jnp.where(better, gscore[g], best)
    w = []
    for e in range(N_EXPERTS):
        g = e // epg
        cnt = jnp.zeros_like(best_g)
        for e2 in range(epg * g, epg * (g + 1)):
            if e2 == e:
                continue
            beats = sel_rows[e2] > sel_rows[e]
            if e2 < e:
                beats = jnp.logical_or(beats, sel_rows[e2] == sel_rows[e])
            cnt = cnt + jnp.where(beats, 1, 0)
        chosen = jnp.logical_and(best_g == g, cnt < 2)
        w.append(jnp.where(chosen, s_rows[e], 0.0))
    tot = w[0]
    for e in range(1, N_EXPERTS):
        tot = tot + w[e]
    return [we / tot for we in w]


def _outproj_kernel(oa_ref, ob_ref, oc_ref, od_ref, x_ref, mod_ref, wout_ref, g_ref, b_ref,
                    rwt_ref, rb_ref, x1_ref, h2_ref, gates_ref, *, alpha):
    d = D_MODEL
    o = jnp.concatenate([oa_ref[...], ob_ref[...], oc_ref[...], od_ref[...]], axis=1)
    y = jnp.dot(o, wout_ref[...], preferred_element_type=F32)
    u = alpha * x_ref[...] + mod_ref[:, 2 * d:3 * d] * y
    x1 = _layer_norm(u) * g_ref[...] + b_ref[...]
    x1_ref[...] = x1
    h2 = _layer_norm(x1) * (1.0 + mod_ref[:, 4 * d:5 * d]) + mod_ref[:, 3 * d:4 * d]
    h2_ref[...] = h2.astype(BF16)

    h_hi, h_lo = _split_bf16(h2)
    w_hi, w_lo = _split_bf16(rwt_ref[...])
    logits = _nt_dot(w_hi, h_hi) + _nt_dot(w_hi, h_lo) + _nt_dot(w_lo, h_hi)
    s = jax.nn.sigmoid(logits)
    sel = s + rb_ref[...]
    s_rows = [s[e:e + 1] for e in range(N_EXPERTS)]
    sel_rows = [sel[e:e + 1] for e in range(N_EXPERTS)]
    gates = _route(sel_rows, s_rows)
    gt = jnp.concatenate(gates + [jnp.zeros((LANES - N_EXPERTS, TQ), F32)], axis=0)
    gates_ref[...] = gt.T


def _outproj(oa, ob, oc, od, xa, modsel, w_out, ln_g, ln_b, rwt, rb, alpha):
    b, nt, d = xa.shape
    nc = nt // TQ

    def rm(w):
        return pl.BlockSpec((None, TQ, w), lambda bi, i: (bi, i, 0))

    def full(a):
        nd = a.ndim
        return pl.BlockSpec(a.shape, lambda bi, i, _n=nd: (0,) * _n)

    in_specs = [rm(GROUP_WIDTH) for _ in range(4)] + [
        rm(d),
        pl.BlockSpec((None, None, 1, 6 * d), lambda bi, i: (bi, jnp.minimum(i, 1), 0, 0)),
        full(w_out), full(ln_g), full(ln_b), full(rwt), full(rb)]
    return pl.pallas_call(
        functools.partial(_outproj_kernel, alpha=alpha),
        grid=(b, nc),
        in_specs=in_specs,
        out_specs=[rm(d), rm(d), rm(LANES)],
        out_shape=[jax.ShapeDtypeStruct((b, nt, d), F32), jax.ShapeDtypeStruct((b, nt, d), BF16),
                   jax.ShapeDtypeStruct((b, nt, LANES), F32)],
        compiler_params=pltpu.CompilerParams(
            dimension_semantics=("arbitrary", "arbitrary"), vmem_limit_bytes=VMEM_LIMIT),
        name="outproj",
    )(oa, ob, oc, od, xa, modsel, w_out, ln_g, ln_b, rwt, rb)


def _moe_kernel(h_ref, gates_ref, x_ref, mod_ref, wg_ref, wu_ref, wd_ref, g_ref, b_ref, o_ref, *, alpha):
    d = D_MODEL
    h = h_ref[...]
    gates = gates_ref[...]
    y = jnp.zeros((TQ, d), F32)
    for g in range(N_EXPERT_GROUPS):
        acts = []
        for el in range(EXPERTS_PER_GROUP):
            e = EXPERTS_PER_GROUP * g + el
            hg = jnp.dot(h, wg_ref[e], preferred_element_type=F32)
            hu = jnp.dot(h, wu_ref[e], preferred_element_type=F32)
            a = hg * jax.nn.sigmoid(hg) * hu * gates[:, e:e + 1]
            acts.append(a.astype(BF16))
        y = y + jnp.dot(jnp.concatenate(acts, axis=1), wd_ref[g], preferred_element_type=F32)
    u = alpha * x_ref[...] + mod_ref[:, 5 * d:6 * d] * y
    o_ref[...] = _layer_norm(u) * g_ref[...] + b_ref[...]


def _moe(h2, gates, x1, modsel, wg, wu, wd, ln_g, ln_b, alpha, latents_only):
    b, nt, d = x1.shape
    nc = nt // TQ
    resident = pl.Buffered(1)
    if latents_only:
        out_spec = pl.BlockSpec((None, TQ, d), lambda bi, i: (bi, jnp.maximum(i - 1, 0), 0))
        out_rows = nt - TQ
    else:
        out_spec = pl.BlockSpec((None, TQ, d), lambda bi, i: (bi, i, 0))
        out_rows = nt

    def rm(w):
        return pl.BlockSpec((None, TQ, w), lambda bi, i: (bi, i, 0))

    def res(a):
        nd = a.ndim
        return pl.BlockSpec(a.shape, lambda bi, i, _n=nd: (0,) * _n, pipeline_mode=resident)

    in_specs = [rm(d), rm(LANES), rm(d),
                pl.BlockSpec((None, None, 1, 6 * d), lambda bi, i: (bi, jnp.minimum(i, 1), 0, 0)),
                res(wg), res(wu), res(wd), res(ln_g), res(ln_b)]
    return pl.pallas_call(
        functools.partial(_moe_kernel, alpha=alpha),
        grid=(b, nc),
        in_specs=in_specs,
        out_specs=out_spec,
        out_shape=jax.ShapeDtypeStruct((b, out_rows, d), F32),
        compiler_params=pltpu.CompilerParams(
            dimension_semantics=("arbitrary", "arbitrary"), vmem_limit_bytes=VMEM_LIMIT),
        name="moe",
    )(h2, gates, x1, modsel, wg, wu, wd, ln_g, ln_b)


def _rope_perm(rot_dim):
    n = rot_dim // 4
    j = np.arange(rot_dim)
    return j ^ n, np.where((j // n) % 2 == 0, -1.0, 1.0).astype(np.float32)


def _swapped(w, rot_dim):
    perm, sign = _rope_perm(rot_dim)
    cols = w.shape[-1]
    idx = (np.arange(cols) // rot_dim) * rot_dim + perm[np.arange(cols) % rot_dim]
    sgn = sign[np.arange(cols) % rot_dim]
    return w[..., idx] * sgn


def _rope_tables(n_lat, ctx_len):
    t = jnp.arange(n_lat, dtype=jnp.int32)
    row, col = (t // GRID_W).astype(F32), (t % GRID_W).astype(F32)

    def pattern(rot_dim):
        n = rot_dim // 4
        inv = ROPE_THETA ** (-jnp.arange(n, dtype=F32) / n)
        ar, ac = row[:, None] * inv, col[:, None] * inv
        ang = jnp.concatenate([ar, ar, ac, ac], axis=-1)
        return jnp.cos(ang), jnp.sin(ang)

    def with_ctx(a, fill):
        return jnp.concatenate([jnp.full((ctx_len, a.shape[1]), fill, F32), a], axis=0)

    c32, s32 = pattern(A_QK)
    c64, s64 = pattern(HEAD_DIM)
    ones = jnp.ones((n_lat, C_NOPE), F32)
    zeros = jnp.zeros((n_lat, C_NOPE), F32)
    pad1 = jnp.ones((n_lat, LANES - C_NOPE - C_ROPE), F32)
    pad0 = jnp.zeros((n_lat, LANES - C_NOPE - C_ROPE), F32)
    return {
        "cos32": with_ctx(jnp.tile(c32, (1, LANES // A_QK)), 1.0),
        "sin32": with_ctx(jnp.tile(s32, (1, LANES // A_QK)), 0.0),
        "cos64": with_ctx(jnp.tile(c64, (1, LANES // HEAD_DIM)), 1.0),
        "sin64": with_ctx(jnp.tile(s64, (1, LANES // HEAD_DIM)), 0.0),
        "cosc": with_ctx(jnp.concatenate([ones, c32, pad1], axis=-1), 1.0),
        "sinc": with_ctx(jnp.concatenate([zeros, s32, pad0], axis=-1), 0.0),
    }


def _prep_layer_weights(w_in, gq, gk, gcq, gckv, w_uq, w_ukv):
    d = w_in.shape[0]
    splits = np.cumsum([256, 256, 256, 256, 128, 128, C_Q_RANK, C_KV_RANK, C_ROPE, 256, 128, 128])[:-1]
    (a_q, a_k, a_v, b_q, b_k, b_v, c_q, c_kv, c_kr, d_q, d_k, d_v) = jnp.split(w_in, splits, axis=1)
    z = lambda n: jnp.zeros((d, n), F32)
    kr4 = jnp.concatenate([z(C_NOPE), c_kr, z(LANES - C_NOPE - C_ROPE)], axis=1)
    kr4s = jnp.concatenate([z(C_NOPE), _swapped(c_kr, C_ROPE), z(LANES - C_NOPE - C_ROPE)], axis=1)
    wrm = jnp.concatenate([
        a_q, _swapped(a_q, A_QK), a_k, _swapped(a_k, A_QK),
        b_q, _swapped(b_q, HEAD_DIM), b_k, _swapped(b_k, HEAD_DIM),
        d_q, _swapped(d_q, HEAD_DIM), d_k, _swapped(d_k, HEAD_DIM),
        c_q, z(256 - C_Q_RANK), c_kv, kr4, kr4s], axis=1)
    wt = jnp.concatenate([a_v, b_v, d_v], axis=1).T

    uq = w_uq.reshape(C_Q_RANK, C_HEADS, C_NOPE + C_ROPE)
    uq_n, uq_r = uq[..., :C_NOPE], uq[..., C_NOPE:]
    zq = lambda n: jnp.zeros((C_Q_RANK, C_HEADS, n), F32)
    pad = LANES - C_NOPE - C_ROPE
    wuq = jnp.concatenate([uq_n, uq_r, zq(pad)], axis=-1).reshape(C_Q_RANK, C_HEADS * LANES)
    wuqs = jnp.concatenate([zq(C_NOPE), _swapped(uq_r, C_ROPE), zq(pad)], axis=-1).reshape(C_Q_RANK, C_HEADS * LANES)
    zrows = jnp.zeros((256 - C_Q_RANK, C_HEADS * LANES), F32)
    wuq, wuqs = jnp.concatenate([wuq, zrows], axis=0), jnp.concatenate([wuqs, zrows], axis=0)
    ukv = w_ukv.reshape(C_KV_RANK, C_HEADS, C_NOPE + HEAD_DIM)
    uk_n, u_v = ukv[..., :C_NOPE], ukv[..., C_NOPE:]
    wukn = jnp.concatenate([uk_n, jnp.zeros((C_KV_RANK, C_HEADS, LANES - C_NOPE), F32)], axis=-1)
    wukn = wukn.reshape(C_KV_RANK, C_HEADS * LANES)
    wuvt = u_v.reshape(C_KV_RANK, C_HEADS * HEAD_DIM).T

    perm64, _ = _rope_perm(HEAD_DIM)
    return {
        "wrm": wrm.astype(BF16), "wt": wt.astype(BF16),
        "wuq": wuq.astype(BF16), "wuqs": wuqs.astype(BF16),
        "wukn": wukn.astype(BF16), "wuvt": wuvt.astype(BF16),
        "gqb": jnp.tile(gq, B_HEADS)[None, :], "gqbs": jnp.tile(gq[perm64], B_HEADS)[None, :],
        "gkb": jnp.tile(gk, B_KV_HEADS)[None, :], "gkbs": jnp.tile(gk[perm64], B_KV_HEADS)[None, :],
        "gcq": jnp.concatenate([gcq, jnp.zeros((256 - C_Q_RANK,), F32)])[None, :],
        "gckv": gckv[None, :],
    }


def kernel(x, c, ctx, c_ctx, w_ada, b_ada, w_in, w_out, diff_lambda_q1, diff_lambda_k1, diff_lambda_q2,
           diff_lambda_k2, diff_subln_g, gqa_q_norm_g, gqa_k_norm_g, mla_q_norm_g, mla_kv_norm_g, mla_w_uq,
           mla_w_ukv, swa_sink, ln1_g, ln1_b, ln2_g, ln2_b, router_w, router_bias,
           exp_w_gate, exp_w_up, exp_w_down):
    b, n_lat, d = x.shape
    ctx_len = ctx.shape[1]
    depth = w_ada.shape[0]
    assert d == D_MODEL and ctx_len == TQ and n_lat % TQ == 0 and b + 1 <= 8
    alpha = (2 * depth) ** 0.25

    xa = jnp.concatenate([ctx, x], axis=1)
    tabs = _rope_tables(n_lat, ctx_len)

    cc = jnp.concatenate([c, c_ctx[None, :], jnp.zeros((8 - b - 1, d), F32)], axis=0)
    mods = _ada(cc, w_ada, b_ada)
    rwt = router_w.T
    rb = router_bias[:, None]

    for l in range(depth):
        lat = mods[l, :b]
        cx = jnp.broadcast_to(mods[l, b], lat.shape)
        modsel = jnp.stack([cx, lat], axis=1)[:, :, None, :]
        lw = _prep_layer_weights(w_in[l], gqa_q_norm_g[l], gqa_k_norm_g[l], mla_q_norm_g[l],
                                 mla_kv_norm_g[l], mla_w_uq[l], mla_w_ukv[l])
        qa, ka, vat, qb, kb, vbt, qc, kc, vct, qd, kd, vdt = _proj(xa, modsel, lw, tabs)

        lam_init = 0.8 - 0.6 * math.exp(-0.3 * l)
        extras = (diff_lambda_q1[l][None, :], diff_lambda_k1[l][None, :], diff_lambda_q2[l][None, :],
                  diff_lambda_k2[l][None, :], diff_subln_g[l][:, None])
        oa = _attn_full("a", qa, ka, vat, extras, lam_init)
        ob = _attn_full("b", qb, kb, vbt)
        oc = _attn_full("c", qc, kc, vct)
        od = _attn_d(qd, kd, vdt, swa_sink[l][None, :], ctx_len)

        x1, h2, gates = _outproj(oa, ob, oc, od, xa, modsel, w_out[l].astype(BF16),
                                 ln1_g[l][None, :], ln1_b[l][None, :], rwt, rb, alpha)
        wg = exp_w_gate[l].astype(BF16)
        wu = exp_w_up[l].astype(BF16)
        wd = exp_w_down[l].astype(BF16).reshape(N_EXPERT_GROUPS, EXPERTS_PER_GROUP * D_EXPERT, d)
        xa = _moe(h2, gates, x1, modsel, wg, wu, wd, ln2_g[l][None, :], ln2_b[l][None, :], alpha,
                  latents_only=(l == depth - 1))
    return xa
```

```python
import functools
import math

import numpy as np
import jax
import jax.numpy as jnp
from jax import lax
from jax.experimental import pallas as pl
from jax.experimental.pallas import tpu as pltpu

F32 = jnp.float32
BF16 = jnp.bfloat16

D_MODEL = 1024
HEAD_DIM = 64
GROUP_WIDTH = 256
GRID_W = 64
ROPE_THETA = 10000.0
EPS = 1e-6
NEG = -1e30
LOG2E = 1.4426950408889634

A_HEADS = 4
A_QK = 32
B_HEADS = 4
B_KV_HEADS = 2
C_HEADS = 4
C_Q_RANK = 192
C_KV_RANK = 128
C_NOPE = 64
C_ROPE = 32
D_HEADS = 4
D_KV_HEADS = 2
WINDOW = 128
N_EXPERTS = 16
EXPERTS_PER_GROUP = 4
N_EXPERT_GROUPS = 4
D_EXPERT = 256

TQ = 256
TK = 256
TK_LAT = 1024
BOUNDED_PIECES = 8
BOUND_SLACK = 1.0 + 2.0 ** -7
MIN_DENOMINATOR = 2.0 ** -80
SWA_T = 128
LANES = 128
ONES_ROWS = 16
VMEM_LIMIT = 56 * 1024 * 1024

_C_QA, _C_QAS, _C_KA, _C_KAS = 0, 256, 512, 768
_C_QB, _C_QBS, _C_KB, _C_KBS = 1024, 1280, 1536, 1664
_C_QD, _C_QDS, _C_KD, _C_KDS = 1792, 2048, 2304, 2432
_C_CQ, _C_CKV, _C_KR, _C_KRS = 2560, 2816, 2944, 3072
_C_TOTAL = 3200
_R_VA, _R_VB, _R_VD, _R_TOTAL = 0, 256, 384, 512


def _nt_dot(a, b):
    return lax.dot_general(a, b, (((1,), (1,)), ((), ())), preferred_element_type=F32)


def _split_bf16(a):
    hi = a.astype(BF16)
    lo = (a - hi.astype(F32)).astype(BF16)
    return hi, lo


def _layer_norm(x):
    mu = jnp.mean(x, axis=-1, keepdims=True)
    xc = x - mu
    var = jnp.mean(xc * xc, axis=-1, keepdims=True)
    return xc * lax.rsqrt(var + EPS)


def _tile_lanes(a, n):
    return jnp.concatenate([a] * n, axis=1)


def _ada_kernel(cc_ref, w_ref, b_ref, o_ref):
    cc = cc_ref[...]
    s = cc * jax.nn.sigmoid(cc)
    s_hi, s_lo = _split_bf16(s)
    w_hi, w_lo = _split_bf16(w_ref[...])
    acc = jnp.dot(s_hi, w_hi, preferred_element_type=F32)
    acc += jnp.dot(s_hi, w_lo, preferred_element_type=F32)
    acc += jnp.dot(s_lo, w_hi, preferred_element_type=F32)
    o_ref[...] = acc + b_ref[...]


def _ada(cc, w_ada, b_ada):
    depth, d, n6 = w_ada.shape
    bn = 1536
    return pl.pallas_call(
        _ada_kernel,
        grid=(depth, n6 // bn),
        in_specs=[
            pl.BlockSpec((8, d), lambda l, j: (0, 0)),
            pl.BlockSpec((None, d, bn), lambda l, j: (l, 0, j)),
            pl.BlockSpec((None, 1, bn), lambda l, j: (l, 0, j)),
        ],
        out_specs=pl.BlockSpec((None, 8, bn), lambda l, j: (l, 0, j)),
        out_shape=jax.ShapeDtypeStruct((depth, 8, n6), F32),
        compiler_params=pltpu.CompilerParams(
            dimension_semantics=("arbitrary", "arbitrary"), vmem_limit_bytes=VMEM_LIMIT),
        name="ada",
    )(cc, w_ada, b_ada.reshape(depth, 1, n6))


def _block_diag_ones(n, blk):
    sh = int(math.log2(blk))
    r = lax.broadcasted_iota(jnp.int32, (n, n), 0) >> sh
    c = lax.broadcasted_iota(jnp.int32, (n, n), 1) >> sh
    return jnp.where(r == c, 1.0, 0.0).astype(BF16)


def _proj_kernel(x_ref, mod_ref, wrm_ref, wt_ref, wuq_ref, wuqs_ref, wukn_ref, wuvt_ref,
                 gqb_ref, gqbs_ref, gkb_ref, gkbs_ref, gcq_ref, gckv_ref,
                 cos32_ref, sin32_ref, cos64_ref, sin64_ref, cosc_ref, sinc_ref,
                 qa_ref, ka_ref, vat_ref, qb_ref, kb_ref, vbt_ref,
                 qc_ref, kc_ref, vct_ref, qd_ref, kd_ref, vdt_ref):
    d = D_MODEL
    xn = _layer_norm(x_ref[...])
    h = (xn * (1.0 + mod_ref[:, d:2 * d]) + mod_ref[:, 0:d]).astype(BF16)

    def cols(lo, width):
        return jnp.dot(h, wrm_ref[:, lo:lo + width], preferred_element_type=F32)

    cos32, sin32 = _tile_lanes(cos32_ref[...], 2), _tile_lanes(sin32_ref[...], 2)
    cos64, sin64 = cos64_ref[...], sin64_ref[...]
    cos64w, sin64w = _tile_lanes(cos64, 2), _tile_lanes(sin64, 2)

    sa = (A_QK ** -0.5) * LOG2E
    qa_ref[...] = ((cols(_C_QA, 256) * cos32 + cols(_C_QAS, 256) * sin32) * sa).astype(BF16)
    ka_ref[...] = (cols(_C_KA, 256) * cos32 + cols(_C_KAS, 256) * sin32).astype(BF16)

    sb = (HEAD_DIM ** -0.5) * LOG2E
    q = cols(_C_QB, 256)
    ssq = jnp.dot((q * q).astype(BF16), _block_diag_ones(256, HEAD_DIM), preferred_element_type=F32)
    r = lax.rsqrt(ssq * (1.0 / HEAD_DIM) + EPS)
    qr = (q * gqb_ref[...]) * cos64w + (cols(_C_QBS, 256) * gqbs_ref[...]) * sin64w
    qb_ref[...] = (qr * (r * sb)).astype(BF16)
    k = cols(_C_KB, 128)
    ssq = jnp.dot((k * k).astype(BF16), _block_diag_ones(128, HEAD_DIM), preferred_element_type=F32)
    r = lax.rsqrt(ssq * (1.0 / HEAD_DIM) + EPS)
    kr = (k * gkb_ref[...]) * cos64 + (cols(_C_KBS, 128) * gkbs_ref[...]) * sin64
    kb_ref[...] = (kr * r).astype(BF16)

    qd_ref[...] = ((cols(_C_QD, 256) * cos64w + cols(_C_QDS, 256) * sin64w) * sb).astype(BF16)
    kd_ref[...] = (cols(_C_KD, 128) * cos64 + cols(_C_KDS, 128) * sin64).astype(BF16)

    sc = ((C_NOPE + C_ROPE) ** -0.5) * LOG2E
    cosc, sinc = cosc_ref[...], sinc_ref[...]
    cq = cols(_C_CQ, 256)
    ms = jnp.sum(cq * cq, axis=-1, keepdims=True) * (1.0 / C_Q_RANK)
    cqn = (cq * lax.rsqrt(ms + EPS) * gcq_ref[...]).astype(BF16)
    qn = jnp.dot(cqn, wuq_ref[...], preferred_element_type=F32)
    qs = jnp.dot(cqn, wuqs_ref[...], preferred_element_type=F32)
    qc_ref[...] = ((qn * _tile_lanes(cosc, 4) + qs * _tile_lanes(sinc, 4)) * sc).astype(BF16)
    ckv = cols(_C_CKV, 128)
    ms = jnp.mean(ckv * ckv, axis=-1, keepdims=True)
    ckvn = (ckv * lax.rsqrt(ms + EPS) * gckv_ref[...]).astype(BF16)
    kn = jnp.dot(ckvn, wukn_ref[...], preferred_element_type=F32)
    krr = cols(_C_KR, 128) * cosc + cols(_C_KRS, 128) * sinc
    kc_ref[...] = (kn + _tile_lanes(krr, 4)).astype(BF16)
    vct_ref[...] = _nt_dot(wuvt_ref[...], ckvn).astype(BF16)

    vt = _nt_dot(wt_ref[...], h)
    vat_ref[...] = vt[_R_VA:_R_VA + 256].astype(BF16)
    vbt_ref[...] = vt[_R_VB:_R_VB + 128].astype(BF16)
    vdt_ref[...] = vt[_R_VD:_R_VD + 128].astype(BF16)


def _proj(xa, modsel, lw, tabs):
    b, nt, d = xa.shape
    nc = nt // TQ

    def full(a):
        nd = a.ndim
        return pl.BlockSpec(a.shape, lambda bi, i, _n=nd: (0,) * _n)

    def tab(a):
        return pl.BlockSpec((TQ, LANES), lambda bi, i: (i, 0))

    def rm(w):
        return pl.BlockSpec((None, TQ, w), lambda bi, i: (bi, i, 0))

    def tr(w):
        return pl.BlockSpec((None, None, w, TQ), lambda bi, i: (bi, i, 0, 0))

    weights = [lw["wrm"], lw["wt"], lw["wuq"], lw["wuqs"], lw["wukn"], lw["wuvt"],
               lw["gqb"], lw["gqbs"], lw["gkb"], lw["gkbs"], lw["gcq"], lw["gckv"]]
    tables = [tabs["cos32"], tabs["sin32"], tabs["cos64"], tabs["sin64"], tabs["cosc"], tabs["sinc"]]
    in_specs = ([pl.BlockSpec((None, TQ, d), lambda bi, i: (bi, i, 0)),
                 pl.BlockSpec((None, None, 1, 6 * d), lambda bi, i: (bi, jnp.minimum(i, 1), 0, 0))]
                + [full(w) for w in weights] + [tab(t) for t in tables])
    widths = [("rm", 256), ("rm", 256), ("tr", 256), ("rm", 256), ("rm", 128), ("tr", 128),
              ("rm", 512), ("rm", 512), ("tr", 256), ("rm", 256), ("rm", 128), ("tr", 128)]
    out_specs, out_shape = [], []
    for kind, w in widths:
        if kind == "rm":
            out_specs.append(rm(w))
            out_shape.append(jax.ShapeDtypeStruct((b, nt, w), BF16))
        else:
            out_specs.append(tr(w))
            out_shape.append(jax.ShapeDtypeStruct((b, nc, w, TQ), BF16))
    return pl.pallas_call(
        _proj_kernel,
        grid=(b, nc),
        in_specs=in_specs,
        out_specs=out_specs,
        out_shape=out_shape,
        compiler_params=pltpu.CompilerParams(
            dimension_semantics=("arbitrary", "arbitrary"), vmem_limit_bytes=VMEM_LIMIT),
        name="proj",
    )(xa, modsel, *weights, *tables)


def _flash_loop(qi, nc, k_ref, vt_ref, qz_ref, m_ref, alpha_ref, acc_ref, s0_ref, s1_ref, kabs_ref, k_lanes, pv_cfg,
                slab_rows):
    per = TK_LAT // TK
    n_lat = (nc - 1) // per
    assert n_lat % 2 == 0 and n_lat >= 2
    same_lanes = all(kl == k_lanes[0] for kl in k_lanes)

    def scores(kc):
        if same_lanes:
            return jnp.dot(kc[:, k_lanes[0][0]:k_lanes[0][1]], qz_ref[...], preferred_element_type=F32)
        return jnp.concatenate(
            [jnp.dot(kc[:, lo:hi], qz_ref[:, TQ * j:TQ * (j + 1)], preferred_element_type=F32)
             for j, (lo, hi) in enumerate(k_lanes)], axis=1)

    def small_chunk(pieces, first):
        if slab_rows >= pieces * TK:
            return (1, pieces, first)
        return (pieces * TK // slab_rows, slab_rows // TK, first)

    ctx_chunk = small_chunk(1, 0)

    def lat_chunk(c):
        return small_chunk(per, 1 + c * per)

    def stats_update(mx):
        m_old = m_ref[...]
        m_new = jnp.maximum(m_old, mx)
        alpha_ref[...] = jnp.exp2(m_old - m_new)
        m_ref[...] = m_new

    def pipelined(cur, cur_ref, nxt, nxt_ref):
        m_cur, alpha = m_ref[...], alpha_ref[...]
        mx, pv = None, [None] * len(pv_cfg)
        n_cur = cur[0] if cur is not None else 0
        n_nxt = nxt[0] if nxt is not None else 0
        for r in range(max(n_cur, n_nxt)):
            if r < n_nxt:
                rows = TK * nxt[1]
                kc = k_ref[pl.ds(pl.multiple_of((nxt[2] + r * nxt[1]) * TK, TK), rows), :]
                s = scores(kc)
                nxt_ref[rows * r:rows * (r + 1), :] = s
                smx = jnp.max(s, axis=0, keepdims=True)
                mx = smx if mx is None else jnp.maximum(mx, smx)
            if r < n_cur:
                rows = TK * cur[1]
                p = jnp.exp2(cur_ref[rows * r:rows * (r + 1), :] - m_cur).astype(BF16)
                vc = jnp.concatenate([vt_ref[cur[2] + r * cur[1] + i] for i in range(cur[1])], axis=1)
                ones = jnp.ones((ONES_ROWS, rows), BF16)
                for g, (vl, vh, j0, nj) in enumerate(pv_cfg):
                    va = jnp.concatenate([vc[vl:vh, :], ones], axis=0)
                    d = jnp.dot(va, p[:, TQ * j0:TQ * (j0 + nj)], preferred_element_type=F32)
                    pv[g] = d if pv[g] is None else pv[g] + d
        if cur is not None:
            for g, (vl, vh, j0, nj) in enumerate(pv_cfg):
                cols = slice(TQ * j0, TQ * (j0 + nj))
                acc_ref[:, cols] = alpha[:, cols] * acc_ref[:, cols] + pv[g]
        if nxt is not None:
            stats_update(mx)

    def online_pass():
        m_ref[...] = jnp.full(m_ref.shape, NEG, F32)
        acc_ref[...] = jnp.zeros(acc_ref.shape, F32)

        @pl.when(qi == 0)
        def _():
            pipelined(None, None, ctx_chunk, s0_ref)
            pipelined(ctx_chunk, s0_ref, None, None)

        @pl.when(qi > 0)
        def _():
            pipelined(None, None, lat_chunk(0), s0_ref)

            def body(i, carry):
                c = 2 * i
                pipelined(lat_chunk(c), s0_ref, lat_chunk(c + 1), s1_ref)
                pipelined(lat_chunk(c + 1), s1_ref, lat_chunk(c + 2), s0_ref)
                return carry

            lax.fori_loop(0, n_lat // 2 - 1, body, 0)
            pipelined(lat_chunk(n_lat - 2), s0_ref, lat_chunk(n_lat - 1), s1_ref)
            pipelined(lat_chunk(n_lat - 1), s1_ref, ctx_chunk, s0_ref)
            pipelined(ctx_chunk, s0_ref, None, None)

    def bounded_pass():
        wk = kabs_ref.shape[1]
        kab = jnp.broadcast_to(kabs_ref[...], (ONES_ROWS, wk)).astype(BF16)
        aq = jnp.abs(qz_ref[...])
        if same_lanes:
            lo, hi = k_lanes[0]
            bound = jnp.dot(kab[:, lo:hi], aq, preferred_element_type=F32)[0:1]
        else:
            bound = jnp.concatenate(
                [jnp.dot(kab[:, lo:hi], aq[:, TQ * j:TQ * (j + 1)], preferred_element_type=F32)[0:1]
                 for j, (lo, hi) in enumerate(k_lanes)], axis=1)
        bound = bound * BOUND_SLACK
        ones = jnp.ones((ONES_ROWS, TK), BF16)
        acc_ref[...] = jnp.zeros(acc_ref.shape, F32)

        def pieces(first, n):
            pv = [None] * len(pv_cfg)

            def probs(r):
                kc = k_ref[pl.ds(pl.multiple_of((first + r) * TK, TK), TK), :]
                return jnp.exp2(scores(kc) - bound).astype(BF16)

            def values(r, p):
                vc = vt_ref[first + r]
                for g, (vl, vh, j0, nj) in enumerate(pv_cfg):
                    va = jnp.concatenate([vc[vl:vh, :], ones], axis=0)
                    d = jnp.dot(va, p[:, TQ * j0:TQ * (j0 + nj)], preferred_element_type=F32)
                    pv[g] = d if pv[g] is None else pv[g] + d

            p_prev = probs(0)
            for r in range(1, n):
                p_cur = probs(r)
                values(r - 1, p_prev)
                p_prev = p_cur
            values(n - 1, p_prev)
            for g, (vl, vh, j0, nj) in enumerate(pv_cfg):
                cols = slice(TQ * j0, TQ * (j0 + nj))
                acc_ref[:, cols] = acc_ref[:, cols] + pv[g]

        pieces(0, 1)

        @pl.when(qi > 0)
        def _():
            def body(i, carry):
                pieces(1 + i * BOUNDED_PIECES, BOUNDED_PIECES)
                return carry

            lax.fori_loop(0, (nc - 1) // BOUNDED_PIECES, body, 0)

    @pl.when(qi == 0)
    def _():
        def body(c, mx):
            blk = k_ref[pl.ds(pl.multiple_of(c * TK, TK), TK), :].astype(F32)
            return jnp.maximum(mx, jnp.max(jnp.abs(blk), axis=0, keepdims=True))

        kabs_ref[...] = lax.fori_loop(0, nc, body, jnp.zeros(kabs_ref.shape, F32))

    bounded_pass()
    dv = acc_ref.shape[0] - ONES_ROWS
    healthy = jnp.min(jnp.where(acc_ref[dv:dv + 1, :] >= MIN_DENOMINATOR, 1.0, 0.0))

    @pl.when(healthy < 0.5)
    def _():
        online_pass()


def _normalized(acc_ref, j, dv):
    a = acc_ref[:, TQ * j:TQ * (j + 1)]
    return a[0:dv] / a[dv:dv + 1]


def _attn_a_kernel(q_ref, k_ref, vt_ref, lq1_ref, lk1_ref, lq2_ref, lk2_ref, g_ref, o_ref,
                   qz_ref, m_ref, alpha_ref, acc_ref, s0_ref, s1_ref, kabs_ref, *, nc, lam_init):
    qi = pl.program_id(1)
    qt = q_ref[...].astype(F32).T
    row = lax.broadcasted_iota(jnp.int32, qt.shape, 0)
    for j in range(2 * A_HEADS):
        keep = (row >= A_QK * j) & (row < A_QK * (j + 1))
        qz_ref[:, TQ * j:TQ * (j + 1)] = jnp.where(keep, qt, 0.0).astype(BF16)
    k_lanes = [(0, 256)] * (2 * A_HEADS)
    pv_cfg = [(HEAD_DIM * hd, HEAD_DIM * (hd + 1), 2 * hd, 2) for hd in range(A_HEADS)]
    _flash_loop(qi, nc, k_ref, vt_ref, qz_ref, m_ref, alpha_ref, acc_ref, s0_ref, s1_ref, kabs_ref, k_lanes, pv_cfg,
                slab_rows=1024)

    lam = (jnp.exp(jnp.sum(lq1_ref[...] * lk1_ref[...], axis=-1, keepdims=True))
           - jnp.exp(jnp.sum(lq2_ref[...] * lk2_ref[...], axis=-1, keepdims=True)) + lam_init)
    outs = []
    for hd in range(A_HEADS):
        o = _normalized(acc_ref, 2 * hd, HEAD_DIM) - lam * _normalized(acc_ref, 2 * hd + 1, HEAD_DIM)
        ms = jnp.mean(o * o, axis=0, keepdims=True)
        outs.append(o * lax.rsqrt(ms + EPS) * g_ref[...] * (1.0 - lam_init))
    o_ref[...] = jnp.concatenate(outs, axis=0).T.astype(BF16)


def _attn_b_kernel(q_ref, k_ref, vt_ref, o_ref, qz_ref, m_ref, alpha_ref, acc_ref, s0_ref, s1_ref, kabs_ref, *, nc):
    qi = pl.program_id(1)
    qt = q_ref[...].astype(F32).T
    zeros = jnp.zeros((HEAD_DIM, TQ), F32)
    rep = B_HEADS // B_KV_HEADS
    for hd in range(B_HEADS):
        parts = [zeros] * B_KV_HEADS
        parts[hd // rep] = qt[HEAD_DIM * hd:HEAD_DIM * (hd + 1)]
        qz_ref[:, TQ * hd:TQ * (hd + 1)] = jnp.concatenate(parts, axis=0).astype(BF16)
    k_lanes = [(0, 128)] * B_HEADS
    pv_cfg = [(HEAD_DIM * g, HEAD_DIM * (g + 1), rep * g, rep) for g in range(B_KV_HEADS)]
    _flash_loop(qi, nc, k_ref, vt_ref, qz_ref, m_ref, alpha_ref, acc_ref, s0_ref, s1_ref, kabs_ref, k_lanes, pv_cfg,
                slab_rows=256)
    outs = [_normalized(acc_ref, hd, HEAD_DIM) for hd in range(B_HEADS)]
    o_ref[...] = jnp.concatenate(outs, axis=0).T.astype(BF16)


def _attn_c_kernel(q_ref, k_ref, vt_ref, o_ref, qz_ref, m_ref, alpha_ref, acc_ref, s0_ref, s1_ref, kabs_ref, *, nc):
    qi = pl.program_id(1)
    qt = q_ref[...].astype(F32).T
    k_lanes, pv_cfg = [], []
    for hd in range(C_HEADS):
        qz_ref[:, TQ * hd:TQ * (hd + 1)] = qt[LANES * hd:LANES * (hd + 1)].astype(BF16)
        k_lanes.append((LANES * hd, LANES * (hd + 1)))
        pv_cfg.append((HEAD_DIM * hd, HEAD_DIM * (hd + 1), hd, 1))
    _flash_loop(qi, nc, k_ref, vt_ref, qz_ref, m_ref, alpha_ref, acc_ref, s0_ref, s1_ref, kabs_ref, k_lanes, pv_cfg,
                slab_rows=256)
    outs = [_normalized(acc_ref, hd, HEAD_DIM) for hd in range(C_HEADS)]
    o_ref[...] = jnp.concatenate(outs, axis=0).T.astype(BF16)


def _attn_full(kind, q, k, vt, extras=(), lam_init=0.0):
    b, nt, wq = q.shape
    wk = k.shape[-1]
    nc, wv = vt.shape[1], vt.shape[2]
    assert (nt - TK) % TK_LAT == 0
    if kind == "a":
        body, n_sm, dk = functools.partial(_attn_a_kernel, nc=nc, lam_init=lam_init), 2 * A_HEADS, 256
    elif kind == "b":
        body, n_sm, dk = functools.partial(_attn_b_kernel, nc=nc), B_HEADS, 128
    else:
        body, n_sm, dk = functools.partial(_attn_c_kernel, nc=nc), C_HEADS, 128
    resident = pl.Buffered(1)
    in_specs = [
        pl.BlockSpec((None, TQ, wq), lambda bi, i: (bi, i, 0)),
        pl.BlockSpec((None, nt, wk), lambda bi, i: (bi, 0, 0), pipeline_mode=resident),
        pl.BlockSpec((None, nc, wv, TK), lambda bi, i: (bi, 0, 0, 0), pipeline_mode=resident),
    ] + [pl.BlockSpec(e.shape, lambda bi, i: (0, 0)) for e in extras]
    return pl.pallas_call(
        body,
        grid=(b, nc),
        in_specs=in_specs,
        out_specs=pl.BlockSpec((None, TQ, GROUP_WIDTH), lambda bi, i: (bi, i, 0)),
        out_shape=jax.ShapeDtypeStruct((b, nt, GROUP_WIDTH), BF16),
        scratch_shapes=[
            pltpu.VMEM((dk, n_sm * TQ), BF16),
            pltpu.VMEM((1, n_sm * TQ), F32),
            pltpu.VMEM((1, n_sm * TQ), F32),
            pltpu.VMEM((HEAD_DIM + ONES_ROWS, n_sm * TQ), F32),
            pltpu.VMEM((TK_LAT, n_sm * TQ), F32),
            pltpu.VMEM((TK_LAT, n_sm * TQ), F32),
            pltpu.VMEM((1, wk), F32),
        ],
        compiler_params=pltpu.CompilerParams(
            dimension_semantics=("arbitrary", "arbitrary"), vmem_limit_bytes=VMEM_LIMIT),
        name="attn_" + kind,
    )(q, k, vt, *extras)


def _attn_d_kernel(q_ref, kc_ref, kp_ref, ko_ref, kn_ref, vc_ref, vp_ref, vo_ref, vn_ref, sink_ref,
                   o_ref, *, nb, ctx_blocks):
    n = pl.program_id(1)
    t = SWA_T
    rep = D_HEADS // D_KV_HEADS
    qt = q_ref[...].astype(F32).T
    zeros = jnp.zeros((HEAD_DIM, t), F32)
    cols = []
    for hd in range(D_HEADS):
        parts = [zeros] * D_KV_HEADS
        parts[hd // rep] = qt[HEAD_DIM * hd:HEAD_DIM * (hd + 1)]
        cols.append(jnp.concatenate(parts, axis=0))
    qz = jnp.concatenate(cols, axis=1).astype(BF16)
    w = D_HEADS * t
    ko_i = lax.broadcasted_iota(jnp.int32, (t, w), 0)
    qo_i = lax.broadcasted_iota(jnp.int32, (t, w), 1) & (t - 1)
    band = n >= ctx_blocks
    ok_prev = jnp.logical_and(n >= ctx_blocks + 1, qo_i <= ko_i)
    ok_own = jnp.logical_and(band, ko_i >= 0)
    ok_next = jnp.logical_and(jnp.logical_and(band, n + 1 <= nb - 1), ko_i <= qo_i)
    s_c = jnp.dot(kc_ref[...], qz, preferred_element_type=F32)
    s_p = jnp.where(ok_prev, jnp.dot(kp_ref[...], qz, preferred_element_type=F32), NEG)
    s_o = jnp.where(ok_own, jnp.dot(ko_ref[...], qz, preferred_element_type=F32), NEG)
    s_n = jnp.where(ok_next, jnp.dot(kn_ref[...], qz, preferred_element_type=F32), NEG)
    sk = jnp.concatenate([jnp.broadcast_to(sink_ref[0:1, hd:hd + 1] * LOG2E, (1, t))
                          for hd in range(D_HEADS)], axis=1)
    m = jnp.maximum(jnp.max(s_c, axis=0, keepdims=True), jnp.max(s_p, axis=0, keepdims=True))
    m = jnp.maximum(m, jnp.max(s_o, axis=0, keepdims=True))
    m = jnp.maximum(m, jnp.max(s_n, axis=0, keepdims=True))
    m = jnp.maximum(m, sk)
    pieces = [(vc_ref[...], jnp.exp2(s_c - m).astype(BF16)), (vp_ref[...], jnp.exp2(s_p - m).astype(BF16)),
              (vo_ref[...], jnp.exp2(s_o - m).astype(BF16)), (vn_ref[...], jnp.exp2(s_n - m).astype(BF16))]
    l_sink = jnp.exp2(sk - m)
    outs = []
    for g in range(D_KV_HEADS):
        lanes = slice(rep * t * g, rep * t * (g + 1))
        o = None
        for v, p in pieces:
            va = jnp.concatenate([v[HEAD_DIM * g:HEAD_DIM * (g + 1)],
                                  jnp.ones((ONES_ROWS, v.shape[1]), BF16)], axis=0)
            d = jnp.dot(va, p[:, lanes], preferred_element_type=F32)
            o = d if o is None else o + d
        o = o[0:HEAD_DIM] / (o[HEAD_DIM:HEAD_DIM + 1] + l_sink[:, lanes])
        outs += [o[:, t * r:t * (r + 1)] for r in range(rep)]
    o_ref[...] = jnp.concatenate(outs, axis=0).T.astype(BF16)


def _attn_d(q, k, vt, sink, ctx_len):
    b, nt, wq = q.shape
    wk = k.shape[-1]
    wv = vt.shape[2]
    t = SWA_T
    nb = nt // t
    per = TQ // t
    ctx_blocks = ctx_len // t
    assert ctx_len == TQ

    def kspec(shift):
        return pl.BlockSpec((None, t, wk), lambda bi, i: (bi, jnp.clip(i + shift, 0, nb - 1), 0))

    def vspec(shift):
        def imap(bi, i):
            j = jnp.clip(i + shift, 0, nb - 1)
            return (bi, j // per, 0, j % per)
        return pl.BlockSpec((None, None, wv, t), imap)

    in_specs = [
        pl.BlockSpec((None, t, wq), lambda bi, i: (bi, i, 0)),
        pl.BlockSpec((None, ctx_len, wk), lambda bi, i: (bi, 0, 0)),
        kspec(-1), kspec(0), kspec(1),
        pl.BlockSpec((None, None, wv, ctx_len), lambda bi, i: (bi, 0, 0, 0)),
        vspec(-1), vspec(0), vspec(1),
        pl.BlockSpec(sink.shape, lambda bi, i: (0, 0)),
    ]
    return pl.pallas_call(
        functools.partial(_attn_d_kernel, nb=nb, ctx_blocks=ctx_blocks),
        grid=(b, nb),
        in_specs=in_specs,
        out_specs=pl.BlockSpec((None, t, GROUP_WIDTH), lambda bi, i: (bi, i, 0)),
        out_shape=jax.ShapeDtypeStruct((b, nt, GROUP_WIDTH), BF16),
        compiler_params=pltpu.CompilerParams(
            dimension_semantics=("arbitrary", "arbitrary"), vmem_limit_bytes=VMEM_LIMIT),
        name="attn_d",
    )(q, k, k, k, k, vt, vt, vt, vt, sink)


def _route(sel_rows, s_rows):
    epg = EXPERTS_PER_GROUP
    gscore = []
    for g in range(N_EXPERT_GROUPS):
        r = sel_rows[epg * g:epg * (g + 1)]
        pair = None
        for i in range(epg):
            for j in range(i + 1, epg):
                v = r[i] + r[j]
                pair = v if pair is None else jnp.maximum(pair, v)
        gscore.append(pair)
    best, best_g = gscore[0], jnp.zeros_like(gscore[0], dtype=jnp.int32)
    for g in range(1, N_EXPERT_GROUPS):
        better = gscore[g] > best
        best_g = jnp.where(better, g, best_g)
        best = jnp.where(better, gscore[g], best)
    w = []
    for e in range(N_EXPERTS):
        g = e // epg
        cnt = jnp.zeros_like(best_g)
        for e2 in range(epg * g, epg * (g + 1)):
            if e2 == e:
                continue
            beats = sel_rows[e2] > sel_rows[e]
            if e2 < e:
                beats = jnp.logical_or(beats, sel_rows[e2] == sel_rows[e])
            cnt = cnt + jnp.where(beats, 1, 0)
        chosen = jnp.logical_and(best_g == g, cnt < 2)
        w.append(jnp.where(chosen, s_rows[e], 0.0))
    tot = w[0]
    for e in range(1, N_EXPERTS):
        tot = tot + w[e]
    return [we / tot for we in w]


def _outproj_kernel(oa_ref, ob_ref, oc_ref, od_ref, x_ref, mod_ref, wout_ref, g_ref, b_ref,
                    rwt_ref, rb_ref, x1_ref, h2_ref, gates_ref, *, alpha):
    d = D_MODEL
    o = jnp.concatenate([oa_ref[...], ob_ref[...], oc_ref[...], od_ref[...]], axis=1)
    y = jnp.dot(o, wout_ref[...], preferred_element_type=F32)
    u = alpha * x_ref[...] + mod_ref[:, 2 * d:3 * d] * y
    x1 = _layer_norm(u) * g_ref[...] + b_ref[...]
    x1_ref[...] = x1
    h2 = _layer_norm(x1) * (1.0 + mod_ref[:, 4 * d:5 * d]) + mod_ref[:, 3 * d:4 * d]
    h2_ref[...] = h2.astype(BF16)

    h_hi, h_lo = _split_bf16(h2)
    w_hi, w_lo = _split_bf16(rwt_ref[...])
    logits = _nt_dot(w_hi, h_hi) + _nt_dot(w_hi, h_lo) + _nt_dot(w_lo, h_hi)
    s = jax.nn.sigmoid(logits)
    sel = s + rb_ref[...]
    s_rows = [s[e:e + 1] for e in range(N_EXPERTS)]
    sel_rows = [sel[e:e + 1] for e in range(N_EXPERTS)]
    gates = _route(sel_rows, s_rows)
    gt = jnp.concatenate(gates + [jnp.zeros((LANES - N_EXPERTS, TQ), F32)], axis=0)
    gates_ref[...] = gt.T


def _outproj(oa, ob, oc, od, xa, modsel, w_out, ln_g, ln_b, rwt, rb, alpha):
    b, nt, d = xa.shape
    nc = nt // TQ

    def rm(w):
        return pl.BlockSpec((None, TQ, w), lambda bi, i: (bi, i, 0))

    def full(a):
        nd = a.ndim
        return pl.BlockSpec(a.shape, lambda bi, i, _n=nd: (0,) * _n)

    in_specs = [rm(GROUP_WIDTH) for _ in range(4)] + [
        rm(d),
        pl.BlockSpec((None, None, 1, 6 * d), lambda bi, i: (bi, jnp.minimum(i, 1), 0, 0)),
        full(w_out), full(ln_g), full(ln_b), full(rwt), full(rb)]
    return pl.pallas_call(
        functools.partial(_outproj_kernel, alpha=alpha),
        grid=(b, nc),
        in_specs=in_specs,
        out_specs=[rm(d), rm(d), rm(LANES)],
        out_shape=[jax.ShapeDtypeStruct((b, nt, d), F32), jax.ShapeDtypeStruct((b, nt, d), BF16),
                   jax.ShapeDtypeStruct((b, nt, LANES), F32)],
        compiler_params=pltpu.CompilerParams(
            dimension_semantics=("arbitrary", "arbitrary"), vmem_limit_bytes=VMEM_LIMIT),
        name="outproj",
    )(oa, ob, oc, od, xa, modsel, w_out, ln_g, ln_b, rwt, rb)


def _moe_kernel(h_ref, gates_ref, x_ref, mod_ref, wg_ref, wu_ref, wd_ref, g_ref, b_ref, o_ref, *, alpha):
    d = D_MODEL
    h = h_ref[...]
    gates = gates_ref[...]
    y = jnp.zeros((TQ, d), F32)
    for g in range(N_EXPERT_GROUPS):
        acts = []
        for el in range(EXPERTS_PER_GROUP):
            e = EXPERTS_PER_GROUP * g + el
            hg = jnp.dot(h, wg_ref[e], preferred_element_type=F32)
            hu = jnp.dot(h, wu_ref[e], preferred_element_type=F32)
            a = hg * jax.nn.sigmoid(hg) * hu * gates[:, e:e + 1]
            acts.append(a.astype(BF16))
        y = y + jnp.dot(jnp.concatenate(acts, axis=1), wd_ref[g], preferred_element_type=F32)
    u = alpha * x_ref[...] + mod_ref[:, 5 * d:6 * d] * y
    o_ref[...] = _layer_norm(u) * g_ref[...] + b_ref[...]


def _moe(h2, gates, x1, modsel, wg, wu, wd, ln_g, ln_b, alpha, latents_only):
    b, nt, d = x1.shape
    nc = nt // TQ
    resident = pl.Buffered(1)
    if latents_only:
        out_spec = pl.BlockSpec((None, TQ, d), lambda bi, i: (bi, jnp.maximum(i - 1, 0), 0))
        out_rows = nt - TQ
    else:
        out_spec = pl.BlockSpec((None, TQ, d), lambda bi, i: (bi, i, 0))
        out_rows = nt

    def rm(w):
        return pl.BlockSpec((None, TQ, w), lambda bi, i: (bi, i, 0))

    def res(a):
        nd = a.ndim
        return pl.BlockSpec(a.shape, lambda bi, i, _n=nd: (0,) * _n, pipeline_mode=resident)

    in_specs = [rm(d), rm(LANES), rm(d),
                pl.BlockSpec((None, None, 1, 6 * d), lambda bi, i: (bi, jnp.minimum(i, 1), 0, 0)),
                res(wg), res(wu), res(wd), res(ln_g), res(ln_b)]
    return pl.pallas_call(
        functools.partial(_moe_kernel, alpha=alpha),
        grid=(b, nc),
        in_specs=in_specs,
        out_specs=out_spec,
        out_shape=jax.ShapeDtypeStruct((b, out_rows, d), F32),
        compiler_params=pltpu.CompilerParams(
            dimension_semantics=("arbitrary", "arbitrary"), vmem_limit_bytes=VMEM_LIMIT),
        name="moe",
    )(h2, gates, x1, modsel, wg, wu, wd, ln_g, ln_b)


def _rope_perm(rot_dim):
    n = rot_dim // 4
    j = np.arange(rot_dim)
    return j ^ n, np.where((j // n) % 2 == 0, -1.0, 1.0).astype(np.float32)


def _swapped(w, rot_dim):
    perm, sign = _rope_perm(rot_dim)
    cols = w.shape[-1]
    idx = (np.arange(cols) // rot_dim) * rot_dim + perm[np.arange(cols) % rot_dim]
    sgn = sign[np.arange(cols) % rot_dim]
    return w[..., idx] * sgn


def _rope_tables(n_lat, ctx_len):
    t = jnp.arange(n_lat, dtype=jnp.int32)
    row, col = (t // GRID_W).astype(F32), (t % GRID_W).astype(F32)

    def pattern(rot_dim):
        n = rot_dim // 4
        inv = ROPE_THETA ** (-jnp.arange(n, dtype=F32) / n)
        ar, ac = row[:, None] * inv, col[:, None] * inv
        ang = jnp.concatenate([ar, ar, ac, ac], axis=-1)
        return jnp.cos(ang), jnp.sin(ang)

    def with_ctx(a, fill):
        return jnp.concatenate([jnp.full((ctx_len, a.shape[1]), fill, F32), a], axis=0)

    c32, s32 = pattern(A_QK)
    c64, s64 = pattern(HEAD_DIM)
    ones = jnp.ones((n_lat, C_NOPE), F32)
    zeros = jnp.zeros((n_lat, C_NOPE), F32)
    pad1 = jnp.ones((n_lat, LANES - C_NOPE - C_ROPE), F32)
    pad0 = jnp.zeros((n_lat, LANES - C_NOPE - C_ROPE), F32)
    return {
        "cos32": with_ctx(jnp.tile(c32, (1, LANES // A_QK)), 1.0),
        "sin32": with_ctx(jnp.tile(s32, (1, LANES // A_QK)), 0.0),
        "cos64": with_ctx(jnp.tile(c64, (1, LANES // HEAD_DIM)), 1.0),
        "sin64": with_ctx(jnp.tile(s64, (1, LANES // HEAD_DIM)), 0.0),
        "cosc": with_ctx(jnp.concatenate([ones, c32, pad1], axis=-1), 1.0),
        "sinc": with_ctx(jnp.concatenate([zeros, s32, pad0], axis=-1), 0.0),
    }


def _prep_layer_weights(w_in, gq, gk, gcq, gckv, w_uq, w_ukv):
    d = w_in.shape[0]
    splits = np.cumsum([256, 256, 256, 256, 128, 128, C_Q_RANK, C_KV_RANK, C_ROPE, 256, 128, 128])[:-1]
    (a_q, a_k, a_v, b_q, b_k, b_v, c_q, c_kv, c_kr, d_q, d_k, d_v) = jnp.split(w_in, splits, axis=1)
    z = lambda n: jnp.zeros((d, n), F32)
    kr4 = jnp.concatenate([z(C_NOPE), c_kr, z(LANES - C_NOPE - C_ROPE)], axis=1)
    kr4s = jnp.concatenate([z(C_NOPE), _swapped(c_kr, C_ROPE), z(LANES - C_NOPE - C_ROPE)], axis=1)
    wrm = jnp.concatenate([
        a_q, _swapped(a_q, A_QK), a_k, _swapped(a_k, A_QK),
        b_q, _swapped(b_q, HEAD_DIM), b_k, _swapped(b_k, HEAD_DIM),
        d_q, _swapped(d_q, HEAD_DIM), d_k, _swapped(d_k, HEAD_DIM),
        c_q, z(256 - C_Q_RANK), c_kv, kr4, kr4s], axis=1)
    wt = jnp.concatenate([a_v, b_v, d_v], axis=1).T

    uq = w_uq.reshape(C_Q_RANK, C_HEADS, C_NOPE + C_ROPE)
    uq_n, uq_r = uq[..., :C_NOPE], uq[..., C_NOPE:]
    zq = lambda n: jnp.zeros((C_Q_RANK, C_HEADS, n), F32)
    pad = LANES - C_NOPE - C_ROPE
    wuq = jnp.concatenate([uq_n, uq_r, zq(pad)], axis=-1).reshape(C_Q_RANK, C_HEADS * LANES)
    wuqs = jnp.concatenate([zq(C_NOPE), _swapped(uq_r, C_ROPE), zq(pad)], axis=-1).reshape(C_Q_RANK, C_HEADS * LANES)
    zrows = jnp.zeros((256 - C_Q_RANK, C_HEADS * LANES), F32)
    wuq, wuqs = jnp.concatenate([wuq, zrows], axis=0), jnp.concatenate([wuqs, zrows], axis=0)
    ukv = w_ukv.reshape(C_KV_RANK, C_HEADS, C_NOPE + HEAD_DIM)
    uk_n, u_v = ukv[..., :C_NOPE], ukv[..., C_NOPE:]
    wukn = jnp.concatenate([uk_n, jnp.zeros((C_KV_RANK, C_HEADS, LANES - C_NOPE), F32)], axis=-1)
    wukn = wukn.reshape(C_KV_RANK, C_HEADS * LANES)
    wuvt = u_v.reshape(C_KV_RANK, C_HEADS * HEAD_DIM).T

    perm64, _ = _rope_perm(HEAD_DIM)
    return {
        "wrm": wrm.astype(BF16), "wt": wt.astype(BF16),
        "wuq": wuq.astype(BF16), "wuqs": wuqs.astype(BF16),
        "wukn": wukn.astype(BF16), "wuvt": wuvt.astype(BF16),
        "gqb": jnp.tile(gq, B_HEADS)[None, :], "gqbs": jnp.tile(gq[perm64], B_HEADS)[None, :],
        "gkb": jnp.tile(gk, B_KV_HEADS)[None, :], "gkbs": jnp.tile(gk[perm64], B_KV_HEADS)[None, :],
        "gcq": jnp.concatenate([gcq, jnp.zeros((256 - C_Q_RANK,), F32)])[None, :],
        "gckv": gckv[None, :],
    }


def kernel(x, c, ctx, c_ctx, w_ada, b_ada, w_in, w_out, diff_lambda_q1, diff_lambda_k1, diff_lambda_q2,
           diff_lambda_k2, diff_subln_g, gqa_q_norm_g, gqa_k_norm_g, mla_q_norm_g, mla_kv_norm_g, mla_w_uq,
           mla_w_ukv, swa_sink, ln1_g, ln1_b, ln2_g, ln2_b, router_w, router_bias,
           exp_w_gate, exp_w_up, exp_w_down):
    b, n_lat, d = x.shape
    ctx_len = ctx.shape[1]
    depth = w_ada.shape[0]
    assert d == D_MODEL and ctx_len == TQ and n_lat % TQ == 0 and b + 1 <= 8
    alpha = (2 * depth) ** 0.25

    xa = jnp.concatenate([ctx, x], axis=1)
    tabs = _rope_tables(n_lat, ctx_len)

    cc = jnp.concatenate([c, c_ctx[None, :], jnp.zeros((8 - b - 1, d), F32)], axis=0)
    mods = _ada(cc, w_ada, b_ada)
    rwt = router_w.T
    rb = router_bias[:, None]

    for l in range(depth):
        lat = mods[l, :b]
        cx = jnp.broadcast_to(mods[l, b], lat.shape)
        modsel = jnp.stack([cx, lat], axis=1)[:, :, None, :]
        lw = _prep_layer_weights(w_in[l], gqa_q_norm_g[l], gqa_k_norm_g[l], mla_q_norm_g[l],
                                 mla_kv_norm_g[l], mla_w_uq[l], mla_w_ukv[l])
        qa, ka, vat, qb, kb, vbt, qc, kc, vct, qd, kd, vdt = _proj(xa, modsel, lw, tabs)

        lam_init = 0.8 - 0.6 * math.exp(-0.3 * l)
        extras = (diff_lambda_q1[l][None, :], diff_lambda_k1[l][None, :], diff_lambda_q2[l][None, :],
                  diff_lambda_k2[l][None, :], diff_subln_g[l][:, None])
        oa = _attn_full("a", qa, ka, vat, extras, lam_init)
        ob = _attn_full("b", qb, kb, vbt)
        oc = _attn_full("c", qc, kc, vct)
        od = _attn_d(qd, kd, vdt, swa_sink[l][None, :], ctx_len)

        x1, h2, gates = _outproj(oa, ob, oc, od, xa, modsel, w_out[l].astype(BF16),
                                 ln1_g[l][None, :], ln1_b[l][None, :], rwt, rb, alpha)
        wg = exp_w_gate[l].astype(BF16)
        wu = exp_w_up[l].astype(BF16)
        wd = exp_w_down[l].astype(BF16).reshape(N_EXPERT_GROUPS, EXPERTS_PER_GROUP * D_EXPERT, d)
        xa = _moe(h2, gates, x1, modsel, wg, wu, wd, ln2_g[l][None, :], ln2_b[l][None, :], alpha,
                  latents_only=(l == depth - 1))
    return xa
```

```python
import functools
import math

import numpy as np
import jax
import jax.numpy as jnp
from jax import lax
from jax.experimental import pallas as pl
from jax.experimental.pallas import tpu as pltpu

F32 = jnp.float32
BF16 = jnp.bfloat16

D_MODEL = 1024
HEAD_DIM = 64
GROUP_WIDTH = 256
GRID_W = 64
ROPE_THETA = 10000.0
EPS = 1e-6
NEG = -1e30
LOG2E = 1.4426950408889634

A_HEADS = 4
A_QK = 32
B_HEADS = 4
B_KV_HEADS = 2
C_HEADS = 4
C_Q_RANK = 192
C_KV_RANK = 128
C_NOPE = 64
C_ROPE = 32
D_HEADS = 4
D_KV_HEADS = 2
WINDOW = 128
N_EXPERTS = 16
EXPERTS_PER_GROUP = 4
N_EXPERT_GROUPS = 4
D_EXPERT = 256

TQ = 256
TK = 256
TK_LAT = 1024
BOUNDED_PIECES = 16
BOUND_SLACK = 1.0 + 2.0 ** -7
MIN_DENOMINATOR = 2.0 ** -80
SWA_T = 128
LANES = 128
ONES_ROWS = 16
VMEM_LIMIT = 56 * 1024 * 1024

_C_QA, _C_QAS, _C_KA, _C_KAS = 0, 256, 512, 768
_C_QB, _C_QBS, _C_KB, _C_KBS = 1024, 1280, 1536, 1664
_C_QD, _C_QDS, _C_KD, _C_KDS = 1792, 2048, 2304, 2432
_C_CQ, _C_CKV, _C_KR, _C_KRS = 2560, 2816, 2944, 3072
_C_TOTAL = 3200
_R_VA, _R_VB, _R_VD, _R_TOTAL = 0, 256, 384, 512


def _nt_dot(a, b):
    return lax.dot_general(a, b, (((1,), (1,)), ((), ())), preferred_element_type=F32)


def _split_bf16(a):
    hi = a.astype(BF16)
    lo = (a - hi.astype(F32)).astype(BF16)
    return hi, lo


def _layer_norm(x):
    mu = jnp.mean(x, axis=-1, keepdims=True)
    xc = x - mu
    var = jnp.mean(xc * xc, axis=-1, keepdims=True)
    return xc * lax.rsqrt(var + EPS)


def _tile_lanes(a, n):
    return jnp.concatenate([a] * n, axis=1)


def _ada_kernel(cc_ref, w_ref, b_ref, o_ref):
    cc = cc_ref[...]
    s = cc * jax.nn.sigmoid(cc)
    s_hi, s_lo = _split_bf16(s)
    w_hi, w_lo = _split_bf16(w_ref[...])
    acc = jnp.dot(s_hi, w_hi, preferred_element_type=F32)
    acc += jnp.dot(s_hi, w_lo, preferred_element_type=F32)
    acc += jnp.dot(s_lo, w_hi, preferred_element_type=F32)
    o_ref[...] = acc + b_ref[...]


def _ada(cc, w_ada, b_ada):
    depth, d, n6 = w_ada.shape
    bn = 1536
    return pl.pallas_call(
        _ada_kernel,
        grid=(depth, n6 // bn),
        in_specs=[
            pl.BlockSpec((8, d), lambda l, j: (0, 0)),
            pl.BlockSpec((None, d, bn), lambda l, j: (l, 0, j)),
            pl.BlockSpec((None, 1, bn), lambda l, j: (l, 0, j)),
        ],
        out_specs=pl.BlockSpec((None, 8, bn), lambda l, j: (l, 0, j)),
        out_shape=jax.ShapeDtypeStruct((depth, 8, n6), F32),
        compiler_params=pltpu.CompilerParams(
            dimension_semantics=("arbitrary", "arbitrary"), vmem_limit_bytes=VMEM_LIMIT),
        name="ada",
    )(cc, w_ada, b_ada.reshape(depth, 1, n6))


def _block_diag_ones(n, blk):
    sh = int(math.log2(blk))
    r = lax.broadcasted_iota(jnp.int32, (n, n), 0) >> sh
    c = lax.broadcasted_iota(jnp.int32, (n, n), 1) >> sh
    return jnp.where(r == c, 1.0, 0.0).astype(BF16)


def _proj_kernel(x_ref, mod_ref, wrm_ref, wt_ref, wuq_ref, wuqs_ref, wukn_ref, wuvt_ref,
                 gqb_ref, gqbs_ref, gkb_ref, gkbs_ref, gcq_ref, gckv_ref,
                 cos32_ref, sin32_ref, cos64_ref, sin64_ref, cosc_ref, sinc_ref,
                 qa_ref, ka_ref, vat_ref, qb_ref, kb_ref, vbt_ref,
                 qc_ref, kc_ref, vct_ref, qd_ref, kd_ref, vdt_ref):
    d = D_MODEL
    xn = _layer_norm(x_ref[...])
    h = (xn * (1.0 + mod_ref[:, d:2 * d]) + mod_ref[:, 0:d]).astype(BF16)

    def cols(lo, width):
        return jnp.dot(h, wrm_ref[:, lo:lo + width], preferred_element_type=F32)

    cos32, sin32 = _tile_lanes(cos32_ref[...], 2), _tile_lanes(sin32_ref[...], 2)
    cos64, sin64 = cos64_ref[...], sin64_ref[...]
    cos64w, sin64w = _tile_lanes(cos64, 2), _tile_lanes(sin64, 2)

    sa = (A_QK ** -0.5) * LOG2E
    qa_ref[...] = ((cols(_C_QA, 256) * cos32 + cols(_C_QAS, 256) * sin32) * sa).astype(BF16)
    ka_ref[...] = (cols(_C_KA, 256) * cos32 + cols(_C_KAS, 256) * sin32).astype(BF16)

    sb = (HEAD_DIM ** -0.5) * LOG2E
    q = cols(_C_QB, 256)
    ssq = jnp.dot((q * q).astype(BF16), _block_diag_ones(256, HEAD_DIM), preferred_element_type=F32)
    r = lax.rsqrt(ssq * (1.0 / HEAD_DIM) + EPS)
    qr = (q * gqb_ref[...]) * cos64w + (cols(_C_QBS, 256) * gqbs_ref[...]) * sin64w
    qb_ref[...] = (qr * (r * sb)).astype(BF16)
    k = cols(_C_KB, 128)
    ssq = jnp.dot((k * k).astype(BF16), _block_diag_ones(128, HEAD_DIM), preferred_element_type=F32)
    r = lax.rsqrt(ssq * (1.0 / HEAD_DIM) + EPS)
    kr = (k * gkb_ref[...]) * cos64 + (cols(_C_KBS, 128) * gkbs_ref[...]) * sin64
    kb_ref[...] = (kr * r).astype(BF16)

    qd_ref[...] = ((cols(_C_QD, 256) * cos64w + cols(_C_QDS, 256) * sin64w) * sb).astype(BF16)
    kd_ref[...] = (cols(_C_KD, 128) * cos64 + cols(_C_KDS, 128) * sin64).astype(BF16)

    sc = ((C_NOPE + C_ROPE) ** -0.5) * LOG2E
    cosc, sinc = cosc_ref[...], sinc_ref[...]
    cq = cols(_C_CQ, 256)
    ms = jnp.sum(cq * cq, axis=-1, keepdims=True) * (1.0 / C_Q_RANK)
    cqn = (cq * lax.rsqrt(ms + EPS) * gcq_ref[...]).astype(BF16)
    qn = jnp.dot(cqn, wuq_ref[...], preferred_element_type=F32)
    qs = jnp.dot(cqn, wuqs_ref[...], preferred_element_type=F32)
    qc_ref[...] = ((qn * _tile_lanes(cosc, 4) + qs * _tile_lanes(sinc, 4)) * sc).astype(BF16)
    ckv = cols(_C_CKV, 128)
    ms = jnp.mean(ckv * ckv, axis=-1, keepdims=True)
    ckvn = (ckv * lax.rsqrt(ms + EPS) * gckv_ref[...]).astype(BF16)
    kn = jnp.dot(ckvn, wukn_ref[...], preferred_element_type=F32)
    krr = cols(_C_KR, 128) * cosc + cols(_C_KRS, 128) * sinc
    kc_ref[...] = (kn + _tile_lanes(krr, 4)).astype(BF16)
    vct_ref[...] = _nt_dot(wuvt_ref[...], ckvn).astype(BF16)

    vt = _nt_dot(wt_ref[...], h)
    vat_ref[...] = vt[_R_VA:_R_VA + 256].astype(BF16)
    vbt_ref[...] = vt[_R_VB:_R_VB + 128].astype(BF16)
    vdt_ref[...] = vt[_R_VD:_R_VD + 128].astype(BF16)


def _proj(xa, modsel, lw, tabs):
    b, nt, d = xa.shape
    nc = nt // TQ

    def full(a):
        nd = a.ndim
        return pl.BlockSpec(a.shape, lambda bi, i, _n=nd: (0,) * _n)

    def tab(a):
        return pl.BlockSpec((TQ, LANES), lambda bi, i: (i, 0))

    def rm(w):
        return pl.BlockSpec((None, TQ, w), lambda bi, i: (bi, i, 0))

    def tr(w):
        return pl.BlockSpec((None, None, w, TQ), lambda bi, i: (bi, i, 0, 0))

    weights = [lw["wrm"], lw["wt"], lw["wuq"], lw["wuqs"], lw["wukn"], lw["wuvt"],
               lw["gqb"], lw["gqbs"], lw["gkb"], lw["gkbs"], lw["gcq"], lw["gckv"]]
    tables = [tabs["cos32"], tabs["sin32"], tabs["cos64"], tabs["sin64"], tabs["cosc"], tabs["sinc"]]
    in_specs = ([pl.BlockSpec((None, TQ, d), lambda bi, i: (bi, i, 0)),
                 pl.BlockSpec((None, None, 1, 6 * d), lambda bi, i: (bi, jnp.minimum(i, 1), 0, 0))]
                + [full(w) for w in weights] + [tab(t) for t in tables])
    widths = [("rm", 256), ("rm", 256), ("tr", 256), ("rm", 256), ("rm", 128), ("tr", 128),
              ("rm", 512), ("rm", 512), ("tr", 256), ("rm", 256), ("rm", 128), ("tr", 128)]
    out_specs, out_shape = [], []
    for kind, w in widths:
        if kind == "rm":
            out_specs.append(rm(w))
            out_shape.append(jax.ShapeDtypeStruct((b, nt, w), BF16))
        else:
            out_specs.append(tr(w))
            out_shape.append(jax.ShapeDtypeStruct((b, nc, w, TQ), BF16))
    return pl.pallas_call(
        _proj_kernel,
        grid=(b, nc),
        in_specs=in_specs,
        out_specs=out_specs,
        out_shape=out_shape,
        compiler_params=pltpu.CompilerParams(
            dimension_semantics=("arbitrary", "arbitrary"), vmem_limit_bytes=VMEM_LIMIT),
        name="proj",
    )(xa, modsel, *weights, *tables)


def _flash_loop(qi, nc, k_ref, vt_ref, qz_ref, m_ref, alpha_ref, acc_ref, s0_ref, s1_ref, kabs_ref, k_lanes, pv_cfg,
                slab_rows):
    per = TK_LAT // TK
    n_lat = (nc - 1) // per
    assert n_lat % 2 == 0 and n_lat >= 2
    same_lanes = all(kl == k_lanes[0] for kl in k_lanes)

    def scores(kc):
        if same_lanes:
            return jnp.dot(kc[:, k_lanes[0][0]:k_lanes[0][1]], qz_ref[...], preferred_element_type=F32)
        return jnp.concatenate(
            [jnp.dot(kc[:, lo:hi], qz_ref[:, TQ * j:TQ * (j + 1)], preferred_element_type=F32)
             for j, (lo, hi) in enumerate(k_lanes)], axis=1)

    def small_chunk(pieces, first):
        if slab_rows >= pieces * TK:
            return (1, pieces, first)
        return (pieces * TK // slab_rows, slab_rows // TK, first)

    ctx_chunk = small_chunk(1, 0)

    def lat_chunk(c):
        return small_chunk(per, 1 + c * per)

    def stats_update(mx):
        m_old = m_ref[...]
        m_new = jnp.maximum(m_old, mx)
        alpha_ref[...] = jnp.exp2(m_old - m_new)
        m_ref[...] = m_new

    def pipelined(cur, cur_ref, nxt, nxt_ref):
        m_cur, alpha = m_ref[...], alpha_ref[...]
        mx, pv = None, [None] * len(pv_cfg)
        n_cur = cur[0] if cur is not None else 0
        n_nxt = nxt[0] if nxt is not None else 0
        for r in range(max(n_cur, n_nxt)):
            if r < n_nxt:
                rows = TK * nxt[1]
                kc = k_ref[pl.ds(pl.multiple_of((nxt[2] + r * nxt[1]) * TK, TK), rows), :]
                s = scores(kc)
                nxt_ref[rows * r:rows * (r + 1), :] = s
                smx = jnp.max(s, axis=0, keepdims=True)
                mx = smx if mx is None else jnp.maximum(mx, smx)
            if r < n_cur:
                rows = TK * cur[1]
                p = jnp.exp2(cur_ref[rows * r:rows * (r + 1), :] - m_cur).astype(BF16)
                vc = jnp.concatenate([vt_ref[cur[2] + r * cur[1] + i] for i in range(cur[1])], axis=1)
                ones = jnp.ones((ONES_ROWS, rows), BF16)
                for g, (vl, vh, j0, nj) in enumerate(pv_cfg):
                    va = jnp.concatenate([vc[vl:vh, :], ones], axis=0)
                    d = jnp.dot(va, p[:, TQ * j0:TQ * (j0 + nj)], preferred_element_type=F32)
                    pv[g] = d if pv[g] is None else pv[g] + d
        if cur is not None:
            for g, (vl, vh, j0, nj) in enumerate(pv_cfg):
                cols = slice(TQ * j0, TQ * (j0 + nj))
                acc_ref[:, cols] = alpha[:, cols] * acc_ref[:, cols] + pv[g]
        if nxt is not None:
            stats_update(mx)

    def online_pass():
        m_ref[...] = jnp.full(m_ref.shape, NEG, F32)
        acc_ref[...] = jnp.zeros(acc_ref.shape, F32)

        @pl.when(qi == 0)
        def _():
            pipelined(None, None, ctx_chunk, s0_ref)
            pipelined(ctx_chunk, s0_ref, None, None)

        @pl.when(qi > 0)
        def _():
            pipelined(None, None, lat_chunk(0), s0_ref)

            def body(i, carry):
                c = 2 * i
                pipelined(lat_chunk(c), s0_ref, lat_chunk(c + 1), s1_ref)
                pipelined(lat_chunk(c + 1), s1_ref, lat_chunk(c + 2), s0_ref)
                return carry

            lax.fori_loop(0, n_lat // 2 - 1, body, 0)
            pipelined(lat_chunk(n_lat - 2), s0_ref, lat_chunk(n_lat - 1), s1_ref)
            pipelined(lat_chunk(n_lat - 1), s1_ref, ctx_chunk, s0_ref)
            pipelined(ctx_chunk, s0_ref, None, None)

    def bounded_pass():
        wk = kabs_ref.shape[1]
        kab = jnp.broadcast_to(kabs_ref[...], (ONES_ROWS, wk)).astype(BF16)
        aq = jnp.abs(qz_ref[...])
        if same_lanes:
            lo, hi = k_lanes[0]
            bound = jnp.dot(kab[:, lo:hi], aq, preferred_element_type=F32)[0:1]
        else:
            bound = jnp.concatenate(
                [jnp.dot(kab[:, lo:hi], aq[:, TQ * j:TQ * (j + 1)], preferred_element_type=F32)[0:1]
                 for j, (lo, hi) in enumerate(k_lanes)], axis=1)
        bound = bound * BOUND_SLACK
        ones = jnp.ones((ONES_ROWS, TK), BF16)
        acc_ref[...] = jnp.zeros(acc_ref.shape, F32)

        def pieces(first, n):
            pv = [None] * len(pv_cfg)

            def probs(r, g):
                _, _, j0, nj = pv_cfg[g]
                kc = k_ref[pl.ds(pl.multiple_of((first + r) * TK, TK), TK), :]
                s = jnp.concatenate(
                    [jnp.dot(kc[:, k_lanes[j][0]:k_lanes[j][1]], qz_ref[:, TQ * j:TQ * (j + 1)],
                             preferred_element_type=F32) for j in range(j0, j0 + nj)], axis=1) \
                    if not same_lanes else jnp.dot(kc[:, k_lanes[0][0]:k_lanes[0][1]],
                                                   qz_ref[:, TQ * j0:TQ * (j0 + nj)], preferred_element_type=F32)
                return jnp.exp2(s - bound[:, TQ * j0:TQ * (j0 + nj)]).astype(BF16)

            def values(r, g, p):
                vl, vh, _, _ = pv_cfg[g]
                va = jnp.concatenate([vt_ref[first + r][vl:vh, :], ones], axis=0)
                d = jnp.dot(va, p, preferred_element_type=F32)
                pv[g] = d if pv[g] is None else pv[g] + d

            n_g = len(pv_cfg)
            p_prev = [probs(0, g) for g in range(n_g)]
            for r in range(1, n):
                p_cur = []
                for g in range(n_g):
                    p_cur.append(probs(r, g))
                    values(r - 1, g, p_prev[g])
                p_prev = p_cur
            for g in range(n_g):
                values(n - 1, g, p_prev[g])
            for g, (vl, vh, j0, nj) in enumerate(pv_cfg):
                cols = slice(TQ * j0, TQ * (j0 + nj))
                acc_ref[:, cols] = acc_ref[:, cols] + pv[g]

        pieces(0, 1)

        @pl.when(qi > 0)
        def _():
            def body(i, carry):
                pieces(1 + i * BOUNDED_PIECES, BOUNDED_PIECES)
                return carry

            lax.fori_loop(0, (nc - 1) // BOUNDED_PIECES, body, 0)

    @pl.when(qi == 0)
    def _():
        def body(c, mx):
            blk = k_ref[pl.ds(pl.multiple_of(c * TK, TK), TK), :].astype(F32)
            return jnp.maximum(mx, jnp.max(jnp.abs(blk), axis=0, keepdims=True))

        kabs_ref[...] = lax.fori_loop(0, nc, body, jnp.zeros(kabs_ref.shape, F32))

    bounded_pass()
    dv = acc_ref.shape[0] - ONES_ROWS
    healthy = jnp.min(jnp.where(acc_ref[dv:dv + 1, :] >= MIN_DENOMINATOR, 1.0, 0.0))

    @pl.when(healthy < 0.5)
    def _():
        online_pass()


def _normalized(acc_ref, j, dv):
    a = acc_ref[:, TQ * j:TQ * (j + 1)]
    return a[0:dv] / a[dv:dv + 1]


def _attn_a_kernel(q_ref, k_ref, vt_ref, lq1_ref, lk1_ref, lq2_ref, lk2_ref, g_ref, o_ref,
                   qz_ref, m_ref, alpha_ref, acc_ref, s0_ref, s1_ref, kabs_ref, *, nc, lam_init):
    qi = pl.program_id(1)
    qt = q_ref[...].astype(F32).T
    row = lax.broadcasted_iota(jnp.int32, qt.shape, 0)
    for j in range(2 * A_HEADS):
        keep = (row >= A_QK * j) & (row < A_QK * (j + 1))
        qz_ref[:, TQ * j:TQ * (j + 1)] = jnp.where(keep, qt, 0.0).astype(BF16)
    k_lanes = [(0, 256)] * (2 * A_HEADS)
    pv_cfg = [(HEAD_DIM * hd, HEAD_DIM * (hd + 1), 2 * hd, 2) for hd in range(A_HEADS)]
    _flash_loop(qi, nc, k_ref, vt_ref, qz_ref, m_ref, alpha_ref, acc_ref, s0_ref, s1_ref, kabs_ref, k_lanes, pv_cfg,
                slab_rows=1024)

    lam = (jnp.exp(jnp.sum(lq1_ref[...] * lk1_ref[...], axis=-1, keepdims=True))
           - jnp.exp(jnp.sum(lq2_ref[...] * lk2_ref[...], axis=-1, keepdims=True)) + lam_init)
    outs = []
    for hd in range(A_HEADS):
        o = _normalized(acc_ref, 2 * hd, HEAD_DIM) - lam * _normalized(acc_ref, 2 * hd + 1, HEAD_DIM)
        ms = jnp.mean(o * o, axis=0, keepdims=True)
        outs.append(o * lax.rsqrt(ms + EPS) * g_ref[...] * (1.0 - lam_init))
    o_ref[...] = jnp.concatenate(outs, axis=0).T.astype(BF16)


def _attn_b_kernel(q_ref, k_ref, vt_ref, o_ref, qz_ref, m_ref, alpha_ref, acc_ref, s0_ref, s1_ref, kabs_ref, *, nc):
    qi = pl.program_id(1)
    qt = q_ref[...].astype(F32).T
    zeros = jnp.zeros((HEAD_DIM, TQ), F32)
    rep = B_HEADS // B_KV_HEADS
    for hd in range(B_HEADS):
        parts = [zeros] * B_KV_HEADS
        parts[hd // rep] = qt[HEAD_DIM * hd:HEAD_DIM * (hd + 1)]
        qz_ref[:, TQ * hd:TQ * (hd + 1)] = jnp.concatenate(parts, axis=0).astype(BF16)
    k_lanes = [(0, 128)] * B_HEADS
    pv_cfg = [(HEAD_DIM * g, HEAD_DIM * (g + 1), rep * g, rep) for g in range(B_KV_HEADS)]
    _flash_loop(qi, nc, k_ref, vt_ref, qz_ref, m_ref, alpha_ref, acc_ref, s0_ref, s1_ref, kabs_ref, k_lanes, pv_cfg,
                slab_rows=256)
    outs = [_normalized(acc_ref, hd, HEAD_DIM) for hd in range(B_HEADS)]
    o_ref[...] = jnp.concatenate(outs, axis=0).T.astype(BF16)


def _attn_c_kernel(q_ref, k_ref, vt_ref, o_ref, qz_ref, m_ref, alpha_ref, acc_ref, s0_ref, s1_ref, kabs_ref, *, nc):
    qi = pl.program_id(1)
    qt = q_ref[...].astype(F32).T
    k_lanes, pv_cfg = [], []
    for hd in range(C_HEADS):
        qz_ref[:, TQ * hd:TQ * (hd + 1)] = qt[LANES * hd:LANES * (hd + 1)].astype(BF16)
        k_lanes.append((LANES * hd, LANES * (hd + 1)))
        pv_cfg.append((HEAD_DIM * hd, HEAD_DIM * (hd + 1), hd, 1))
    _flash_loop(qi, nc, k_ref, vt_ref, qz_ref, m_ref, alpha_ref, acc_ref, s0_ref, s1_ref, kabs_ref, k_lanes, pv_cfg,
                slab_rows=256)
    outs = [_normalized(acc_ref, hd, HEAD_DIM) for hd in range(C_HEADS)]
    o_ref[...] = jnp.concatenate(outs, axis=0).T.astype(BF16)


def _attn_full(kind, q, k, vt, extras=(), lam_init=0.0):
    b, nt, wq = q.shape
    wk = k.shape[-1]
    nc, wv = vt.shape[1], vt.shape[2]
    assert (nt - TK) % TK_LAT == 0
    if kind == "a":
        body, n_sm, dk = functools.partial(_attn_a_kernel, nc=nc, lam_init=lam_init), 2 * A_HEADS, 256
    elif kind == "b":
        body, n_sm, dk = functools.partial(_attn_b_kernel, nc=nc), B_HEADS, 128
    else:
        body, n_sm, dk = functools.partial(_attn_c_kernel, nc=nc), C_HEADS, 128
    resident = pl.Buffered(1)
    in_specs = [
        pl.BlockSpec((None, TQ, wq), lambda bi, i: (bi, i, 0)),
        pl.BlockSpec((None, nt, wk), lambda bi, i: (bi, 0, 0), pipeline_mode=resident),
        pl.BlockSpec((None, nc, wv, TK), lambda bi, i: (bi, 0, 0, 0), pipeline_mode=resident),
    ] + [pl.BlockSpec(e.shape, lambda bi, i: (0, 0)) for e in extras]
    return pl.pallas_call(
        body,
        grid=(b, nc),
        in_specs=in_specs,
        out_specs=pl.BlockSpec((None, TQ, GROUP_WIDTH), lambda bi, i: (bi, i, 0)),
        out_shape=jax.ShapeDtypeStruct((b, nt, GROUP_WIDTH), BF16),
        scratch_shapes=[
            pltpu.VMEM((dk, n_sm * TQ), BF16),
            pltpu.VMEM((1, n_sm * TQ), F32),
            pltpu.VMEM((1, n_sm * TQ), F32),
            pltpu.VMEM((HEAD_DIM + ONES_ROWS, n_sm * TQ), F32),
            pltpu.VMEM((TK_LAT, n_sm * TQ), F32),
            pltpu.VMEM((TK_LAT, n_sm * TQ), F32),
            pltpu.VMEM((1, wk), F32),
        ],
        compiler_params=pltpu.CompilerParams(
            dimension_semantics=("arbitrary", "arbitrary"), vmem_limit_bytes=VMEM_LIMIT),
        name="attn_" + kind,
    )(q, k, vt, *extras)


def _attn_d_kernel(q_ref, kc_ref, kp_ref, ko_ref, kn_ref, vc_ref, vp_ref, vo_ref, vn_ref, sink_ref,
                   o_ref, *, nb, ctx_blocks):
    n = pl.program_id(1)
    t = SWA_T
    rep = D_HEADS // D_KV_HEADS
    qt = q_ref[...].astype(F32).T
    zeros = jnp.zeros((HEAD_DIM, t), F32)
    cols = []
    for hd in range(D_HEADS):
        parts = [zeros] * D_KV_HEADS
        parts[hd // rep] = qt[HEAD_DIM * hd:HEAD_DIM * (hd + 1)]
        cols.append(jnp.concatenate(parts, axis=0))
    qz = jnp.concatenate(cols, axis=1).astype(BF16)
    w = D_HEADS * t
    ko_i = lax.broadcasted_iota(jnp.int32, (t, w), 0)
    qo_i = lax.broadcasted_iota(jnp.int32, (t, w), 1) & (t - 1)
    band = n >= ctx_blocks
    ok_prev = jnp.logical_and(n >= ctx_blocks + 1, qo_i <= ko_i)
    ok_own = jnp.logical_and(band, ko_i >= 0)
    ok_next = jnp.logical_and(jnp.logical_and(band, n + 1 <= nb - 1), ko_i <= qo_i)
    s_c = jnp.dot(kc_ref[...], qz, preferred_element_type=F32)
    s_p = jnp.where(ok_prev, jnp.dot(kp_ref[...], qz, preferred_element_type=F32), NEG)
    s_o = jnp.where(ok_own, jnp.dot(ko_ref[...], qz, preferred_element_type=F32), NEG)
    s_n = jnp.where(ok_next, jnp.dot(kn_ref[...], qz, preferred_element_type=F32), NEG)
    sk = jnp.concatenate([jnp.broadcast_to(sink_ref[0:1, hd:hd + 1] * LOG2E, (1, t))
                          for hd in range(D_HEADS)], axis=1)
    m = jnp.maximum(jnp.max(s_c, axis=0, keepdims=True), jnp.max(s_p, axis=0, keepdims=True))
    m = jnp.maximum(m, jnp.max(s_o, axis=0, keepdims=True))
    m = jnp.maximum(m, jnp.max(s_n, axis=0, keepdims=True))
    m = jnp.maximum(m, sk)
    pieces = [(vc_ref[...], jnp.exp2(s_c - m).astype(BF16)), (vp_ref[...], jnp.exp2(s_p - m).astype(BF16)),
              (vo_ref[...], jnp.exp2(s_o - m).astype(BF16)), (vn_ref[...], jnp.exp2(s_n - m).astype(BF16))]
    l_sink = jnp.exp2(sk - m)
    outs = []
    for g in range(D_KV_HEADS):
        lanes = slice(rep * t * g, rep * t * (g + 1))
        o = None
        for v, p in pieces:
            va = jnp.concatenate([v[HEAD_DIM * g:HEAD_DIM * (g + 1)],
                                  jnp.ones((ONES_ROWS, v.shape[1]), BF16)], axis=0)
            d = jnp.dot(va, p[:, lanes], preferred_element_type=F32)
            o = d if o is None else o + d
        o = o[0:HEAD_DIM] / (o[HEAD_DIM:HEAD_DIM + 1] + l_sink[:, lanes])
        outs += [o[:, t * r:t * (r + 1)] for r in range(rep)]
    o_ref[...] = jnp.concatenate(outs, axis=0).T.astype(BF16)


def _attn_d(q, k, vt, sink, ctx_len):
    b, nt, wq = q.shape
    wk = k.shape[-1]
    wv = vt.shape[2]
    t = SWA_T
    nb = nt // t
    per = TQ // t
    ctx_blocks = ctx_len // t
    assert ctx_len == TQ

    def kspec(shift):
        return pl.BlockSpec((None, t, wk), lambda bi, i: (bi, jnp.clip(i + shift, 0, nb - 1), 0))

    def vspec(shift):
        def imap(bi, i):
            j = jnp.clip(i + shift, 0, nb - 1)
            return (bi, j // per, 0, j % per)
        return pl.BlockSpec((None, None, wv, t), imap)

    in_specs = [
        pl.BlockSpec((None, t, wq), lambda bi, i: (bi, i, 0)),
        pl.BlockSpec((None, ctx_len, wk), lambda bi, i: (bi, 0, 0)),
        kspec(-1), kspec(0), kspec(1),
        pl.BlockSpec((None, None, wv, ctx_len), lambda bi, i: (bi, 0, 0, 0)),
        vspec(-1), vspec(0), vspec(1),
        pl.BlockSpec(sink.shape, lambda bi, i: (0, 0)),
    ]
    return pl.pallas_call(
        functools.partial(_attn_d_kernel, nb=nb, ctx_blocks=ctx_blocks),
        grid=(b, nb),
        in_specs=in_specs,
        out_specs=pl.BlockSpec((None, t, GROUP_WIDTH), lambda bi, i: (bi, i, 0)),
        out_shape=jax.ShapeDtypeStruct((b, nt, GROUP_WIDTH), BF16),
        compiler_params=pltpu.CompilerParams(
            dimension_semantics=("arbitrary", "arbitrary"), vmem_limit_bytes=VMEM_LIMIT),
        name="attn_d",
    )(q, k, k, k, k, vt, vt, vt, vt, sink)


def _route(sel_rows, s_rows):
    epg = EXPERTS_PER_GROUP
    gscore = []
    for g in range(N_EXPERT_GROUPS):
        r = sel_rows[epg * g:epg * (g + 1)]
        pair = None
        for i in range(epg):
            for j in range(i + 1, epg):
                v = r[i] + r[j]
                pair = v if pair is None else jnp.maximum(pair, v)
        gscore.append(pair)
    best, best_g = gscore[0], jnp.zeros_like(gscore[0], dtype=jnp.int32)
    for g in range(1, N_EXPERT_GROUPS):
        better = gscore[g] > best
        best_g = jnp.where(better, g, best_g)
        best = jnp.where(better, gscore[g], best)
    w = []
    for e in range(N_EXPERTS):
        g = e // epg
        cnt = jnp.zeros_like(best_g)
        for e2 in range(epg * g, epg * (g + 1)):
            if e2 == e:
                continue
            beats = sel_rows[e2] > sel_rows[e]
            if e2 < e:
                beats = jnp.logical_or(beats, sel_rows[e2] == sel_rows[e])
            cnt = cnt + jnp.where(beats, 1, 0)
        chosen = jnp.logical_and(best_g == g, cnt < 2)
        w.append(jnp.where(chosen, s_rows[e], 0.0))
    tot = w[0]
    for e in range(1, N_EXPERTS):
        tot = tot + w[e]
    return [we / tot for we in w]


def _outproj_kernel(oa_ref, ob_ref, oc_ref, od_ref, x_ref, mod_ref, wout_ref, g_ref, b_ref,
                    rwt_ref, rb_ref, x1_ref, h2_ref, gates_ref, *, alpha):
    d = D_MODEL
    o = jnp.concatenate([oa_ref[...], ob_ref[...], oc_ref[...], od_ref[...]], axis=1)
    y = jnp.dot(o, wout_ref[...], preferred_element_type=F32)
    u = alpha * x_ref[...] + mod_ref[:, 2 * d:3 * d] * y
    x1 = _layer_norm(u) * g_ref[...] + b_ref[...]
    x1_ref[...] = x1
    h2 = _layer_norm(x1) * (1.0 + mod_ref[:, 4 * d:5 * d]) + mod_ref[:, 3 * d:4 * d]
    h2_ref[...] = h2.astype(BF16)

    h_hi, h_lo = _split_bf16(h2)
    w_hi, w_lo = _split_bf16(rwt_ref[...])
    logits = _nt_dot(w_hi, h_hi) + _nt_dot(w_hi, h_lo) + _nt_dot(w_lo, h_hi)
    s = jax.nn.sigmoid(logits)
    sel = s + rb_ref[...]
    s_rows = [s[e:e + 1] for e in range(N_EXPERTS)]
    sel_rows = [sel[e:e + 1] for e in range(N_EXPERTS)]
    gates = _route(sel_rows, s_rows)
    gt = jnp.concatenate(gates + [jnp.zeros((LANES - N_EXPERTS, TQ), F32)], axis=0)
    gates_ref[...] = gt.T


def _outproj(oa, ob, oc, od, xa, modsel, w_out, ln_g, ln_b, rwt, rb, alpha):
    b, nt, d = xa.shape
    nc = nt // TQ

    def rm(w):
        return pl.BlockSpec((None, TQ, w), lambda bi, i: (bi, i, 0))

    def full(a):
        nd = a.ndim
        return pl.BlockSpec(a.shape, lambda bi, i, _n=nd: (0,) * _n)

    in_specs = [rm(GROUP_WIDTH) for _ in range(4)] + [
        rm(d),
        pl.BlockSpec((None, None, 1, 6 * d), lambda bi, i: (bi, jnp.minimum(i, 1), 0, 0)),
        full(w_out), full(ln_g), full(ln_b), full(rwt), full(rb)]
    return pl.pallas_call(
        functools.partial(_outproj_kernel, alpha=alpha),
        grid=(b, nc),
        in_specs=in_specs,
        out_specs=[rm(d), rm(d), rm(LANES)],
        out_shape=[jax.ShapeDtypeStruct((b, nt, d), F32), jax.ShapeDtypeStruct((b, nt, d), BF16),
                   jax.ShapeDtypeStruct((b, nt, LANES), F32)],
        compiler_params=pltpu.CompilerParams(
            dimension_semantics=("arbitrary", "arbitrary"), vmem_limit_bytes=VMEM_LIMIT),
        name="outproj",
    )(oa, ob, oc, od, xa, modsel, w_out, ln_g, ln_b, rwt, rb)


def _moe_kernel(h_ref, gates_ref, x_ref, mod_ref, wg_ref, wu_ref, wd_ref, g_ref, b_ref, o_ref, *, alpha):
    d = D_MODEL
    h = h_ref[...]
    gates = gates_ref[...]
    y = jnp.zeros((TQ, d), F32)
    for g in range(N_EXPERT_GROUPS):
        acts = []
        for el in range(EXPERTS_PER_GROUP):
            e = EXPERTS_PER_GROUP * g + el
            hg = jnp.dot(h, wg_ref[e], preferred_element_type=F32)
            hu = jnp.dot(h, wu_ref[e], preferred_element_type=F32)
            a = hg * jax.nn.sigmoid(hg) * hu * gates[:, e:e + 1]
            acts.append(a.astype(BF16))
        y = y + jnp.dot(jnp.concatenate(acts, axis=1), wd_ref[g], preferred_element_type=F32)
    u = alpha * x_ref[...] + mod_ref[:, 5 * d:6 * d] * y
    o_ref[...] = _layer_norm(u) * g_ref[...] + b_ref[...]


def _moe(h2, gates, x1, modsel, wg, wu, wd, ln_g, ln_b, alpha, latents_only):
    b, nt, d = x1.shape
    nc = nt // TQ
    resident = pl.Buffered(1)
    if latents_only:
        out_spec = pl.BlockSpec((None, TQ, d), lambda bi, i: (bi, jnp.maximum(i - 1, 0), 0))
        out_rows = nt - TQ
    else:
        out_spec = pl.BlockSpec((None, TQ, d), lambda bi, i: (bi, i, 0))
        out_rows = nt

    def rm(w):
        return pl.BlockSpec((None, TQ, w), lambda bi, i: (bi, i, 0))

    def res(a):
        nd = a.ndim
        return pl.BlockSpec(a.shape, lambda bi, i, _n=nd: (0,) * _n, pipeline_mode=resident)

    in_specs = [rm(d), rm(LANES), rm(d),
                pl.BlockSpec((None, None, 1, 6 * d), lambda bi, i: (bi, jnp.minimum(i, 1), 0, 0)),
                res(wg), res(wu), res(wd), res(ln_g), res(ln_b)]
    return pl.pallas_call(
        functools.partial(_moe_kernel, alpha=alpha),
        grid=(b, nc),
        in_specs=in_specs,
        out_specs=out_spec,
        out_shape=jax.ShapeDtypeStruct((b, out_rows, d), F32),
        compiler_params=pltpu.CompilerParams(
            dimension_semantics=("arbitrary", "arbitrary"), vmem_limit_bytes=VMEM_LIMIT),
        name="moe",
    )(h2, gates, x1, modsel, wg, wu, wd, ln_g, ln_b)


def _rope_perm(rot_dim):
    n = rot_dim // 4
    j = np.arange(rot_dim)
    return j ^ n, np.where((j // n) % 2 == 0, -1.0, 1.0).astype(np.float32)


def _swapped(w, rot_dim):
    perm, sign = _rope_perm(rot_dim)
    cols = w.shape[-1]
    idx = (np.arange(cols) // rot_dim) * rot_dim + perm[np.arange(cols) % rot_dim]
    sgn = sign[np.arange(cols) % rot_dim]
    return w[..., idx] * sgn


def _rope_tables(n_lat, ctx_len):
    t = jnp.arange(n_lat, dtype=jnp.int32)
    row, col = (t // GRID_W).astype(F32), (t % GRID_W).astype(F32)

    def pattern(rot_dim):
        n = rot_dim // 4
        inv = ROPE_THETA ** (-jnp.arange(n, dtype=F32) / n)
        ar, ac = row[:, None] * inv, col[:, None] * inv
        ang = jnp.concatenate([ar, ar, ac, ac], axis=-1)
        return jnp.cos(ang), jnp.sin(ang)

    def with_ctx(a, fill):
        return jnp.concatenate([jnp.full((ctx_len, a.shape[1]), fill, F32), a], axis=0)

    c32, s32 = pattern(A_QK)
    c64, s64 = pattern(HEAD_DIM)
    ones = jnp.ones((n_lat, C_NOPE), F32)
    zeros = jnp.zeros((n_lat, C_NOPE), F32)
    pad1 = jnp.ones((n_lat, LANES - C_NOPE - C_ROPE), F32)
    pad0 = jnp.zeros((n_lat, LANES - C_NOPE - C_ROPE), F32)
    return {
        "cos32": with_ctx(jnp.tile(c32, (1, LANES // A_QK)), 1.0),
        "sin32": with_ctx(jnp.tile(s32, (1, LANES // A_QK)), 0.0),
        "cos64": with_ctx(jnp.tile(c64, (1, LANES // HEAD_DIM)), 1.0),
        "sin64": with_ctx(jnp.tile(s64, (1, LANES // HEAD_DIM)), 0.0),
        "cosc": with_ctx(jnp.concatenate([ones, c32, pad1], axis=-1), 1.0),
        "sinc": with_ctx(jnp.concatenate([zeros, s32, pad0], axis=-1), 0.0),
    }


def _prep_layer_weights(w_in, gq, gk, gcq, gckv, w_uq, w_ukv):
    d = w_in.shape[0]
    splits = np.cumsum([256, 256, 256, 256, 128, 128, C_Q_RANK, C_KV_RANK, C_ROPE, 256, 128, 128])[:-1]
    (a_q, a_k, a_v, b_q, b_k, b_v, c_q, c_kv, c_kr, d_q, d_k, d_v) = jnp.split(w_in, splits, axis=1)
    z = lambda n: jnp.zeros((d, n), F32)
    kr4 = jnp.concatenate([z(C_NOPE), c_kr, z(LANES - C_NOPE - C_ROPE)], axis=1)
    kr4s = jnp.concatenate([z(C_NOPE), _swapped(c_kr, C_ROPE), z(LANES - C_NOPE - C_ROPE)], axis=1)
    wrm = jnp.concatenate([
        a_q, _swapped(a_q, A_QK), a_k, _swapped(a_k, A_QK),
        b_q, _swapped(b_q, HEAD_DIM), b_k, _swapped(b_k, HEAD_DIM),
        d_q, _swapped(d_q, HEAD_DIM), d_k, _swapped(d_k, HEAD_DIM),
        c_q, z(256 - C_Q_RANK), c_kv, kr4, kr4s], axis=1)
    wt = jnp.concatenate([a_v, b_v, d_v], axis=1).T

    uq = w_uq.reshape(C_Q_RANK, C_HEADS, C_NOPE + C_ROPE)
    uq_n, uq_r = uq[..., :C_NOPE], uq[..., C_NOPE:]
    zq = lambda n: jnp.zeros((C_Q_RANK, C_HEADS, n), F32)
    pad = LANES - C_NOPE - C_ROPE
    wuq = jnp.concatenate([uq_n, uq_r, zq(pad)], axis=-1).reshape(C_Q_RANK, C_HEADS * LANES)
    wuqs = jnp.concatenate([zq(C_NOPE), _swapped(uq_r, C_ROPE), zq(pad)], axis=-1).reshape(C_Q_RANK, C_HEADS * LANES)
    zrows = jnp.zeros((256 - C_Q_RANK, C_HEADS * LANES), F32)
    wuq, wuqs = jnp.concatenate([wuq, zrows], axis=0), jnp.concatenate([wuqs, zrows], axis=0)
    ukv = w_ukv.reshape(C_KV_RANK, C_HEADS, C_NOPE + HEAD_DIM)
    uk_n, u_v = ukv[..., :C_NOPE], ukv[..., C_NOPE:]
    wukn = jnp.concatenate([uk_n, jnp.zeros((C_KV_RANK, C_HEADS, LANES - C_NOPE), F32)], axis=-1)
    wukn = wukn.reshape(C_KV_RANK, C_HEADS * LANES)
    wuvt = u_v.reshape(C_KV_RANK, C_HEADS * HEAD_DIM).T

    perm64, _ = _rope_perm(HEAD_DIM)
    return {
        "wrm": wrm.astype(BF16), "wt": wt.astype(BF16),
        "wuq": wuq.astype(BF16), "wuqs": wuqs.astype(BF16),
        "wukn": wukn.astype(BF16), "wuvt": wuvt.astype(BF16),
        "gqb": jnp.tile(gq, B_HEADS)[None, :], "gqbs": jnp.tile(gq[perm64], B_HEADS)[None, :],
        "gkb": jnp.tile(gk, B_KV_HEADS)[None, :], "gkbs": jnp.tile(gk[perm64], B_KV_HEADS)[None, :],
        "gcq": jnp.concatenate([gcq, jnp.zeros((256 - C_Q_RANK,), F32)])[None, :],
        "gckv": gckv[None, :],
    }


def kernel(x, c, ctx, c_ctx, w_ada, b_ada, w_in, w_out, diff_lambda_q1, diff_lambda_k1, diff_lambda_q2,
           diff_lambda_k2, diff_subln_g, gqa_q_norm_g, gqa_k_norm_g, mla_q_norm_g, mla_kv_norm_g, mla_w_uq,
           mla_w_ukv, swa_sink, ln1_g, ln1_b, ln2_g, ln2_b, router_w, router_bias,
           exp_w_gate, exp_w_up, exp_w_down):
    b, n_lat, d = x.shape
    ctx_len = ctx.shape[1]
    depth = w_ada.shape[0]
    assert d == D_MODEL and ctx_len == TQ and n_lat % TQ == 0 and b + 1 <= 8
    alpha = (2 * depth) ** 0.25

    xa = jnp.concatenate([ctx, x], axis=1)
    tabs = _rope_tables(n_lat, ctx_len)

    cc = jnp.concatenate([c, c_ctx[None, :], jnp.zeros((8 - b - 1, d), F32)], axis=0)
    mods = _ada(cc, w_ada, b_ada)
    rwt = router_w.T
    rb = router_bias[:, None]

    for l in range(depth):
        lat = mods[l, :b]
        cx = jnp.broadcast_to(mods[l, b], lat.shape)
        modsel = jnp.stack([cx, lat], axis=1)[:, :, None, :]
        lw = _prep_layer_weights(w_in[l], gqa_q_norm_g[l], gqa_k_norm_g[l], mla_q_norm_g[l],
                                 mla_kv_norm_g[l], mla_w_uq[l], mla_w_ukv[l])
        qa, ka, vat, qb, kb, vbt, qc, kc, vct, qd, kd, vdt = _proj(xa, modsel, lw, tabs)

        lam_init = 0.8 - 0.6 * math.exp(-0.3 * l)
        extras = (diff_lambda_q1[l][None, :], diff_lambda_k1[l][None, :], diff_lambda_q2[l][None, :],
                  diff_lambda_k2[l][None, :], diff_subln_g[l][:, None])
        oa = _attn_full("a", qa, ka, vat, extras, lam_init)
        ob = _attn_full("b", qb, kb, vbt)
        oc = _attn_full("c", qc, kc, vct)
        od = _attn_d(qd, kd, vdt, swa_sink[l][None, :], ctx_len)

        x1, h2, gates = _outproj(oa, ob, oc, od, xa, modsel, w_out[l].astype(BF16),
                                 ln1_g[l][None, :], ln1_b[l][None, :], rwt, rb, alpha)
        wg = exp_w_gate[l].astype(BF16)
        wu = exp_w_up[l].astype(BF16)
        wd = exp_w_down[l].astype(BF16).reshape(N_EXPERT_GROUPS, EXPERTS_PER_GROUP * D_EXPERT, d)
        xa = _moe(h2, gates, x1, modsel, wg, wu, wd, ln2_g[l][None, :], ln2_b[l][None, :], alpha,
                  latents_only=(l == depth - 1))
    return xa
```

```python
import functools
import math

import numpy as np
import jax
import jax.numpy as jnp
from jax import lax
from jax.experimental import pallas as pl
from jax.experimental.pallas import tpu as pltpu

F32 = jnp.float32
BF16 = jnp.bfloat16

D_MODEL = 1024
HEAD_DIM = 64
GROUP_WIDTH = 256
GRID_W = 64
ROPE_THETA = 10000.0
EPS = 1e-6
NEG = -1e30
LOG2E = 1.4426950408889634

A_HEADS = 4
A_QK = 32
B_HEADS = 4
B_KV_HEADS = 2
C_HEADS = 4
C_Q_RANK = 192
C_KV_RANK = 128
C_NOPE = 64
C_ROPE = 32
D_HEADS = 4
D_KV_HEADS = 2
WINDOW = 128
N_EXPERTS = 16
EXPERTS_PER_GROUP = 4
N_EXPERT_GROUPS = 4
D_EXPERT = 256

TQ = 256
TK = 256
TK_LAT = 1024
BOUND_SLACK = 1.0 + 2.0 ** -7
MIN_DENOMINATOR = 2.0 ** -80
SWA_T = 128
LANES = 128
ONES_ROWS = 16
VMEM_LIMIT = 56 * 1024 * 1024

_C_QA, _C_QAS, _C_KA, _C_KAS = 0, 256, 512, 768
_C_QB, _C_QBS, _C_KB, _C_KBS = 1024, 1280, 1536, 1664
_C_QD, _C_QDS, _C_KD, _C_KDS = 1792, 2048, 2304, 2432
_C_CQ, _C_CKV, _C_KR, _C_KRS = 2560, 2816, 2944, 3072
_C_TOTAL = 3200
_R_VA, _R_VB, _R_VD, _R_TOTAL = 0, 256, 384, 512


def _nt_dot(a, b):
    return lax.dot_general(a, b, (((1,), (1,)), ((), ())), preferred_element_type=F32)


def _split_bf16(a):
    hi = a.astype(BF16)
    lo = (a - hi.astype(F32)).astype(BF16)
    return hi, lo


def _layer_norm(x):
    mu = jnp.mean(x, axis=-1, keepdims=True)
    xc = x - mu
    var = jnp.mean(xc * xc, axis=-1, keepdims=True)
    return xc * lax.rsqrt(var + EPS)


def _tile_lanes(a, n):
    return jnp.concatenate([a] * n, axis=1)


def _ada_kernel(cc_ref, w_ref, b_ref, o_ref):
    cc = cc_ref[...]
    s = cc * jax.nn.sigmoid(cc)
    s_hi, s_lo = _split_bf16(s)
    w_hi, w_lo = _split_bf16(w_ref[...])
    acc = jnp.dot(s_hi, w_hi, preferred_element_type=F32)
    acc += jnp.dot(s_hi, w_lo, preferred_element_type=F32)
    acc += jnp.dot(s_lo, w_hi, preferred_element_type=F32)
    o_ref[...] = acc + b_ref[...]


def _ada(cc, w_ada, b_ada):
    depth, d, n6 = w_ada.shape
    bn = 1536
    return pl.pallas_call(
        _ada_kernel,
        grid=(depth, n6 // bn),
        in_specs=[
            pl.BlockSpec((8, d), lambda l, j: (0, 0)),
            pl.BlockSpec((None, d, bn), lambda l, j: (l, 0, j)),
            pl.BlockSpec((None, 1, bn), lambda l, j: (l, 0, j)),
        ],
        out_specs=pl.BlockSpec((None, 8, bn), lambda l, j: (l, 0, j)),
        out_shape=jax.ShapeDtypeStruct((depth, 8, n6), F32),
        compiler_params=pltpu.CompilerParams(
            dimension_semantics=("arbitrary", "arbitrary"), vmem_limit_bytes=VMEM_LIMIT),
        name="ada",
    )(cc, w_ada, b_ada.reshape(depth, 1, n6))


def _block_diag_ones(n, blk):
    sh = int(math.log2(blk))
    r = lax.broadcasted_iota(jnp.int32, (n, n), 0) >> sh
    c = lax.broadcasted_iota(jnp.int32, (n, n), 1) >> sh
    return jnp.where(r == c, 1.0, 0.0).astype(BF16)


def _proj_kernel(x_ref, mod_ref, wrm_ref, wt_ref, wuq_ref, wuqs_ref, wukn_ref, wuvt_ref,
                 gqb_ref, gqbs_ref, gkb_ref, gkbs_ref, gcq_ref, gckv_ref,
                 cos32_ref, sin32_ref, cos64_ref, sin64_ref, cosc_ref, sinc_ref,
                 qa_ref, ka_ref, vat_ref, qb_ref, kb_ref, vbt_ref,
                 qc_ref, kc_ref, vct_ref, qd_ref, kd_ref, vdt_ref):
    d = D_MODEL
    xn = _layer_norm(x_ref[...])
    h = (xn * (1.0 + mod_ref[:, d:2 * d]) + mod_ref[:, 0:d]).astype(BF16)

    def cols(lo, width):
        return jnp.dot(h, wrm_ref[:, lo:lo + width], preferred_element_type=F32)

    cos32, sin32 = _tile_lanes(cos32_ref[...], 2), _tile_lanes(sin32_ref[...], 2)
    cos64, sin64 = cos64_ref[...], sin64_ref[...]
    cos64w, sin64w = _tile_lanes(cos64, 2), _tile_lanes(sin64, 2)

    sa = (A_QK ** -0.5) * LOG2E
    qa_ref[...] = ((cols(_C_QA, 256) * cos32 + cols(_C_QAS, 256) * sin32) * sa).astype(BF16)
    ka_ref[...] = (cols(_C_KA, 256) * cos32 + cols(_C_KAS, 256) * sin32).astype(BF16)

    sb = (HEAD_DIM ** -0.5) * LOG2E
    q = cols(_C_QB, 256)
    ssq = jnp.dot((q * q).astype(BF16), _block_diag_ones(256, HEAD_DIM), preferred_element_type=F32)
    r = lax.rsqrt(ssq * (1.0 / HEAD_DIM) + EPS)
    qr = (q * gqb_ref[...]) * cos64w + (cols(_C_QBS, 256) * gqbs_ref[...]) * sin64w
    qb_ref[...] = (qr * (r * sb)).astype(BF16)
    k = cols(_C_KB, 128)
    ssq = jnp.dot((k * k).astype(BF16), _block_diag_ones(128, HEAD_DIM), preferred_element_type=F32)
    r = lax.rsqrt(ssq * (1.0 / HEAD_DIM) + EPS)
    kr = (k * gkb_ref[...]) * cos64 + (cols(_C_KBS, 128) * gkbs_ref[...]) * sin64
    kb_ref[...] = (kr * r).astype(BF16)

    qd_ref[...] = ((cols(_C_QD, 256) * cos64w + cols(_C_QDS, 256) * sin64w) * sb).astype(BF16)
    kd_ref[...] = (cols(_C_KD, 128) * cos64 + cols(_C_KDS, 128) * sin64).astype(BF16)

    sc = ((C_NOPE + C_ROPE) ** -0.5) * LOG2E
    cosc, sinc = cosc_ref[...], sinc_ref[...]
    cq = cols(_C_CQ, 256)
    ms = jnp.sum(cq * cq, axis=-1, keepdims=True) * (1.0 / C_Q_RANK)
    cqn = (cq * lax.rsqrt(ms + EPS) * gcq_ref[...]).astype(BF16)
    qn = jnp.dot(cqn, wuq_ref[...], preferred_element_type=F32)
    qs = jnp.dot(cqn, wuqs_ref[...], preferred_element_type=F32)
    qc_ref[...] = ((qn * _tile_lanes(cosc, 4) + qs * _tile_lanes(sinc, 4)) * sc).astype(BF16)
    ckv = cols(_C_CKV, 128)
    ms = jnp.mean(ckv * ckv, axis=-1, keepdims=True)
    ckvn = (ckv * lax.rsqrt(ms + EPS) * gckv_ref[...]).astype(BF16)
    kn = jnp.dot(ckvn, wukn_ref[...], preferred_element_type=F32)
    krr = cols(_C_KR, 128) * cosc + cols(_C_KRS, 128) * sinc
    kc_ref[...] = (kn + _tile_lanes(krr, 4)).astype(BF16)
    vct_ref[...] = _nt_dot(wuvt_ref[...], ckvn).astype(BF16)

    vt = _nt_dot(wt_ref[...], h)
    vat_ref[...] = vt[_R_VA:_R_VA + 256].astype(BF16)
    vbt_ref[...] = vt[_R_VB:_R_VB + 128].astype(BF16)
    vdt_ref[...] = vt[_R_VD:_R_VD + 128].astype(BF16)


def _proj(xa, modsel, lw, tabs):
    b, nt, d = xa.shape
    nc = nt // TQ

    def full(a):
        nd = a.ndim
        return pl.BlockSpec(a.shape, lambda bi, i, _n=nd: (0,) * _n)

    def tab(a):
        return pl.BlockSpec((TQ, LANES), lambda bi, i: (i, 0))

    def rm(w):
        return pl.BlockSpec((None, TQ, w), lambda bi, i: (bi, i, 0))

    def tr(w):
        return pl.BlockSpec((None, None, w, TQ), lambda bi, i: (bi, i, 0, 0))

    weights = [lw["wrm"], lw["wt"], lw["wuq"], lw["wuqs"], lw["wukn"], lw["wuvt"],
               lw["gqb"], lw["gqbs"], lw["gkb"], lw["gkbs"], lw["gcq"], lw["gckv"]]
    tables = [tabs["cos32"], tabs["sin32"], tabs["cos64"], tabs["sin64"], tabs["cosc"], tabs["sinc"]]
    in_specs = ([pl.BlockSpec((None, TQ, d), lambda bi, i: (bi, i, 0)),
                 pl.BlockSpec((None, None, 1, 6 * d), lambda bi, i: (bi, jnp.minimum(i, 1), 0, 0))]
                + [full(w) for w in weights] + [tab(t) for t in tables])
    widths = [("rm", 256), ("rm", 256), ("tr", 256), ("rm", 256), ("rm", 128), ("tr", 128),
              ("rm", 512), ("rm", 512), ("tr", 256), ("rm", 256), ("rm", 128), ("tr", 128)]
    out_specs, out_shape = [], []
    for kind, w in widths:
        if kind == "rm":
            out_specs.append(rm(w))
            out_shape.append(jax.ShapeDtypeStruct((b, nt, w), BF16))
        else:
            out_specs.append(tr(w))
            out_shape.append(jax.ShapeDtypeStruct((b, nc, w, TQ), BF16))
    return pl.pallas_call(
        _proj_kernel,
        grid=(b, nc),
        in_specs=in_specs,
        out_specs=out_specs,
        out_shape=out_shape,
        compiler_params=pltpu.CompilerParams(
            dimension_semantics=("arbitrary", "arbitrary"), vmem_limit_bytes=VMEM_LIMIT),
        name="proj",
    )(xa, modsel, *weights, *tables)


def _flash_loop(qi, nc, k_ref, vt_ref, qz_ref, m_ref, alpha_ref, acc_ref, s0_ref, s1_ref, kabs_ref, k_lanes, pv_cfg,
                slab_rows, bounded_pieces, groups_per_sweep):
    per = TK_LAT // TK
    n_lat = (nc - 1) // per
    assert n_lat % 2 == 0 and n_lat >= 2
    same_lanes = all(kl == k_lanes[0] for kl in k_lanes)

    def scores(kc):
        if same_lanes:
            return jnp.dot(kc[:, k_lanes[0][0]:k_lanes[0][1]], qz_ref[...], preferred_element_type=F32)
        return jnp.concatenate(
            [jnp.dot(kc[:, lo:hi], qz_ref[:, TQ * j:TQ * (j + 1)], preferred_element_type=F32)
             for j, (lo, hi) in enumerate(k_lanes)], axis=1)

    def small_chunk(pieces, first):
        if slab_rows >= pieces * TK:
            return (1, pieces, first)
        return (pieces * TK // slab_rows, slab_rows // TK, first)

    ctx_chunk = small_chunk(1, 0)

    def lat_chunk(c):
        return small_chunk(per, 1 + c * per)

    def stats_update(mx):
        m_old = m_ref[...]
        m_new = jnp.maximum(m_old, mx)
        alpha_ref[...] = jnp.exp2(m_old - m_new)
        m_ref[...] = m_new

    def pipelined(cur, cur_ref, nxt, nxt_ref):
        m_cur, alpha = m_ref[...], alpha_ref[...]
        mx, pv = None, [None] * len(pv_cfg)
        n_cur = cur[0] if cur is not None else 0
        n_nxt = nxt[0] if nxt is not None else 0
        for r in range(max(n_cur, n_nxt)):
            if r < n_nxt:
                rows = TK * nxt[1]
                kc = k_ref[pl.ds(pl.multiple_of((nxt[2] + r * nxt[1]) * TK, TK), rows), :]
                s = scores(kc)
                nxt_ref[rows * r:rows * (r + 1), :] = s
                smx = jnp.max(s, axis=0, keepdims=True)
                mx = smx if mx is None else jnp.maximum(mx, smx)
            if r < n_cur:
                rows = TK * cur[1]
                p = jnp.exp2(cur_ref[rows * r:rows * (r + 1), :] - m_cur).astype(BF16)
                vc = jnp.concatenate([vt_ref[cur[2] + r * cur[1] + i] for i in range(cur[1])], axis=1)
                ones = jnp.ones((ONES_ROWS, rows), BF16)
                for g, (vl, vh, j0, nj) in enumerate(pv_cfg):
                    va = jnp.concatenate([vc[vl:vh, :], ones], axis=0)
                    d = jnp.dot(va, p[:, TQ * j0:TQ * (j0 + nj)], preferred_element_type=F32)
                    pv[g] = d if pv[g] is None else pv[g] + d
        if cur is not None:
            for g, (vl, vh, j0, nj) in enumerate(pv_cfg):
                cols = slice(TQ * j0, TQ * (j0 + nj))
                acc_ref[:, cols] = alpha[:, cols] * acc_ref[:, cols] + pv[g]
        if nxt is not None:
            stats_update(mx)

    def online_pass():
        m_ref[...] = jnp.full(m_ref.shape, NEG, F32)
        acc_ref[...] = jnp.zeros(acc_ref.shape, F32)

        @pl.when(qi == 0)
        def _():
            pipelined(None, None, ctx_chunk, s0_ref)
            pipelined(ctx_chunk, s0_ref, None, None)

        @pl.when(qi > 0)
        def _():
            pipelined(None, None, lat_chunk(0), s0_ref)

            def body(i, carry):
                c = 2 * i
                pipelined(lat_chunk(c), s0_ref, lat_chunk(c + 1), s1_ref)
                pipelined(lat_chunk(c + 1), s1_ref, lat_chunk(c + 2), s0_ref)
                return carry

            lax.fori_loop(0, n_lat // 2 - 1, body, 0)
            pipelined(lat_chunk(n_lat - 2), s0_ref, lat_chunk(n_lat - 1), s1_ref)
            pipelined(lat_chunk(n_lat - 1), s1_ref, ctx_chunk, s0_ref)
            pipelined(ctx_chunk, s0_ref, None, None)

    def bounded_pass():
        wk = kabs_ref.shape[1]
        kab = jnp.broadcast_to(kabs_ref[...], (ONES_ROWS, wk)).astype(BF16)
        aq = jnp.abs(qz_ref[...])
        if same_lanes:
            lo, hi = k_lanes[0]
            bound = jnp.dot(kab[:, lo:hi], aq, preferred_element_type=F32)[0:1]
        else:
            bound = jnp.concatenate(
                [jnp.dot(kab[:, lo:hi], aq[:, TQ * j:TQ * (j + 1)], preferred_element_type=F32)[0:1]
                 for j, (lo, hi) in enumerate(k_lanes)], axis=1)
        bound = bound * BOUND_SLACK
        ones = jnp.ones((ONES_ROWS, TK), BF16)
        acc_ref[...] = jnp.zeros(acc_ref.shape, F32)

        def pieces(first, n, groups):
            pv = {g: None for g in groups}

            def probs(r, g):
                _, _, j0, nj = pv_cfg[g]
                kc = k_ref[pl.ds(pl.multiple_of((first + r) * TK, TK), TK), :]
                if all(k_lanes[j] == k_lanes[j0] for j in range(j0, j0 + nj)):
                    s = jnp.dot(kc[:, k_lanes[j0][0]:k_lanes[j0][1]], qz_ref[:, TQ * j0:TQ * (j0 + nj)],
                                preferred_element_type=F32)
                else:
                    s = jnp.concatenate(
                        [jnp.dot(kc[:, k_lanes[j][0]:k_lanes[j][1]], qz_ref[:, TQ * j:TQ * (j + 1)],
                                 preferred_element_type=F32) for j in range(j0, j0 + nj)], axis=1)
                return jnp.exp2(s - bound[:, TQ * j0:TQ * (j0 + nj)]).astype(BF16)

            def values(r, g, p):
                vl, vh, _, _ = pv_cfg[g]
                va = jnp.concatenate([vt_ref[first + r][vl:vh, :], ones], axis=0)
                d = jnp.dot(va, p, preferred_element_type=F32)
                pv[g] = d if pv[g] is None else pv[g] + d

            p_prev = {g: probs(0, g) for g in groups}
            for r in range(1, n):
                p_cur = {}
                for g in groups:
                    p_cur[g] = probs(r, g)
                    values(r - 1, g, p_prev[g])
                p_prev = p_cur
            for g in groups:
                values(n - 1, g, p_prev[g])
            for g in groups:
                _, _, j0, nj = pv_cfg[g]
                cols = slice(TQ * j0, TQ * (j0 + nj))
                acc_ref[:, cols] = acc_ref[:, cols] + pv[g]

        n_sets = len(pv_cfg) // groups_per_sweep
        for st in range(n_sets):
            groups = list(range(st * groups_per_sweep, (st + 1) * groups_per_sweep))
            pieces(0, 1, groups)

            @pl.when(qi > 0)
            def _(groups=groups):
                def body(i, carry):
                    pieces(1 + i * bounded_pieces, bounded_pieces, groups)
                    return carry

                lax.fori_loop(0, (nc - 1) // bounded_pieces, body, 0)

    @pl.when(qi == 0)
    def _():
        def body(c, mx):
            blk = k_ref[pl.ds(pl.multiple_of(c * TK, TK), TK), :].astype(F32)
            return jnp.maximum(mx, jnp.max(jnp.abs(blk), axis=0, keepdims=True))

        kabs_ref[...] = lax.fori_loop(0, nc, body, jnp.zeros(kabs_ref.shape, F32))

    bounded_pass()
    dv = acc_ref.shape[0] - ONES_ROWS
    healthy = jnp.min(jnp.where(acc_ref[dv:dv + 1, :] >= MIN_DENOMINATOR, 1.0, 0.0))

    @pl.when(healthy < 0.5)
    def _():
        online_pass()


def _normalized(acc_ref, j, dv):
    a = acc_ref[:, TQ * j:TQ * (j + 1)]
    return a[0:dv] / a[dv:dv + 1]


def _attn_a_kernel(q_ref, k_ref, vt_ref, lq1_ref, lk1_ref, lq2_ref, lk2_ref, g_ref, o_ref,
                   qz_ref, m_ref, alpha_ref, acc_ref, s0_ref, s1_ref, kabs_ref, *, nc, lam_init):
    qi = pl.program_id(1)
    qt = q_ref[...].astype(F32).T
    row = lax.broadcasted_iota(jnp.int32, (LANES, TQ), 0)
    per_blk = LANES // A_QK
    k_lanes = []
    for j in range(2 * A_HEADS):
        blk, sub = j // per_blk, j % per_blk
        keep = (row >= A_QK * sub) & (row < A_QK * (sub + 1))
        qz_ref[:, TQ * j:TQ * (j + 1)] = jnp.where(keep, qt[LANES * blk:LANES * (blk + 1)], 0.0).astype(BF16)
        k_lanes.append((LANES * blk, LANES * (blk + 1)))
    pv_cfg = [(HEAD_DIM * hd, HEAD_DIM * (hd + 1), 2 * hd, 2) for hd in range(A_HEADS)]
    _flash_loop(qi, nc, k_ref, vt_ref, qz_ref, m_ref, alpha_ref, acc_ref, s0_ref, s1_ref, kabs_ref, k_lanes, pv_cfg,
                slab_rows=1024, bounded_pieces=16, groups_per_sweep=2)

    lam = (jnp.exp(jnp.sum(lq1_ref[...] * lk1_ref[...], axis=-1, keepdims=True))
           - jnp.exp(jnp.sum(lq2_ref[...] * lk2_ref[...], axis=-1, keepdims=True)) + lam_init)
    outs = []
    for hd in range(A_HEADS):
        o = _normalized(acc_ref, 2 * hd, HEAD_DIM) - lam * _normalized(acc_ref, 2 * hd + 1, HEAD_DIM)
        ms = jnp.mean(o * o, axis=0, keepdims=True)
        outs.append(o * lax.rsqrt(ms + EPS) * g_ref[...] * (1.0 - lam_init))
    o_ref[...] = jnp.concatenate(outs, axis=0).T.astype(BF16)


def _attn_b_kernel(q_ref, k_ref, vt_ref, o_ref, qz_ref, m_ref, alpha_ref, acc_ref, s0_ref, s1_ref, kabs_ref, *, nc):
    qi = pl.program_id(1)
    qt = q_ref[...].astype(F32).T
    zeros = jnp.zeros((HEAD_DIM, TQ), F32)
    rep = B_HEADS // B_KV_HEADS
    for hd in range(B_HEADS):
        parts = [zeros] * B_KV_HEADS
        parts[hd // rep] = qt[HEAD_DIM * hd:HEAD_DIM * (hd + 1)]
        qz_ref[:, TQ * hd:TQ * (hd + 1)] = jnp.concatenate(parts, axis=0).astype(BF16)
    k_lanes = [(0, 128)] * B_HEADS
    pv_cfg = [(HEAD_DIM * g, HEAD_DIM * (g + 1), rep * g, rep) for g in range(B_KV_HEADS)]
    _flash_loop(qi, nc, k_ref, vt_ref, qz_ref, m_ref, alpha_ref, acc_ref, s0_ref, s1_ref, kabs_ref, k_lanes, pv_cfg,
                slab_rows=256, bounded_pieces=16, groups_per_sweep=B_KV_HEADS)
    outs = [_normalized(acc_ref, hd, HEAD_DIM) for hd in range(B_HEADS)]
    o_ref[...] = jnp.concatenate(outs, axis=0).T.astype(BF16)


def _attn_c_kernel(q_ref, k_ref, vt_ref, o_ref, qz_ref, m_ref, alpha_ref, acc_ref, s0_ref, s1_ref, kabs_ref, *, nc):
    qi = pl.program_id(1)
    qt = q_ref[...].astype(F32).T
    k_lanes, pv_cfg = [], []
    for hd in range(C_HEADS):
        qz_ref[:, TQ * hd:TQ * (hd + 1)] = qt[LANES * hd:LANES * (hd + 1)].astype(BF16)
        k_lanes.append((LANES * hd, LANES * (hd + 1)))
        pv_cfg.append((HEAD_DIM * hd, HEAD_DIM * (hd + 1), hd, 1))
    _flash_loop(qi, nc, k_ref, vt_ref, qz_ref, m_ref, alpha_ref, acc_ref, s0_ref, s1_ref, kabs_ref, k_lanes, pv_cfg,
                slab_rows=256, bounded_pieces=16, groups_per_sweep=C_HEADS)
    outs = [_normalized(acc_ref, hd, HEAD_DIM) for hd in range(C_HEADS)]
    o_ref[...] = jnp.concatenate(outs, axis=0).T.astype(BF16)


def _attn_full(kind, q, k, vt, extras=(), lam_init=0.0):
    b, nt, wq = q.shape
    wk = k.shape[-1]
    nc, wv = vt.shape[1], vt.shape[2]
    assert (nt - TK) % TK_LAT == 0
    if kind == "a":
        body, n_sm, dk = functools.partial(_attn_a_kernel, nc=nc, lam_init=lam_init), 2 * A_HEADS, 128
    elif kind == "b":
        body, n_sm, dk = functools.partial(_attn_b_kernel, nc=nc), B_HEADS, 128
    else:
        body, n_sm, dk = functools.partial(_attn_c_kernel, nc=nc), C_HEADS, 128
    resident = pl.Buffered(1)
    in_specs = [
        pl.BlockSpec((None, TQ, wq), lambda bi, i: (bi, i, 0)),
        pl.BlockSpec((None, nt, wk), lambda bi, i: (bi, 0, 0), pipeline_mode=resident),
        pl.BlockSpec((None, nc, wv, TK), lambda bi, i: (bi, 0, 0, 0), pipeline_mode=resident),
    ] + [pl.BlockSpec(e.shape, lambda bi, i: (0, 0)) for e in extras]
    return pl.pallas_call(
        body,
        grid=(b, nc),
        in_specs=in_specs,
        out_specs=pl.BlockSpec((None, TQ, GROUP_WIDTH), lambda bi, i: (bi, i, 0)),
        out_shape=jax.ShapeDtypeStruct((b, nt, GROUP_WIDTH), BF16),
        scratch_shapes=[
            pltpu.VMEM((dk, n_sm * TQ), BF16),
            pltpu.VMEM((1, n_sm * TQ), F32),
            pltpu.VMEM((1, n_sm * TQ), F32),
            pltpu.VMEM((HEAD_DIM + ONES_ROWS, n_sm * TQ), F32),
            pltpu.VMEM((TK_LAT, n_sm * TQ), F32),
            pltpu.VMEM((TK_LAT, n_sm * TQ), F32),
            pltpu.VMEM((1, wk), F32),
        ],
        compiler_params=pltpu.CompilerParams(
            dimension_semantics=("arbitrary", "arbitrary"), vmem_limit_bytes=VMEM_LIMIT),
        name="attn_" + kind,
    )(q, k, vt, *extras)


def _attn_d_kernel(q_ref, kc_ref, kp_ref, ko_ref, kn_ref, vc_ref, vp_ref, vo_ref, vn_ref, sink_ref,
                   o_ref, *, nb, ctx_blocks):
    n = pl.program_id(1)
    t = SWA_T
    rep = D_HEADS // D_KV_HEADS
    qt = q_ref[...].astype(F32).T
    zeros = jnp.zeros((HEAD_DIM, t), F32)
    cols = []
    for hd in range(D_HEADS):
        parts = [zeros] * D_KV_HEADS
        parts[hd // rep] = qt[HEAD_DIM * hd:HEAD_DIM * (hd + 1)]
        cols.append(jnp.concatenate(parts, axis=0))
    qz = jnp.concatenate(cols, axis=1).astype(BF16)
    w = D_HEADS * t
    ko_i = lax.broadcasted_iota(jnp.int32, (t, w), 0)
    qo_i = lax.broadcasted_iota(jnp.int32, (t, w), 1) & (t - 1)
    band = n >= ctx_blocks
    ok_prev = jnp.logical_and(n >= ctx_blocks + 1, qo_i <= ko_i)
    ok_own = jnp.logical_and(band, ko_i >= 0)
    ok_next = jnp.logical_and(jnp.logical_and(band, n + 1 <= nb - 1), ko_i <= qo_i)
    s_c = jnp.dot(kc_ref[...], qz, preferred_element_type=F32)
    s_p = jnp.where(ok_prev, jnp.dot(kp_ref[...], qz, preferred_element_type=F32), NEG)
    s_o = jnp.where(ok_own, jnp.dot(ko_ref[...], qz, preferred_element_type=F32), NEG)
    s_n = jnp.where(ok_next, jnp.dot(kn_ref[...], qz, preferred_element_type=F32), NEG)
    sk = jnp.concatenate([jnp.broadcast_to(sink_ref[0:1, hd:hd + 1] * LOG2E, (1, t))
                          for hd in range(D_HEADS)], axis=1)
    m = jnp.maximum(jnp.max(s_c, axis=0, keepdims=True), jnp.max(s_p, axis=0, keepdims=True))
    m = jnp.maximum(m, jnp.max(s_o, axis=0, keepdims=True))
    m = jnp.maximum(m, jnp.max(s_n, axis=0, keepdims=True))
    m = jnp.maximum(m, sk)
    pieces = [(vc_ref[...], jnp.exp2(s_c - m).astype(BF16)), (vp_ref[...], jnp.exp2(s_p - m).astype(BF16)),
              (vo_ref[...], jnp.exp2(s_o - m).astype(BF16)), (vn_ref[...], jnp.exp2(s_n - m).astype(BF16))]
    l_sink = jnp.exp2(sk - m)
    outs = []
    for g in range(D_KV_HEADS):
        lanes = slice(rep * t * g, rep * t * (g + 1))
        o = None
        for v, p in pieces:
            va = jnp.concatenate([v[HEAD_DIM * g:HEAD_DIM * (g + 1)],
                                  jnp.ones((ONES_ROWS, v.shape[1]), BF16)], axis=0)
            d = jnp.dot(va, p[:, lanes], preferred_element_type=F32)
            o = d if o is None else o + d
        o = o[0:HEAD_DIM] / (o[HEAD_DIM:HEAD_DIM + 1] + l_sink[:, lanes])
        outs += [o[:, t * r:t * (r + 1)] for r in range(rep)]
    o_ref[...] = jnp.concatenate(outs, axis=0).T.astype(BF16)


def _attn_d(q, k, vt, sink, ctx_len):
    b, nt, wq = q.shape
    wk = k.shape[-1]
    wv = vt.shape[2]
    t = SWA_T
    nb = nt // t
    per = TQ // t
    ctx_blocks = ctx_len // t
    assert ctx_len == TQ

    def kspec(shift):
        return pl.BlockSpec((None, t, wk), lambda bi, i: (bi, jnp.clip(i + shift, 0, nb - 1), 0))

    def vspec(shift):
        def imap(bi, i):
            j = jnp.clip(i + shift, 0, nb - 1)
            return (bi, j // per, 0, j % per)
        return pl.BlockSpec((None, None, wv, t), imap)

    in_specs = [
        pl.BlockSpec((None, t, wq), lambda bi, i: (bi, i, 0)),
        pl.BlockSpec((None, ctx_len, wk), lambda bi, i: (bi, 0, 0)),
        kspec(-1), kspec(0), kspec(1),
        pl.BlockSpec((None, None, wv, ctx_len), lambda bi, i: (bi, 0, 0, 0)),
        vspec(-1), vspec(0), vspec(1),
        pl.BlockSpec(sink.shape, lambda bi, i: (0, 0)),
    ]
    return pl.pallas_call(
        functools.partial(_attn_d_kernel, nb=nb, ctx_blocks=ctx_blocks),
        grid=(b, nb),
        in_specs=in_specs,
        out_specs=pl.BlockSpec((None, t, GROUP_WIDTH), lambda bi, i: (bi, i, 0)),
        out_shape=jax.ShapeDtypeStruct((b, nt, GROUP_WIDTH), BF16),
        compiler_params=pltpu.CompilerParams(
            dimension_semantics=("arbitrary", "arbitrary"), vmem_limit_bytes=VMEM_LIMIT),
        name="attn_d",
    )(q, k, k, k, k, vt, vt, vt, vt, sink)


def _route(sel_rows, s_rows):
    epg = EXPERTS_PER_GROUP
    gscore = []
    for g in range(N_EXPERT_GROUPS):
        r = sel_rows[epg * g:epg * (g + 1)]
        pair = None
        for i in range(epg):
            for j in range(i + 1, epg):
                v = r[i] + r[j]
                pair = v if pair is None else jnp.maximum(pair, v)
        gscore.append(pair)
    best, best_g = gscore[0], jnp.zeros_like(gscore[0], dtype=jnp.int32)
    for g in range(1, N_EXPERT_GROUPS):
        better = gscore[g] > best
        best_g = jnp.where(better, g, best_g)
        best = jnp.where(better, gscore[g], best)
    w = []
    for e in range(N_EXPERTS):
        g = e // epg
        cnt = jnp.zeros_like(best_g)
        for e2 in range(epg * g, epg * (g + 1)):
            if e2 == e:
                continue
            beats = sel_rows[e2] > sel_rows[e]
            if e2 < e:
                beats = jnp.logical_or(beats, sel_rows[e2] == sel_rows[e])
            cnt = cnt + jnp.where(beats, 1, 0)
        chosen = jnp.logical_and(best_g == g, cnt < 2)
        w.append(jnp.where(chosen, s_rows[e], 0.0))
    tot = w[0]
    for e in range(1, N_EXPERTS):
        tot = tot + w[e]
    return [we / tot for we in w]


def _outproj_kernel(oa_ref, ob_ref, oc_ref, od_ref, x_ref, mod_ref, wout_ref, g_ref, b_ref,
                    rwt_ref, rb_ref, x1_ref, h2_ref, gates_ref, *, alpha):
    d = D_MODEL
    o = jnp.concatenate([oa_ref[...], ob_ref[...], oc_ref[...], od_ref[...]], axis=1)
    y = jnp.dot(o, wout_ref[...], preferred_element_type=F32)
    u = alpha * x_ref[...] + mod_ref[:, 2 * d:3 * d] * y
    x1 = _layer_norm(u) * g_ref[...] + b_ref[...]
    x1_ref[...] = x1
    h2 = _layer_norm(x1) * (1.0 + mod_ref[:, 4 * d:5 * d]) + mod_ref[:, 3 * d:4 * d]
    h2_ref[...] = h2.astype(BF16)

    h_hi, h_lo = _split_bf16(h2)
    w_hi, w_lo = _split_bf16(rwt_ref[...])
    logits = _nt_dot(w_hi, h_hi) + _nt_dot(w_hi, h_lo) + _nt_dot(w_lo, h_hi)
    s = jax.nn.sigmoid(logits)
    sel = s + rb_ref[...]
    s_rows = [s[e:e + 1] for e in range(N_EXPERTS)]
    sel_rows = [sel[e:e + 1] for e in range(N_EXPERTS)]
    gates = _route(sel_rows, s_rows)
    gt = jnp.concatenate(gates + [jnp.zeros((LANES - N_EXPERTS, TQ), F32)], axis=0)
    gates_ref[...] = gt.T


def _outproj(oa, ob, oc, od, xa, modsel, w_out, ln_g, ln_b, rwt, rb, alpha):
    b, nt, d = xa.shape
    nc = nt // TQ

    def rm(w):
        return pl.BlockSpec((None, TQ, w), lambda bi, i: (bi, i, 0))

    def full(a):
        nd = a.ndim
        return pl.BlockSpec(a.shape, lambda bi, i, _n=nd: (0,) * _n)

    in_specs = [rm(GROUP_WIDTH) for _ in range(4)] + [
        rm(d),
        pl.BlockSpec((None, None, 1, 6 * d), lambda bi, i: (bi, jnp.minimum(i, 1), 0, 0)),
        full(w_out), full(ln_g), full(ln_b), full(rwt), full(rb)]
    return pl.pallas_call(
        functools.partial(_outproj_kernel, alpha=alpha),
        grid=(b, nc),
        in_specs=in_specs,
        out_specs=[rm(d), rm(d), rm(LANES)],
        out_shape=[jax.ShapeDtypeStruct((b, nt, d), F32), jax.ShapeDtypeStruct((b, nt, d), BF16),
                   jax.ShapeDtypeStruct((b, nt, LANES), F32)],
        compiler_params=pltpu.CompilerParams(
            dimension_semantics=("arbitrary", "arbitrary"), vmem_limit_bytes=VMEM_LIMIT),
        name="outproj",
    )(oa, ob, oc, od, xa, modsel, w_out, ln_g, ln_b, rwt, rb)


def _moe_kernel(h_ref, gates_ref, x_ref, mod_ref, wg_ref, wu_ref, wd_ref, g_ref, b_ref, o_ref, *, alpha):
    d = D_MODEL
    h = h_ref[...]
    gates = gates_ref[...]
    y = jnp.zeros((TQ, d), F32)
    for g in range(N_EXPERT_GROUPS):
        acts = []
        for el in range(EXPERTS_PER_GROUP):
            e = EXPERTS_PER_GROUP * g + el
            hg = jnp.dot(h, wg_ref[e], preferred_element_type=F32)
            hu = jnp.dot(h, wu_ref[e], preferred_element_type=F32)
            a = hg * jax.nn.sigmoid(hg) * hu * gates[:, e:e + 1]
            acts.append(a.astype(BF16))
        y = y + jnp.dot(jnp.concatenate(acts, axis=1), wd_ref[g], preferred_element_type=F32)
    u = alpha * x_ref[...] + mod_ref[:, 5 * d:6 * d] * y
    o_ref[...] = _layer_norm(u) * g_ref[...] + b_ref[...]


def _moe(h2, gates, x1, modsel, wg, wu, wd, ln_g, ln_b, alpha, latents_only):
    b, nt, d = x1.shape
    nc = nt // TQ
    resident = pl.Buffered(1)
    if latents_only:
        out_spec = pl.BlockSpec((None, TQ, d), lambda bi, i: (bi, jnp.maximum(i - 1, 0), 0))
        out_rows = nt - TQ
    else:
        out_spec = pl.BlockSpec((None, TQ, d), lambda bi, i: (bi, i, 0))
        out_rows = nt

    def rm(w):
        return pl.BlockSpec((None, TQ, w), lambda bi, i: (bi, i, 0))

    def res(a):
        nd = a.ndim
        return pl.BlockSpec(a.shape, lambda bi, i, _n=nd: (0,) * _n, pipeline_mode=resident)

    in_specs = [rm(d), rm(LANES), rm(d),
                pl.BlockSpec((None, None, 1, 6 * d), lambda bi, i: (bi, jnp.minimum(i, 1), 0, 0)),
                res(wg), res(wu), res(wd), res(ln_g), res(ln_b)]
    return pl.pallas_call(
        functools.partial(_moe_kernel, alpha=alpha),
        grid=(b, nc),
        in_specs=in_specs,
        out_specs=out_spec,
        out_shape=jax.ShapeDtypeStruct((b, out_rows, d), F32),
        compiler_params=pltpu.CompilerParams(
            dimension_semantics=("arbitrary", "arbitrary"), vmem_limit_bytes=VMEM_LIMIT),
        name="moe",
    )(h2, gates, x1, modsel, wg, wu, wd, ln_g, ln_b)


def _rope_perm(rot_dim):
    n = rot_dim // 4
    j = np.arange(rot_dim)
    return j ^ n, np.where((j // n) % 2 == 0, -1.0, 1.0).astype(np.float32)


def _swapped(w, rot_dim):
    perm, sign = _rope_perm(rot_dim)
    cols = w.shape[-1]
    idx = (np.arange(cols) // rot_dim) * rot_dim + perm[np.arange(cols) % rot_dim]
    sgn = sign[np.arange(cols) % rot_dim]
    return w[..., idx] * sgn


def _rope_tables(n_lat, ctx_len):
    t = jnp.arange(n_lat, dtype=jnp.int32)
    row, col = (t // GRID_W).astype(F32), (t % GRID_W).astype(F32)

    def pattern(rot_dim):
        n = rot_dim // 4
        inv = ROPE_THETA ** (-jnp.arange(n, dtype=F32) / n)
        ar, ac = row[:, None] * inv, col[:, None] * inv
        ang = jnp.concatenate([ar, ar, ac, ac], axis=-1)
        return jnp.cos(ang), jnp.sin(ang)

    def with_ctx(a, fill):
        return jnp.concatenate([jnp.full((ctx_len, a.shape[1]), fill, F32), a], axis=0)

    c32, s32 = pattern(A_QK)
    c64, s64 = pattern(HEAD_DIM)
    ones = jnp.ones((n_lat, C_NOPE), F32)
    zeros = jnp.zeros((n_lat, C_NOPE), F32)
    pad1 = jnp.ones((n_lat, LANES - C_NOPE - C_ROPE), F32)
    pad0 = jnp.zeros((n_lat, LANES - C_NOPE - C_ROPE), F32)
    return {
        "cos32": with_ctx(jnp.tile(c32, (1, LANES // A_QK)), 1.0),
        "sin32": with_ctx(jnp.tile(s32, (1, LANES // A_QK)), 0.0),
        "cos64": with_ctx(jnp.tile(c64, (1, LANES // HEAD_DIM)), 1.0),
        "sin64": with_ctx(jnp.tile(s64, (1, LANES // HEAD_DIM)), 0.0),
        "cosc": with_ctx(jnp.concatenate([ones, c32, pad1], axis=-1), 1.0),
        "sinc": with_ctx(jnp.concatenate([zeros, s32, pad0], axis=-1), 0.0),
    }


def _prep_layer_weights(w_in, gq, gk, gcq, gckv, w_uq, w_ukv):
    d = w_in.shape[0]
    splits = np.cumsum([256, 256, 256, 256, 128, 128, C_Q_RANK, C_KV_RANK, C_ROPE, 256, 128, 128])[:-1]
    (a_q, a_k, a_v, b_q, b_k, b_v, c_q, c_kv, c_kr, d_q, d_k, d_v) = jnp.split(w_in, splits, axis=1)
    z = lambda n: jnp.zeros((d, n), F32)
    kr4 = jnp.concatenate([z(C_NOPE), c_kr, z(LANES - C_NOPE - C_ROPE)], axis=1)
    kr4s = jnp.concatenate([z(C_NOPE), _swapped(c_kr, C_ROPE), z(LANES - C_NOPE - C_ROPE)], axis=1)
    wrm = jnp.concatenate([
        a_q, _swapped(a_q, A_QK), a_k, _swapped(a_k, A_QK),
        b_q, _swapped(b_q, HEAD_DIM), b_k, _swapped(b_k, HEAD_DIM),
        d_q, _swapped(d_q, HEAD_DIM), d_k, _swapped(d_k, HEAD_DIM),
        c_q, z(256 - C_Q_RANK), c_kv, kr4, kr4s], axis=1)
    wt = jnp.concatenate([a_v, b_v, d_v], axis=1).T

    uq = w_uq.reshape(C_Q_RANK, C_HEADS, C_NOPE + C_ROPE)
    uq_n, uq_r = uq[..., :C_NOPE], uq[..., C_NOPE:]
    zq = lambda n: jnp.zeros((C_Q_RANK, C_HEADS, n), F32)
    pad = LANES - C_NOPE - C_ROPE
    wuq = jnp.concatenate([uq_n, uq_r, zq(pad)], axis=-1).reshape(C_Q_RANK, C_HEADS * LANES)
    wuqs = jnp.concatenate([zq(C_NOPE), _swapped(uq_r, C_ROPE), zq(pad)], axis=-1).reshape(C_Q_RANK, C_HEADS * LANES)
    zrows = jnp.zeros((256 - C_Q_RANK, C_HEADS * LANES), F32)
    wuq, wuqs = jnp.concatenate([wuq, zrows], axis=0), jnp.concatenate([wuqs, zrows], axis=0)
    ukv = w_ukv.reshape(C_KV_RANK, C_HEADS, C_NOPE + HEAD_DIM)
    uk_n, u_v = ukv[..., :C_NOPE], ukv[..., C_NOPE:]
    wukn = jnp.concatenate([uk_n, jnp.zeros((C_KV_RANK, C_HEADS, LANES - C_NOPE), F32)], axis=-1)
    wukn = wukn.reshape(C_KV_RANK, C_HEADS * LANES)
    wuvt = u_v.reshape(C_KV_RANK, C_HEADS * HEAD_DIM).T

    perm64, _ = _rope_perm(HEAD_DIM)
    return {
        "wrm": wrm.astype(BF16), "wt": wt.astype(BF16),
        "wuq": wuq.astype(BF16), "wuqs": wuqs.astype(BF16),
        "wukn": wukn.astype(BF16), "wuvt": wuvt.astype(BF16),
        "gqb": jnp.tile(gq, B_HEADS)[None, :], "gqbs": jnp.tile(gq[perm64], B_HEADS)[None, :],
        "gkb": jnp.tile(gk, B_KV_HEADS)[None, :], "gkbs": jnp.tile(gk[perm64], B_KV_HEADS)[None, :],
        "gcq": jnp.concatenate([gcq, jnp.zeros((256 - C_Q_RANK,), F32)])[None, :],
        "gckv": gckv[None, :],
    }


def kernel(x, c, ctx, c_ctx, w_ada, b_ada, w_in, w_out, diff_lambda_q1, diff_lambda_k1, diff_lambda_q2,
           diff_lambda_k2, diff_subln_g, gqa_q_norm_g, gqa_k_norm_g, mla_q_norm_g, mla_kv_norm_g, mla_w_uq,
           mla_w_ukv, swa_sink, ln1_g, ln1_b, ln2_g, ln2_b, router_w, router_bias,
           exp_w_gate, exp_w_up, exp_w_down):
    b, n_lat, d = x.shape
    ctx_len = ctx.shape[1]
    depth = w_ada.shape[0]
    assert d == D_MODEL and ctx_len == TQ and n_lat % TQ == 0 and b + 1 <= 8
    alpha = (2 * depth) ** 0.25

    xa = jnp.concatenate([ctx, x], axis=1)
    tabs = _rope_tables(n_lat, ctx_len)

    cc = jnp.concatenate([c, c_ctx[None, :], jnp.zeros((8 - b - 1, d), F32)], axis=0)
    mods = _ada(cc, w_ada, b_ada)
    rwt = router_w.T
    rb = router_bias[:, None]

    for l in range(depth):
        lat = mods[l, :b]
        cx = jnp.broadcast_to(mods[l, b], lat.shape)
        modsel = jnp.stack([cx, lat], axis=1)[:, :, None, :]
        lw = _prep_layer_weights(w_in[l], gqa_q_norm_g[l], gqa_k_norm_g[l], mla_q_norm_g[l],
                                 mla_kv_norm_g[l], mla_w_uq[l], mla_w_ukv[l])
        qa, ka, vat, qb, kb, vbt, qc, kc, vct, qd, kd, vdt = _proj(xa, modsel, lw, tabs)

        lam_init = 0.8 - 0.6 * math.exp(-0.3 * l)
        extras = (diff_lambda_q1[l][None, :], diff_lambda_k1[l][None, :], diff_lambda_q2[l][None, :],
                  diff_lambda_k2[l][None, :], diff_subln_g[l][:, None])
        oa = _attn_full("a", qa, ka, vat, extras, lam_init)
        ob = _attn_full("b", qb, kb, vbt)
        oc = _attn_full("c", qc, kc, vct)
        od = _attn_d(qd, kd, vdt, swa_sink[l][None, :], ctx_len)

        x1, h2, gates = _outproj(oa, ob, oc, od, xa, modsel, w_out[l].astype(BF16),
                                 ln1_g[l][None, :], ln1_b[l][None, :], rwt, rb, alpha)
        wg = exp_w_gate[l].astype(BF16)
        wu = exp_w_up[l].astype(BF16)
        wd = exp_w_down[l].astype(BF16).reshape(N_EXPERT_GROUPS, EXPERTS_PER_GROUP * D_EXPERT, d)
        xa = _moe(h2, gates, x1, modsel, wg, wu, wd, ln2_g[l][None, :], ln2_b[l][None, :], alpha,
                  latents_only=(l == depth - 1))
    return xa
```

```python
import functools
import math

import numpy as np
import jax
import jax.numpy as jnp
from jax import lax
from jax.experimental import pallas as pl
from jax.experimental.pallas import tpu as pltpu

F32 = jnp.float32
BF16 = jnp.bfloat16

D_MODEL = 1024
HEAD_DIM = 64
GROUP_WIDTH = 256
GRID_W = 64
ROPE_THETA = 10000.0
EPS = 1e-6
NEG = -1e30
LOG2E = 1.4426950408889634

A_HEADS = 4
A_QK = 32
B_HEADS = 4
B_KV_HEADS = 2
C_HEADS = 4
C_Q_RANK = 192
C_KV_RANK = 128
C_NOPE = 64
C_ROPE = 32
D_HEADS = 4
D_KV_HEADS = 2
WINDOW = 128
N_EXPERTS = 16
EXPERTS_PER_GROUP = 4
N_EXPERT_GROUPS = 4
D_EXPERT = 256

TQ = 256
TK = 256
TK_LAT = 1024
BOUND_SLACK = 1.0 + 2.0 ** -7
MIN_DENOMINATOR = 2.0 ** -80
SWA_T = 128
LANES = 128
ONES_ROWS = 16
VMEM_LIMIT = 56 * 1024 * 1024

_C_QA, _C_QAS, _C_KA, _C_KAS = 0, 256, 512, 768
_C_QB, _C_QBS, _C_KB, _C_KBS = 1024, 1280, 1536, 1664
_C_QD, _C_QDS, _C_KD, _C_KDS = 1792, 2048, 2304, 2432
_C_CQ, _C_CKV, _C_KR, _C_KRS = 2560, 2816, 2944, 3072
_C_TOTAL = 3200
_R_VA, _R_VB, _R_VD, _R_TOTAL = 0, 256, 384, 512


def _nt_dot(a, b):
    return lax.dot_general(a, b, (((1,), (1,)), ((), ())), preferred_element_type=F32)


def _split_bf16(a):
    hi = a.astype(BF16)
    lo = (a - hi.astype(F32)).astype(BF16)
    return hi, lo


def _layer_norm(x):
    mu = jnp.mean(x, axis=-1, keepdims=True)
    xc = x - mu
    var = jnp.mean(xc * xc, axis=-1, keepdims=True)
    return xc * lax.rsqrt(var + EPS)


def _tile_lanes(a, n):
    return jnp.concatenate([a] * n, axis=1)


def _ada_kernel(cc_ref, w_ref, b_ref, o_ref):
    cc = cc_ref[...]
    s = cc * jax.nn.sigmoid(cc)
    s_hi, s_lo = _split_bf16(s)
    w_hi, w_lo = _split_bf16(w_ref[...])
    acc = jnp.dot(s_hi, w_hi, preferred_element_type=F32)
    acc += jnp.dot(s_hi, w_lo, preferred_element_type=F32)
    acc += jnp.dot(s_lo, w_hi, preferred_element_type=F32)
    o_ref[...] = acc + b_ref[...]


def _ada(cc, w_ada, b_ada):
    depth, d, n6 = w_ada.shape
    bn = 1536
    return pl.pallas_call(
        _ada_kernel,
        grid=(depth, n6 // bn),
        in_specs=[
            pl.BlockSpec((8, d), lambda l, j: (0, 0)),
            pl.BlockSpec((None, d, bn), lambda l, j: (l, 0, j)),
            pl.BlockSpec((None, 1, bn), lambda l, j: (l, 0, j)),
        ],
        out_specs=pl.BlockSpec((None, 8, bn), lambda l, j: (l, 0, j)),
        out_shape=jax.ShapeDtypeStruct((depth, 8, n6), F32),
        compiler_params=pltpu.CompilerParams(
            dimension_semantics=("arbitrary", "arbitrary"), vmem_limit_bytes=VMEM_LIMIT),
        name="ada",
    )(cc, w_ada, b_ada.reshape(depth, 1, n6))


def _block_diag_ones(n, blk):
    sh = int(math.log2(blk))
    r = lax.broadcasted_iota(jnp.int32, (n, n), 0) >> sh
    c = lax.broadcasted_iota(jnp.int32, (n, n), 1) >> sh
    return jnp.where(r == c, 1.0, 0.0).astype(BF16)


def _proj_kernel(x_ref, mod_ref, wrm_ref, wt_ref, wuq_ref, wuqs_ref, wukn_ref, wuvt_ref,
                 gqb_ref, gqbs_ref, gkb_ref, gkbs_ref, gcq_ref, gckv_ref,
                 cos32_ref, sin32_ref, cos64_ref, sin64_ref, cosc_ref, sinc_ref,
                 qa_ref, ka_ref, vat_ref, qb_ref, kb_ref, vbt_ref,
                 qc_ref, kc_ref, vct_ref, qd_ref, kd_ref, vdt_ref):
    d = D_MODEL
    xn = _layer_norm(x_ref[...])
    h = (xn * (1.0 + mod_ref[:, d:2 * d]) + mod_ref[:, 0:d]).astype(BF16)

    def cols(lo, width):
        return jnp.dot(h, wrm_ref[:, lo:lo + width], preferred_element_type=F32)

    cos32, sin32 = _tile_lanes(cos32_ref[...], 2), _tile_lanes(sin32_ref[...], 2)
    cos64, sin64 = cos64_ref[...], sin64_ref[...]
    cos64w, sin64w = _tile_lanes(cos64, 2), _tile_lanes(sin64, 2)

    sa = (A_QK ** -0.5) * LOG2E
    qa_ref[...] = ((cols(_C_QA, 256) * cos32 + cols(_C_QAS, 256) * sin32) * sa).astype(BF16)
    ka_ref[...] = (cols(_C_KA, 256) * cos32 + cols(_C_KAS, 256) * sin32).astype(BF16)

    sb = (HEAD_DIM ** -0.5) * LOG2E
    q = cols(_C_QB, 256)
    ssq = jnp.dot((q * q).astype(BF16), _block_diag_ones(256, HEAD_DIM), preferred_element_type=F32)
    r = lax.rsqrt(ssq * (1.0 / HEAD_DIM) + EPS)
    qr = (q * gqb_ref[...]) * cos64w + (cols(_C_QBS, 256) * gqbs_ref[...]) * sin64w
    qb_ref[...] = (qr * (r * sb)).astype(BF16)
    k = cols(_C_KB, 128)
    ssq = jnp.dot((k * k).astype(BF16), _block_diag_ones(128, HEAD_DIM), preferred_element_type=F32)
    r = lax.rsqrt(ssq * (1.0 / HEAD_DIM) + EPS)
    kr = (k * gkb_ref[...]) * cos64 + (cols(_C_KBS, 128) * gkbs_ref[...]) * sin64
    kb_ref[...] = (kr * r).astype(BF16)

    qd_ref[...] = ((cols(_C_QD, 256) * cos64w + cols(_C_QDS, 256) * sin64w) * sb).astype(BF16)
    kd_ref[...] = (cols(_C_KD, 128) * cos64 + cols(_C_KDS, 128) * sin64).astype(BF16)

    sc = ((C_NOPE + C_ROPE) ** -0.5) * LOG2E
    cosc, sinc = cosc_ref[...], sinc_ref[...]
    cq = cols(_C_CQ, 256)
    ms = jnp.sum(cq * cq, axis=-1, keepdims=True) * (1.0 / C_Q_RANK)
    cqn = (cq * lax.rsqrt(ms + EPS) * gcq_ref[...]).astype(BF16)
    qn = jnp.dot(cqn, wuq_ref[...], preferred_element_type=F32)
    qs = jnp.dot(cqn, wuqs_ref[...], preferred_element_type=F32)
    qc_ref[...] = ((qn * _tile_lanes(cosc, 4) + qs * _tile_lanes(sinc, 4)) * sc).astype(BF16)
    ckv = cols(_C_CKV, 128)
    ms = jnp.mean(ckv * ckv, axis=-1, keepdims=True)
    ckvn = (ckv * lax.rsqrt(ms + EPS) * gckv_ref[...]).astype(BF16)
    kn = jnp.dot(ckvn, wukn_ref[...], preferred_element_type=F32)
    krr = cols(_C_KR, 128) * cosc + cols(_C_KRS, 128) * sinc
    kc_ref[...] = (kn + _tile_lanes(krr, 4)).astype(BF16)
    vct_ref[...] = _nt_dot(wuvt_ref[...], ckvn).astype(BF16)

    vt = _nt_dot(wt_ref[...], h)
    vat_ref[...] = vt[_R_VA:_R_VA + 256].astype(BF16)
    vbt_ref[...] = vt[_R_VB:_R_VB + 128].astype(BF16)
    vdt_ref[...] = vt[_R_VD:_R_VD + 128].astype(BF16)


def _proj(xa, modsel, lw, tabs):
    b, nt, d = xa.shape
    nc = nt // TQ

    def full(a):
        nd = a.ndim
        return pl.BlockSpec(a.shape, lambda bi, i, _n=nd: (0,) * _n)

    def tab(a):
        return pl.BlockSpec((TQ, LANES), lambda bi, i: (i, 0))

    def rm(w):
        return pl.BlockSpec((None, TQ, w), lambda bi, i: (bi, i, 0))

    def tr(w):
        return pl.BlockSpec((None, None, w, TQ), lambda bi, i: (bi, i, 0, 0))

    weights = [lw["wrm"], lw["wt"], lw["wuq"], lw["wuqs"], lw["wukn"], lw["wuvt"],
               lw["gqb"], lw["gqbs"], lw["gkb"], lw["gkbs"], lw["gcq"], lw["gckv"]]
    tables = [tabs["cos32"], tabs["sin32"], tabs["cos64"], tabs["sin64"], tabs["cosc"], tabs["sinc"]]
    in_specs = ([pl.BlockSpec((None, TQ, d), lambda bi, i: (bi, i, 0)),
                 pl.BlockSpec((None, None, 1, 6 * d), lambda bi, i: (bi, jnp.minimum(i, 1), 0, 0))]
                + [full(w) for w in weights] + [tab(t) for t in tables])
    widths = [("rm", 256), ("rm", 256), ("tr", 256), ("rm", 256), ("rm", 128), ("tr", 128),
              ("rm", 512), ("rm", 512), ("tr", 256), ("rm", 256), ("rm", 128), ("tr", 128)]
    out_specs, out_shape = [], []
    for kind, w in widths:
        if kind == "rm":
            out_specs.append(rm(w))
            out_shape.append(jax.ShapeDtypeStruct((b, nt, w), BF16))
        else:
            out_specs.append(tr(w))
            out_shape.append(jax.ShapeDtypeStruct((b, nc, w, TQ), BF16))
    return pl.pallas_call(
        _proj_kernel,
        grid=(b, nc),
        in_specs=in_specs,
        out_specs=out_specs,
        out_shape=out_shape,
        compiler_params=pltpu.CompilerParams(
            dimension_semantics=("arbitrary", "arbitrary"), vmem_limit_bytes=VMEM_LIMIT),
        name="proj",
    )(xa, modsel, *weights, *tables)


def _flash_loop(qi, nc, k_ref, vt_ref, qz_ref, m_ref, alpha_ref, acc_ref, s0_ref, s1_ref, kabs_ref, k_lanes, pv_cfg,
                slab_rows, bounded_pieces, groups_per_sweep):
    per = TK_LAT // TK
    n_lat = (nc - 1) // per
    assert n_lat % 2 == 0 and n_lat >= 2 and (nc - 1) % bounded_pieces == 0
    same_lanes = all(kl == k_lanes[0] for kl in k_lanes)

    def scores(kc):
        if same_lanes:
            return jnp.dot(kc[:, k_lanes[0][0]:k_lanes[0][1]], qz_ref[...], preferred_element_type=F32)
        return jnp.concatenate(
            [jnp.dot(kc[:, lo:hi], qz_ref[:, TQ * j:TQ * (j + 1)], preferred_element_type=F32)
             for j, (lo, hi) in enumerate(k_lanes)], axis=1)

    def small_chunk(pieces, first):
        if slab_rows >= pieces * TK:
            return (1, pieces, first)
        return (pieces * TK // slab_rows, slab_rows // TK, first)

    ctx_chunk = small_chunk(1, 0)

    def lat_chunk(c):
        return small_chunk(per, 1 + c * per)

    def stats_update(mx):
        m_old = m_ref[...]
        m_new = jnp.maximum(m_old, mx)
        alpha_ref[...] = jnp.exp2(m_old - m_new)
        m_ref[...] = m_new

    def pipelined(cur, cur_ref, nxt, nxt_ref):
        m_cur, alpha = m_ref[...], alpha_ref[...]
        mx, pv = None, [None] * len(pv_cfg)
        n_cur = cur[0] if cur is not None else 0
        n_nxt = nxt[0] if nxt is not None else 0
        for r in range(max(n_cur, n_nxt)):
            if r < n_nxt:
                rows = TK * nxt[1]
                kc = k_ref[pl.ds(pl.multiple_of((nxt[2] + r * nxt[1]) * TK, TK), rows), :]
                s = scores(kc)
                nxt_ref[rows * r:rows * (r + 1), :] = s
                smx = jnp.max(s, axis=0, keepdims=True)
                mx = smx if mx is None else jnp.maximum(mx, smx)
            if r < n_cur:
                rows = TK * cur[1]
                p = jnp.exp2(cur_ref[rows * r:rows * (r + 1), :] - m_cur).astype(BF16)
                vc = jnp.concatenate([vt_ref[cur[2] + r * cur[1] + i] for i in range(cur[1])], axis=1)
                ones = jnp.ones((ONES_ROWS, rows), BF16)
                for g, (vl, vh, j0, nj) in enumerate(pv_cfg):
                    va = jnp.concatenate([vc[vl:vh, :], ones], axis=0)
                    d = jnp.dot(va, p[:, TQ * j0:TQ * (j0 + nj)], preferred_element_type=F32)
                    pv[g] = d if pv[g] is None else pv[g] + d
        if cur is not None:
            for g, (vl, vh, j0, nj) in enumerate(pv_cfg):
                cols = slice(TQ * j0, TQ * (j0 + nj))
                acc_ref[:, cols] = alpha[:, cols] * acc_ref[:, cols] + pv[g]
        if nxt is not None:
            stats_update(mx)

    def online_pass():
        m_ref[...] = jnp.full(m_ref.shape, NEG, F32)
        acc_ref[...] = jnp.zeros(acc_ref.shape, F32)

        @pl.when(qi == 0)
        def _():
            pipelined(None, None, ctx_chunk, s0_ref)
            pipelined(ctx_chunk, s0_ref, None, None)

        @pl.when(qi > 0)
        def _():
            pipelined(None, None, lat_chunk(0), s0_ref)

            def body(i, carry):
                c = 2 * i
                pipelined(lat_chunk(c), s0_ref, lat_chunk(c + 1), s1_ref)
                pipelined(lat_chunk(c + 1), s1_ref, lat_chunk(c + 2), s0_ref)
                return carry

            lax.fori_loop(0, n_lat // 2 - 1, body, 0)
            pipelined(lat_chunk(n_lat - 2), s0_ref, lat_chunk(n_lat - 1), s1_ref)
            pipelined(lat_chunk(n_lat - 1), s1_ref, ctx_chunk, s0_ref)
            pipelined(ctx_chunk, s0_ref, None, None)

    def bounded_pass():
        wk = kabs_ref.shape[1]
        kab = jnp.broadcast_to(kabs_ref[...], (ONES_ROWS, wk)).astype(BF16)
        aq = jnp.abs(qz_ref[...])
        if same_lanes:
            lo, hi = k_lanes[0]
            bound = jnp.dot(kab[:, lo:hi], aq, preferred_element_type=F32)[0:1]
        else:
            bound = jnp.concatenate(
                [jnp.dot(kab[:, lo:hi], aq[:, TQ * j:TQ * (j + 1)], preferred_element_type=F32)[0:1]
                 for j, (lo, hi) in enumerate(k_lanes)], axis=1)
        bound = bound * BOUND_SLACK
        ones = jnp.ones((ONES_ROWS, TK), BF16)
        acc_ref[...] = jnp.zeros(acc_ref.shape, F32)

        def pieces(first, n, groups):
            pv = {g: None for g in groups}

            def probs(r, g):
                _, _, j0, nj = pv_cfg[g]
                kc = k_ref[pl.ds(pl.multiple_of((first + r) * TK, TK), TK), :]
                if all(k_lanes[j] == k_lanes[j0] for j in range(j0, j0 + nj)):
                    s = jnp.dot(kc[:, k_lanes[j0][0]:k_lanes[j0][1]], qz_ref[:, TQ * j0:TQ * (j0 + nj)],
                                preferred_element_type=F32)
                else:
                    s = jnp.concatenate(
                        [jnp.dot(kc[:, k_lanes[j][0]:k_lanes[j][1]], qz_ref[:, TQ * j:TQ * (j + 1)],
                                 preferred_element_type=F32) for j in range(j0, j0 + nj)], axis=1)
                return jnp.exp2(s - bound[:, TQ * j0:TQ * (j0 + nj)]).astype(BF16)

            def values(r, g, p):
                vl, vh, _, _ = pv_cfg[g]
                va = jnp.concatenate([vt_ref[first + r][vl:vh, :], ones], axis=0)
                d = jnp.dot(va, p, preferred_element_type=F32)
                pv[g] = d if pv[g] is None else pv[g] + d

            p_prev = {g: probs(0, g) for g in groups}
            for r in range(1, n):
                p_cur = {}
                for g in groups:
                    p_cur[g] = probs(r, g)
                    values(r - 1, g, p_prev[g])
                p_prev = p_cur
            for g in groups:
                values(n - 1, g, p_prev[g])
            for g in groups:
                _, _, j0, nj = pv_cfg[g]
                cols = slice(TQ * j0, TQ * (j0 + nj))
                acc_ref[:, cols] = acc_ref[:, cols] + pv[g]

        n_sets = len(pv_cfg) // groups_per_sweep
        for st in range(n_sets):
            groups = list(range(st * groups_per_sweep, (st + 1) * groups_per_sweep))
            pieces(0, 1, groups)

            @pl.when(qi > 0)
            def _(groups=groups):
                def body(i, carry):
                    pieces(1 + i * bounded_pieces, bounded_pieces, groups)
                    return carry

                lax.fori_loop(0, (nc - 1) // bounded_pieces, body, 0)

    @pl.when(qi == 0)
    def _():
        def body(c, mx):
            blk = k_ref[pl.ds(pl.multiple_of(c * TK, TK), TK), :].astype(F32)
            return jnp.maximum(mx, jnp.max(jnp.abs(blk), axis=0, keepdims=True))

        kabs_ref[...] = lax.fori_loop(0, nc, body, jnp.zeros(kabs_ref.shape, F32))

    bounded_pass()
    dv = acc_ref.shape[0] - ONES_ROWS
    healthy = jnp.min(jnp.where(acc_ref[dv:dv + 1, :] >= MIN_DENOMINATOR, 1.0, 0.0))

    @pl.when(healthy < 0.5)
    def _():
        online_pass()


def _normalized(acc_ref, j, dv):
    a = acc_ref[:, TQ * j:TQ * (j + 1)]
    return a[0:dv] / a[dv:dv + 1]


def _attn_a_kernel(q_ref, k_ref, vt_ref, lq1_ref, lk1_ref, lq2_ref, lk2_ref, g_ref, o_ref,
                   qz_ref, m_ref, alpha_ref, acc_ref, s0_ref, s1_ref, kabs_ref, *, nc, lam_init):
    qi = pl.program_id(1)
    qt = q_ref[...].astype(F32).T
    row = lax.broadcasted_iota(jnp.int32, (LANES, TQ), 0)
    per_blk = LANES // A_QK
    k_lanes = []
    for j in range(2 * A_HEADS):
        blk, sub = j // per_blk, j % per_blk
        keep = (row >= A_QK * sub) & (row < A_QK * (sub + 1))
        qz_ref[:, TQ * j:TQ * (j + 1)] = jnp.where(keep, qt[LANES * blk:LANES * (blk + 1)], 0.0).astype(BF16)
        k_lanes.append((LANES * blk, LANES * (blk + 1)))
    pv_cfg = [(HEAD_DIM * hd, HEAD_DIM * (hd + 1), 2 * hd, 2) for hd in range(A_HEADS)]
    _flash_loop(qi, nc, k_ref, vt_ref, qz_ref, m_ref, alpha_ref, acc_ref, s0_ref, s1_ref, kabs_ref, k_lanes, pv_cfg,
                slab_rows=1024, bounded_pieces=32, groups_per_sweep=2)

    lam = (jnp.exp(jnp.sum(lq1_ref[...] * lk1_ref[...], axis=-1, keepdims=True))
           - jnp.exp(jnp.sum(lq2_ref[...] * lk2_ref[...], axis=-1, keepdims=True)) + lam_init)
    outs = []
    for hd in range(A_HEADS):
        o = _normalized(acc_ref, 2 * hd, HEAD_DIM) - lam * _normalized(acc_ref, 2 * hd + 1, HEAD_DIM)
        ms = jnp.mean(o * o, axis=0, keepdims=True)
        outs.append(o * lax.rsqrt(ms + EPS) * g_ref[...] * (1.0 - lam_init))
    o_ref[...] = jnp.concatenate(outs, axis=0).T.astype(BF16)


def _attn_b_kernel(q_ref, k_ref, vt_ref, o_ref, qz_ref, m_ref, alpha_ref, acc_ref, s0_ref, s1_ref, kabs_ref, *, nc):
    qi = pl.program_id(1)
    qt = q_ref[...].astype(F32).T
    zeros = jnp.zeros((HEAD_DIM, TQ), F32)
    rep = B_HEADS // B_KV_HEADS
    for hd in range(B_HEADS):
        parts = [zeros] * B_KV_HEADS
        parts[hd // rep] = qt[HEAD_DIM * hd:HEAD_DIM * (hd + 1)]
        qz_ref[:, TQ * hd:TQ * (hd + 1)] = jnp.concatenate(parts, axis=0).astype(BF16)
    k_lanes = [(0, 128)] * B_HEADS
    pv_cfg = [(HEAD_DIM * g, HEAD_DIM * (g + 1), rep * g, rep) for g in range(B_KV_HEADS)]
    _flash_loop(qi, nc, k_ref, vt_ref, qz_ref, m_ref, alpha_ref, acc_ref, s0_ref, s1_ref, kabs_ref, k_lanes, pv_cfg,
                slab_rows=256, bounded_pieces=32, groups_per_sweep=B_KV_HEADS)
    outs = [_normalized(acc_ref, hd, HEAD_DIM) for hd in range(B_HEADS)]
    o_ref[...] = jnp.concatenate(outs, axis=0).T.astype(BF16)


def _attn_c_kernel(q_ref, k_ref, vt_ref, o_ref, qz_ref, m_ref, alpha_ref, acc_ref, s0_ref, s1_ref, kabs_ref, *, nc):
    qi = pl.program_id(1)
    qt = q_ref[...].astype(F32).T
    k_lanes, pv_cfg = [], []
    for hd in range(C_HEADS):
        qz_ref[:, TQ * hd:TQ * (hd + 1)] = qt[LANES * hd:LANES * (hd + 1)].astype(BF16)
        k_lanes.append((LANES * hd, LANES * (hd + 1)))
        pv_cfg.append((HEAD_DIM * hd, HEAD_DIM * (hd + 1), hd, 1))
    _flash_loop(qi, nc, k_ref, vt_ref, qz_ref, m_ref, alpha_ref, acc_ref, s0_ref, s1_ref, kabs_ref, k_lanes, pv_cfg,
                slab_rows=256, bounded_pieces=32, groups_per_sweep=C_HEADS)
    outs = [_normalized(acc_ref, hd, HEAD_DIM) for hd in range(C_HEADS)]
    o_ref[...] = jnp.concatenate(outs, axis=0).T.astype(BF16)


def _attn_full(kind, q, k, vt, extras=(), lam_init=0.0):
    b, nt, wq = q.shape
    wk = k.shape[-1]
    nc, wv = vt.shape[1], vt.shape[2]
    assert (nt - TK) % TK_LAT == 0
    if kind == "a":
        body, n_sm, dk = functools.partial(_attn_a_kernel, nc=nc, lam_init=lam_init), 2 * A_HEADS, 128
    elif kind == "b":
        body, n_sm, dk = functools.partial(_attn_b_kernel, nc=nc), B_HEADS, 128
    else:
        body, n_sm, dk = functools.partial(_attn_c_kernel, nc=nc), C_HEADS, 128
    resident = pl.Buffered(1)
    in_specs = [
        pl.BlockSpec((None, TQ, wq), lambda bi, i: (bi, i, 0)),
        pl.BlockSpec((None, nt, wk), lambda bi, i: (bi, 0, 0), pipeline_mode=resident),
        pl.BlockSpec((None, nc, wv, TK), lambda bi, i: (bi, 0, 0, 0), pipeline_mode=resident),
    ] + [pl.BlockSpec(e.shape, lambda bi, i: (0, 0)) for e in extras]
    return pl.pallas_call(
        body,
        grid=(b, nc),
        in_specs=in_specs,
        out_specs=pl.BlockSpec((None, TQ, GROUP_WIDTH), lambda bi, i: (bi, i, 0)),
        out_shape=jax.ShapeDtypeStruct((b, nt, GROUP_WIDTH), BF16),
        scratch_shapes=[
            pltpu.VMEM((dk, n_sm * TQ), BF16),
            pltpu.VMEM((1, n_sm * TQ), F32),
            pltpu.VMEM((1, n_sm * TQ), F32),
            pltpu.VMEM((HEAD_DIM + ONES_ROWS, n_sm * TQ), F32),
            pltpu.VMEM((TK_LAT, n_sm * TQ), F32),
            pltpu.VMEM((TK_LAT, n_sm * TQ), F32),
            pltpu.VMEM((1, wk), F32),
        ],
        compiler_params=pltpu.CompilerParams(
            dimension_semantics=("arbitrary", "arbitrary"), vmem_limit_bytes=VMEM_LIMIT),
        name="attn_" + kind,
    )(q, k, vt, *extras)


def _attn_d_kernel(q_ref, kc_ref, kp_ref, ko_ref, kn_ref, vc_ref, vp_ref, vo_ref, vn_ref, sink_ref,
                   o_ref, *, nb, ctx_blocks):
    n = pl.program_id(1)
    t = SWA_T
    rep = D_HEADS // D_KV_HEADS
    qt = q_ref[...].astype(F32).T
    zeros = jnp.zeros((HEAD_DIM, t), F32)
    cols = []
    for hd in range(D_HEADS):
        parts = [zeros] * D_KV_HEADS
        parts[hd // rep] = qt[HEAD_DIM * hd:HEAD_DIM * (hd + 1)]
        cols.append(jnp.concatenate(parts, axis=0))
    qz = jnp.concatenate(cols, axis=1).astype(BF16)
    w = D_HEADS * t
    ko_i = lax.broadcasted_iota(jnp.int32, (t, w), 0)
    qo_i = lax.broadcasted_iota(jnp.int32, (t, w), 1) & (t - 1)
    band = n >= ctx_blocks
    ok_prev = jnp.logical_and(n >= ctx_blocks + 1, qo_i <= ko_i)
    ok_own = jnp.logical_and(band, ko_i >= 0)
    ok_next = jnp.logical_and(jnp.logical_and(band, n + 1 <= nb - 1), ko_i <= qo_i)
    s_c = jnp.dot(kc_ref[...], qz, preferred_element_type=F32)
    s_p = jnp.where(ok_prev, jnp.dot(kp_ref[...], qz, preferred_element_type=F32), NEG)
    s_o = jnp.where(ok_own, jnp.dot(ko_ref[...], qz, preferred_element_type=F32), NEG)
    s_n = jnp.where(ok_next, jnp.dot(kn_ref[...], qz, preferred_element_type=F32), NEG)
    sk = jnp.concatenate([jnp.broadcast_to(sink_ref[0:1, hd:hd + 1] * LOG2E, (1, t))
                          for hd in range(D_HEADS)], axis=1)
    m = jnp.maximum(jnp.max(s_c, axis=0, keepdims=True), jnp.max(s_p, axis=0, keepdims=True))
    m = jnp.maximum(m, jnp.max(s_o, axis=0, keepdims=True))
    m = jnp.maximum(m, jnp.max(s_n, axis=0, keepdims=True))
    m = jnp.maximum(m, sk)
    pieces = [(vc_ref[...], jnp.exp2(s_c - m).astype(BF16)), (vp_ref[...], jnp.exp2(s_p - m).astype(BF16)),
              (vo_ref[...], jnp.exp2(s_o - m).astype(BF16)), (vn_ref[...], jnp.exp2(s_n - m).astype(BF16))]
    l_sink = jnp.exp2(sk - m)
    outs = []
    for g in range(D_KV_HEADS):
        lanes = slice(rep * t * g, rep * t * (g + 1))
        o = None
        for v, p in pieces:
            va = jnp.concatenate([v[HEAD_DIM * g:HEAD_DIM * (g + 1)],
                                  jnp.ones((ONES_ROWS, v.shape[1]), BF16)], axis=0)
            d = jnp.dot(va, p[:, lanes], preferred_element_type=F32)
            o = d if o is None else o + d
        o = o[0:HEAD_DIM] / (o[HEAD_DIM:HEAD_DIM + 1] + l_sink[:, lanes])
        outs += [o[:, t * r:t * (r + 1)] for r in range(rep)]
    o_ref[...] = jnp.concatenate(outs, axis=0).T.astype(BF16)


def _attn_d(q, k, vt, sink, ctx_len):
    b, nt, wq = q.shape
    wk = k.shape[-1]
    wv = vt.shape[2]
    t = SWA_T
    nb = nt // t
    per = TQ // t
    ctx_blocks = ctx_len // t
    assert ctx_len == TQ

    def kspec(shift):
        return pl.BlockSpec((None, t, wk), lambda bi, i: (bi, jnp.clip(i + shift, 0, nb - 1), 0))

    def vspec(shift):
        def imap(bi, i):
            j = jnp.clip(i + shift, 0, nb - 1)
            return (bi, j // per, 0, j % per)
        return pl.BlockSpec((None, None, wv, t), imap)

    in_specs = [
        pl.BlockSpec((None, t, wq), lambda bi, i: (bi, i, 0)),
        pl.BlockSpec((None, ctx_len, wk), lambda bi, i: (bi, 0, 0)),
        kspec(-1), kspec(0), kspec(1),
        pl.BlockSpec((None, None, wv, ctx_len), lambda bi, i: (bi, 0, 0, 0)),
        vspec(-1), vspec(0), vspec(1),
        pl.BlockSpec(sink.shape, lambda bi, i: (0, 0)),
    ]
    return pl.pallas_call(
        functools.partial(_attn_d_kernel, nb=nb, ctx_blocks=ctx_blocks),
        grid=(b, nb),
        in_specs=in_specs,
        out_specs=pl.BlockSpec((None, t, GROUP_WIDTH), lambda bi, i: (bi, i, 0)),
        out_shape=jax.ShapeDtypeStruct((b, nt, GROUP_WIDTH), BF16),
        compiler_params=pltpu.CompilerParams(
            dimension_semantics=("arbitrary", "arbitrary"), vmem_limit_bytes=VMEM_LIMIT),
        name="attn_d",
    )(q, k, k, k, k, vt, vt, vt, vt, sink)


def _route(sel_rows, s_rows):
    epg = EXPERTS_PER_GROUP
    gscore = []
    for g in range(N_EXPERT_GROUPS):
        r = sel_rows[epg * g:epg * (g + 1)]
        pair = None
        for i in range(epg):
            for j in range(i + 1, epg):
                v = r[i] + r[j]
                pair = v if pair is None else jnp.maximum(pair, v)
        gscore.append(pair)
    best, best_g = gscore[0], jnp.zeros_like(gscore[0], dtype=jnp.int32)
    for g in range(1, N_EXPERT_GROUPS):
        better = gscore[g] > best
        best_g = jnp.where(better, g, best_g)
        best = jnp.where(better, gscore[g], best)
    w = []
    for e in range(N_EXPERTS):
        g = e // epg
        cnt = jnp.zeros_like(best_g)
        for e2 in range(epg * g, epg * (g + 1)):
            if e2 == e:
                continue
            beats = sel_rows[e2] > sel_rows[e]
            if e2 < e:
                beats = jnp.logical_or(beats, sel_rows[e2] == sel_rows[e])
            cnt = cnt + jnp.where(beats, 1, 0)
        chosen = jnp.logical_and(best_g == g, cnt < 2)
        w.append(jnp.where(chosen, s_rows[e], 0.0))
    tot = w[0]
    for e in range(1, N_EXPERTS):
        tot = tot + w[e]
    return [we / tot for we in w]


def _outproj_kernel(oa_ref, ob_ref, oc_ref, od_ref, x_ref, mod_ref, wout_ref, g_ref, b_ref,
                    rwt_ref, rb_ref, x1_ref, h2_ref, gates_ref, *, alpha):
    d = D_MODEL
    o = jnp.concatenate([oa_ref[...], ob_ref[...], oc_ref[...], od_ref[...]], axis=1)
    y = jnp.dot(o, wout_ref[...], preferred_element_type=F32)
    u = alpha * x_ref[...] + mod_ref[:, 2 * d:3 * d] * y
    x1 = _layer_norm(u) * g_ref[...] + b_ref[...]
    x1_ref[...] = x1
    h2 = _layer_norm(x1) * (1.0 + mod_ref[:, 4 * d:5 * d]) + mod_ref[:, 3 * d:4 * d]
    h2_ref[...] = h2.astype(BF16)

    h_hi, h_lo = _split_bf16(h2)
    w_hi, w_lo = _split_bf16(rwt_ref[...])
    logits = _nt_dot(w_hi, h_hi) + _nt_dot(w_hi, h_lo) + _nt_dot(w_lo, h_hi)
    s = jax.nn.sigmoid(logits)
    sel = s + rb_ref[...]
    s_rows = [s[e:e + 1] for e in range(N_EXPERTS)]
    sel_rows = [sel[e:e + 1] for e in range(N_EXPERTS)]
    gates = _route(sel_rows, s_rows)
    gt = jnp.concatenate(gates + [jnp.zeros((LANES - N_EXPERTS, TQ), F32)], axis=0)
    gates_ref[...] = gt.T


def _outproj(oa, ob, oc, od, xa, modsel, w_out, ln_g, ln_b, rwt, rb, alpha):
    b, nt, d = xa.shape
    nc = nt // TQ

    def rm(w):
        return pl.BlockSpec((None, TQ, w), lambda bi, i: (bi, i, 0))

    def full(a):
        nd = a.ndim
        return pl.BlockSpec(a.shape, lambda bi, i, _n=nd: (0,) * _n)

    in_specs = [rm(GROUP_WIDTH) for _ in range(4)] + [
        rm(d),
        pl.BlockSpec((None, None, 1, 6 * d), lambda bi, i: (bi, jnp.minimum(i, 1), 0, 0)),
        full(w_out), full(ln_g), full(ln_b), full(rwt), full(rb)]
    return pl.pallas_call(
        functools.partial(_outproj_kernel, alpha=alpha),
        grid=(b, nc),
        in_specs=in_specs,
        out_specs=[rm(d), rm(d), rm(LANES)],
        out_shape=[jax.ShapeDtypeStruct((b, nt, d), F32), jax.ShapeDtypeStruct((b, nt, d), BF16),
                   jax.ShapeDtypeStruct((b, nt, LANES), F32)],
        compiler_params=pltpu.CompilerParams(
            dimension_semantics=("arbitrary", "arbitrary"), vmem_limit_bytes=VMEM_LIMIT),
        name="outproj",
    )(oa, ob, oc, od, xa, modsel, w_out, ln_g, ln_b, rwt, rb)


def _moe_kernel(h_ref, gates_ref, x_ref, mod_ref, wg_ref, wu_ref, wd_ref, g_ref, b_ref, o_ref, *, alpha):
    d = D_MODEL
    h = h_ref[...]
    gates = gates_ref[...]
    y = jnp.zeros((TQ, d), F32)
    for g in range(N_EXPERT_GROUPS):
        acts = []
        for el in range(EXPERTS_PER_GROUP):
            e = EXPERTS_PER_GROUP * g + el
            hg = jnp.dot(h, wg_ref[e], preferred_element_type=F32)
            hu = jnp.dot(h, wu_ref[e], preferred_element_type=F32)
            a = hg * jax.nn.sigmoid(hg) * hu * gates[:, e:e + 1]
            acts.append(a.astype(BF16))
        y = y + jnp.dot(jnp.concatenate(acts, axis=1), wd_ref[g], preferred_element_type=F32)
    u = alpha * x_ref[...] + mod_ref[:, 5 * d:6 * d] * y
    o_ref[...] = _layer_norm(u) * g_ref[...] + b_ref[...]


def _moe(h2, gates, x1, modsel, wg, wu, wd, ln_g, ln_b, alpha, latents_only):
    b, nt, d = x1.shape
    nc = nt // TQ
    resident = pl.Buffered(1)
    if latents_only:
        out_spec = pl.BlockSpec((None, TQ, d), lambda bi, i: (bi, jnp.maximum(i - 1, 0), 0))
        out_rows = nt - TQ
    else:
        out_spec = pl.BlockSpec((None, TQ, d), lambda bi, i: (bi, i, 0))
        out_rows = nt

    def rm(w):
        return pl.BlockSpec((None, TQ, w), lambda bi, i: (bi, i, 0))

    def res(a):
        nd = a.ndim
        return pl.BlockSpec(a.shape, lambda bi, i, _n=nd: (0,) * _n, pipeline_mode=resident)

    in_specs = [rm(d), rm(LANES), rm(d),
                pl.BlockSpec((None, None, 1, 6 * d), lambda bi, i: (bi, jnp.minimum(i, 1), 0, 0)),
                res(wg), res(wu), res(wd), res(ln_g), res(ln_b)]
    return pl.pallas_call(
        functools.partial(_moe_kernel, alpha=alpha),
        grid=(b, nc),
        in_specs=in_specs,
        out_specs=out_spec,
        out_shape=jax.ShapeDtypeStruct((b, out_rows, d), F32),
        compiler_params=pltpu.CompilerParams(
            dimension_semantics=("arbitrary", "arbitrary"), vmem_limit_bytes=VMEM_LIMIT),
        name="moe",
    )(h2, gates, x1, modsel, wg, wu, wd, ln_g, ln_b)


def _rope_perm(rot_dim):
    n = rot_dim // 4
    j = np.arange(rot_dim)
    return j ^ n, np.where((j // n) % 2 == 0, -1.0, 1.0).astype(np.float32)


def _swapped(w, rot_dim):
    perm, sign = _rope_perm(rot_dim)
    cols = w.shape[-1]
    idx = (np.arange(cols) // rot_dim) * rot_dim + perm[np.arange(cols) % rot_dim]
    sgn = sign[np.arange(cols) % rot_dim]
    return w[..., idx] * sgn


def _rope_tables(n_lat, ctx_len):
    t = jnp.arange(n_lat, dtype=jnp.int32)
    row, col = (t // GRID_W).astype(F32), (t % GRID_W).astype(F32)

    def pattern(rot_dim):
        n = rot_dim // 4
        inv = ROPE_THETA ** (-jnp.arange(n, dtype=F32) / n)
        ar, ac = row[:, None] * inv, col[:, None] * inv
        ang = jnp.concatenate([ar, ar, ac, ac], axis=-1)
        return jnp.cos(ang), jnp.sin(ang)

    def with_ctx(a, fill):
        return jnp.concatenate([jnp.full((ctx_len, a.shape[1]), fill, F32), a], axis=0)

    c32, s32 = pattern(A_QK)
    c64, s64 = pattern(HEAD_DIM)
    ones = jnp.ones((n_lat, C_NOPE), F32)
    zeros = jnp.zeros((n_lat, C_NOPE), F32)
    pad1 = jnp.ones((n_lat, LANES - C_NOPE - C_ROPE), F32)
    pad0 = jnp.zeros((n_lat, LANES - C_NOPE - C_ROPE), F32)
    return {
        "cos32": with_ctx(jnp.tile(c32, (1, LANES // A_QK)), 1.0),
        "sin32": with_ctx(jnp.tile(s32, (1, LANES // A_QK)), 0.0),
        "cos64": with_ctx(jnp.tile(c64, (1, LANES // HEAD_DIM)), 1.0),
        "sin64": with_ctx(jnp.tile(s64, (1, LANES // HEAD_DIM)), 0.0),
        "cosc": with_ctx(jnp.concatenate([ones, c32, pad1], axis=-1), 1.0),
        "sinc": with_ctx(jnp.concatenate([zeros, s32, pad0], axis=-1), 0.0),
    }


def _prep_layer_weights(w_in, gq, gk, gcq, gckv, w_uq, w_ukv):
    d = w_in.shape[0]
    splits = np.cumsum([256, 256, 256, 256, 128, 128, C_Q_RANK, C_KV_RANK, C_ROPE, 256, 128, 128])[:-1]
    (a_q, a_k, a_v, b_q, b_k, b_v, c_q, c_kv, c_kr, d_q, d_k, d_v) = jnp.split(w_in, splits, axis=1)
    z = lambda n: jnp.zeros((d, n), F32)
    kr4 = jnp.concatenate([z(C_NOPE), c_kr, z(LANES - C_NOPE - C_ROPE)], axis=1)
    kr4s = jnp.concatenate([z(C_NOPE), _swapped(c_kr, C_ROPE), z(LANES - C_NOPE - C_ROPE)], axis=1)
    wrm = jnp.concatenate([
        a_q, _swapped(a_q, A_QK), a_k, _swapped(a_k, A_QK),
        b_q, _swapped(b_q, HEAD_DIM), b_k, _swapped(b_k, HEAD_DIM),
        d_q, _swapped(d_q, HEAD_DIM), d_k, _swapped(d_k, HEAD_DIM),
        c_q, z(256 - C_Q_RANK), c_kv, kr4, kr4s], axis=1)
    wt = jnp.concatenate([a_v, b_v, d_v], axis=1).T

    uq = w_uq.reshape(C_Q_RANK, C_HEADS, C_NOPE + C_ROPE)
    uq_n, uq_r = uq[..., :C_NOPE], uq[..., C_NOPE:]
    zq = lambda n: jnp.zeros((C_Q_RANK, C_HEADS, n), F32)
    pad = LANES - C_NOPE - C_ROPE
    wuq = jnp.concatenate([uq_n, uq_r, zq(pad)], axis=-1).reshape(C_Q_RANK, C_HEADS * LANES)
    wuqs = jnp.concatenate([zq(C_NOPE), _swapped(uq_r, C_ROPE), zq(pad)], axis=-1).reshape(C_Q_RANK, C_HEADS * LANES)
    zrows = jnp.zeros((256 - C_Q_RANK, C_HEADS * LANES), F32)
    wuq, wuqs = jnp.concatenate([wuq, zrows], axis=0), jnp.concatenate([wuqs, zrows], axis=0)
    ukv = w_ukv.reshape(C_KV_RANK, C_HEADS, C_NOPE + HEAD_DIM)
    uk_n, u_v = ukv[..., :C_NOPE], ukv[..., C_NOPE:]
    wukn = jnp.concatenate([uk_n, jnp.zeros((C_KV_RANK, C_HEADS, LANES - C_NOPE), F32)], axis=-1)
    wukn = wukn.reshape(C_KV_RANK, C_HEADS * LANES)
    wuvt = u_v.reshape(C_KV_RANK, C_HEADS * HEAD_DIM).T

    perm64, _ = _rope_perm(HEAD_DIM)
    return {
        "wrm": wrm.astype(BF16), "wt": wt.astype(BF16),
        "wuq": wuq.astype(BF16), "wuqs": wuqs.astype(BF16),
        "wukn": wukn.astype(BF16), "wuvt": wuvt.astype(BF16),
        "gqb": jnp.tile(gq, B_HEADS)[None, :], "gqbs": jnp.tile(gq[perm64], B_HEADS)[None, :],
        "gkb": jnp.tile(gk, B_KV_HEADS)[None, :], "gkbs": jnp.tile(gk[perm64], B_KV_HEADS)[None, :],
        "gcq": jnp.concatenate([gcq, jnp.zeros((256 - C_Q_RANK,), F32)])[None, :],
        "gckv": gckv[None, :],
    }


def kernel(x, c, ctx, c_ctx, w_ada, b_ada, w_in, w_out, diff_lambda_q1, diff_lambda_k1, diff_lambda_q2,
           diff_lambda_k2, diff_subln_g, gqa_q_norm_g, gqa_k_norm_g, mla_q_norm_g, mla_kv_norm_g, mla_w_uq,
           mla_w_ukv, swa_sink, ln1_g, ln1_b, ln2_g, ln2_b, router_w, router_bias,
           exp_w_gate, exp_w_up, exp_w_down):
    b, n_lat, d = x.shape
    ctx_len = ctx.shape[1]
    depth = w_ada.shape[0]
    assert d == D_MODEL and ctx_len == TQ and n_lat % TQ == 0 and b + 1 <= 8
    alpha = (2 * depth) ** 0.25

    xa = jnp.concatenate([ctx, x], axis=1)
    tabs = _rope_tables(n_lat, ctx_len)

    cc = jnp.concatenate([c, c_ctx[None, :], jnp.zeros((8 - b - 1, d), F32)], axis=0)
    mods = _ada(cc, w_ada, b_ada)
    rwt = router_w.T
    rb = router_bias[:, None]

    for l in range(depth):
        lat = mods[l, :b]
        cx = jnp.broadcast_to(mods[l, b], lat.shape)
        modsel = jnp.stack([cx, lat], axis=1)[:, :, None, :]
        lw = _prep_layer_weights(w_in[l], gqa_q_norm_g[l], gqa_k_norm_g[l], mla_q_norm_g[l],
                                 mla_kv_norm_g[l], mla_w_uq[l], mla_w_ukv[l])
        qa, ka, vat, qb, kb, vbt, qc, kc, vct, qd, kd, vdt = _proj(xa, modsel, lw, tabs)

        lam_init = 0.8 - 0.6 * math.exp(-0.3 * l)
        extras = (diff_lambda_q1[l][None, :], diff_lambda_k1[l][None, :], diff_lambda_q2[l][None, :],
                  diff_lambda_k2[l][None, :], diff_subln_g[l][:, None])
        oa = _attn_full("a", qa, ka, vat, extras, lam_init)
        ob = _attn_full("b", qb, kb, vbt)
        oc = _attn_full("c", qc, kc, vct)
        od = _attn_d(qd, kd, vdt, swa_sink[l][None, :], ctx_len)

        x1, h2, gates = _outproj(oa, ob, oc, od, xa, modsel, w_out[l].astype(BF16),
                                 ln1_g[l][None, :], ln1_b[l][None, :], rwt, rb, alpha)
        wg = exp_w_gate[l].astype(BF16)
        wu = exp_w_up[l].astype(BF16)
        wd = exp_w_down[l].astype(BF16).reshape(N_EXPERT_GROUPS, EXPERTS_PER_GROUP * D_EXPERT, d)
        xa = _moe(h2, gates, x1, modsel, wg, wu, wd, ln2_g[l][None, :], ln2_b[l][None, :], alpha,
                  latents_only=(l == depth - 1))
    return xa
```

```python
import functools
import math

import numpy as np
import jax
import jax.numpy as jnp
from jax import lax
from jax.experimental import pallas as pl
from jax.experimental.pallas import tpu as pltpu

F32 = jnp.float32
BF16 = jnp.bfloat16

D_MODEL = 1024
HEAD_DIM = 64
GROUP_WIDTH = 256
GRID_W = 64
ROPE_THETA = 10000.0
EPS = 1e-6
NEG = -1e30
LOG2E = 1.4426950408889634

A_HEADS = 4
A_QK = 32
B_HEADS = 4
B_KV_HEADS = 2
C_HEADS = 4
C_Q_RANK = 192
C_KV_RANK = 128
C_NOPE = 64
C_ROPE = 32
D_HEADS = 4
D_KV_HEADS = 2
WINDOW = 128
N_EXPERTS = 16
EXPERTS_PER_GROUP = 4
N_EXPERT_GROUPS = 4
D_EXPERT = 256

TQ = 256
TK = 256
TK_LAT = 1024
BOUND_SLACK = 1.0 + 2.0 ** -7
MIN_DENOMINATOR = 2.0 ** -80
LANES = 128
ONES_ROWS = 16
VMEM_LIMIT = 56 * 1024 * 1024

_C_QA, _C_QAS, _C_KA, _C_KAS = 0, 256, 512, 768
_C_QB, _C_QBS, _C_KB, _C_KBS = 1024, 1280, 1536, 1664
_C_QD, _C_QDS, _C_KD, _C_KDS = 1792, 2048, 2304, 2432
_C_CQ, _C_CKV, _C_KR, _C_KRS = 2560, 2816, 2944, 3072
_C_TOTAL = 3200
_R_VA, _R_VB, _R_VD, _R_TOTAL = 0, 256, 384, 512


def _nt_dot(a, b):
    return lax.dot_general(a, b, (((1,), (1,)), ((), ())), preferred_element_type=F32)


def _split_bf16(a):
    hi = a.astype(BF16)
    lo = (a - hi.astype(F32)).astype(BF16)
    return hi, lo


def _layer_norm(x):
    mu = jnp.mean(x, axis=-1, keepdims=True)
    xc = x - mu
    var = jnp.mean(xc * xc, axis=-1, keepdims=True)
    return xc * lax.rsqrt(var + EPS)


def _tile_lanes(a, n):
    return jnp.concatenate([a] * n, axis=1)


def _ada_kernel(cc_ref, w_ref, b_ref, o_ref):
    cc = cc_ref[...]
    s = cc * jax.nn.sigmoid(cc)
    s_hi, s_lo = _split_bf16(s)
    w_hi, w_lo = _split_bf16(w_ref[...])
    acc = jnp.dot(s_hi, w_hi, preferred_element_type=F32)
    acc += jnp.dot(s_hi, w_lo, preferred_element_type=F32)
    acc += jnp.dot(s_lo, w_hi, preferred_element_type=F32)
    o_ref[...] = acc + b_ref[...]


def _ada(cc, w_ada, b_ada):
    depth, d, n6 = w_ada.shape
    bn = 1536
    return pl.pallas_call(
        _ada_kernel,
        grid=(depth, n6 // bn),
        in_specs=[
            pl.BlockSpec((8, d), lambda l, j: (0, 0)),
            pl.BlockSpec((None, d, bn), lambda l, j: (l, 0, j)),
            pl.BlockSpec((None, 1, bn), lambda l, j: (l, 0, j)),
        ],
        out_specs=pl.BlockSpec((None, 8, bn), lambda l, j: (l, 0, j)),
        out_shape=jax.ShapeDtypeStruct((depth, 8, n6), F32),
        compiler_params=pltpu.CompilerParams(
            dimension_semantics=("arbitrary", "arbitrary"), vmem_limit_bytes=VMEM_LIMIT),
        name="ada",
    )(cc, w_ada, b_ada.reshape(depth, 1, n6))


def _block_diag_ones(n, blk):
    sh = int(math.log2(blk))
    r = lax.broadcasted_iota(jnp.int32, (n, n), 0) >> sh
    c = lax.broadcasted_iota(jnp.int32, (n, n), 1) >> sh
    return jnp.where(r == c, 1.0, 0.0).astype(BF16)


def _proj_kernel(x_ref, mod_ref, wrm_ref, wt_ref, wuq_ref, wuqs_ref, wukn_ref, wuvt_ref,
                 gqb_ref, gqbs_ref, gkb_ref, gkbs_ref, gcq_ref, gckv_ref,
                 cos32_ref, sin32_ref, cos64_ref, sin64_ref, cosc_ref, sinc_ref,
                 qa_ref, ka_ref, vat_ref, qb_ref, kb_ref, vbt_ref,
                 qc_ref, kc_ref, vct_ref, qd_ref, kd_ref, vdt_ref):
    d = D_MODEL
    xn = _layer_norm(x_ref[...])
    h = (xn * (1.0 + mod_ref[:, d:2 * d]) + mod_ref[:, 0:d]).astype(BF16)

    def cols(lo, width):
        return jnp.dot(h, wrm_ref[:, lo:lo + width], preferred_element_type=F32)

    cos32, sin32 = _tile_lanes(cos32_ref[...], 2), _tile_lanes(sin32_ref[...], 2)
    cos64, sin64 = cos64_ref[...], sin64_ref[...]
    cos64w, sin64w = _tile_lanes(cos64, 2), _tile_lanes(sin64, 2)

    sa = (A_QK ** -0.5) * LOG2E
    qa_ref[...] = ((cols(_C_QA, 256) * cos32 + cols(_C_QAS, 256) * sin32) * sa).astype(BF16)
    ka_ref[...] = (cols(_C_KA, 256) * cos32 + cols(_C_KAS, 256) * sin32).astype(BF16)

    sb = (HEAD_DIM ** -0.5) * LOG2E
    q = cols(_C_QB, 256)
    ssq = jnp.dot((q * q).astype(BF16), _block_diag_ones(256, HEAD_DIM), preferred_element_type=F32)
    r = lax.rsqrt(ssq * (1.0 / HEAD_DIM) + EPS)
    qr = (q * gqb_ref[...]) * cos64w + (cols(_C_QBS, 256) * gqbs_ref[...]) * sin64w
    qb_ref[...] = (qr * (r * sb)).astype(BF16)
    k = cols(_C_KB, 128)
    ssq = jnp.dot((k * k).astype(BF16), _block_diag_ones(128, HEAD_DIM), preferred_element_type=F32)
    r = lax.rsqrt(ssq * (1.0 / HEAD_DIM) + EPS)
    kr = (k * gkb_ref[...]) * cos64 + (cols(_C_KBS, 128) * gkbs_ref[...]) * sin64
    kb_ref[...] = (kr * r).astype(BF16)

    qd_ref[...] = ((cols(_C_QD, 256) * cos64w + cols(_C_QDS, 256) * sin64w) * sb).astype(BF16)
    kd_ref[...] = (cols(_C_KD, 128) * cos64 + cols(_C_KDS, 128) * sin64).astype(BF16)

    sc = ((C_NOPE + C_ROPE) ** -0.5) * LOG2E
    cosc, sinc = cosc_ref[...], sinc_ref[...]
    cq = cols(_C_CQ, 256)
    ms = jnp.sum(cq * cq, axis=-1, keepdims=True) * (1.0 / C_Q_RANK)
    cqn = (cq * lax.rsqrt(ms + EPS) * gcq_ref[...]).astype(BF16)
    qn = jnp.dot(cqn, wuq_ref[...], preferred_element_type=F32)
    qs = jnp.dot(cqn, wuqs_ref[...], preferred_element_type=F32)
    qc_ref[...] = ((qn * _tile_lanes(cosc, 4) + qs * _tile_lanes(sinc, 4)) * sc).astype(BF16)
    ckv = cols(_C_CKV, 128)
    ms = jnp.mean(ckv * ckv, axis=-1, keepdims=True)
    ckvn = (ckv * lax.rsqrt(ms + EPS) * gckv_ref[...]).astype(BF16)
    kn = jnp.dot(ckvn, wukn_ref[...], preferred_element_type=F32)
    krr = cols(_C_KR, 128) * cosc + cols(_C_KRS, 128) * sinc
    kc_ref[...] = (kn + _tile_lanes(krr, 4)).astype(BF16)
    vct_ref[...] = _nt_dot(wuvt_ref[...], ckvn).astype(BF16)

    vt = _nt_dot(wt_ref[...], h)
    vat_ref[...] = vt[_R_VA:_R_VA + 256].astype(BF16)
    vbt_ref[...] = vt[_R_VB:_R_VB + 128].astype(BF16)
    vdt_ref[...] = vt[_R_VD:_R_VD + 128].astype(BF16)


def _proj(xa, modsel, lw, tabs):
    b, nt, d = xa.shape
    nc = nt // TQ

    def full(a):
        nd = a.ndim
        return pl.BlockSpec(a.shape, lambda bi, i, _n=nd: (0,) * _n)

    def tab(a):
        return pl.BlockSpec((TQ, LANES), lambda bi, i: (i, 0))

    def rm(w):
        return pl.BlockSpec((None, TQ, w), lambda bi, i: (bi, i, 0))

    def tr(w):
        return pl.BlockSpec((None, None, w, TQ), lambda bi, i: (bi, i, 0, 0))

    weights = [lw["wrm"], lw["wt"], lw["wuq"], lw["wuqs"], lw["wukn"], lw["wuvt"],
               lw["gqb"], lw["gqbs"], lw["gkb"], lw["gkbs"], lw["gcq"], lw["gckv"]]
    tables = [tabs["cos32"], tabs["sin32"], tabs["cos64"], tabs["sin64"], tabs["cosc"], tabs["sinc"]]
    in_specs = ([pl.BlockSpec((None, TQ, d), lambda bi, i: (bi, i, 0)),
                 pl.BlockSpec((None, None, 1, 6 * d), lambda bi, i: (bi, jnp.minimum(i, 1), 0, 0))]
                + [full(w) for w in weights] + [tab(t) for t in tables])
    widths = [("rm", 256), ("rm", 256), ("tr", 256), ("rm", 256), ("rm", 128), ("tr", 128),
              ("rm", 512), ("rm", 512), ("tr", 256), ("rm", 256), ("rm", 128), ("tr", 128)]
    out_specs, out_shape = [], []
    for kind, w in widths:
        if kind == "rm":
            out_specs.append(rm(w))
            out_shape.append(jax.ShapeDtypeStruct((b, nt, w), BF16))
        else:
            out_specs.append(tr(w))
            out_shape.append(jax.ShapeDtypeStruct((b, nc, w, TQ), BF16))
    return pl.pallas_call(
        _proj_kernel,
        grid=(b, nc),
        in_specs=in_specs,
        out_specs=out_specs,
        out_shape=out_shape,
        compiler_params=pltpu.CompilerParams(
            dimension_semantics=("arbitrary", "arbitrary"), vmem_limit_bytes=VMEM_LIMIT),
        name="proj",
    )(xa, modsel, *weights, *tables)


def _flash_loop(qi, nc, k_ref, vt_ref, qz_ref, m_ref, alpha_ref, acc_ref, s0_ref, s1_ref, kabs_ref, k_lanes, pv_cfg,
                slab_rows, bounded_pieces, groups_per_sweep):
    per = TK_LAT // TK
    n_lat = (nc - 1) // per
    assert n_lat % 2 == 0 and n_lat >= 2 and (nc - 1) % bounded_pieces == 0
    same_lanes = all(kl == k_lanes[0] for kl in k_lanes)

    def scores(kc):
        if same_lanes:
            return jnp.dot(kc[:, k_lanes[0][0]:k_lanes[0][1]], qz_ref[...], preferred_element_type=F32)
        return jnp.concatenate(
            [jnp.dot(kc[:, lo:hi], qz_ref[:, TQ * j:TQ * (j + 1)], preferred_element_type=F32)
             for j, (lo, hi) in enumerate(k_lanes)], axis=1)

    def small_chunk(pieces, first):
        if slab_rows >= pieces * TK:
            return (1, pieces, first)
        return (pieces * TK // slab_rows, slab_rows // TK, first)

    ctx_chunk = small_chunk(1, 0)

    def lat_chunk(c):
        return small_chunk(per, 1 + c * per)

    def stats_update(mx):
        m_old = m_ref[...]
        m_new = jnp.maximum(m_old, mx)
        alpha_ref[...] = jnp.exp2(m_old - m_new)
        m_ref[...] = m_new

    def pipelined(cur, cur_ref, nxt, nxt_ref):
        m_cur, alpha = m_ref[...], alpha_ref[...]
        mx, pv = None, [None] * len(pv_cfg)
        n_cur = cur[0] if cur is not None else 0
        n_nxt = nxt[0] if nxt is not None else 0
        for r in range(max(n_cur, n_nxt)):
            if r < n_nxt:
                rows = TK * nxt[1]
                kc = k_ref[pl.ds(pl.multiple_of((nxt[2] + r * nxt[1]) * TK, TK), rows), :]
                s = scores(kc)
                nxt_ref[rows * r:rows * (r + 1), :] = s
                smx = jnp.max(s, axis=0, keepdims=True)
                mx = smx if mx is None else jnp.maximum(mx, smx)
            if r < n_cur:
                rows = TK * cur[1]
                p = jnp.exp2(cur_ref[rows * r:rows * (r + 1), :] - m_cur).astype(BF16)
                vc = jnp.concatenate([vt_ref[cur[2] + r * cur[1] + i] for i in range(cur[1])], axis=1)
                ones = jnp.ones((ONES_ROWS, rows), BF16)
                for g, (vl, vh, j0, nj) in enumerate(pv_cfg):
                    va = jnp.concatenate([vc[vl:vh, :], ones], axis=0)
                    d = jnp.dot(va, p[:, TQ * j0:TQ * (j0 + nj)], preferred_element_type=F32)
                    pv[g] = d if pv[g] is None else pv[g] + d
        if cur is not None:
            for g, (vl, vh, j0, nj) in enumerate(pv_cfg):
                cols = slice(TQ * j0, TQ * (j0 + nj))
                acc_ref[:, cols] = alpha[:, cols] * acc_ref[:, cols] + pv[g]
        if nxt is not None:
            stats_update(mx)

    def online_pass():
        m_ref[...] = jnp.full(m_ref.shape, NEG, F32)
        acc_ref[...] = jnp.zeros(acc_ref.shape, F32)

        @pl.when(qi == 0)
        def _():
            pipelined(None, None, ctx_chunk, s0_ref)
            pipelined(ctx_chunk, s0_ref, None, None)

        @pl.when(qi > 0)
        def _():
            pipelined(None, None, lat_chunk(0), s0_ref)

            def body(i, carry):
                c = 2 * i
                pipelined(lat_chunk(c), s0_ref, lat_chunk(c + 1), s1_ref)
                pipelined(lat_chunk(c + 1), s1_ref, lat_chunk(c + 2), s0_ref)
                return carry

            lax.fori_loop(0, n_lat // 2 - 1, body, 0)
            pipelined(lat_chunk(n_lat - 2), s0_ref, lat_chunk(n_lat - 1), s1_ref)
            pipelined(lat_chunk(n_lat - 1), s1_ref, ctx_chunk, s0_ref)
            pipelined(ctx_chunk, s0_ref, None, None)

    def bounded_pass():
        wk = kabs_ref.shape[1]
        kab = jnp.broadcast_to(kabs_ref[...], (ONES_ROWS, wk)).astype(BF16)
        aq = jnp.abs(qz_ref[...])
        if same_lanes:
            lo, hi = k_lanes[0]
            bound = jnp.dot(kab[:, lo:hi], aq, preferred_element_type=F32)[0:1]
        else:
            bound = jnp.concatenate(
                [jnp.dot(kab[:, lo:hi], aq[:, TQ * j:TQ * (j + 1)], preferred_element_type=F32)[0:1]
                 for j, (lo, hi) in enumerate(k_lanes)], axis=1)
        bound = bound * BOUND_SLACK
        ones = jnp.ones((ONES_ROWS, TK), BF16)
        acc_ref[...] = jnp.zeros(acc_ref.shape, F32)

        def pieces(first, n, groups):
            pv = {g: None for g in groups}

            def probs(r, g):
                _, _, j0, nj = pv_cfg[g]
                kc = k_ref[pl.ds(pl.multiple_of((first + r) * TK, TK), TK), :]
                if all(k_lanes[j] == k_lanes[j0] for j in range(j0, j0 + nj)):
                    s = jnp.dot(kc[:, k_lanes[j0][0]:k_lanes[j0][1]], qz_ref[:, TQ * j0:TQ * (j0 + nj)],
                                preferred_element_type=F32)
                else:
                    s = jnp.concatenate(
                        [jnp.dot(kc[:, k_lanes[j][0]:k_lanes[j][1]], qz_ref[:, TQ * j:TQ * (j + 1)],
                                 preferred_element_type=F32) for j in range(j0, j0 + nj)], axis=1)
                return jnp.exp2(s - bound[:, TQ * j0:TQ * (j0 + nj)]).astype(BF16)

            def values(r, g, p):
                vl, vh, _, _ = pv_cfg[g]
                va = jnp.concatenate([vt_ref[first + r][vl:vh, :], ones], axis=0)
                d = jnp.dot(va, p, preferred_element_type=F32)
                pv[g] = d if pv[g] is None else pv[g] + d

            p_prev = {g: probs(0, g) for g in groups}
            for r in range(1, n):
                p_cur = {}
                for g in groups:
                    p_cur[g] = probs(r, g)
                    values(r - 1, g, p_prev[g])
                p_prev = p_cur
            for g in groups:
                values(n - 1, g, p_prev[g])
            for g in groups:
                _, _, j0, nj = pv_cfg[g]
                cols = slice(TQ * j0, TQ * (j0 + nj))
                acc_ref[:, cols] = acc_ref[:, cols] + pv[g]

        n_sets = len(pv_cfg) // groups_per_sweep
        for st in range(n_sets):
            groups = list(range(st * groups_per_sweep, (st + 1) * groups_per_sweep))
            pieces(0, 1, groups)

            @pl.when(qi > 0)
            def _(groups=groups):
                def body(i, carry):
                    pieces(1 + i * bounded_pieces, bounded_pieces, groups)
                    return carry

                lax.fori_loop(0, (nc - 1) // bounded_pieces, body, 0)

    @pl.when(qi == 0)
    def _():
        def body(c, mx):
            blk = k_ref[pl.ds(pl.multiple_of(c * TK, TK), TK), :].astype(F32)
            return jnp.maximum(mx, jnp.max(jnp.abs(blk), axis=0, keepdims=True))

        kabs_ref[...] = lax.fori_loop(0, nc, body, jnp.zeros(kabs_ref.shape, F32))

    bounded_pass()
    dv = acc_ref.shape[0] - ONES_ROWS
    healthy = jnp.min(jnp.where(acc_ref[dv:dv + 1, :] >= MIN_DENOMINATOR, 1.0, 0.0))

    @pl.when(healthy < 0.5)
    def _():
        online_pass()


def _normalized(acc_ref, j, dv):
    a = acc_ref[:, TQ * j:TQ * (j + 1)]
    return a[0:dv] / a[dv:dv + 1]


def _attn_a_kernel(q_ref, k_ref, vt_ref, lq1_ref, lk1_ref, lq2_ref, lk2_ref, g_ref, o_ref,
                   qz_ref, m_ref, alpha_ref, acc_ref, s0_ref, s1_ref, kabs_ref, *, nc, lam_init):
    qi = pl.program_id(1)
    qt = q_ref[...].astype(F32).T
    row = lax.broadcasted_iota(jnp.int32, (LANES, TQ), 0)
    per_blk = LANES // A_QK
    k_lanes = []
    for j in range(2 * A_HEADS):
        blk, sub = j // per_blk, j % per_blk
        keep = (row >= A_QK * sub) & (row < A_QK * (sub + 1))
        qz_ref[:, TQ * j:TQ * (j + 1)] = jnp.where(keep, qt[LANES * blk:LANES * (blk + 1)], 0.0).astype(BF16)
        k_lanes.append((LANES * blk, LANES * (blk + 1)))
    pv_cfg = [(HEAD_DIM * hd, HEAD_DIM * (hd + 1), 2 * hd, 2) for hd in range(A_HEADS)]
    _flash_loop(qi, nc, k_ref, vt_ref, qz_ref, m_ref, alpha_ref, acc_ref, s0_ref, s1_ref, kabs_ref, k_lanes, pv_cfg,
                slab_rows=1024, bounded_pieces=32, groups_per_sweep=2)

    lam = (jnp.exp(jnp.sum(lq1_ref[...] * lk1_ref[...], axis=-1, keepdims=True))
           - jnp.exp(jnp.sum(lq2_ref[...] * lk2_ref[...], axis=-1, keepdims=True)) + lam_init)
    outs = []
    for hd in range(A_HEADS):
        o = _normalized(acc_ref, 2 * hd, HEAD_DIM) - lam * _normalized(acc_ref, 2 * hd + 1, HEAD_DIM)
        ms = jnp.mean(o * o, axis=0, keepdims=True)
        outs.append(o * lax.rsqrt(ms + EPS) * g_ref[...] * (1.0 - lam_init))
    o_ref[...] = jnp.concatenate(outs, axis=0).T.astype(BF16)


def _attn_b_kernel(q_ref, k_ref, vt_ref, o_ref, qz_ref, m_ref, alpha_ref, acc_ref, s0_ref, s1_ref, kabs_ref, *, nc):
    qi = pl.program_id(1)
    qt = q_ref[...].astype(F32).T
    zeros = jnp.zeros((HEAD_DIM, TQ), F32)
    rep = B_HEADS // B_KV_HEADS
    for hd in range(B_HEADS):
        parts = [zeros] * B_KV_HEADS
        parts[hd // rep] = qt[HEAD_DIM * hd:HEAD_DIM * (hd + 1)]
        qz_ref[:, TQ * hd:TQ * (hd + 1)] = jnp.concatenate(parts, axis=0).astype(BF16)
    k_lanes = [(0, 128)] * B_HEADS
    pv_cfg = [(HEAD_DIM * g, HEAD_DIM * (g + 1), rep * g, rep) for g in range(B_KV_HEADS)]
    _flash_loop(qi, nc, k_ref, vt_ref, qz_ref, m_ref, alpha_ref, acc_ref, s0_ref, s1_ref, kabs_ref, k_lanes, pv_cfg,
                slab_rows=256, bounded_pieces=32, groups_per_sweep=B_KV_HEADS)
    outs = [_normalized(acc_ref, hd, HEAD_DIM) for hd in range(B_HEADS)]
    o_ref[...] = jnp.concatenate(outs, axis=0).T.astype(BF16)


def _attn_c_kernel(q_ref, k_ref, vt_ref, o_ref, qz_ref, m_ref, alpha_ref, acc_ref, s0_ref, s1_ref, kabs_ref, *, nc):
    qi = pl.program_id(1)
    qt = q_ref[...].astype(F32).T
    k_lanes, pv_cfg = [], []
    for hd in range(C_HEADS):
        qz_ref[:, TQ * hd:TQ * (hd + 1)] = qt[LANES * hd:LANES * (hd + 1)].astype(BF16)
        k_lanes.append((LANES * hd, LANES * (hd + 1)))
        pv_cfg.append((HEAD_DIM * hd, HEAD_DIM * (hd + 1), hd, 1))
    _flash_loop(qi, nc, k_ref, vt_ref, qz_ref, m_ref, alpha_ref, acc_ref, s0_ref, s1_ref, kabs_ref, k_lanes, pv_cfg,
                slab_rows=256, bounded_pieces=32, groups_per_sweep=C_HEADS)
    outs = [_normalized(acc_ref, hd, HEAD_DIM) for hd in range(C_HEADS)]
    o_ref[...] = jnp.concatenate(outs, axis=0).T.astype(BF16)


def _attn_full(kind, q, k, vt, extras=(), lam_init=0.0):
    b, nt, wq = q.shape
    wk = k.shape[-1]
    nc, wv = vt.shape[1], vt.shape[2]
    assert (nt - TK) % TK_LAT == 0
    if kind == "a":
        body, n_sm, dk = functools.partial(_attn_a_kernel, nc=nc, lam_init=lam_init), 2 * A_HEADS, 128
    elif kind == "b":
        body, n_sm, dk = functools.partial(_attn_b_kernel, nc=nc), B_HEADS, 128
    else:
        body, n_sm, dk = functools.partial(_attn_c_kernel, nc=nc), C_HEADS, 128
    resident = pl.Buffered(1)
    in_specs = [
        pl.BlockSpec((None, TQ, wq), lambda bi, i: (bi, i, 0)),
        pl.BlockSpec((None, nt, wk), lambda bi, i: (bi, 0, 0), pipeline_mode=resident),
        pl.BlockSpec((None, nc, wv, TK), lambda bi, i: (bi, 0, 0, 0), pipeline_mode=resident),
    ] + [pl.BlockSpec(e.shape, lambda bi, i: (0, 0)) for e in extras]
    return pl.pallas_call(
        body,
        grid=(b, nc),
        in_specs=in_specs,
        out_specs=pl.BlockSpec((None, TQ, GROUP_WIDTH), lambda bi, i: (bi, i, 0)),
        out_shape=jax.ShapeDtypeStruct((b, nt, GROUP_WIDTH), BF16),
        scratch_shapes=[
            pltpu.VMEM((dk, n_sm * TQ), BF16),
            pltpu.VMEM((1, n_sm * TQ), F32),
            pltpu.VMEM((1, n_sm * TQ), F32),
            pltpu.VMEM((HEAD_DIM + ONES_ROWS, n_sm * TQ), F32),
            pltpu.VMEM((TK_LAT, n_sm * TQ), F32),
            pltpu.VMEM((TK_LAT, n_sm * TQ), F32),
            pltpu.VMEM((1, wk), F32),
        ],
        compiler_params=pltpu.CompilerParams(
            dimension_semantics=("arbitrary", "arbitrary"), vmem_limit_bytes=VMEM_LIMIT),
        name="attn_" + kind,
    )(q, k, vt, *extras)


def _attn_d_kernel(q_ref, kc_ref, kp_ref, ko_ref, kn_ref, vc_ref, vp_ref, vo_ref, vn_ref, sink_ref,
                   o_ref, *, n_blocks):
    n = pl.program_id(1)
    t, wn = TQ, WINDOW
    rep = D_HEADS // D_KV_HEADS
    qt = q_ref[...].astype(F32).T
    zeros = jnp.zeros((HEAD_DIM, t), F32)
    cols = []
    for hd in range(D_HEADS):
        parts = [zeros] * D_KV_HEADS
        parts[hd // rep] = qt[HEAD_DIM * hd:HEAD_DIM * (hd + 1)]
        cols.append(jnp.concatenate(parts, axis=0))
    qz = jnp.concatenate(cols, axis=1).astype(BF16)
    w = D_HEADS * t

    def offsets(rows):
        ko = lax.broadcasted_iota(jnp.int32, (rows, w), 0)
        qo = lax.broadcasted_iota(jnp.int32, (rows, w), 1) & (t - 1)
        return ko, qo

    band = n >= 1
    ko, qo = offsets(wn)
    ok_prev = jnp.logical_and(n >= 2, qo <= ko)
    ok_next = jnp.logical_and(jnp.logical_and(band, n + 1 <= n_blocks - 1), qo >= ko + (t - wn))
    ko, qo = offsets(t)
    ok_own = jnp.logical_and(band, jnp.abs(qo - ko) <= wn)
    s_c = jnp.dot(kc_ref[...], qz, preferred_element_type=F32)
    s_p = jnp.where(ok_prev, jnp.dot(kp_ref[...], qz, preferred_element_type=F32), NEG)
    s_o = jnp.where(ok_own, jnp.dot(ko_ref[...], qz, preferred_element_type=F32), NEG)
    s_n = jnp.where(ok_next, jnp.dot(kn_ref[...], qz, preferred_element_type=F32), NEG)
    sk = jnp.concatenate([jnp.broadcast_to(sink_ref[0:1, hd:hd + 1] * LOG2E, (1, t))
                          for hd in range(D_HEADS)], axis=1)
    m = jnp.maximum(jnp.max(s_c, axis=0, keepdims=True), jnp.max(s_p, axis=0, keepdims=True))
    m = jnp.maximum(m, jnp.max(s_o, axis=0, keepdims=True))
    m = jnp.maximum(m, jnp.max(s_n, axis=0, keepdims=True))
    m = jnp.maximum(m, sk)
    pieces = [(vc_ref[...], jnp.exp2(s_c - m).astype(BF16)), (vp_ref[...], jnp.exp2(s_p - m).astype(BF16)),
              (vo_ref[...], jnp.exp2(s_o - m).astype(BF16)), (vn_ref[...], jnp.exp2(s_n - m).astype(BF16))]
    l_sink = jnp.exp2(sk - m)
    outs = []
    for g in range(D_KV_HEADS):
        lanes = slice(rep * t * g, rep * t * (g + 1))
        o = None
        for v, p in pieces:
            va = jnp.concatenate([v[HEAD_DIM * g:HEAD_DIM * (g + 1)],
                                  jnp.ones((ONES_ROWS, v.shape[1]), BF16)], axis=0)
            d = jnp.dot(va, p[:, lanes], preferred_element_type=F32)
            o = d if o is None else o + d
        o = o[0:HEAD_DIM] / (o[HEAD_DIM:HEAD_DIM + 1] + l_sink[:, lanes])
        outs += [o[:, t * r:t * (r + 1)] for r in range(rep)]
    o_ref[...] = jnp.concatenate(outs, axis=0).T.astype(BF16)


def _attn_d(q, k, vt, sink, ctx_len):
    b, nt, wq = q.shape
    wk = k.shape[-1]
    n_blocks, wv = vt.shape[1], vt.shape[2]
    half = TQ // WINDOW
    n_half = nt // WINDOW
    assert ctx_len == TQ and TQ % WINDOW == 0 and half == 2

    in_specs = [
        pl.BlockSpec((None, TQ, wq), lambda bi, i: (bi, i, 0)),
        pl.BlockSpec((None, ctx_len, wk), lambda bi, i: (bi, 0, 0)),
        pl.BlockSpec((None, WINDOW, wk), lambda bi, i: (bi, jnp.clip(half * i - 1, 0, n_half - 1), 0)),
        pl.BlockSpec((None, TQ, wk), lambda bi, i: (bi, i, 0)),
        pl.BlockSpec((None, WINDOW, wk), lambda bi, i: (bi, jnp.clip(half * i + half, 0, n_half - 1), 0)),
        pl.BlockSpec((None, None, wv, ctx_len), lambda bi, i: (bi, 0, 0, 0)),
        pl.BlockSpec((None, None, wv, WINDOW), lambda bi, i: (bi, jnp.maximum(i - 1, 0), 0, half - 1)),
        pl.BlockSpec((None, None, wv, TQ), lambda bi, i: (bi, i, 0, 0)),
        pl.BlockSpec((None, None, wv, WINDOW), lambda bi, i: (bi, jnp.minimum(i + 1, n_blocks - 1), 0, 0)),
        pl.BlockSpec(sink.shape, lambda bi, i: (0, 0)),
    ]
    return pl.pallas_call(
        functools.partial(_attn_d_kernel, n_blocks=n_blocks),
        grid=(b, n_blocks),
        in_specs=in_specs,
        out_specs=pl.BlockSpec((None, TQ, GROUP_WIDTH), lambda bi, i: (bi, i, 0)),
        out_shape=jax.ShapeDtypeStruct((b, nt, GROUP_WIDTH), BF16),
        compiler_params=pltpu.CompilerParams(
            dimension_semantics=("arbitrary", "arbitrary"), vmem_limit_bytes=VMEM_LIMIT),
        name="attn_d",
    )(q, k, k, k, k, vt, vt, vt, vt, sink)


def _route(sel_rows, s_rows):
    epg = EXPERTS_PER_GROUP
    gscore = []
    for g in range(N_EXPERT_GROUPS):
        r = sel_rows[epg * g:epg * (g + 1)]
        pair = None
        for i in range(epg):
            for j in range(i + 1, epg):
                v = r[i] + r[j]
                pair = v if pair is None else jnp.maximum(pair, v)
        gscore.append(pair)
    best, best_g = gscore[0], jnp.zeros_like(gscore[0], dtype=jnp.int32)
    for g in range(1, N_EXPERT_GROUPS):
        better = gscore[g] > best
        best_g = jnp.where(better, g, best_g)
        best = jnp.where(better, gscore[g], best)
    w = []
    for e in range(N_EXPERTS):
        g = e // epg
        cnt = jnp.zeros_like(best_g)
        for e2 in range(epg * g, epg * (g + 1)):
            if e2 == e:
                continue
            beats = sel_rows[e2] > sel_rows[e]
            if e2 < e:
                beats = jnp.logical_or(beats, sel_rows[e2] == sel_rows[e])
            cnt = cnt + jnp.where(beats, 1, 0)
        chosen = jnp.logical_and(best_g == g, cnt < 2)
        w.append(jnp.where(chosen, s_rows[e], 0.0))
    tot = w[0]
    for e in range(1, N_EXPERTS):
        tot = tot + w[e]
    return [we / tot for we in w]


def _outproj_kernel(oa_ref, ob_ref, oc_ref, od_ref, x_ref, mod_ref, wout_ref, g_ref, b_ref,
                    rwt_ref, rb_ref, x1_ref, h2_ref, gates_ref, *, alpha):
    d = D_MODEL
    o = jnp.concatenate([oa_ref[...], ob_ref[...], oc_ref[...], od_ref[...]], axis=1)
    y = jnp.dot(o, wout_ref[...], preferred_element_type=F32)
    u = alpha * x_ref[...] + mod_ref[:, 2 * d:3 * d] * y
    x1 = _layer_norm(u) * g_ref[...] + b_ref[...]
    x1_ref[...] = x1
    h2 = _layer_norm(x1) * (1.0 + mod_ref[:, 4 * d:5 * d]) + mod_ref[:, 3 * d:4 * d]
    h2_ref[...] = h2.astype(BF16)

    h_hi, h_lo = _split_bf16(h2)
    w_hi, w_lo = _split_bf16(rwt_ref[...])
    logits = _nt_dot(w_hi, h_hi) + _nt_dot(w_hi, h_lo) + _nt_dot(w_lo, h_hi)
    s = jax.nn.sigmoid(logits)
    sel = s + rb_ref[...]
    s_rows = [s[e:e + 1] for e in range(N_EXPERTS)]
    sel_rows = [sel[e:e + 1] for e in range(N_EXPERTS)]
    gates = _route(sel_rows, s_rows)
    gt = jnp.concatenate(gates + [jnp.zeros((LANES - N_EXPERTS, TQ), F32)], axis=0)
    gates_ref[...] = gt.T


def _outproj(oa, ob, oc, od, xa, modsel, w_out, ln_g, ln_b, rwt, rb, alpha):
    b, nt, d = xa.shape
    nc = nt // TQ

    def rm(w):
        return pl.BlockSpec((None, TQ, w), lambda bi, i: (bi, i, 0))

    def full(a):
        nd = a.ndim
        return pl.BlockSpec(a.shape, lambda bi, i, _n=nd: (0,) * _n)

    in_specs = [rm(GROUP_WIDTH) for _ in range(4)] + [
        rm(d),
        pl.BlockSpec((None, None, 1, 6 * d), lambda bi, i: (bi, jnp.minimum(i, 1), 0, 0)),
        full(w_out), full(ln_g), full(ln_b), full(rwt), full(rb)]
    return pl.pallas_call(
        functools.partial(_outproj_kernel, alpha=alpha),
        grid=(b, nc),
        in_specs=in_specs,
        out_specs=[rm(d), rm(d), rm(LANES)],
        out_shape=[jax.ShapeDtypeStruct((b, nt, d), F32), jax.ShapeDtypeStruct((b, nt, d), BF16),
                   jax.ShapeDtypeStruct((b, nt, LANES), F32)],
        compiler_params=pltpu.CompilerParams(
            dimension_semantics=("arbitrary", "arbitrary"), vmem_limit_bytes=VMEM_LIMIT),
        name="outproj",
    )(oa, ob, oc, od, xa, modsel, w_out, ln_g, ln_b, rwt, rb)


def _moe_kernel(h_ref, gates_ref, x_ref, mod_ref, wg_ref, wu_ref, wd_ref, g_ref, b_ref, o_ref, *, alpha):
    d = D_MODEL
    h = h_ref[...]
    gates = gates_ref[...]
    y = jnp.zeros((TQ, d), F32)
    for g in range(N_EXPERT_GROUPS):
        acts = []
        for el in range(EXPERTS_PER_GROUP):
            e = EXPERTS_PER_GROUP * g + el
            hg = jnp.dot(h, wg_ref[e], preferred_element_type=F32)
            hu = jnp.dot(h, wu_ref[e], preferred_element_type=F32)
            a = hg * jax.nn.sigmoid(hg) * hu * gates[:, e:e + 1]
            acts.append(a.astype(BF16))
        y = y + jnp.dot(jnp.concatenate(acts, axis=1), wd_ref[g], preferred_element_type=F32)
    u = alpha * x_ref[...] + mod_ref[:, 5 * d:6 * d] * y
    o_ref[...] = _layer_norm(u) * g_ref[...] + b_ref[...]


def _moe(h2, gates, x1, modsel, wg, wu, wd, ln_g, ln_b, alpha, latents_only):
    b, nt, d = x1.shape
    nc = nt // TQ
    resident = pl.Buffered(1)
    if latents_only:
        out_spec = pl.BlockSpec((None, TQ, d), lambda bi, i: (bi, jnp.maximum(i - 1, 0), 0))
        out_rows = nt - TQ
    else:
        out_spec = pl.BlockSpec((None, TQ, d), lambda bi, i: (bi, i, 0))
        out_rows = nt

    def rm(w):
        return pl.BlockSpec((None, TQ, w), lambda bi, i: (bi, i, 0))

    def res(a):
        nd = a.ndim
        return pl.BlockSpec(a.shape, lambda bi, i, _n=nd: (0,) * _n, pipeline_mode=resident)

    in_specs = [rm(d), rm(LANES), rm(d),
                pl.BlockSpec((None, None, 1, 6 * d), lambda bi, i: (bi, jnp.minimum(i, 1), 0, 0)),
                res(wg), res(wu), res(wd), res(ln_g), res(ln_b)]
    return pl.pallas_call(
        functools.partial(_moe_kernel, alpha=alpha),
        grid=(b, nc),
        in_specs=in_specs,
        out_specs=out_spec,
        out_shape=jax.ShapeDtypeStruct((b, out_rows, d), F32),
        compiler_params=pltpu.CompilerParams(
            dimension_semantics=("arbitrary", "arbitrary"), vmem_limit_bytes=VMEM_LIMIT),
        name="moe",
    )(h2, gates, x1, modsel, wg, wu, wd, ln_g, ln_b)


def _rope_perm(rot_dim):
    n = rot_dim // 4
    j = np.arange(rot_dim)
    return j ^ n, np.where((j // n) % 2 == 0, -1.0, 1.0).astype(np.float32)


def _swapped(w, rot_dim):
    perm, sign = _rope_perm(rot_dim)
    cols = w.shape[-1]
    idx = (np.arange(cols) // rot_dim) * rot_dim + perm[np.arange(cols) % rot_dim]
    sgn = sign[np.arange(cols) % rot_dim]
    return w[..., idx] * sgn


def _rope_tables(n_lat, ctx_len):
    t = jnp.arange(n_lat, dtype=jnp.int32)
    row, col = (t // GRID_W).astype(F32), (t % GRID_W).astype(F32)

    def pattern(rot_dim):
        n = rot_dim // 4
        inv = ROPE_THETA ** (-jnp.arange(n, dtype=F32) / n)
        ar, ac = row[:, None] * inv, col[:, None] * inv
        ang = jnp.concatenate([ar, ar, ac, ac], axis=-1)
        return jnp.cos(ang), jnp.sin(ang)

    def with_ctx(a, fill):
        return jnp.concatenate([jnp.full((ctx_len, a.shape[1]), fill, F32), a], axis=0)

    c32, s32 = pattern(A_QK)
    c64, s64 = pattern(HEAD_DIM)
    ones = jnp.ones((n_lat, C_NOPE), F32)
    zeros = jnp.zeros((n_lat, C_NOPE), F32)
    pad1 = jnp.ones((n_lat, LANES - C_NOPE - C_ROPE), F32)
    pad0 = jnp.zeros((n_lat, LANES - C_NOPE - C_ROPE), F32)
    return {
        "cos32": with_ctx(jnp.tile(c32, (1, LANES // A_QK)), 1.0),
        "sin32": with_ctx(jnp.tile(s32, (1, LANES // A_QK)), 0.0),
        "cos64": with_ctx(jnp.tile(c64, (1, LANES // HEAD_DIM)), 1.0),
        "sin64": with_ctx(jnp.tile(s64, (1, LANES // HEAD_DIM)), 0.0),
        "cosc": with_ctx(jnp.concatenate([ones, c32, pad1], axis=-1), 1.0),
        "sinc": with_ctx(jnp.concatenate([zeros, s32, pad0], axis=-1), 0.0),
    }


def _prep_layer_weights(w_in, gq, gk, gcq, gckv, w_uq, w_ukv):
    d = w_in.shape[0]
    splits = np.cumsum([256, 256, 256, 256, 128, 128, C_Q_RANK, C_KV_RANK, C_ROPE, 256, 128, 128])[:-1]
    (a_q, a_k, a_v, b_q, b_k, b_v, c_q, c_kv, c_kr, d_q, d_k, d_v) = jnp.split(w_in, splits, axis=1)
    z = lambda n: jnp.zeros((d, n), F32)
    kr4 = jnp.concatenate([z(C_NOPE), c_kr, z(LANES - C_NOPE - C_ROPE)], axis=1)
    kr4s = jnp.concatenate([z(C_NOPE), _swapped(c_kr, C_ROPE), z(LANES - C_NOPE - C_ROPE)], axis=1)
    wrm = jnp.concatenate([
        a_q, _swapped(a_q, A_QK), a_k, _swapped(a_k, A_QK),
        b_q, _swapped(b_q, HEAD_DIM), b_k, _swapped(b_k, HEAD_DIM),
        d_q, _swapped(d_q, HEAD_DIM), d_k, _swapped(d_k, HEAD_DIM),
        c_q, z(256 - C_Q_RANK), c_kv, kr4, kr4s], axis=1)
    wt = jnp.concatenate([a_v, b_v, d_v], axis=1).T

    uq = w_uq.reshape(C_Q_RANK, C_HEADS, C_NOPE + C_ROPE)
    uq_n, uq_r = uq[..., :C_NOPE], uq[..., C_NOPE:]
    zq = lambda n: jnp.zeros((C_Q_RANK, C_HEADS, n), F32)
    pad = LANES - C_NOPE - C_ROPE
    wuq = jnp.concatenate([uq_n, uq_r, zq(pad)], axis=-1).reshape(C_Q_RANK, C_HEADS * LANES)
    wuqs = jnp.concatenate([zq(C_NOPE), _swapped(uq_r, C_ROPE), zq(pad)], axis=-1).reshape(C_Q_RANK, C_HEADS * LANES)
    zrows = jnp.zeros((256 - C_Q_RANK, C_HEADS * LANES), F32)
    wuq, wuqs = jnp.concatenate([wuq, zrows], axis=0), jnp.concatenate([wuqs, zrows], axis=0)
    ukv = w_ukv.reshape(C_KV_RANK, C_HEADS, C_NOPE + HEAD_DIM)
    uk_n, u_v = ukv[..., :C_NOPE], ukv[..., C_NOPE:]
    wukn = jnp.concatenate([uk_n, jnp.zeros((C_KV_RANK, C_HEADS, LANES - C_NOPE), F32)], axis=-1)
    wukn = wukn.reshape(C_KV_RANK, C_HEADS * LANES)
    wuvt = u_v.reshape(C_KV_RANK, C_HEADS * HEAD_DIM).T

    perm64, _ = _rope_perm(HEAD_DIM)
    return {
        "wrm": wrm.astype(BF16), "wt": wt.astype(BF16),
        "wuq": wuq.astype(BF16), "wuqs": wuqs.astype(BF16),
        "wukn": wukn.astype(BF16), "wuvt": wuvt.astype(BF16),
        "gqb": jnp.tile(gq, B_HEADS)[None, :], "gqbs": jnp.tile(gq[perm64], B_HEADS)[None, :],
        "gkb": jnp.tile(gk, B_KV_HEADS)[None, :], "gkbs": jnp.tile(gk[perm64], B_KV_HEADS)[None, :],
        "gcq": jnp.concatenate([gcq, jnp.zeros((256 - C_Q_RANK,), F32)])[None, :],
        "gckv": gckv[None, :],
    }


def kernel(x, c, ctx, c_ctx, w_ada, b_ada, w_in, w_out, diff_lambda_q1, diff_lambda_k1, diff_lambda_q2,
           diff_lambda_k2, diff_subln_g, gqa_q_norm_g, gqa_k_norm_g, mla_q_norm_g, mla_kv_norm_g, mla_w_uq,
           mla_w_ukv, swa_sink, ln1_g, ln1_b, ln2_g, ln2_b, router_w, router_bias,
           exp_w_gate, exp_w_up, exp_w_down):
    b, n_lat, d = x.shape
    ctx_len = ctx.shape[1]
    depth = w_ada.shape[0]
    assert d == D_MODEL and ctx_len == TQ and n_lat % TQ == 0 and b + 1 <= 8
    alpha = (2 * depth) ** 0.25

    xa = jnp.concatenate([ctx, x], axis=1)
    tabs = _rope_tables(n_lat, ctx_len)

    cc = jnp.concatenate([c, c_ctx[None, :], jnp.zeros((8 - b - 1, d), F32)], axis=0)
    mods = _ada(cc, w_ada, b_ada)
    rwt = router_w.T
    rb = router_bias[:, None]

    for l in range(depth):
        lat = mods[l, :b]
        cx = jnp.broadcast_to(mods[l, b], lat.shape)
        modsel = jnp.stack([cx, lat], axis=1)[:, :, None, :]
        lw = _prep_layer_weights(w_in[l], gqa_q_norm_g[l], gqa_k_norm_g[l], mla_q_norm_g[l],
                                 mla_kv_norm_g[l], mla_w_uq[l], mla_w_ukv[l])
        qa, ka, vat, qb, kb, vbt, qc, kc, vct, qd, kd, vdt = _proj(xa, modsel, lw, tabs)

        lam_init = 0.8 - 0.6 * math.exp(-0.3 * l)
        extras = (diff_lambda_q1[l][None, :], diff_lambda_k1[l][None, :], diff_lambda_q2[l][None, :],
                  diff_lambda_k2[l][None, :], diff_subln_g[l][:, None])
        oa = _attn_full("a", qa, ka, vat, extras, lam_init)
        ob = _attn_full("b", qb, kb, vbt)
        oc = _attn_full("c", qc, kc, vct)
        od = _attn_d(qd, kd, vdt, swa_sink[l][None, :], ctx_len)

        x1, h2, gates = _outproj(oa, ob, oc, od, xa, modsel, w_out[l].astype(BF16),
                                 ln1_g[l][None, :], ln1_b[l][None, :], rwt, rb, alpha)
        wg = exp_w_gate[l].astype(BF16)
        wu = exp_w_up[l].astype(BF16)
        wd = exp_w_down[l].astype(BF16).reshape(N_EXPERT_GROUPS, EXPERTS_PER_GROUP * D_EXPERT, d)
        xa = _moe(h2, gates, x1, modsel, wg, wu, wd, ln2_g[l][None, :], ln2_b[l][None, :], alpha,
                  latents_only=(l == depth - 1))
    return xa
```

```python
import functools
import math

import numpy as np
import jax
import jax.numpy as jnp
from jax import lax
from jax.experimental import pallas as pl
from jax.experimental.pallas import tpu as pltpu

F32 = jnp.float32
BF16 = jnp.bfloat16

D_MODEL = 1024
HEAD_DIM = 64
GROUP_WIDTH = 256
GRID_W = 64
ROPE_THETA = 10000.0
EPS = 1e-6
NEG = -1e30
LOG2E = 1.4426950408889634

A_HEADS = 4
A_QK = 32
B_HEADS = 4
B_KV_HEADS = 2
C_HEADS = 4
C_Q_RANK = 192
C_KV_RANK = 128
C_NOPE = 64
C_ROPE = 32
D_HEADS = 4
D_KV_HEADS = 2
WINDOW = 128
N_EXPERTS = 16
EXPERTS_PER_GROUP = 4
N_EXPERT_GROUPS = 4
D_EXPERT = 256

TQ = 256
TK = 256
TK_LAT = 1024
BOUND_SLACK = 1.0 + 2.0 ** -7
MIN_DENOMINATOR = 2.0 ** -80
LANES = 128
ONES_ROWS = 16
VMEM_LIMIT = 56 * 1024 * 1024

_C_QA, _C_QAS, _C_KA, _C_KAS = 0, 256, 512, 768
_C_QB, _C_QBS, _C_KB, _C_KBS = 1024, 1280, 1536, 1664
_C_QD, _C_QDS, _C_KD, _C_KDS = 1792, 2048, 2304, 2432
_C_CQ, _C_CKV, _C_KR, _C_KRS = 2560, 2816, 2944, 3072
_C_TOTAL = 3200
_R_VA, _R_VB, _R_VD, _R_TOTAL = 0, 256, 384, 512


def _nt_dot(a, b):
    return lax.dot_general(a, b, (((1,), (1,)), ((), ())), preferred_element_type=F32)


def _split_bf16(a):
    hi = a.astype(BF16)
    lo = (a - hi.astype(F32)).astype(BF16)
    return hi, lo


def _layer_norm(x):
    mu = jnp.mean(x, axis=-1, keepdims=True)
    xc = x - mu
    var = jnp.mean(xc * xc, axis=-1, keepdims=True)
    return xc * lax.rsqrt(var + EPS)


def _tile_lanes(a, n):
    return jnp.concatenate([a] * n, axis=1)


def _ada_kernel(cc_ref, w_ref, b_ref, o_ref):
    cc = cc_ref[...]
    s = cc * jax.nn.sigmoid(cc)
    s_hi, s_lo = _split_bf16(s)
    w_hi, w_lo = _split_bf16(w_ref[...])
    acc = jnp.dot(s_hi, w_hi, preferred_element_type=F32)
    acc += jnp.dot(s_hi, w_lo, preferred_element_type=F32)
    acc += jnp.dot(s_lo, w_hi, preferred_element_type=F32)
    o_ref[...] = acc + b_ref[...]


def _ada(cc, w_ada, b_ada):
    depth, d, n6 = w_ada.shape
    bn = 1536
    return pl.pallas_call(
        _ada_kernel,
        grid=(depth, n6 // bn),
        in_specs=[
            pl.BlockSpec((8, d), lambda l, j: (0, 0)),
            pl.BlockSpec((None, d, bn), lambda l, j: (l, 0, j)),
            pl.BlockSpec((None, 1, bn), lambda l, j: (l, 0, j)),
        ],
        out_specs=pl.BlockSpec((None, 8, bn), lambda l, j: (l, 0, j)),
        out_shape=jax.ShapeDtypeStruct((depth, 8, n6), F32),
        compiler_params=pltpu.CompilerParams(
            dimension_semantics=("arbitrary", "arbitrary"), vmem_limit_bytes=VMEM_LIMIT),
        name="ada",
    )(cc, w_ada, b_ada.reshape(depth, 1, n6))


def _block_diag_ones(n, blk):
    sh = int(math.log2(blk))
    r = lax.broadcasted_iota(jnp.int32, (n, n), 0) >> sh
    c = lax.broadcasted_iota(jnp.int32, (n, n), 1) >> sh
    return jnp.where(r == c, 1.0, 0.0).astype(BF16)


def _proj_kernel(x_ref, mod_ref, wrm_ref, wt_ref, wuq_ref, wuqs_ref, wukn_ref, wuvt_ref,
                 gqb_ref, gqbs_ref, gkb_ref, gkbs_ref, gcq_ref, gckv_ref,
                 cos32_ref, sin32_ref, cos64_ref, sin64_ref, cosc_ref, sinc_ref,
                 qa_ref, ka_ref, vat_ref, qb_ref, kb_ref, vbt_ref,
                 qc_ref, kc_ref, vct_ref, qd_ref, kd_ref, vdt_ref):
    d = D_MODEL
    xn = _layer_norm(x_ref[...])
    h = (xn * (1.0 + mod_ref[:, d:2 * d]) + mod_ref[:, 0:d]).astype(BF16)

    def cols(lo, width):
        return jnp.dot(h, wrm_ref[:, lo:lo + width], preferred_element_type=F32)

    cos32, sin32 = _tile_lanes(cos32_ref[...], 2), _tile_lanes(sin32_ref[...], 2)
    cos64, sin64 = cos64_ref[...], sin64_ref[...]
    cos64w, sin64w = _tile_lanes(cos64, 2), _tile_lanes(sin64, 2)

    sa = (A_QK ** -0.5) * LOG2E
    qa_ref[...] = ((cols(_C_QA, 256) * cos32 + cols(_C_QAS, 256) * sin32) * sa).astype(BF16)
    ka_ref[...] = (cols(_C_KA, 256) * cos32 + cols(_C_KAS, 256) * sin32).astype(BF16)

    sb = (HEAD_DIM ** -0.5) * LOG2E
    q = cols(_C_QB, 256)
    ssq = jnp.dot((q * q).astype(BF16), _block_diag_ones(256, HEAD_DIM), preferred_element_type=F32)
    r = lax.rsqrt(ssq * (1.0 / HEAD_DIM) + EPS)
    qr = (q * gqb_ref[...]) * cos64w + (cols(_C_QBS, 256) * gqbs_ref[...]) * sin64w
    qb_ref[...] = (qr * (r * sb)).astype(BF16)
    k = cols(_C_KB, 128)
    ssq = jnp.dot((k * k).astype(BF16), _block_diag_ones(128, HEAD_DIM), preferred_element_type=F32)
    r = lax.rsqrt(ssq * (1.0 / HEAD_DIM) + EPS)
    kr = (k * gkb_ref[...]) * cos64 + (cols(_C_KBS, 128) * gkbs_ref[...]) * sin64
    kb_ref[...] = (kr * r).astype(BF16)

    qd_ref[...] = ((cols(_C_QD, 256) * cos64w + cols(_C_QDS, 256) * sin64w) * sb).astype(BF16)
    kd_ref[...] = (cols(_C_KD, 128) * cos64 + cols(_C_KDS, 128) * sin64).astype(BF16)

    sc = ((C_NOPE + C_ROPE) ** -0.5) * LOG2E
    cosc, sinc = cosc_ref[...], sinc_ref[...]
    cq = cols(_C_CQ, 256)
    ms = jnp.sum(cq * cq, axis=-1, keepdims=True) * (1.0 / C_Q_RANK)
    cqn = (cq * lax.rsqrt(ms + EPS) * gcq_ref[...]).astype(BF16)
    qn = jnp.dot(cqn, wuq_ref[...], preferred_element_type=F32)
    qs = jnp.dot(cqn, wuqs_ref[...], preferred_element_type=F32)
    qc_ref[...] = ((qn * _tile_lanes(cosc, 4) + qs * _tile_lanes(sinc, 4)) * sc).astype(BF16)
    ckv = cols(_C_CKV, 128)
    ms = jnp.mean(ckv * ckv, axis=-1, keepdims=True)
    ckvn = (ckv * lax.rsqrt(ms + EPS) * gckv_ref[...]).astype(BF16)
    kn = jnp.dot(ckvn, wukn_ref[...], preferred_element_type=F32)
    krr = cols(_C_KR, 128) * cosc + cols(_C_KRS, 128) * sinc
    kc_ref[...] = (kn + _tile_lanes(krr, 4)).astype(BF16)
    vct_ref[...] = _nt_dot(wuvt_ref[...], ckvn).astype(BF16)

    vt = _nt_dot(wt_ref[...], h)
    vat_ref[...] = vt[_R_VA:_R_VA + 256].astype(BF16)
    vbt_ref[...] = vt[_R_VB:_R_VB + 128].astype(BF16)
    vdt_ref[...] = vt[_R_VD:_R_VD + 128].astype(BF16)


def _proj(xa, modsel, lw, tabs):
    b, nt, d = xa.shape
    nc = nt // TQ

    def full(a):
        nd = a.ndim
        return pl.BlockSpec(a.shape, lambda bi, i, _n=nd: (0,) * _n)

    def tab(a):
        return pl.BlockSpec((TQ, LANES), lambda bi, i: (i, 0))

    def rm(w):
        return pl.BlockSpec((None, TQ, w), lambda bi, i: (bi, i, 0))

    def tr(w):
        return pl.BlockSpec((None, None, w, TQ), lambda bi, i: (bi, i, 0, 0))

    weights = [lw["wrm"], lw["wt"], lw["wuq"], lw["wuqs"], lw["wukn"], lw["wuvt"],
               lw["gqb"], lw["gqbs"], lw["gkb"], lw["gkbs"], lw["gcq"], lw["gckv"]]
    tables = [tabs["cos32"], tabs["sin32"], tabs["cos64"], tabs["sin64"], tabs["cosc"], tabs["sinc"]]
    in_specs = ([pl.BlockSpec((None, TQ, d), lambda bi, i: (bi, i, 0)),
                 pl.BlockSpec((None, None, 1, 6 * d), lambda bi, i: (bi, jnp.minimum(i, 1), 0, 0))]
                + [full(w) for w in weights] + [tab(t) for t in tables])
    widths = [("rm", 256), ("rm", 256), ("tr", 256), ("rm", 256), ("rm", 128), ("tr", 128),
              ("rm", 512), ("rm", 512), ("tr", 256), ("rm", 256), ("rm", 128), ("tr", 128)]
    out_specs, out_shape = [], []
    for kind, w in widths:
        if kind == "rm":
            out_specs.append(rm(w))
            out_shape.append(jax.ShapeDtypeStruct((b, nt, w), BF16))
        else:
            out_specs.append(tr(w))
            out_shape.append(jax.ShapeDtypeStruct((b, nc, w, TQ), BF16))
    return pl.pallas_call(
        _proj_kernel,
        grid=(b, nc),
        in_specs=in_specs,
        out_specs=out_specs,
        out_shape=out_shape,
        compiler_params=pltpu.CompilerParams(
            dimension_semantics=("arbitrary", "arbitrary"), vmem_limit_bytes=VMEM_LIMIT),
        name="proj",
    )(xa, modsel, *weights, *tables)


def _flash_loop(qi, nc, k_ref, vt_ref, qz_ref, m_ref, alpha_ref, acc_ref, s0_ref, s1_ref, kabs_ref, k_lanes, pv_cfg,
                slab_rows, bounded_pieces, groups_per_sweep):
    per = TK_LAT // TK
    n_lat = (nc - 1) // per
    assert n_lat % 2 == 0 and n_lat >= 2 and (nc - 1) % bounded_pieces == 0
    same_lanes = all(kl == k_lanes[0] for kl in k_lanes)

    def scores(kc):
        if same_lanes:
            return jnp.dot(kc[:, k_lanes[0][0]:k_lanes[0][1]], qz_ref[...], preferred_element_type=F32)
        return jnp.concatenate(
            [jnp.dot(kc[:, lo:hi], qz_ref[:, TQ * j:TQ * (j + 1)], preferred_element_type=F32)
             for j, (lo, hi) in enumerate(k_lanes)], axis=1)

    def small_chunk(pieces, first):
        if slab_rows >= pieces * TK:
            return (1, pieces, first)
        return (pieces * TK // slab_rows, slab_rows // TK, first)

    ctx_chunk = small_chunk(1, 0)

    def lat_chunk(c):
        return small_chunk(per, 1 + c * per)

    def stats_update(mx):
        m_old = m_ref[...]
        m_new = jnp.maximum(m_old, mx)
        alpha_ref[...] = jnp.exp2(m_old - m_new)
        m_ref[...] = m_new

    def pipelined(cur, cur_ref, nxt, nxt_ref):
        m_cur, alpha = m_ref[...], alpha_ref[...]
        mx, pv = None, [None] * len(pv_cfg)
        n_cur = cur[0] if cur is not None else 0
        n_nxt = nxt[0] if nxt is not None else 0
        for r in range(max(n_cur, n_nxt)):
            if r < n_nxt:
                rows = TK * nxt[1]
                kc = k_ref[pl.ds(pl.multiple_of((nxt[2] + r * nxt[1]) * TK, TK), rows), :]
                s = scores(kc)
                nxt_ref[rows * r:rows * (r + 1), :] = s
                smx = jnp.max(s, axis=0, keepdims=True)
                mx = smx if mx is None else jnp.maximum(mx, smx)
            if r < n_cur:
                rows = TK * cur[1]
                p = jnp.exp2(cur_ref[rows * r:rows * (r + 1), :] - m_cur).astype(BF16)
                vc = jnp.concatenate([vt_ref[cur[2] + r * cur[1] + i] for i in range(cur[1])], axis=1)
                ones = jnp.ones((ONES_ROWS, rows), BF16)
                for g, (vl, vh, j0, nj) in enumerate(pv_cfg):
                    va = jnp.concatenate([vc[vl:vh, :], ones], axis=0)
                    d = jnp.dot(va, p[:, TQ * j0:TQ * (j0 + nj)], preferred_element_type=F32)
                    pv[g] = d if pv[g] is None else pv[g] + d
        if cur is not None:
            for g, (vl, vh, j0, nj) in enumerate(pv_cfg):
                cols = slice(TQ * j0, TQ * (j0 + nj))
                acc_ref[:, cols] = alpha[:, cols] * acc_ref[:, cols] + pv[g]
        if nxt is not None:
            stats_update(mx)

    def online_pass():
        m_ref[...] = jnp.full(m_ref.shape, NEG, F32)
        acc_ref[...] = jnp.zeros(acc_ref.shape, F32)

        @pl.when(qi == 0)
        def _():
            pipelined(None, None, ctx_chunk, s0_ref)
            pipelined(ctx_chunk, s0_ref, None, None)

        @pl.when(qi > 0)
        def _():
            pipelined(None, None, lat_chunk(0), s0_ref)

            def body(i, carry):
                c = 2 * i
                pipelined(lat_chunk(c), s0_ref, lat_chunk(c + 1), s1_ref)
                pipelined(lat_chunk(c + 1), s1_ref, lat_chunk(c + 2), s0_ref)
                return carry

            lax.fori_loop(0, n_lat // 2 - 1, body, 0)
            pipelined(lat_chunk(n_lat - 2), s0_ref, lat_chunk(n_lat - 1), s1_ref)
            pipelined(lat_chunk(n_lat - 1), s1_ref, ctx_chunk, s0_ref)
            pipelined(ctx_chunk, s0_ref, None, None)

    def bounded_pass():
        wk = kabs_ref.shape[1]
        kab = jnp.broadcast_to(kabs_ref[...], (ONES_ROWS, wk)).astype(BF16)
        aq = jnp.abs(qz_ref[...])
        if same_lanes:
            lo, hi = k_lanes[0]
            bound = jnp.dot(kab[:, lo:hi], aq, preferred_element_type=F32)[0:1]
        else:
            bound = jnp.concatenate(
                [jnp.dot(kab[:, lo:hi], aq[:, TQ * j:TQ * (j + 1)], preferred_element_type=F32)[0:1]
                 for j, (lo, hi) in enumerate(k_lanes)], axis=1)
        bound = bound * BOUND_SLACK
        ones = jnp.ones((ONES_ROWS, TK), BF16)
        acc_ref[...] = jnp.zeros(acc_ref.shape, F32)

        def pieces(first, n, groups):
            pv = {g: None for g in groups}

            def probs(r, g):
                _, _, j0, nj = pv_cfg[g]
                kc = k_ref[pl.ds(pl.multiple_of((first + r) * TK, TK), TK), :]
                if all(k_lanes[j] == k_lanes[j0] for j in range(j0, j0 + nj)):
                    s = jnp.dot(kc[:, k_lanes[j0][0]:k_lanes[j0][1]], qz_ref[:, TQ * j0:TQ * (j0 + nj)],
                                preferred_element_type=F32)
                else:
                    s = jnp.concatenate(
                        [jnp.dot(kc[:, k_lanes[j][0]:k_lanes[j][1]], qz_ref[:, TQ * j:TQ * (j + 1)],
                                 preferred_element_type=F32) for j in range(j0, j0 + nj)], axis=1)
                return jnp.exp2(s - bound[:, TQ * j0:TQ * (j0 + nj)]).astype(BF16)

            def values(r, g, p):
                vl, vh, _, _ = pv_cfg[g]
                va = jnp.concatenate([vt_ref[first + r][vl:vh, :], ones], axis=0)
                d = jnp.dot(va, p, preferred_element_type=F32)
                pv[g] = d if pv[g] is None else pv[g] + d

            p_prev = {g: probs(0, g) for g in groups}
            for r in range(1, n):
                p_cur = {}
                for g in groups:
                    p_cur[g] = probs(r, g)
                    values(r - 1, g, p_prev[g])
                p_prev = p_cur
            for g in groups:
                values(n - 1, g, p_prev[g])
            for g in groups:
                _, _, j0, nj = pv_cfg[g]
                cols = slice(TQ * j0, TQ * (j0 + nj))
                acc_ref[:, cols] = acc_ref[:, cols] + pv[g]

        n_sets = len(pv_cfg) // groups_per_sweep
        for st in range(n_sets):
            groups = list(range(st * groups_per_sweep, (st + 1) * groups_per_sweep))
            pieces(0, 1, groups)

            @pl.when(qi > 0)
            def _(groups=groups):
                def body(i, carry):
                    pieces(1 + i * bounded_pieces, bounded_pieces, groups)
                    return carry

                lax.fori_loop(0, (nc - 1) // bounded_pieces, body, 0)

    @pl.when(qi == 0)
    def _():
        def body(c, mx):
            blk = k_ref[pl.ds(pl.multiple_of(c * TK, TK), TK), :].astype(F32)
            return jnp.maximum(mx, jnp.max(jnp.abs(blk), axis=0, keepdims=True))

        kabs_ref[...] = lax.fori_loop(0, nc, body, jnp.zeros(kabs_ref.shape, F32))

    bounded_pass()
    dv = acc_ref.shape[0] - ONES_ROWS
    healthy = jnp.min(jnp.where(acc_ref[dv:dv + 1, :] >= MIN_DENOMINATOR, 1.0, 0.0))

    @pl.when(healthy < 0.5)
    def _():
        online_pass()


def _normalized(acc_ref, j, dv):
    a = acc_ref[:, TQ * j:TQ * (j + 1)]
    return a[0:dv] / a[dv:dv + 1]


def _attn_a_kernel(q_ref, k_ref, vt_ref, lq1_ref, lk1_ref, lq2_ref, lk2_ref, g_ref, o_ref,
                   qz_ref, m_ref, alpha_ref, acc_ref, s0_ref, s1_ref, kabs_ref, *, nc, lam_init):
    qi = pl.program_id(1)
    qt = q_ref[...].astype(F32).T
    row = lax.broadcasted_iota(jnp.int32, (LANES, TQ), 0)
    per_blk = LANES // A_QK
    k_lanes = []
    for j in range(2 * A_HEADS):
        blk, sub = j // per_blk, j % per_blk
        keep = (row >= A_QK * sub) & (row < A_QK * (sub + 1))
        qz_ref[:, TQ * j:TQ * (j + 1)] = jnp.where(keep, qt[LANES * blk:LANES * (blk + 1)], 0.0).astype(BF16)
        k_lanes.append((LANES * blk, LANES * (blk + 1)))
    pv_cfg = [(HEAD_DIM * hd, HEAD_DIM * (hd + 1), 2 * hd, 2) for hd in range(A_HEADS)]
    _flash_loop(qi, nc, k_ref, vt_ref, qz_ref, m_ref, alpha_ref, acc_ref, s0_ref, s1_ref, kabs_ref, k_lanes, pv_cfg,
                slab_rows=1024, bounded_pieces=64, groups_per_sweep=2)

    lam = (jnp.exp(jnp.sum(lq1_ref[...] * lk1_ref[...], axis=-1, keepdims=True))
           - jnp.exp(jnp.sum(lq2_ref[...] * lk2_ref[...], axis=-1, keepdims=True)) + lam_init)
    outs = []
    for hd in range(A_HEADS):
        o = _normalized(acc_ref, 2 * hd, HEAD_DIM) - lam * _normalized(acc_ref, 2 * hd + 1, HEAD_DIM)
        ms = jnp.mean(o * o, axis=0, keepdims=True)
        outs.append(o * lax.rsqrt(ms + EPS) * g_ref[...] * (1.0 - lam_init))
    o_ref[...] = jnp.concatenate(outs, axis=0).T.astype(BF16)


def _attn_b_kernel(q_ref, k_ref, vt_ref, o_ref, qz_ref, m_ref, alpha_ref, acc_ref, s0_ref, s1_ref, kabs_ref, *, nc):
    qi = pl.program_id(1)
    qt = q_ref[...].astype(F32).T
    zeros = jnp.zeros((HEAD_DIM, TQ), F32)
    rep = B_HEADS // B_KV_HEADS
    for hd in range(B_HEADS):
        parts = [zeros] * B_KV_HEADS
        parts[hd // rep] = qt[HEAD_DIM * hd:HEAD_DIM * (hd + 1)]
        qz_ref[:, TQ * hd:TQ * (hd + 1)] = jnp.concatenate(parts, axis=0).astype(BF16)
    k_lanes = [(0, 128)] * B_HEADS
    pv_cfg = [(HEAD_DIM * g, HEAD_DIM * (g + 1), rep * g, rep) for g in range(B_KV_HEADS)]
    _flash_loop(qi, nc, k_ref, vt_ref, qz_ref, m_ref, alpha_ref, acc_ref, s0_ref, s1_ref, kabs_ref, k_lanes, pv_cfg,
                slab_rows=256, bounded_pieces=64, groups_per_sweep=B_KV_HEADS)
    outs = [_normalized(acc_ref, hd, HEAD_DIM) for hd in range(B_HEADS)]
    o_ref[...] = jnp.concatenate(outs, axis=0).T.astype(BF16)


def _attn_c_kernel(q_ref, k_ref, vt_ref, o_ref, qz_ref, m_ref, alpha_ref, acc_ref, s0_ref, s1_ref, kabs_ref, *, nc):
    qi = pl.program_id(1)
    qt = q_ref[...].astype(F32).T
    k_lanes, pv_cfg = [], []
    for hd in range(C_HEADS):
        qz_ref[:, TQ * hd:TQ * (hd + 1)] = qt[LANES * hd:LANES * (hd + 1)].astype(BF16)
        k_lanes.append((LANES * hd, LANES * (hd + 1)))
        pv_cfg.append((HEAD_DIM * hd, HEAD_DIM * (hd + 1), hd, 1))
    _flash_loop(qi, nc, k_ref, vt_ref, qz_ref, m_ref, alpha_ref, acc_ref, s0_ref, s1_ref, kabs_ref, k_lanes, pv_cfg,
                slab_rows=256, bounded_pieces=64, groups_per_sweep=C_HEADS)
    outs = [_normalized(acc_ref, hd, HEAD_DIM) for hd in range(C_HEADS)]
    o_ref[...] = jnp.concatenate(outs, axis=0).T.astype(BF16)


def _attn_full(kind, q, k, vt, extras=(), lam_init=0.0):
    b, nt, wq = q.shape
    wk = k.shape[-1]
    nc, wv = vt.shape[1], vt.shape[2]
    assert (nt - TK) % TK_LAT == 0
    if kind == "a":
        body, n_sm, dk = functools.partial(_attn_a_kernel, nc=nc, lam_init=lam_init), 2 * A_HEADS, 128
    elif kind == "b":
        body, n_sm, dk = functools.partial(_attn_b_kernel, nc=nc), B_HEADS, 128
    else:
        body, n_sm, dk = functools.partial(_attn_c_kernel, nc=nc), C_HEADS, 128
    resident = pl.Buffered(1)
    in_specs = [
        pl.BlockSpec((None, TQ, wq), lambda bi, i: (bi, i, 0)),
        pl.BlockSpec((None, nt, wk), lambda bi, i: (bi, 0, 0), pipeline_mode=resident),
        pl.BlockSpec((None, nc, wv, TK), lambda bi, i: (bi, 0, 0, 0), pipeline_mode=resident),
    ] + [pl.BlockSpec(e.shape, lambda bi, i: (0, 0)) for e in extras]
    return pl.pallas_call(
        body,
        grid=(b, nc),
        in_specs=in_specs,
        out_specs=pl.BlockSpec((None, TQ, GROUP_WIDTH), lambda bi, i: (bi, i, 0)),
        out_shape=jax.ShapeDtypeStruct((b, nt, GROUP_WIDTH), BF16),
        scratch_shapes=[
            pltpu.VMEM((dk, n_sm * TQ), BF16),
            pltpu.VMEM((1, n_sm * TQ), F32),
            pltpu.VMEM((1, n_sm * TQ), F32),
            pltpu.VMEM((HEAD_DIM + ONES_ROWS, n_sm * TQ), F32),
            pltpu.VMEM((TK_LAT, n_sm * TQ), F32),
            pltpu.VMEM((TK_LAT, n_sm * TQ), F32),
            pltpu.VMEM((1, wk), F32),
        ],
        compiler_params=pltpu.CompilerParams(
            dimension_semantics=("arbitrary", "arbitrary"), vmem_limit_bytes=VMEM_LIMIT),
        name="attn_" + kind,
    )(q, k, vt, *extras)


def _attn_d_kernel(q_ref, kc_ref, kp_ref, ko_ref, kn_ref, vc_ref, vp_ref, vo_ref, vn_ref, sink_ref,
                   o_ref, *, n_blocks):
    n = pl.program_id(1)
    t, wn = TQ, WINDOW
    rep = D_HEADS // D_KV_HEADS
    qt = q_ref[...].astype(F32).T
    zeros = jnp.zeros((HEAD_DIM, t), F32)
    cols = []
    for hd in range(D_HEADS):
        parts = [zeros] * D_KV_HEADS
        parts[hd // rep] = qt[HEAD_DIM * hd:HEAD_DIM * (hd + 1)]
        cols.append(jnp.concatenate(parts, axis=0))
    qz = jnp.concatenate(cols, axis=1).astype(BF16)
    w = D_HEADS * t

    def offsets(rows):
        ko = lax.broadcasted_iota(jnp.int32, (rows, w), 0)
        qo = lax.broadcasted_iota(jnp.int32, (rows, w), 1) & (t - 1)
        return ko, qo

    band = n >= 1
    ko, qo = offsets(wn)
    ok_prev = jnp.logical_and(n >= 2, qo <= ko)
    ok_next = jnp.logical_and(jnp.logical_and(band, n + 1 <= n_blocks - 1), qo >= ko + (t - wn))
    ko, qo = offsets(t)
    ok_own = jnp.logical_and(band, jnp.abs(qo - ko) <= wn)
    s_c = jnp.dot(kc_ref[...], qz, preferred_element_type=F32)
    s_p = jnp.where(ok_prev, jnp.dot(kp_ref[...], qz, preferred_element_type=F32), NEG)
    s_o = jnp.where(ok_own, jnp.dot(ko_ref[...], qz, preferred_element_type=F32), NEG)
    s_n = jnp.where(ok_next, jnp.dot(kn_ref[...], qz, preferred_element_type=F32), NEG)
    sk = jnp.concatenate([jnp.broadcast_to(sink_ref[0:1, hd:hd + 1] * LOG2E, (1, t))
                          for hd in range(D_HEADS)], axis=1)
    m = jnp.maximum(jnp.max(s_c, axis=0, keepdims=True), jnp.max(s_p, axis=0, keepdims=True))
    m = jnp.maximum(m, jnp.max(s_o, axis=0, keepdims=True))
    m = jnp.maximum(m, jnp.max(s_n, axis=0, keepdims=True))
    m = jnp.maximum(m, sk)
    pieces = [(vc_ref[...], jnp.exp2(s_c - m).astype(BF16)), (vp_ref[...], jnp.exp2(s_p - m).astype(BF16)),
              (vo_ref[...], jnp.exp2(s_o - m).astype(BF16)), (vn_ref[...], jnp.exp2(s_n - m).astype(BF16))]
    l_sink = jnp.exp2(sk - m)
    outs = []
    for g in range(D_KV_HEADS):
        lanes = slice(rep * t * g, rep * t * (g + 1))
        o = None
        for v, p in pieces:
            va = jnp.concatenate([v[HEAD_DIM * g:HEAD_DIM * (g + 1)],
                                  jnp.ones((ONES_ROWS, v.shape[1]), BF16)], axis=0)
            d = jnp.dot(va, p[:, lanes], preferred_element_type=F32)
            o = d if o is None else o + d
        o = o[0:HEAD_DIM] / (o[HEAD_DIM:HEAD_DIM + 1] + l_sink[:, lanes])
        outs += [o[:, t * r:t * (r + 1)] for r in range(rep)]
    o_ref[...] = jnp.concatenate(outs, axis=0).T.astype(BF16)


def _attn_d(q, k, vt, sink, ctx_len):
    b, nt, wq = q.shape
    wk = k.shape[-1]
    n_blocks, wv = vt.shape[1], vt.shape[2]
    half = TQ // WINDOW
    n_half = nt // WINDOW
    assert ctx_len == TQ and TQ % WINDOW == 0 and half == 2

    in_specs = [
        pl.BlockSpec((None, TQ, wq), lambda bi, i: (bi, i, 0)),
        pl.BlockSpec((None, ctx_len, wk), lambda bi, i: (bi, 0, 0)),
        pl.BlockSpec((None, WINDOW, wk), lambda bi, i: (bi, jnp.clip(half * i - 1, 0, n_half - 1), 0)),
        pl.BlockSpec((None, TQ, wk), lambda bi, i: (bi, i, 0)),
        pl.BlockSpec((None, WINDOW, wk), lambda bi, i: (bi, jnp.clip(half * i + half, 0, n_half - 1), 0)),
        pl.BlockSpec((None, None, wv, ctx_len), lambda bi, i: (bi, 0, 0, 0)),
        pl.BlockSpec((None, None, wv, WINDOW), lambda bi, i: (bi, jnp.maximum(i - 1, 0), 0, half - 1)),
        pl.BlockSpec((None, None, wv, TQ), lambda bi, i: (bi, i, 0, 0)),
        pl.BlockSpec((None, None, wv, WINDOW), lambda bi, i: (bi, jnp.minimum(i + 1, n_blocks - 1), 0, 0)),
        pl.BlockSpec(sink.shape, lambda bi, i: (0, 0)),
    ]
    return pl.pallas_call(
        functools.partial(_attn_d_kernel, n_blocks=n_blocks),
        grid=(b, n_blocks),
        in_specs=in_specs,
        out_specs=pl.BlockSpec((None, TQ, GROUP_WIDTH), lambda bi, i: (bi, i, 0)),
        out_shape=jax.ShapeDtypeStruct((b, nt, GROUP_WIDTH), BF16),
        compiler_params=pltpu.CompilerParams(
            dimension_semantics=("arbitrary", "arbitrary"), vmem_limit_bytes=VMEM_LIMIT),
        name="attn_d",
    )(q, k, k, k, k, vt, vt, vt, vt, sink)


def _route(sel_rows, s_rows):
    epg = EXPERTS_PER_GROUP
    gscore = []
    for g in range(N_EXPERT_GROUPS):
        r = sel_rows[epg * g:epg * (g + 1)]
        pair = None
        for i in range(epg):
            for j in range(i + 1, epg):
                v = r[i] + r[j]
                pair = v if pair is None else jnp.maximum(pair, v)
        gscore.append(pair)
    best, best_g = gscore[0], jnp.zeros_like(gscore[0], dtype=jnp.int32)
    for g in range(1, N_EXPERT_GROUPS):
        better = gscore[g] > best
        best_g = jnp.where(better, g, best_g)
        best = jnp.where(better, gscore[g], best)
    w = []
    for e in range(N_EXPERTS):
        g = e // epg
        cnt = jnp.zeros_like(best_g)
        for e2 in range(epg * g, epg * (g + 1)):
            if e2 == e:
                continue
            beats = sel_rows[e2] > sel_rows[e]
            if e2 < e:
                beats = jnp.logical_or(beats, sel_rows[e2] == sel_rows[e])
            cnt = cnt + jnp.where(beats, 1, 0)
        chosen = jnp.logical_and(best_g == g, cnt < 2)
        w.append(jnp.where(chosen, s_rows[e], 0.0))
    tot = w[0]
    for e in range(1, N_EXPERTS):
        tot = tot + w[e]
    return [we / tot for we in w]


def _outproj_kernel(oa_ref, ob_ref, oc_ref, od_ref, x_ref, mod_ref, wout_ref, g_ref, b_ref,
                    rwt_ref, rb_ref, x1_ref, h2_ref, gates_ref, *, alpha):
    d = D_MODEL
    o = jnp.concatenate([oa_ref[...], ob_ref[...], oc_ref[...], od_ref[...]], axis=1)
    y = jnp.dot(o, wout_ref[...], preferred_element_type=F32)
    u = alpha * x_ref[...] + mod_ref[:, 2 * d:3 * d] * y
    x1 = _layer_norm(u) * g_ref[...] + b_ref[...]
    x1_ref[...] = x1
    h2 = _layer_norm(x1) * (1.0 + mod_ref[:, 4 * d:5 * d]) + mod_ref[:, 3 * d:4 * d]
    h2_ref[...] = h2.astype(BF16)

    h_hi, h_lo = _split_bf16(h2)
    w_hi, w_lo = _split_bf16(rwt_ref[...])
    logits = _nt_dot(w_hi, h_hi) + _nt_dot(w_hi, h_lo) + _nt_dot(w_lo, h_hi)
    s = jax.nn.sigmoid(logits)
    sel = s + rb_ref[...]
    s_rows = [s[e:e + 1] for e in range(N_EXPERTS)]
    sel_rows = [sel[e:e + 1] for e in range(N_EXPERTS)]
    gates = _route(sel_rows, s_rows)
    gt = jnp.concatenate(gates + [jnp.zeros((LANES - N_EXPERTS, TQ), F32)], axis=0)
    gates_ref[...] = gt.T


def _outproj(oa, ob, oc, od, xa, modsel, w_out, ln_g, ln_b, rwt, rb, alpha):
    b, nt, d = xa.shape
    nc = nt // TQ

    def rm(w):
        return pl.BlockSpec((None, TQ, w), lambda bi, i: (bi, i, 0))

    def full(a):
        nd = a.ndim
        return pl.BlockSpec(a.shape, lambda bi, i, _n=nd: (0,) * _n)

    in_specs = [rm(GROUP_WIDTH) for _ in range(4)] + [
        rm(d),
        pl.BlockSpec((None, None, 1, 6 * d), lambda bi, i: (bi, jnp.minimum(i, 1), 0, 0)),
        full(w_out), full(ln_g), full(ln_b), full(rwt), full(rb)]
    return pl.pallas_call(
        functools.partial(_outproj_kernel, alpha=alpha),
        grid=(b, nc),
        in_specs=in_specs,
        out_specs=[rm(d), rm(d), rm(LANES)],
        out_shape=[jax.ShapeDtypeStruct((b, nt, d), F32), jax.ShapeDtypeStruct((b, nt, d), BF16),
                   jax.ShapeDtypeStruct((b, nt, LANES), F32)],
        compiler_params=pltpu.CompilerParams(
            dimension_semantics=("arbitrary", "arbitrary"), vmem_limit_bytes=VMEM_LIMIT),
        name="outproj",
    )(oa, ob, oc, od, xa, modsel, w_out, ln_g, ln_b, rwt, rb)


def _moe_kernel(h_ref, gates_ref, x_ref, mod_ref, wg_ref, wu_ref, wd_ref, g_ref, b_ref, o_ref, *, alpha):
    d = D_MODEL
    h = h_ref[...]
    gates = gates_ref[...]
    y = jnp.zeros((TQ, d), F32)
    for g in range(N_EXPERT_GROUPS):
        acts = []
        for el in range(EXPERTS_PER_GROUP):
            e = EXPERTS_PER_GROUP * g + el
            hg = jnp.dot(h, wg_ref[e], preferred_element_type=F32)
            hu = jnp.dot(h, wu_ref[e], preferred_element_type=F32)
            a = hg * jax.nn.sigmoid(hg) * hu * gates[:, e:e + 1]
            acts.append(a.astype(BF16))
        y = y + jnp.dot(jnp.concatenate(acts, axis=1), wd_ref[g], preferred_element_type=F32)
    u = alpha * x_ref[...] + mod_ref[:, 5 * d:6 * d] * y
    o_ref[...] = _layer_norm(u) * g_ref[...] + b_ref[...]


def _moe(h2, gates, x1, modsel, wg, wu, wd, ln_g, ln_b, alpha, latents_only):
    b, nt, d = x1.shape
    nc = nt // TQ
    resident = pl.Buffered(1)
    if latents_only:
        out_spec = pl.BlockSpec((None, TQ, d), lambda bi, i: (bi, jnp.maximum(i - 1, 0), 0))
        out_rows = nt - TQ
    else:
        out_spec = pl.BlockSpec((None, TQ, d), lambda bi, i: (bi, i, 0))
        out_rows = nt

    def rm(w):
        return pl.BlockSpec((None, TQ, w), lambda bi, i: (bi, i, 0))

    def res(a):
        nd = a.ndim
        return pl.BlockSpec(a.shape, lambda bi, i, _n=nd: (0,) * _n, pipeline_mode=resident)

    in_specs = [rm(d), rm(LANES), rm(d),
                pl.BlockSpec((None, None, 1, 6 * d), lambda bi, i: (bi, jnp.minimum(i, 1), 0, 0)),
                res(wg), res(wu), res(wd), res(ln_g), res(ln_b)]
    return pl.pallas_call(
        functools.partial(_moe_kernel, alpha=alpha),
        grid=(b, nc),
        in_specs=in_specs,
        out_specs=out_spec,
        out_shape=jax.ShapeDtypeStruct((b, out_rows, d), F32),
        compiler_params=pltpu.CompilerParams(
            dimension_semantics=("arbitrary", "arbitrary"), vmem_limit_bytes=VMEM_LIMIT),
        name="moe",
    )(h2, gates, x1, modsel, wg, wu, wd, ln_g, ln_b)


def _rope_perm(rot_dim):
    n = rot_dim // 4
    j = np.arange(rot_dim)
    return j ^ n, np.where((j // n) % 2 == 0, -1.0, 1.0).astype(np.float32)


def _swapped(w, rot_dim):
    perm, sign = _rope_perm(rot_dim)
    cols = w.shape[-1]
    idx = (np.arange(cols) // rot_dim) * rot_dim + perm[np.arange(cols) % rot_dim]
    sgn = sign[np.arange(cols) % rot_dim]
    return w[..., idx] * sgn


def _rope_tables(n_lat, ctx_len):
    t = jnp.arange(n_lat, dtype=jnp.int32)
    row, col = (t // GRID_W).astype(F32), (t % GRID_W).astype(F32)

    def pattern(rot_dim):
        n = rot_dim // 4
        inv = ROPE_THETA ** (-jnp.arange(n, dtype=F32) / n)
        ar, ac = row[:, None] * inv, col[:, None] * inv
        ang = jnp.concatenate([ar, ar, ac, ac], axis=-1)
        return jnp.cos(ang), jnp.sin(ang)

    def with_ctx(a, fill):
        return jnp.concatenate([jnp.full((ctx_len, a.shape[1]), fill, F32), a], axis=0)

    c32, s32 = pattern(A_QK)
    c64, s64 = pattern(HEAD_DIM)
    ones = jnp.ones((n_lat, C_NOPE), F32)
    zeros = jnp.zeros((n_lat, C_NOPE), F32)
    pad1 = jnp.ones((n_lat, LANES - C_NOPE - C_ROPE), F32)
    pad0 = jnp.zeros((n_lat, LANES - C_NOPE - C_ROPE), F32)
    return {
        "cos32": with_ctx(jnp.tile(c32, (1, LANES // A_QK)), 1.0),
        "sin32": with_ctx(jnp.tile(s32, (1, LANES // A_QK)), 0.0),
        "cos64": with_ctx(jnp.tile(c64, (1, LANES // HEAD_DIM)), 1.0),
        "sin64": with_ctx(jnp.tile(s64, (1, LANES // HEAD_DIM)), 0.0),
        "cosc": with_ctx(jnp.concatenate([ones, c32, pad1], axis=-1), 1.0),
        "sinc": with_ctx(jnp.concatenate([zeros, s32, pad0], axis=-1), 0.0),
    }


def _prep_layer_weights(w_in, gq, gk, gcq, gckv, w_uq, w_ukv):
    d = w_in.shape[0]
    splits = np.cumsum([256, 256, 256, 256, 128, 128, C_Q_RANK, C_KV_RANK, C_ROPE, 256, 128, 128])[:-1]
    (a_q, a_k, a_v, b_q, b_k, b_v, c_q, c_kv, c_kr, d_q, d_k, d_v) = jnp.split(w_in, splits, axis=1)
    z = lambda n: jnp.zeros((d, n), F32)
    kr4 = jnp.concatenate([z(C_NOPE), c_kr, z(LANES - C_NOPE - C_ROPE)], axis=1)
    kr4s = jnp.concatenate([z(C_NOPE), _swapped(c_kr, C_ROPE), z(LANES - C_NOPE - C_ROPE)], axis=1)
    wrm = jnp.concatenate([
        a_q, _swapped(a_q, A_QK), a_k, _swapped(a_k, A_QK),
        b_q, _swapped(b_q, HEAD_DIM), b_k, _swapped(b_k, HEAD_DIM),
        d_q, _swapped(d_q, HEAD_DIM), d_k, _swapped(d_k, HEAD_DIM),
        c_q, z(256 - C_Q_RANK), c_kv, kr4, kr4s], axis=1)
    wt = jnp.concatenate([a_v, b_v, d_v], axis=1).T

    uq = w_uq.reshape(C_Q_RANK, C_HEADS, C_NOPE + C_ROPE)
    uq_n, uq_r = uq[..., :C_NOPE], uq[..., C_NOPE:]
    zq = lambda n: jnp.zeros((C_Q_RANK, C_HEADS, n), F32)
    pad = LANES - C_NOPE - C_ROPE
    wuq = jnp.concatenate([uq_n, uq_r, zq(pad)], axis=-1).reshape(C_Q_RANK, C_HEADS * LANES)
    wuqs = jnp.concatenate([zq(C_NOPE), _swapped(uq_r, C_ROPE), zq(pad)], axis=-1).reshape(C_Q_RANK, C_HEADS * LANES)
    zrows = jnp.zeros((256 - C_Q_RANK, C_HEADS * LANES), F32)
    wuq, wuqs = jnp.concatenate([wuq, zrows], axis=0), jnp.concatenate([wuqs, zrows], axis=0)
    ukv = w_ukv.reshape(C_KV_RANK, C_HEADS, C_NOPE + HEAD_DIM)
    uk_n, u_v = ukv[..., :C_NOPE], ukv[..., C_NOPE:]
    wukn = jnp.concatenate([uk_n, jnp.zeros((C_KV_RANK, C_HEADS, LANES - C_NOPE), F32)], axis=-1)
    wukn = wukn.reshape(C_KV_RANK, C_HEADS * LANES)
    wuvt = u_v.reshape(C_KV_RANK, C_HEADS * HEAD_DIM).T

    perm64, _ = _rope_perm(HEAD_DIM)
    return {
        "wrm": wrm.astype(BF16), "wt": wt.astype(BF16),
        "wuq": wuq.astype(BF16), "wuqs": wuqs.astype(BF16),
        "wukn": wukn.astype(BF16), "wuvt": wuvt.astype(BF16),
        "gqb": jnp.tile(gq, B_HEADS)[None, :], "gqbs": jnp.tile(gq[perm64], B_HEADS)[None, :],
        "gkb": jnp.tile(gk, B_KV_HEADS)[None, :], "gkbs": jnp.tile(gk[perm64], B_KV_HEADS)[None, :],
        "gcq": jnp.concatenate([gcq, jnp.zeros((256 - C_Q_RANK,), F32)])[None, :],
        "gckv": gckv[None, :],
    }


def kernel(x, c, ctx, c_ctx, w_ada, b_ada, w_in, w_out, diff_lambda_q1, diff_lambda_k1, diff_lambda_q2,
           diff_lambda_k2, diff_subln_g, gqa_q_norm_g, gqa_k_norm_g, mla_q_norm_g, mla_kv_norm_g, mla_w_uq,
           mla_w_ukv, swa_sink, ln1_g, ln1_b, ln2_g, ln2_b, router_w, router_bias,
           exp_w_gate, exp_w_up, exp_w_down):
    b, n_lat, d = x.shape
    ctx_len = ctx.shape[1]
    depth = w_ada.shape[0]
    assert d == D_MODEL and ctx_len == TQ and n_lat % TQ == 0 and b + 1 <= 8
    alpha = (2 * depth) ** 0.25

    xa = jnp.concatenate([ctx, x], axis=1)
    tabs = _rope_tables(n_lat, ctx_len)

    cc = jnp.concatenate([c, c_ctx[None, :], jnp.zeros((8 - b - 1, d), F32)], axis=0)
    mods = _ada(cc, w_ada, b_ada)
    rwt = router_w.T
    rb = router_bias[:, None]

    for l in range(depth):
        lat = mods[l, :b]
        cx = jnp.broadcast_to(mods[l, b], lat.shape)
        modsel = jnp.stack([cx, lat], axis=1)[:, :, None, :]
        lw = _prep_layer_weights(w_in[l], gqa_q_norm_g[l], gqa_k_norm_g[l], mla_q_norm_g[l],
                                 mla_kv_norm_g[l], mla_w_uq[l], mla_w_ukv[l])
        qa, ka, vat, qb, kb, vbt, qc, kc, vct, qd, kd, vdt = _proj(xa, modsel, lw, tabs)

        lam_init = 0.8 - 0.6 * math.exp(-0.3 * l)
        extras = (diff_lambda_q1[l][None, :], diff_lambda_k1[l][None, :], diff_lambda_q2[l][None, :],
                  diff_lambda_k2[l][None, :], diff_subln_g[l][:, None])
        oa = _attn_full("a", qa, ka, vat, extras, lam_init)
        ob = _attn_full("b", qb, kb, vbt)
        oc = _attn_full("c", qc, kc, vct)
        od = _attn_d(qd, kd, vdt, swa_sink[l][None, :], ctx_len)

        x1, h2, gates = _outproj(oa, ob, oc, od, xa, modsel, w_out[l].astype(BF16),
                                 ln1_g[l][None, :], ln1_b[l][None, :], rwt, rb, alpha)
        wg = exp_w_gate[l].astype(BF16)
        wu = exp_w_up[l].astype(BF16)
        wd = exp_w_down[l].astype(BF16).reshape(N_EXPERT_GROUPS, EXPERTS_PER_GROUP * D_EXPERT, d)
        xa = _moe(h2, gates, x1, modsel, wg, wu, wd, ln2_g[l][None, :], ln2_b[l][None, :], alpha,
                  latents_only=(l == depth - 1))
    return xa
```

```python
import functools
import math

import numpy as np
import jax
import jax.numpy as jnp
from jax import lax
from jax.experimental import pallas as pl
from jax.experimental.pallas import tpu as pltpu

F32 = jnp.float32
BF16 = jnp.bfloat16

D_MODEL = 1024
HEAD_DIM = 64
GROUP_WIDTH = 256
GRID_W = 64
ROPE_THETA = 10000.0
EPS = 1e-6
NEG = -1e30
LOG2E = 1.4426950408889634

A_HEADS = 4
A_QK = 32
B_HEADS = 4
B_KV_HEADS = 2
C_HEADS = 4
C_Q_RANK = 192
C_KV_RANK = 128
C_NOPE = 64
C_ROPE = 32
D_HEADS = 4
D_KV_HEADS = 2
WINDOW = 128
N_EXPERTS = 16
EXPERTS_PER_GROUP = 4
N_EXPERT_GROUPS = 4
D_EXPERT = 256

TQ = 256
TK = 256
TK_LAT = 1024
PIECE_SPAN = 2
BOUND_SLACK = 1.0 + 2.0 ** -7
MIN_DENOMINATOR = 2.0 ** -80
LANES = 128
ONES_ROWS = 16
VMEM_LIMIT = 56 * 1024 * 1024

_C_QA, _C_QAS, _C_KA, _C_KAS = 0, 256, 512, 768
_C_QB, _C_QBS, _C_KB, _C_KBS = 1024, 1280, 1536, 1664
_C_QD, _C_QDS, _C_KD, _C_KDS = 1792, 2048, 2304, 2432
_C_CQ, _C_CKV, _C_KR, _C_KRS = 2560, 2816, 2944, 3072
_C_TOTAL = 3200
_R_VA, _R_VB, _R_VD, _R_TOTAL = 0, 256, 384, 512


def _nt_dot(a, b):
    return lax.dot_general(a, b, (((1,), (1,)), ((), ())), preferred_element_type=F32)


def _split_bf16(a):
    hi = a.astype(BF16)
    lo = (a - hi.astype(F32)).astype(BF16)
    return hi, lo


def _layer_norm(x):
    mu = jnp.mean(x, axis=-1, keepdims=True)
    xc = x - mu
    var = jnp.mean(xc * xc, axis=-1, keepdims=True)
    return xc * lax.rsqrt(var + EPS)


def _tile_lanes(a, n):
    return jnp.concatenate([a] * n, axis=1)


def _ada_kernel(cc_ref, w_ref, b_ref, o_ref):
    cc = cc_ref[...]
    s = cc * jax.nn.sigmoid(cc)
    s_hi, s_lo = _split_bf16(s)
    w_hi, w_lo = _split_bf16(w_ref[...])
    acc = jnp.dot(s_hi, w_hi, preferred_element_type=F32)
    acc += jnp.dot(s_hi, w_lo, preferred_element_type=F32)
    acc += jnp.dot(s_lo, w_hi, preferred_element_type=F32)
    o_ref[...] = acc + b_ref[...]


def _ada(cc, w_ada, b_ada):
    depth, d, n6 = w_ada.shape
    bn = 1536
    return pl.pallas_call(
        _ada_kernel,
        grid=(depth, n6 // bn),
        in_specs=[
            pl.BlockSpec((8, d), lambda l, j: (0, 0)),
            pl.BlockSpec((None, d, bn), lambda l, j: (l, 0, j)),
            pl.BlockSpec((None, 1, bn), lambda l, j: (l, 0, j)),
        ],
        out_specs=pl.BlockSpec((None, 8, bn), lambda l, j: (l, 0, j)),
        out_shape=jax.ShapeDtypeStruct((depth, 8, n6), F32),
        compiler_params=pltpu.CompilerParams(
            dimension_semantics=("arbitrary", "arbitrary"), vmem_limit_bytes=VMEM_LIMIT),
        name="ada",
    )(cc, w_ada, b_ada.reshape(depth, 1, n6))


def _block_diag_ones(n, blk):
    sh = int(math.log2(blk))
    r = lax.broadcasted_iota(jnp.int32, (n, n), 0) >> sh
    c = lax.broadcasted_iota(jnp.int32, (n, n), 1) >> sh
    return jnp.where(r == c, 1.0, 0.0).astype(BF16)


def _proj_kernel(x_ref, mod_ref, wrm_ref, wt_ref, wuq_ref, wuqs_ref, wukn_ref, wuvt_ref,
                 gqb_ref, gqbs_ref, gkb_ref, gkbs_ref, gcq_ref, gckv_ref,
                 cos32_ref, sin32_ref, cos64_ref, sin64_ref, cosc_ref, sinc_ref,
                 qa_ref, ka_ref, vat_ref, qb_ref, kb_ref, vbt_ref,
                 qc_ref, kc_ref, vct_ref, qd_ref, kd_ref, vdt_ref):
    d = D_MODEL
    xn = _layer_norm(x_ref[...])
    h = (xn * (1.0 + mod_ref[:, d:2 * d]) + mod_ref[:, 0:d]).astype(BF16)

    def cols(lo, width):
        return jnp.dot(h, wrm_ref[:, lo:lo + width], preferred_element_type=F32)

    cos32, sin32 = _tile_lanes(cos32_ref[...], 2), _tile_lanes(sin32_ref[...], 2)
    cos64, sin64 = cos64_ref[...], sin64_ref[...]
    cos64w, sin64w = _tile_lanes(cos64, 2), _tile_lanes(sin64, 2)

    sa = (A_QK ** -0.5) * LOG2E
    qa_ref[...] = ((cols(_C_QA, 256) * cos32 + cols(_C_QAS, 256) * sin32) * sa).astype(BF16)
    ka_ref[...] = (cols(_C_KA, 256) * cos32 + cols(_C_KAS, 256) * sin32).astype(BF16)

    sb = (HEAD_DIM ** -0.5) * LOG2E
    q = cols(_C_QB, 256)
    ssq = jnp.dot((q * q).astype(BF16), _block_diag_ones(256, HEAD_DIM), preferred_element_type=F32)
    r = lax.rsqrt(ssq * (1.0 / HEAD_DIM) + EPS)
    qr = (q * gqb_ref[...]) * cos64w + (cols(_C_QBS, 256) * gqbs_ref[...]) * sin64w
    qb_ref[...] = (qr * (r * sb)).astype(BF16)
    k = cols(_C_KB, 128)
    ssq = jnp.dot((k * k).astype(BF16), _block_diag_ones(128, HEAD_DIM), preferred_element_type=F32)
    r = lax.rsqrt(ssq * (1.0 / HEAD_DIM) + EPS)
    kr = (k * gkb_ref[...]) * cos64 + (cols(_C_KBS, 128) * gkbs_ref[...]) * sin64
    kb_ref[...] = (kr * r).astype(BF16)

    qd_ref[...] = ((cols(_C_QD, 256) * cos64w + cols(_C_QDS, 256) * sin64w) * sb).astype(BF16)
    kd_ref[...] = (cols(_C_KD, 128) * cos64 + cols(_C_KDS, 128) * sin64).astype(BF16)

    sc = ((C_NOPE + C_ROPE) ** -0.5) * LOG2E
    cosc, sinc = cosc_ref[...], sinc_ref[...]
    cq = cols(_C_CQ, 256)
    ms = jnp.sum(cq * cq, axis=-1, keepdims=True) * (1.0 / C_Q_RANK)
    cqn = (cq * lax.rsqrt(ms + EPS) * gcq_ref[...]).astype(BF16)
    qn = jnp.dot(cqn, wuq_ref[...], preferred_element_type=F32)
    qs = jnp.dot(cqn, wuqs_ref[...], preferred_element_type=F32)
    qc_ref[...] = ((qn * _tile_lanes(cosc, 4) + qs * _tile_lanes(sinc, 4)) * sc).astype(BF16)
    ckv = cols(_C_CKV, 128)
    ms = jnp.mean(ckv * ckv, axis=-1, keepdims=True)
    ckvn = (ckv * lax.rsqrt(ms + EPS) * gckv_ref[...]).astype(BF16)
    kn = jnp.dot(ckvn, wukn_ref[...], preferred_element_type=F32)
    krr = cols(_C_KR, 128) * cosc + cols(_C_KRS, 128) * sinc
    kc_ref[...] = (kn + _tile_lanes(krr, 4)).astype(BF16)
    vct_ref[...] = _nt_dot(wuvt_ref[...], ckvn).astype(BF16)

    vt = _nt_dot(wt_ref[...], h)
    vat_ref[...] = vt[_R_VA:_R_VA + 256].astype(BF16)
    vbt_ref[...] = vt[_R_VB:_R_VB + 128].astype(BF16)
    vdt_ref[...] = vt[_R_VD:_R_VD + 128].astype(BF16)


def _proj(xa, modsel, lw, tabs):
    b, nt, d = xa.shape
    nc = nt // TQ

    def full(a):
        nd = a.ndim
        return pl.BlockSpec(a.shape, lambda bi, i, _n=nd: (0,) * _n)

    def tab(a):
        return pl.BlockSpec((TQ, LANES), lambda bi, i: (i, 0))

    def rm(w):
        return pl.BlockSpec((None, TQ, w), lambda bi, i: (bi, i, 0))

    def tr(w):
        return pl.BlockSpec((None, None, w, TQ), lambda bi, i: (bi, i, 0, 0))

    weights = [lw["wrm"], lw["wt"], lw["wuq"], lw["wuqs"], lw["wukn"], lw["wuvt"],
               lw["gqb"], lw["gqbs"], lw["gkb"], lw["gkbs"], lw["gcq"], lw["gckv"]]
    tables = [tabs["cos32"], tabs["sin32"], tabs["cos64"], tabs["sin64"], tabs["cosc"], tabs["sinc"]]
    in_specs = ([pl.BlockSpec((None, TQ, d), lambda bi, i: (bi, i, 0)),
                 pl.BlockSpec((None, None, 1, 6 * d), lambda bi, i: (bi, jnp.minimum(i, 1), 0, 0))]
                + [full(w) for w in weights] + [tab(t) for t in tables])
    widths = [("rm", 256), ("rm", 256), ("tr", 256), ("rm", 256), ("rm", 128), ("tr", 128),
              ("rm", 512), ("rm", 512), ("tr", 256), ("rm", 256), ("rm", 128), ("tr", 128)]
    out_specs, out_shape = [], []
    for kind, w in widths:
        if kind == "rm":
            out_specs.append(rm(w))
            out_shape.append(jax.ShapeDtypeStruct((b, nt, w), BF16))
        else:
            out_specs.append(tr(w))
            out_shape.append(jax.ShapeDtypeStruct((b, nc, w, TQ), BF16))
    return pl.pallas_call(
        _proj_kernel,
        grid=(b, nc),
        in_specs=in_specs,
        out_specs=out_specs,
        out_shape=out_shape,
        compiler_params=pltpu.CompilerParams(
            dimension_semantics=("arbitrary", "arbitrary"), vmem_limit_bytes=VMEM_LIMIT),
        name="proj",
    )(xa, modsel, *weights, *tables)


def _flash_loop(qi, nc, k_ref, vt_ref, qz_ref, m_ref, alpha_ref, acc_ref, s0_ref, s1_ref, kabs_ref, k_lanes, pv_cfg,
                slab_rows, bounded_pieces, groups_per_sweep):
    per = TK_LAT // TK
    n_lat = (nc - 1) // per
    assert n_lat % 2 == 0 and n_lat >= 2 and (nc - 1) % bounded_pieces == 0
    same_lanes = all(kl == k_lanes[0] for kl in k_lanes)

    def scores(kc):
        if same_lanes:
            return jnp.dot(kc[:, k_lanes[0][0]:k_lanes[0][1]], qz_ref[...], preferred_element_type=F32)
        return jnp.concatenate(
            [jnp.dot(kc[:, lo:hi], qz_ref[:, TQ * j:TQ * (j + 1)], preferred_element_type=F32)
             for j, (lo, hi) in enumerate(k_lanes)], axis=1)

    def small_chunk(pieces, first):
        if slab_rows >= pieces * TK:
            return (1, pieces, first)
        return (pieces * TK // slab_rows, slab_rows // TK, first)

    ctx_chunk = small_chunk(1, 0)

    def lat_chunk(c):
        return small_chunk(per, 1 + c * per)

    def stats_update(mx):
        m_old = m_ref[...]
        m_new = jnp.maximum(m_old, mx)
        alpha_ref[...] = jnp.exp2(m_old - m_new)
        m_ref[...] = m_new

    def pipelined(cur, cur_ref, nxt, nxt_ref):
        m_cur, alpha = m_ref[...], alpha_ref[...]
        mx, pv = None, [None] * len(pv_cfg)
        n_cur = cur[0] if cur is not None else 0
        n_nxt = nxt[0] if nxt is not None else 0
        for r in range(max(n_cur, n_nxt)):
            if r < n_nxt:
                rows = TK * nxt[1]
                kc = k_ref[pl.ds(pl.multiple_of((nxt[2] + r * nxt[1]) * TK, TK), rows), :]
                s = scores(kc)
                nxt_ref[rows * r:rows * (r + 1), :] = s
                smx = jnp.max(s, axis=0, keepdims=True)
                mx = smx if mx is None else jnp.maximum(mx, smx)
            if r < n_cur:
                rows = TK * cur[1]
                p = jnp.exp2(cur_ref[rows * r:rows * (r + 1), :] - m_cur).astype(BF16)
                vc = jnp.concatenate([vt_ref[cur[2] + r * cur[1] + i] for i in range(cur[1])], axis=1)
                ones = jnp.ones((ONES_ROWS, rows), BF16)
                for g, (vl, vh, j0, nj) in enumerate(pv_cfg):
                    va = jnp.concatenate([vc[vl:vh, :], ones], axis=0)
                    d = jnp.dot(va, p[:, TQ * j0:TQ * (j0 + nj)], preferred_element_type=F32)
                    pv[g] = d if pv[g] is None else pv[g] + d
        if cur is not None:
            for g, (vl, vh, j0, nj) in enumerate(pv_cfg):
                cols = slice(TQ * j0, TQ * (j0 + nj))
                acc_ref[:, cols] = alpha[:, cols] * acc_ref[:, cols] + pv[g]
        if nxt is not None:
            stats_update(mx)

    def online_pass():
        m_ref[...] = jnp.full(m_ref.shape, NEG, F32)
        acc_ref[...] = jnp.zeros(acc_ref.shape, F32)

        @pl.when(qi == 0)
        def _():
            pipelined(None, None, ctx_chunk, s0_ref)
            pipelined(ctx_chunk, s0_ref, None, None)

        @pl.when(qi > 0)
        def _():
            pipelined(None, None, lat_chunk(0), s0_ref)

            def body(i, carry):
                c = 2 * i
                pipelined(lat_chunk(c), s0_ref, lat_chunk(c + 1), s1_ref)
                pipelined(lat_chunk(c + 1), s1_ref, lat_chunk(c + 2), s0_ref)
                return carry

            lax.fori_loop(0, n_lat // 2 - 1, body, 0)
            pipelined(lat_chunk(n_lat - 2), s0_ref, lat_chunk(n_lat - 1), s1_ref)
            pipelined(lat_chunk(n_lat - 1), s1_ref, ctx_chunk, s0_ref)
            pipelined(ctx_chunk, s0_ref, None, None)

    def bounded_pass():
        wk = kabs_ref.shape[1]
        kab = jnp.broadcast_to(kabs_ref[...], (ONES_ROWS, wk)).astype(BF16)
        aq = jnp.abs(qz_ref[...])
        if same_lanes:
            lo, hi = k_lanes[0]
            bound = jnp.dot(kab[:, lo:hi], aq, preferred_element_type=F32)[0:1]
        else:
            bound = jnp.concatenate(
                [jnp.dot(kab[:, lo:hi], aq[:, TQ * j:TQ * (j + 1)], preferred_element_type=F32)[0:1]
                 for j, (lo, hi) in enumerate(k_lanes)], axis=1)
        bound = bound * BOUND_SLACK
        acc_ref[...] = jnp.zeros(acc_ref.shape, F32)

        def pieces(first, n, groups, span):
            pv = {g: None for g in groups}

            def probs(r, g):
                _, _, j0, nj = pv_cfg[g]
                kc = k_ref[pl.ds(pl.multiple_of((first + r * span) * TK, TK), span * TK), :]
                if all(k_lanes[j] == k_lanes[j0] for j in range(j0, j0 + nj)):
                    s = jnp.dot(kc[:, k_lanes[j0][0]:k_lanes[j0][1]], qz_ref[:, TQ * j0:TQ * (j0 + nj)],
                                preferred_element_type=F32)
                else:
                    s = jnp.concatenate(
                        [jnp.dot(kc[:, k_lanes[j][0]:k_lanes[j][1]], qz_ref[:, TQ * j:TQ * (j + 1)],
                                 preferred_element_type=F32) for j in range(j0, j0 + nj)], axis=1)
                return jnp.exp2(s - bound[:, TQ * j0:TQ * (j0 + nj)]).astype(BF16)

            def values(r, g, p):
                vl, vh, _, _ = pv_cfg[g]
                vc = jnp.concatenate([vt_ref[first + r * span + i][vl:vh, :] for i in range(span)], axis=1)
                va = jnp.concatenate([vc, jnp.ones((ONES_ROWS, span * TK), BF16)], axis=0)
                d = jnp.dot(va, p, preferred_element_type=F32)
                pv[g] = d if pv[g] is None else pv[g] + d

            p_prev = {g: probs(0, g) for g in groups}
            for r in range(1, n):
                p_cur = {}
                for g in groups:
                    p_cur[g] = probs(r, g)
                    values(r - 1, g, p_prev[g])
                p_prev = p_cur
            for g in groups:
                values(n - 1, g, p_prev[g])
            for g in groups:
                _, _, j0, nj = pv_cfg[g]
                cols = slice(TQ * j0, TQ * (j0 + nj))
                acc_ref[:, cols] = acc_ref[:, cols] + pv[g]

        n_sets = len(pv_cfg) // groups_per_sweep
        for st in range(n_sets):
            groups = list(range(st * groups_per_sweep, (st + 1) * groups_per_sweep))
            pieces(0, 1, groups, 1)

            @pl.when(qi > 0)
            def _(groups=groups):
                def body(i, carry):
                    pieces(1 + i * bounded_pieces, bounded_pieces // PIECE_SPAN, groups, PIECE_SPAN)
                    return carry

                lax.fori_loop(0, (nc - 1) // bounded_pieces, body, 0)

    @pl.when(qi == 0)
    def _():
        def body(c, mx):
            blk = k_ref[pl.ds(pl.multiple_of(c * TK, TK), TK), :].astype(F32)
            return jnp.maximum(mx, jnp.max(jnp.abs(blk), axis=0, keepdims=True))

        kabs_ref[...] = lax.fori_loop(0, nc, body, jnp.zeros(kabs_ref.shape, F32))

    bounded_pass()
    dv = acc_ref.shape[0] - ONES_ROWS
    healthy = jnp.min(jnp.where(acc_ref[dv:dv + 1, :] >= MIN_DENOMINATOR, 1.0, 0.0))

    @pl.when(healthy < 0.5)
    def _():
        online_pass()


def _normalized(acc_ref, j, dv):
    a = acc_ref[:, TQ * j:TQ * (j + 1)]
    return a[0:dv] / a[dv:dv + 1]


def _attn_a_kernel(q_ref, k_ref, vt_ref, lq1_ref, lk1_ref, lq2_ref, lk2_ref, g_ref, o_ref,
                   qz_ref, m_ref, alpha_ref, acc_ref, s0_ref, s1_ref, kabs_ref, *, nc, lam_init):
    qi = pl.program_id(1)
    qt = q_ref[...].astype(F32).T
    row = lax.broadcasted_iota(jnp.int32, (LANES, TQ), 0)
    per_blk = LANES // A_QK
    k_lanes = []
    for j in range(2 * A_HEADS):
        blk, sub = j // per_blk, j % per_blk
        keep = (row >= A_QK * sub) & (row < A_QK * (sub + 1))
        qz_ref[:, TQ * j:TQ * (j + 1)] = jnp.where(keep, qt[LANES * blk:LANES * (blk + 1)], 0.0).astype(BF16)
        k_lanes.append((LANES * blk, LANES * (blk + 1)))
    pv_cfg = [(HEAD_DIM * hd, HEAD_DIM * (hd + 1), 2 * hd, 2) for hd in range(A_HEADS)]
    _flash_loop(qi, nc, k_ref, vt_ref, qz_ref, m_ref, alpha_ref, acc_ref, s0_ref, s1_ref, kabs_ref, k_lanes, pv_cfg,
                slab_rows=1024, bounded_pieces=64, groups_per_sweep=2)

    lam = (jnp.exp(jnp.sum(lq1_ref[...] * lk1_ref[...], axis=-1, keepdims=True))
           - jnp.exp(jnp.sum(lq2_ref[...] * lk2_ref[...], axis=-1, keepdims=True)) + lam_init)
    outs = []
    for hd in range(A_HEADS):
        o = _normalized(acc_ref, 2 * hd, HEAD_DIM) - lam * _normalized(acc_ref, 2 * hd + 1, HEAD_DIM)
        ms = jnp.mean(o * o, axis=0, keepdims=True)
        outs.append(o * lax.rsqrt(ms + EPS) * g_ref[...] * (1.0 - lam_init))
    o_ref[...] = jnp.concatenate(outs, axis=0).T.astype(BF16)


def _attn_b_kernel(q_ref, k_ref, vt_ref, o_ref, qz_ref, m_ref, alpha_ref, acc_ref, s0_ref, s1_ref, kabs_ref, *, nc):
    qi = pl.program_id(1)
    qt = q_ref[...].astype(F32).T
    zeros = jnp.zeros((HEAD_DIM, TQ), F32)
    rep = B_HEADS // B_KV_HEADS
    for hd in range(B_HEADS):
        parts = [zeros] * B_KV_HEADS
        parts[hd // rep] = qt[HEAD_DIM * hd:HEAD_DIM * (hd + 1)]
        qz_ref[:, TQ * hd:TQ * (hd + 1)] = jnp.concatenate(parts, axis=0).astype(BF16)
    k_lanes = [(0, 128)] * B_HEADS
    pv_cfg = [(HEAD_DIM * g, HEAD_DIM * (g + 1), rep * g, rep) for g in range(B_KV_HEADS)]
    _flash_loop(qi, nc, k_ref, vt_ref, qz_ref, m_ref, alpha_ref, acc_ref, s0_ref, s1_ref, kabs_ref, k_lanes, pv_cfg,
                slab_rows=256, bounded_pieces=64, groups_per_sweep=B_KV_HEADS)
    outs = [_normalized(acc_ref, hd, HEAD_DIM) for hd in range(B_HEADS)]
    o_ref[...] = jnp.concatenate(outs, axis=0).T.astype(BF16)


def _attn_c_kernel(q_ref, k_ref, vt_ref, o_ref, qz_ref, m_ref, alpha_ref, acc_ref, s0_ref, s1_ref, kabs_ref, *, nc):
    qi = pl.program_id(1)
    qt = q_ref[...].astype(F32).T
    k_lanes, pv_cfg = [], []
    for hd in range(C_HEADS):
        qz_ref[:, TQ * hd:TQ * (hd + 1)] = qt[LANES * hd:LANES * (hd + 1)].astype(BF16)
        k_lanes.append((LANES * hd, LANES * (hd + 1)))
        pv_cfg.append((HEAD_DIM * hd, HEAD_DIM * (hd + 1), hd, 1))
    _flash_loop(qi, nc, k_ref, vt_ref, qz_ref, m_ref, alpha_ref, acc_ref, s0_ref, s1_ref, kabs_ref, k_lanes, pv_cfg,
                slab_rows=256, bounded_pieces=64, groups_per_sweep=C_HEADS)
    outs = [_normalized(acc_ref, hd, HEAD_DIM) for hd in range(C_HEADS)]
    o_ref[...] = jnp.concatenate(outs, axis=0).T.astype(BF16)


def _attn_full(kind, q, k, vt, extras=(), lam_init=0.0):
    b, nt, wq = q.shape
    wk = k.shape[-1]
    nc, wv = vt.shape[1], vt.shape[2]
    assert (nt - TK) % TK_LAT == 0
    if kind == "a":
        body, n_sm, dk = functools.partial(_attn_a_kernel, nc=nc, lam_init=lam_init), 2 * A_HEADS, 128
    elif kind == "b":
        body, n_sm, dk = functools.partial(_attn_b_kernel, nc=nc), B_HEADS, 128
    else:
        body, n_sm, dk = functools.partial(_attn_c_kernel, nc=nc), C_HEADS, 128
    resident = pl.Buffered(1)
    in_specs = [
        pl.BlockSpec((None, TQ, wq), lambda bi, i: (bi, i, 0)),
        pl.BlockSpec((None, nt, wk), lambda bi, i: (bi, 0, 0), pipeline_mode=resident),
        pl.BlockSpec((None, nc, wv, TK), lambda bi, i: (bi, 0, 0, 0), pipeline_mode=resident),
    ] + [pl.BlockSpec(e.shape, lambda bi, i: (0, 0)) for e in extras]
    return pl.pallas_call(
        body,
        grid=(b, nc),
        in_specs=in_specs,
        out_specs=pl.BlockSpec((None, TQ, GROUP_WIDTH), lambda bi, i: (bi, i, 0)),
        out_shape=jax.ShapeDtypeStruct((b, nt, GROUP_WIDTH), BF16),
        scratch_shapes=[
            pltpu.VMEM((dk, n_sm * TQ), BF16),
            pltpu.VMEM((1, n_sm * TQ), F32),
            pltpu.VMEM((1, n_sm * TQ), F32),
            pltpu.VMEM((HEAD_DIM + ONES_ROWS, n_sm * TQ), F32),
            pltpu.VMEM((TK_LAT, n_sm * TQ), F32),
            pltpu.VMEM((TK_LAT, n_sm * TQ), F32),
            pltpu.VMEM((1, wk), F32),
        ],
        compiler_params=pltpu.CompilerParams(
            dimension_semantics=("arbitrary", "arbitrary"), vmem_limit_bytes=VMEM_LIMIT),
        name="attn_" + kind,
    )(q, k, vt, *extras)


def _attn_d_kernel(q_ref, kc_ref, kp_ref, ko_ref, kn_ref, vc_ref, vp_ref, vo_ref, vn_ref, sink_ref,
                   o_ref, *, n_blocks):
    n = pl.program_id(1)
    t, wn = TQ, WINDOW
    rep = D_HEADS // D_KV_HEADS
    qt = q_ref[...].astype(F32).T
    zeros = jnp.zeros((HEAD_DIM, t), F32)
    cols = []
    for hd in range(D_HEADS):
        parts = [zeros] * D_KV_HEADS
        parts[hd // rep] = qt[HEAD_DIM * hd:HEAD_DIM * (hd + 1)]
        cols.append(jnp.concatenate(parts, axis=0))
    qz = jnp.concatenate(cols, axis=1).astype(BF16)
    w = D_HEADS * t

    def offsets(rows):
        ko = lax.broadcasted_iota(jnp.int32, (rows, w), 0)
        qo = lax.broadcasted_iota(jnp.int32, (rows, w), 1) & (t - 1)
        return ko, qo

    band = n >= 1
    ko, qo = offsets(wn)
    ok_prev = jnp.logical_and(n >= 2, qo <= ko)
    ok_next = jnp.logical_and(jnp.logical_and(band, n + 1 <= n_blocks - 1), qo >= ko + (t - wn))
    ko, qo = offsets(t)
    ok_own = jnp.logical_and(band, jnp.abs(qo - ko) <= wn)
    s_c = jnp.dot(kc_ref[...], qz, preferred_element_type=F32)
    s_p = jnp.where(ok_prev, jnp.dot(kp_ref[...], qz, preferred_element_type=F32), NEG)
    s_o = jnp.where(ok_own, jnp.dot(ko_ref[...], qz, preferred_element_type=F32), NEG)
    s_n = jnp.where(ok_next, jnp.dot(kn_ref[...], qz, preferred_element_type=F32), NEG)
    sk = jnp.concatenate([jnp.broadcast_to(sink_ref[0:1, hd:hd + 1] * LOG2E, (1, t))
                          for hd in range(D_HEADS)], axis=1)
    m = jnp.maximum(jnp.max(s_c, axis=0, keepdims=True), jnp.max(s_p, axis=0, keepdims=True))
    m = jnp.maximum(m, jnp.max(s_o, axis=0, keepdims=True))
    m = jnp.maximum(m, jnp.max(s_n, axis=0, keepdims=True))
    m = jnp.maximum(m, sk)
    pieces = [(vc_ref[...], jnp.exp2(s_c - m).astype(BF16)), (vp_ref[...], jnp.exp2(s_p - m).astype(BF16)),
              (vo_ref[...], jnp.exp2(s_o - m).astype(BF16)), (vn_ref[...], jnp.exp2(s_n - m).astype(BF16))]
    l_sink = jnp.exp2(sk - m)
    outs = []
    for g in range(D_KV_HEADS):
        lanes = slice(rep * t * g, rep * t * (g + 1))
        o = None
        for v, p in pieces:
            va = jnp.concatenate([v[HEAD_DIM * g:HEAD_DIM * (g + 1)],
                                  jnp.ones((ONES_ROWS, v.shape[1]), BF16)], axis=0)
            d = jnp.dot(va, p[:, lanes], preferred_element_type=F32)
            o = d if o is None else o + d
        o = o[0:HEAD_DIM] / (o[HEAD_DIM:HEAD_DIM + 1] + l_sink[:, lanes])
        outs += [o[:, t * r:t * (r + 1)] for r in range(rep)]
    o_ref[...] = jnp.concatenate(outs, axis=0).T.astype(BF16)


def _attn_d(q, k, vt, sink, ctx_len):
    b, nt, wq = q.shape
    wk = k.shape[-1]
    n_blocks, wv = vt.shape[1], vt.shape[2]
    half = TQ // WINDOW
    n_half = nt // WINDOW
    assert ctx_len == TQ and TQ % WINDOW == 0 and half == 2

    in_specs = [
        pl.BlockSpec((None, TQ, wq), lambda bi, i: (bi, i, 0)),
        pl.BlockSpec((None, ctx_len, wk), lambda bi, i: (bi, 0, 0)),
        pl.BlockSpec((None, WINDOW, wk), lambda bi, i: (bi, jnp.clip(half * i - 1, 0, n_half - 1), 0)),
        pl.BlockSpec((None, TQ, wk), lambda bi, i: (bi, i, 0)),
        pl.BlockSpec((None, WINDOW, wk), lambda bi, i: (bi, jnp.clip(half * i + half, 0, n_half - 1), 0)),
        pl.BlockSpec((None, None, wv, ctx_len), lambda bi, i: (bi, 0, 0, 0)),
        pl.BlockSpec((None, None, wv, WINDOW), lambda bi, i: (bi, jnp.maximum(i - 1, 0), 0, half - 1)),
        pl.BlockSpec((None, None, wv, TQ), lambda bi, i: (bi, i, 0, 0)),
        pl.BlockSpec((None, None, wv, WINDOW), lambda bi, i: (bi, jnp.minimum(i + 1, n_blocks - 1), 0, 0)),
        pl.BlockSpec(sink.shape, lambda bi, i: (0, 0)),
    ]
    return pl.pallas_call(
        functools.partial(_attn_d_kernel, n_blocks=n_blocks),
        grid=(b, n_blocks),
        in_specs=in_specs,
        out_specs=pl.BlockSpec((None, TQ, GROUP_WIDTH), lambda bi, i: (bi, i, 0)),
        out_shape=jax.ShapeDtypeStruct((b, nt, GROUP_WIDTH), BF16),
        compiler_params=pltpu.CompilerParams(
            dimension_semantics=("arbitrary", "arbitrary"), vmem_limit_bytes=VMEM_LIMIT),
        name="attn_d",
    )(q, k, k, k, k, vt, vt, vt, vt, sink)


def _route(sel_rows, s_rows):
    epg = EXPERTS_PER_GROUP
    gscore = []
    for g in range(N_EXPERT_GROUPS):
        r = sel_rows[epg * g:epg * (g + 1)]
        pair = None
        for i in range(epg):
            for j in range(i + 1, epg):
                v = r[i] + r[j]
                pair = v if pair is None else jnp.maximum(pair, v)
        gscore.append(pair)
    best, best_g = gscore[0], jnp.zeros_like(gscore[0], dtype=jnp.int32)
    for g in range(1, N_EXPERT_GROUPS):
        better = gscore[g] > best
        best_g = jnp.where(better, g, best_g)
        best = jnp.where(better, gscore[g], best)
    w = []
    for e in range(N_EXPERTS):
        g = e // epg
        cnt = jnp.zeros_like(best_g)
        for e2 in range(epg * g, epg * (g + 1)):
            if e2 == e:
                continue
            beats = sel_rows[e2] > sel_rows[e]
            if e2 < e:
                beats = jnp.logical_or(beats, sel_rows[e2] == sel_rows[e])
            cnt = cnt + jnp.where(beats, 1, 0)
        chosen = jnp.logical_and(best_g == g, cnt < 2)
        w.append(jnp.where(chosen, s_rows[e], 0.0))
    tot = w[0]
    for e in range(1, N_EXPERTS):
        tot = tot + w[e]
    return [we / tot for we in w]


def _outproj_kernel(oa_ref, ob_ref, oc_ref, od_ref, x_ref, mod_ref, wout_ref, g_ref, b_ref,
                    rwt_ref, rb_ref, x1_ref, h2_ref, gates_ref, *, alpha):
    d = D_MODEL
    o = jnp.concatenate([oa_ref[...], ob_ref[...], oc_ref[...], od_ref[...]], axis=1)
    y = jnp.dot(o, wout_ref[...], preferred_element_type=F32)
    u = alpha * x_ref[...] + mod_ref[:, 2 * d:3 * d] * y
    x1 = _layer_norm(u) * g_ref[...] + b_ref[...]
    x1_ref[...] = x1
    h2 = _layer_norm(x1) * (1.0 + mod_ref[:, 4 * d:5 * d]) + mod_ref[:, 3 * d:4 * d]
    h2_ref[...] = h2.astype(BF16)

    h_hi, h_lo = _split_bf16(h2)
    w_hi, w_lo = _split_bf16(rwt_ref[...])
    logits = _nt_dot(w_hi, h_hi) + _nt_dot(w_hi, h_lo) + _nt_dot(w_lo, h_hi)
    s = jax.nn.sigmoid(logits)
    sel = s + rb_ref[...]
    s_rows = [s[e:e + 1] for e in range(N_EXPERTS)]
    sel_rows = [sel[e:e + 1] for e in range(N_EXPERTS)]
    gates = _route(sel_rows, s_rows)
    gt = jnp.concatenate(gates + [jnp.zeros((LANES - N_EXPERTS, TQ), F32)], axis=0)
    gates_ref[...] = gt.T


def _outproj(oa, ob, oc, od, xa, modsel, w_out, ln_g, ln_b, rwt, rb, alpha):
    b, nt, d = xa.shape
    nc = nt // TQ

    def rm(w):
        return pl.BlockSpec((None, TQ, w), lambda bi, i: (bi, i, 0))

    def full(a):
        nd = a.ndim
        return pl.BlockSpec(a.shape, lambda bi, i, _n=nd: (0,) * _n)

    in_specs = [rm(GROUP_WIDTH) for _ in range(4)] + [
        rm(d),
        pl.BlockSpec((None, None, 1, 6 * d), lambda bi, i: (bi, jnp.minimum(i, 1), 0, 0)),
        full(w_out), full(ln_g), full(ln_b), full(rwt), full(rb)]
    return pl.pallas_call(
        functools.partial(_outproj_kernel, alpha=alpha),
        grid=(b, nc),
        in_specs=in_specs,
        out_specs=[rm(d), rm(d), rm(LANES)],
        out_shape=[jax.ShapeDtypeStruct((b, nt, d), F32), jax.ShapeDtypeStruct((b, nt, d), BF16),
                   jax.ShapeDtypeStruct((b, nt, LANES), F32)],
        compiler_params=pltpu.CompilerParams(
            dimension_semantics=("arbitrary", "arbitrary"), vmem_limit_bytes=VMEM_LIMIT),
        name="outproj",
    )(oa, ob, oc, od, xa, modsel, w_out, ln_g, ln_b, rwt, rb)


def _moe_kernel(h_ref, gates_ref, x_ref, mod_ref, wg_ref, wu_ref, wd_ref, g_ref, b_ref, o_ref, *, alpha):
    d = D_MODEL
    h = h_ref[...]
    gates = gates_ref[...]
    y = jnp.zeros((TQ, d), F32)
    for g in range(N_EXPERT_GROUPS):
        acts = []
        for el in range(EXPERTS_PER_GROUP):
            e = EXPERTS_PER_GROUP * g + el
            hg = jnp.dot(h, wg_ref[e], preferred_element_type=F32)
            hu = jnp.dot(h, wu_ref[e], preferred_element_type=F32)
            a = hg * jax.nn.sigmoid(hg) * hu * gates[:, e:e + 1]
            acts.append(a.astype(BF16))
        y = y + jnp.dot(jnp.concatenate(acts, axis=1), wd_ref[g], preferred_element_type=F32)
    u = alpha * x_ref[...] + mod_ref[:, 5 * d:6 * d] * y
    o_ref[...] = _layer_norm(u) * g_ref[...] + b_ref[...]


def _moe(h2, gates, x1, modsel, wg, wu, wd, ln_g, ln_b, alpha, latents_only):
    b, nt, d = x1.shape
    nc = nt // TQ
    resident = pl.Buffered(1)
    if latents_only:
        out_spec = pl.BlockSpec((None, TQ, d), lambda bi, i: (bi, jnp.maximum(i - 1, 0), 0))
        out_rows = nt - TQ
    else:
        out_spec = pl.BlockSpec((None, TQ, d), lambda bi, i: (bi, i, 0))
        out_rows = nt

    def rm(w):
        return pl.BlockSpec((None, TQ, w), lambda bi, i: (bi, i, 0))

    def res(a):
        nd = a.ndim
        return pl.BlockSpec(a.shape, lambda bi, i, _n=nd: (0,) * _n, pipeline_mode=resident)

    in_specs = [rm(d), rm(LANES), rm(d),
                pl.BlockSpec((None, None, 1, 6 * d), lambda bi, i: (bi, jnp.minimum(i, 1), 0, 0)),
                res(wg), res(wu), res(wd), res(ln_g), res(ln_b)]
    return pl.pallas_call(
        functools.partial(_moe_kernel, alpha=alpha),
        grid=(b, nc),
        in_specs=in_specs,
        out_specs=out_spec,
        out_shape=jax.ShapeDtypeStruct((b, out_rows, d), F32),
        compiler_params=pltpu.CompilerParams(
            dimension_semantics=("arbitrary", "arbitrary"), vmem_limit_bytes=VMEM_LIMIT),
        name="moe",
    )(h2, gates, x1, modsel, wg, wu, wd, ln_g, ln_b)


def _rope_perm(rot_dim):
    n = rot_dim // 4
    j = np.arange(rot_dim)
    return j ^ n, np.where((j // n) % 2 == 0, -1.0, 1.0).astype(np.float32)


def _swapped(w, rot_dim):
    perm, sign = _rope_perm(rot_dim)
    cols = w.shape[-1]
    idx = (np.arange(cols) // rot_dim) * rot_dim + perm[np.arange(cols) % rot_dim]
    sgn = sign[np.arange(cols) % rot_dim]
    return w[..., idx] * sgn


def _rope_tables(n_lat, ctx_len):
    t = jnp.arange(n_lat, dtype=jnp.int32)
    row, col = (t // GRID_W).astype(F32), (t % GRID_W).astype(F32)

    def pattern(rot_dim):
        n = rot_dim // 4
        inv = ROPE_THETA ** (-jnp.arange(n, dtype=F32) / n)
        ar, ac = row[:, None] * inv, col[:, None] * inv
        ang = jnp.concatenate([ar, ar, ac, ac], axis=-1)
        return jnp.cos(ang), jnp.sin(ang)

    def with_ctx(a, fill):
        return jnp.concatenate([jnp.full((ctx_len, a.shape[1]), fill, F32), a], axis=0)

    c32, s32 = pattern(A_QK)
    c64, s64 = pattern(HEAD_DIM)
    ones = jnp.ones((n_lat, C_NOPE), F32)
    zeros = jnp.zeros((n_lat, C_NOPE), F32)
    pad1 = jnp.ones((n_lat, LANES - C_NOPE - C_ROPE), F32)
    pad0 = jnp.zeros((n_lat, LANES - C_NOPE - C_ROPE), F32)
    return {
        "cos32": with_ctx(jnp.tile(c32, (1, LANES // A_QK)), 1.0),
        "sin32": with_ctx(jnp.tile(s32, (1, LANES // A_QK)), 0.0),
        "cos64": with_ctx(jnp.tile(c64, (1, LANES // HEAD_DIM)), 1.0),
        "sin64": with_ctx(jnp.tile(s64, (1, LANES // HEAD_DIM)), 0.0),
        "cosc": with_ctx(jnp.concatenate([ones, c32, pad1], axis=-1), 1.0),
        "sinc": with_ctx(jnp.concatenate([zeros, s32, pad0], axis=-1), 0.0),
    }


def _prep_layer_weights(w_in, gq, gk, gcq, gckv, w_uq, w_ukv):
    d = w_in.shape[0]
    splits = np.cumsum([256, 256, 256, 256, 128, 128, C_Q_RANK, C_KV_RANK, C_ROPE, 256, 128, 128])[:-1]
    (a_q, a_k, a_v, b_q, b_k, b_v, c_q, c_kv, c_kr, d_q, d_k, d_v) = jnp.split(w_in, splits, axis=1)
    z = lambda n: jnp.zeros((d, n), F32)
    kr4 = jnp.concatenate([z(C_NOPE), c_kr, z(LANES - C_NOPE - C_ROPE)], axis=1)
    kr4s = jnp.concatenate([z(C_NOPE), _swapped(c_kr, C_ROPE), z(LANES - C_NOPE - C_ROPE)], axis=1)
    wrm = jnp.concatenate([
        a_q, _swapped(a_q, A_QK), a_k, _swapped(a_k, A_QK),
        b_q, _swapped(b_q, HEAD_DIM), b_k, _swapped(b_k, HEAD_DIM),
        d_q, _swapped(d_q, HEAD_DIM), d_k, _swapped(d_k, HEAD_DIM),
        c_q, z(256 - C_Q_RANK), c_kv, kr4, kr4s], axis=1)
    wt = jnp.concatenate([a_v, b_v, d_v], axis=1).T

    uq = w_uq.reshape(C_Q_RANK, C_HEADS, C_NOPE + C_ROPE)
    uq_n, uq_r = uq[..., :C_NOPE], uq[..., C_NOPE:]
    zq = lambda n: jnp.zeros((C_Q_RANK, C_HEADS, n), F32)
    pad = LANES - C_NOPE - C_ROPE
    wuq = jnp.concatenate([uq_n, uq_r, zq(pad)], axis=-1).reshape(C_Q_RANK, C_HEADS * LANES)
    wuqs = jnp.concatenate([zq(C_NOPE), _swapped(uq_r, C_ROPE), zq(pad)], axis=-1).reshape(C_Q_RANK, C_HEADS * LANES)
    zrows = jnp.zeros((256 - C_Q_RANK, C_HEADS * LANES), F32)
    wuq, wuqs = jnp.concatenate([wuq, zrows], axis=0), jnp.concatenate([wuqs, zrows], axis=0)
    ukv = w_ukv.reshape(C_KV_RANK, C_HEADS, C_NOPE + HEAD_DIM)
    uk_n, u_v = ukv[..., :C_NOPE], ukv[..., C_NOPE:]
    wukn = jnp.concatenate([uk_n, jnp.zeros((C_KV_RANK, C_HEADS, LANES - C_NOPE), F32)], axis=-1)
    wukn = wukn.reshape(C_KV_RANK, C_HEADS * LANES)
    wuvt = u_v.reshape(C_KV_RANK, C_HEADS * HEAD_DIM).T

    perm64, _ = _rope_perm(HEAD_DIM)
    return {
        "wrm": wrm.astype(BF16), "wt": wt.astype(BF16),
        "wuq": wuq.astype(BF16), "wuqs": wuqs.astype(BF16),
        "wukn": wukn.astype(BF16), "wuvt": wuvt.astype(BF16),
        "gqb": jnp.tile(gq, B_HEADS)[None, :], "gqbs": jnp.tile(gq[perm64], B_HEADS)[None, :],
        "gkb": jnp.tile(gk, B_KV_HEADS)[None, :], "gkbs": jnp.tile(gk[perm64], B_KV_HEADS)[None, :],
        "gcq": jnp.concatenate([gcq, jnp.zeros((256 - C_Q_RANK,), F32)])[None, :],
        "gckv": gckv[None, :],
    }


def kernel(x, c, ctx, c_ctx, w_ada, b_ada, w_in, w_out, diff_lambda_q1, diff_lambda_k1, diff_lambda_q2,
           diff_lambda_k2, diff_subln_g, gqa_q_norm_g, gqa_k_norm_g, mla_q_norm_g, mla_kv_norm_g, mla_w_uq,
           mla_w_ukv, swa_sink, ln1_g, ln1_b, ln2_g, ln2_b, router_w, router_bias,
           exp_w_gate, exp_w_up, exp_w_down):
    b, n_lat, d = x.shape
    ctx_len = ctx.shape[1]
    depth = w_ada.shape[0]
    assert d == D_MODEL and ctx_len == TQ and n_lat % TQ == 0 and b + 1 <= 8
    alpha = (2 * depth) ** 0.25

    xa = jnp.concatenate([ctx, x], axis=1)
    tabs = _rope_tables(n_lat, ctx_len)

    cc = jnp.concatenate([c, c_ctx[None, :], jnp.zeros((8 - b - 1, d), F32)], axis=0)
    mods = _ada(cc, w_ada, b_ada)
    rwt = router_w.T
    rb = router_bias[:, None]

    for l in range(depth):
        lat = mods[l, :b]
        cx = jnp.broadcast_to(mods[l, b], lat.shape)
        modsel = jnp.stack([cx, lat], axis=1)[:, :, None, :]
        lw = _prep_layer_weights(w_in[l], gqa_q_norm_g[l], gqa_k_norm_g[l], mla_q_norm_g[l],
                                 mla_kv_norm_g[l], mla_w_uq[l], mla_w_ukv[l])
        qa, ka, vat, qb, kb, vbt, qc, kc, vct, qd, kd, vdt = _proj(xa, modsel, lw, tabs)

        lam_init = 0.8 - 0.6 * math.exp(-0.3 * l)
        extras = (diff_lambda_q1[l][None, :], diff_lambda_k1[l][None, :], diff_lambda_q2[l][None, :],
                  diff_lambda_k2[l][None, :], diff_subln_g[l][:, None])
        oa = _attn_full("a", qa, ka, vat, extras, lam_init)
        ob = _attn_full("b", qb, kb, vbt)
        oc = _attn_full("c", qc, kc, vct)
        od = _attn_d(qd, kd, vdt, swa_sink[l][None, :], ctx_len)

        x1, h2, gates = _outproj(oa, ob, oc, od, xa, modsel, w_out[l].astype(BF16),
                                 ln1_g[l][None, :], ln1_b[l][None, :], rwt, rb, alpha)
        wg = exp_w_gate[l].astype(BF16)
        wu = exp_w_up[l].astype(BF16)
        wd = exp_w_down[l].astype(BF16).reshape(N_EXPERT_GROUPS, EXPERTS_PER_GROUP * D_EXPERT, d)
        xa = _moe(h2, gates, x1, modsel, wg, wu, wd, ln2_g[l][None, :], ln2_b[l][None, :], alpha,
                  latents_only=(l == depth - 1))
    return xa
```

```python
import functools
import math

import numpy as np
import jax
import jax.numpy as jnp
from jax import lax
from jax.experimental import pallas as pl
from jax.experimental.pallas import tpu as pltpu

F32 = jnp.float32
BF16 = jnp.bfloat16

D_MODEL = 1024
HEAD_DIM = 64
GROUP_WIDTH = 256
GRID_W = 64
ROPE_THETA = 10000.0
EPS = 1e-6
NEG = -1e30
LOG2E = 1.4426950408889634

A_HEADS = 4
A_QK = 32
B_HEADS = 4
B_KV_HEADS = 2
C_HEADS = 4
C_Q_RANK = 192
C_KV_RANK = 128
C_NOPE = 64
C_ROPE = 32
D_HEADS = 4
D_KV_HEADS = 2
WINDOW = 128
N_EXPERTS = 16
EXPERTS_PER_GROUP = 4
N_EXPERT_GROUPS = 4
D_EXPERT = 256

TQ = 256
TK = 256
TK_LAT = 1024
BOUND_SLACK = 1.0 + 2.0 ** -7
MIN_DENOMINATOR = 2.0 ** -80
LANES = 128
ONES_ROWS = 16
VMEM_LIMIT = 56 * 1024 * 1024

_C_QA, _C_QAS, _C_KA, _C_KAS = 0, 256, 512, 768
_C_QB, _C_QBS, _C_KB, _C_KBS = 1024, 1280, 1536, 1664
_C_QD, _C_QDS, _C_KD, _C_KDS = 1792, 2048, 2304, 2432
_C_CQ, _C_CKV, _C_KR, _C_KRS = 2560, 2816, 2944, 3072
_C_TOTAL = 3200
_R_VA, _R_VB, _R_VD, _R_TOTAL = 0, 256, 384, 512


def _nt_dot(a, b):
    return lax.dot_general(a, b, (((1,), (1,)), ((), ())), preferred_element_type=F32)


def _split_bf16(a):
    hi = a.astype(BF16)
    lo = (a - hi.astype(F32)).astype(BF16)
    return hi, lo


def _layer_norm(x):
    mu = jnp.mean(x, axis=-1, keepdims=True)
    xc = x - mu
    var = jnp.mean(xc * xc, axis=-1, keepdims=True)
    return xc * lax.rsqrt(var + EPS)


def _tile_lanes(a, n):
    return jnp.concatenate([a] * n, axis=1)


def _ada_kernel(cc_ref, w_ref, b_ref, o_ref):
    cc = cc_ref[...]
    s = cc * jax.nn.sigmoid(cc)
    s_hi, s_lo = _split_bf16(s)
    w_hi, w_lo = _split_bf16(w_ref[...])
    acc = jnp.dot(s_hi, w_hi, preferred_element_type=F32)
    acc += jnp.dot(s_hi, w_lo, preferred_element_type=F32)
    acc += jnp.dot(s_lo, w_hi, preferred_element_type=F32)
    o_ref[...] = acc + b_ref[...]


def _ada(cc, w_ada, b_ada):
    depth, d, n6 = w_ada.shape
    bn = 1536
    return pl.pallas_call(
        _ada_kernel,
        grid=(depth, n6 // bn),
        in_specs=[
            pl.BlockSpec((8, d), lambda l, j: (0, 0)),
            pl.BlockSpec((None, d, bn), lambda l, j: (l, 0, j)),
            pl.BlockSpec((None, 1, bn), lambda l, j: (l, 0, j)),
        ],
        out_specs=pl.BlockSpec((None, 8, bn), lambda l, j: (l, 0, j)),
        out_shape=jax.ShapeDtypeStruct((depth, 8, n6), F32),
        compiler_params=pltpu.CompilerParams(
            dimension_semantics=("arbitrary", "arbitrary"), vmem_limit_bytes=VMEM_LIMIT),
        name="ada",
    )(cc, w_ada, b_ada.reshape(depth, 1, n6))


def _block_diag_ones(n, blk):
    sh = int(math.log2(blk))
    r = lax.broadcasted_iota(jnp.int32, (n, n), 0) >> sh
    c = lax.broadcasted_iota(jnp.int32, (n, n), 1) >> sh
    return jnp.where(r == c, 1.0, 0.0).astype(BF16)


def _proj_kernel(x_ref, mod_ref, wrm_ref, wt_ref, wuq_ref, wuqs_ref, wukn_ref, wuvt_ref,
                 gqb_ref, gqbs_ref, gkb_ref, gkbs_ref, gcq_ref, gckv_ref,
                 cos32_ref, sin32_ref, cos64_ref, sin64_ref, cosc_ref, sinc_ref,
                 qa_ref, ka_ref, vat_ref, qb_ref, kb_ref, vbt_ref,
                 qc_ref, kc_ref, vct_ref, qd_ref, kd_ref, vdt_ref):
    d = D_MODEL
    xn = _layer_norm(x_ref[...])
    h = (xn * (1.0 + mod_ref[:, d:2 * d]) + mod_ref[:, 0:d]).astype(BF16)

    def cols(lo, width):
        return jnp.dot(h, wrm_ref[:, lo:lo + width], preferred_element_type=F32)

    cos32, sin32 = _tile_lanes(cos32_ref[...], 2), _tile_lanes(sin32_ref[...], 2)
    cos64, sin64 = cos64_ref[...], sin64_ref[...]
    cos64w, sin64w = _tile_lanes(cos64, 2), _tile_lanes(sin64, 2)

    sa = (A_QK ** -0.5) * LOG2E
    qa_ref[...] = ((cols(_C_QA, 256) * cos32 + cols(_C_QAS, 256) * sin32) * sa).astype(BF16)
    ka_ref[...] = (cols(_C_KA, 256) * cos32 + cols(_C_KAS, 256) * sin32).astype(BF16)

    sb = (HEAD_DIM ** -0.5) * LOG2E
    q = cols(_C_QB, 256)
    ssq = jnp.dot((q * q).astype(BF16), _block_diag_ones(256, HEAD_DIM), preferred_element_type=F32)
    r = lax.rsqrt(ssq * (1.0 / HEAD_DIM) + EPS)
    qr = (q * gqb_ref[...]) * cos64w + (cols(_C_QBS, 256) * gqbs_ref[...]) * sin64w
    qb_ref[...] = (qr * (r * sb)).astype(BF16)
    k = cols(_C_KB, 128)
    ssq = jnp.dot((k * k).astype(BF16), _block_diag_ones(128, HEAD_DIM), preferred_element_type=F32)
    r = lax.rsqrt(ssq * (1.0 / HEAD_DIM) + EPS)
    kr = (k * gkb_ref[...]) * cos64 + (cols(_C_KBS, 128) * gkbs_ref[...]) * sin64
    kb_ref[...] = (kr * r).astype(BF16)

    qd_ref[...] = ((cols(_C_QD, 256) * cos64w + cols(_C_QDS, 256) * sin64w) * sb).astype(BF16)
    kd_ref[...] = (cols(_C_KD, 128) * cos64 + cols(_C_KDS, 128) * sin64).astype(BF16)

    sc = ((C_NOPE + C_ROPE) ** -0.5) * LOG2E
    cosc, sinc = cosc_ref[...], sinc_ref[...]
    cq = cols(_C_CQ, 256)
    ms = jnp.sum(cq * cq, axis=-1, keepdims=True) * (1.0 / C_Q_RANK)
    cqn = (cq * lax.rsqrt(ms + EPS) * gcq_ref[...]).astype(BF16)
    qn = jnp.dot(cqn, wuq_ref[...], preferred_element_type=F32)
    qs = jnp.dot(cqn, wuqs_ref[...], preferred_element_type=F32)
    qc_ref[...] = ((qn * _tile_lanes(cosc, 4) + qs * _tile_lanes(sinc, 4)) * sc).astype(BF16)
    ckv = cols(_C_CKV, 128)
    ms = jnp.mean(ckv * ckv, axis=-1, keepdims=True)
    ckvn = (ckv * lax.rsqrt(ms + EPS) * gckv_ref[...]).astype(BF16)
    kn = jnp.dot(ckvn, wukn_ref[...], preferred_element_type=F32)
    krr = cols(_C_KR, 128) * cosc + cols(_C_KRS, 128) * sinc
    kc_ref[...] = (kn + _tile_lanes(krr, 4)).astype(BF16)
    vct_ref[...] = _nt_dot(wuvt_ref[...], ckvn).astype(BF16)

    vt = _nt_dot(wt_ref[...], h)
    vat_ref[...] = vt[_R_VA:_R_VA + 256].astype(BF16)
    vbt_ref[...] = vt[_R_VB:_R_VB + 128].astype(BF16)
    vdt_ref[...] = vt[_R_VD:_R_VD + 128].astype(BF16)


def _proj(xa, modsel, lw, tabs):
    b, nt, d = xa.shape
    nc = nt // TQ

    def full(a):
        nd = a.ndim
        return pl.BlockSpec(a.shape, lambda bi, i, _n=nd: (0,) * _n)

    def tab(a):
        return pl.BlockSpec((TQ, LANES), lambda bi, i: (i, 0))

    def rm(w):
        return pl.BlockSpec((None, TQ, w), lambda bi, i: (bi, i, 0))

    def tr(w):
        return pl.BlockSpec((None, None, w, TQ), lambda bi, i: (bi, i, 0, 0))

    weights = [lw["wrm"], lw["wt"], lw["wuq"], lw["wuqs"], lw["wukn"], lw["wuvt"],
               lw["gqb"], lw["gqbs"], lw["gkb"], lw["gkbs"], lw["gcq"], lw["gckv"]]
    tables = [tabs["cos32"], tabs["sin32"], tabs["cos64"], tabs["sin64"], tabs["cosc"], tabs["sinc"]]
    in_specs = ([pl.BlockSpec((None, TQ, d), lambda bi, i: (bi, i, 0)),
                 pl.BlockSpec((None, None, 1, 6 * d), lambda bi, i: (bi, jnp.minimum(i, 1), 0, 0))]
                + [full(w) for w in weights] + [tab(t) for t in tables])
    widths = [("rm", 256), ("rm", 256), ("tr", 256), ("rm", 256), ("rm", 128), ("tr", 128),
              ("rm", 512), ("rm", 512), ("tr", 256), ("rm", 256), ("rm", 128), ("tr", 128)]
    out_specs, out_shape = [], []
    for kind, w in widths:
        if kind == "rm":
            out_specs.append(rm(w))
            out_shape.append(jax.ShapeDtypeStruct((b, nt, w), BF16))
        else:
            out_specs.append(tr(w))
            out_shape.append(jax.ShapeDtypeStruct((b, nc, w, TQ), BF16))
    return pl.pallas_call(
        _proj_kernel,
        grid=(b, nc),
        in_specs=in_specs,
        out_specs=out_specs,
        out_shape=out_shape,
        compiler_params=pltpu.CompilerParams(
            dimension_semantics=("arbitrary", "arbitrary"), vmem_limit_bytes=VMEM_LIMIT),
        name="proj",
    )(xa, modsel, *weights, *tables)


def _flash_loop(qi, nc, k_ref, vt_ref, qz_ref, m_ref, alpha_ref, acc_ref, s0_ref, s1_ref, kabs_ref, k_lanes, pv_cfg,
                slab_rows, bounded_pieces, groups_per_sweep):
    per = TK_LAT // TK
    n_lat = (nc - 1) // per
    assert n_lat % 2 == 0 and n_lat >= 2 and (nc - 1) % bounded_pieces == 0
    same_lanes = all(kl == k_lanes[0] for kl in k_lanes)

    def scores(kc):
        if same_lanes:
            return jnp.dot(kc[:, k_lanes[0][0]:k_lanes[0][1]], qz_ref[...], preferred_element_type=F32)
        return jnp.concatenate(
            [jnp.dot(kc[:, lo:hi], qz_ref[:, TQ * j:TQ * (j + 1)], preferred_element_type=F32)
             for j, (lo, hi) in enumerate(k_lanes)], axis=1)

    def small_chunk(pieces, first):
        if slab_rows >= pieces * TK:
            return (1, pieces, first)
        return (pieces * TK // slab_rows, slab_rows // TK, first)

    ctx_chunk = small_chunk(1, 0)

    def lat_chunk(c):
        return small_chunk(per, 1 + c * per)

    def stats_update(mx):
        m_old = m_ref[...]
        m_new = jnp.maximum(m_old, mx)
        alpha_ref[...] = jnp.exp2(m_old - m_new)
        m_ref[...] = m_new

    def pipelined(cur, cur_ref, nxt, nxt_ref):
        m_cur, alpha = m_ref[...], alpha_ref[...]
        mx, pv = None, [None] * len(pv_cfg)
        n_cur = cur[0] if cur is not None else 0
        n_nxt = nxt[0] if nxt is not None else 0
        for r in range(max(n_cur, n_nxt)):
            if r < n_nxt:
                rows = TK * nxt[1]
                kc = k_ref[pl.ds(pl.multiple_of((nxt[2] + r * nxt[1]) * TK, TK), rows), :]
                s = scores(kc)
                nxt_ref[rows * r:rows * (r + 1), :] = s
                smx = jnp.max(s, axis=0, keepdims=True)
                mx = smx if mx is None else jnp.maximum(mx, smx)
            if r < n_cur:
                rows = TK * cur[1]
                p = jnp.exp2(cur_ref[rows * r:rows * (r + 1), :] - m_cur).astype(BF16)
                vc = jnp.concatenate([vt_ref[cur[2] + r * cur[1] + i] for i in range(cur[1])], axis=1)
                ones = jnp.ones((ONES_ROWS, rows), BF16)
                for g, (vl, vh, j0, nj) in enumerate(pv_cfg):
                    va = jnp.concatenate([vc[vl:vh, :], ones], axis=0)
                    d = jnp.dot(va, p[:, TQ * j0:TQ * (j0 + nj)], preferred_element_type=F32)
                    pv[g] = d if pv[g] is None else pv[g] + d
        if cur is not None:
            for g, (vl, vh, j0, nj) in enumerate(pv_cfg):
                cols = slice(TQ * j0, TQ * (j0 + nj))
                acc_ref[:, cols] = alpha[:, cols] * acc_ref[:, cols] + pv[g]
        if nxt is not None:
            stats_update(mx)

    def online_pass():
        m_ref[...] = jnp.full(m_ref.shape, NEG, F32)
        acc_ref[...] = jnp.zeros(acc_ref.shape, F32)

        @pl.when(qi == 0)
        def _():
            pipelined(None, None, ctx_chunk, s0_ref)
            pipelined(ctx_chunk, s0_ref, None, None)

        @pl.when(qi > 0)
        def _():
            pipelined(None, None, lat_chunk(0), s0_ref)

            def body(i, carry):
                c = 2 * i
                pipelined(lat_chunk(c), s0_ref, lat_chunk(c + 1), s1_ref)
                pipelined(lat_chunk(c + 1), s1_ref, lat_chunk(c + 2), s0_ref)
                return carry

            lax.fori_loop(0, n_lat // 2 - 1, body, 0)
            pipelined(lat_chunk(n_lat - 2), s0_ref, lat_chunk(n_lat - 1), s1_ref)
            pipelined(lat_chunk(n_lat - 1), s1_ref, ctx_chunk, s0_ref)
            pipelined(ctx_chunk, s0_ref, None, None)

    def bounded_pass():
        wk = kabs_ref.shape[1]
        kab = jnp.broadcast_to(kabs_ref[...], (ONES_ROWS, wk)).astype(BF16)
        aq = jnp.abs(qz_ref[...])
        if same_lanes:
            lo, hi = k_lanes[0]
            bound = jnp.dot(kab[:, lo:hi], aq, preferred_element_type=F32)[0:1]
        else:
            bound = jnp.concatenate(
                [jnp.dot(kab[:, lo:hi], aq[:, TQ * j:TQ * (j + 1)], preferred_element_type=F32)[0:1]
                 for j, (lo, hi) in enumerate(k_lanes)], axis=1)
        bound = bound * BOUND_SLACK
        ones = jnp.ones((ONES_ROWS, TK), BF16)
        acc_ref[...] = jnp.zeros(acc_ref.shape, F32)

        def pieces(first, n, groups):
            pv = {g: None for g in groups}

            def probs(r, g):
                _, _, j0, nj = pv_cfg[g]
                start = (first + r) * TK
                kc = k_ref[pl.ds(start if isinstance(start, int) else pl.multiple_of(start, TK), TK), :]
                if all(k_lanes[j] == k_lanes[j0] for j in range(j0, j0 + nj)):
                    s = jnp.dot(kc[:, k_lanes[j0][0]:k_lanes[j0][1]], qz_ref[:, TQ * j0:TQ * (j0 + nj)],
                                preferred_element_type=F32)
                else:
                    s = jnp.concatenate(
                        [jnp.dot(kc[:, k_lanes[j][0]:k_lanes[j][1]], qz_ref[:, TQ * j:TQ * (j + 1)],
                                 preferred_element_type=F32) for j in range(j0, j0 + nj)], axis=1)
                return jnp.exp2(s - bound[:, TQ * j0:TQ * (j0 + nj)]).astype(BF16)

            def values(r, g, p):
                vl, vh, _, _ = pv_cfg[g]
                va = jnp.concatenate([vt_ref[first + r][vl:vh, :], ones], axis=0)
                d = jnp.dot(va, p, preferred_element_type=F32)
                pv[g] = d if pv[g] is None else pv[g] + d

            p_prev = {g: probs(0, g) for g in groups}
            for r in range(1, n):
                p_cur = {}
                for g in groups:
                    p_cur[g] = probs(r, g)
                    values(r - 1, g, p_prev[g])
                p_prev = p_cur
            for g in groups:
                values(n - 1, g, p_prev[g])
            for g in groups:
                _, _, j0, nj = pv_cfg[g]
                cols = slice(TQ * j0, TQ * (j0 + nj))
                acc_ref[:, cols] = acc_ref[:, cols] + pv[g]

        n_sets = len(pv_cfg) // groups_per_sweep
        for st in range(n_sets):
            groups = list(range(st * groups_per_sweep, (st + 1) * groups_per_sweep))
            pieces(0, 1, groups)

            @pl.when(qi > 0)
            def _(groups=groups):
                for i in range((nc - 1) // bounded_pieces):
                    pieces(1 + i * bounded_pieces, bounded_pieces, groups)

    @pl.when(qi == 0)
    def _():
        def body(c, mx):
            blk = k_ref[pl.ds(pl.multiple_of(c * TK, TK), TK), :].astype(F32)
            return jnp.maximum(mx, jnp.max(jnp.abs(blk), axis=0, keepdims=True))

        kabs_ref[...] = lax.fori_loop(0, nc, body, jnp.zeros(kabs_ref.shape, F32))

    bounded_pass()
    dv = acc_ref.shape[0] - ONES_ROWS
    healthy = jnp.min(jnp.where(acc_ref[dv:dv + 1, :] >= MIN_DENOMINATOR, 1.0, 0.0))

    @pl.when(healthy < 0.5)
    def _():
        online_pass()


def _normalized(acc_ref, j, dv):
    a = acc_ref[:, TQ * j:TQ * (j + 1)]
    return a[0:dv] / a[dv:dv + 1]


def _attn_a_kernel(q_ref, k_ref, vt_ref, lq1_ref, lk1_ref, lq2_ref, lk2_ref, g_ref, o_ref,
                   qz_ref, m_ref, alpha_ref, acc_ref, s0_ref, s1_ref, kabs_ref, *, nc, lam_init):
    qi = pl.program_id(1)
    qt = q_ref[...].astype(F32).T
    row = lax.broadcasted_iota(jnp.int32, (LANES, TQ), 0)
    per_blk = LANES // A_QK
    k_lanes = []
    for j in range(2 * A_HEADS):
        blk, sub = j // per_blk, j % per_blk
        keep = (row >= A_QK * sub) & (row < A_QK * (sub + 1))
        qz_ref[:, TQ * j:TQ * (j + 1)] = jnp.where(keep, qt[LANES * blk:LANES * (blk + 1)], 0.0).astype(BF16)
        k_lanes.append((LANES * blk, LANES * (blk + 1)))
    pv_cfg = [(HEAD_DIM * hd, HEAD_DIM * (hd + 1), 2 * hd, 2) for hd in range(A_HEADS)]
    _flash_loop(qi, nc, k_ref, vt_ref, qz_ref, m_ref, alpha_ref, acc_ref, s0_ref, s1_ref, kabs_ref, k_lanes, pv_cfg,
                slab_rows=1024, bounded_pieces=64, groups_per_sweep=2)

    lam = (jnp.exp(jnp.sum(lq1_ref[...] * lk1_ref[...], axis=-1, keepdims=True))
           - jnp.exp(jnp.sum(lq2_ref[...] * lk2_ref[...], axis=-1, keepdims=True)) + lam_init)
    outs = []
    for hd in range(A_HEADS):
        o = _normalized(acc_ref, 2 * hd, HEAD_DIM) - lam * _normalized(acc_ref, 2 * hd + 1, HEAD_DIM)
        ms = jnp.mean(o * o, axis=0, keepdims=True)
        outs.append(o * lax.rsqrt(ms + EPS) * g_ref[...] * (1.0 - lam_init))
    o_ref[...] = jnp.concatenate(outs, axis=0).T.astype(BF16)


def _attn_b_kernel(q_ref, k_ref, vt_ref, o_ref, qz_ref, m_ref, alpha_ref, acc_ref, s0_ref, s1_ref, kabs_ref, *, nc):
    qi = pl.program_id(1)
    qt = q_ref[...].astype(F32).T
    zeros = jnp.zeros((HEAD_DIM, TQ), F32)
    rep = B_HEADS // B_KV_HEADS
    for hd in range(B_HEADS):
        parts = [zeros] * B_KV_HEADS
        parts[hd // rep] = qt[HEAD_DIM * hd:HEAD_DIM * (hd + 1)]
        qz_ref[:, TQ * hd:TQ * (hd + 1)] = jnp.concatenate(parts, axis=0).astype(BF16)
    k_lanes = [(0, 128)] * B_HEADS
    pv_cfg = [(HEAD_DIM * g, HEAD_DIM * (g + 1), rep * g, rep) for g in range(B_KV_HEADS)]
    _flash_loop(qi, nc, k_ref, vt_ref, qz_ref, m_ref, alpha_ref, acc_ref, s0_ref, s1_ref, kabs_ref, k_lanes, pv_cfg,
                slab_rows=256, bounded_pieces=64, groups_per_sweep=B_KV_HEADS)
    outs = [_normalized(acc_ref, hd, HEAD_DIM) for hd in range(B_HEADS)]
    o_ref[...] = jnp.concatenate(outs, axis=0).T.astype(BF16)


def _attn_c_kernel(q_ref, k_ref, vt_ref, o_ref, qz_ref, m_ref, alpha_ref, acc_ref, s0_ref, s1_ref, kabs_ref, *, nc):
    qi = pl.program_id(1)
    qt = q_ref[...].astype(F32).T
    k_lanes, pv_cfg = [], []
    for hd in range(C_HEADS):
        qz_ref[:, TQ * hd:TQ * (hd + 1)] = qt[LANES * hd:LANES * (hd + 1)].astype(BF16)
        k_lanes.append((LANES * hd, LANES * (hd + 1)))
        pv_cfg.append((HEAD_DIM * hd, HEAD_DIM * (hd + 1), hd, 1))
    _flash_loop(qi, nc, k_ref, vt_ref, qz_ref, m_ref, alpha_ref, acc_ref, s0_ref, s1_ref, kabs_ref, k_lanes, pv_cfg,
                slab_rows=256, bounded_pieces=64, groups_per_sweep=C_HEADS)
    outs = [_normalized(acc_ref, hd, HEAD_DIM) for hd in range(C_HEADS)]
    o_ref[...] = jnp.concatenate(outs, axis=0).T.astype(BF16)


def _attn_full(kind, q, k, vt, extras=(), lam_init=0.0):
    b, nt, wq = q.shape
    wk = k.shape[-1]
    nc, wv = vt.shape[1], vt.shape[2]
    assert (nt - TK) % TK_LAT == 0
    if kind == "a":
        body, n_sm, dk = functools.partial(_attn_a_kernel, nc=nc, lam_init=lam_init), 2 * A_HEADS, 128
    elif kind == "b":
        body, n_sm, dk = functools.partial(_attn_b_kernel, nc=nc), B_HEADS, 128
    else:
        body, n_sm, dk = functools.partial(_attn_c_kernel, nc=nc), C_HEADS, 128
    resident = pl.Buffered(1)
    in_specs = [
        pl.BlockSpec((None, TQ, wq), lambda bi, i: (bi, i, 0)),
        pl.BlockSpec((None, nt, wk), lambda bi, i: (bi, 0, 0), pipeline_mode=resident),
        pl.BlockSpec((None, nc, wv, TK), lambda bi, i: (bi, 0, 0, 0), pipeline_mode=resident),
    ] + [pl.BlockSpec(e.shape, lambda bi, i: (0, 0)) for e in extras]
    return pl.pallas_call(
        body,
        grid=(b, nc),
        in_specs=in_specs,
        out_specs=pl.BlockSpec((None, TQ, GROUP_WIDTH), lambda bi, i: (bi, i, 0)),
        out_shape=jax.ShapeDtypeStruct((b, nt, GROUP_WIDTH), BF16),
        scratch_shapes=[
            pltpu.VMEM((dk, n_sm * TQ), BF16),
            pltpu.VMEM((1, n_sm * TQ), F32),
            pltpu.VMEM((1, n_sm * TQ), F32),
            pltpu.VMEM((HEAD_DIM + ONES_ROWS, n_sm * TQ), F32),
            pltpu.VMEM((TK_LAT, n_sm * TQ), F32),
            pltpu.VMEM((TK_LAT, n_sm * TQ), F32),
            pltpu.VMEM((1, wk), F32),
        ],
        compiler_params=pltpu.CompilerParams(
            dimension_semantics=("arbitrary", "arbitrary"), vmem_limit_bytes=VMEM_LIMIT),
        name="attn_" + kind,
    )(q, k, vt, *extras)


def _attn_d_kernel(q_ref, kc_ref, kp_ref, ko_ref, kn_ref, vc_ref, vp_ref, vo_ref, vn_ref, sink_ref,
                   o_ref, *, n_blocks):
    n = pl.program_id(1)
    t, wn = TQ, WINDOW
    rep = D_HEADS // D_KV_HEADS
    qt = q_ref[...].astype(F32).T
    zeros = jnp.zeros((HEAD_DIM, t), F32)
    cols = []
    for hd in range(D_HEADS):
        parts = [zeros] * D_KV_HEADS
        parts[hd // rep] = qt[HEAD_DIM * hd:HEAD_DIM * (hd + 1)]
        cols.append(jnp.concatenate(parts, axis=0))
    qz = jnp.concatenate(cols, axis=1).astype(BF16)
    w = D_HEADS * t

    def offsets(rows):
        ko = lax.broadcasted_iota(jnp.int32, (rows, w), 0)
        qo = lax.broadcasted_iota(jnp.int32, (rows, w), 1) & (t - 1)
        return ko, qo

    band = n >= 1
    ko, qo = offsets(wn)
    ok_prev = jnp.logical_and(n >= 2, qo <= ko)
    ok_next = jnp.logical_and(jnp.logical_and(band, n + 1 <= n_blocks - 1), qo >= ko + (t - wn))
    ko, qo = offsets(t)
    ok_own = jnp.logical_and(band, jnp.abs(qo - ko) <= wn)
    s_c = jnp.dot(kc_ref[...], qz, preferred_element_type=F32)
    s_p = jnp.where(ok_prev, jnp.dot(kp_ref[...], qz, preferred_element_type=F32), NEG)
    s_o = jnp.where(ok_own, jnp.dot(ko_ref[...], qz, preferred_element_type=F32), NEG)
    s_n = jnp.where(ok_next, jnp.dot(kn_ref[...], qz, preferred_element_type=F32), NEG)
    sk = jnp.concatenate([jnp.broadcast_to(sink_ref[0:1, hd:hd + 1] * LOG2E, (1, t))
                          for hd in range(D_HEADS)], axis=1)
    m = jnp.maximum(jnp.max(s_c, axis=0, keepdims=True), jnp.max(s_p, axis=0, keepdims=True))
    m = jnp.maximum(m, jnp.max(s_o, axis=0, keepdims=True))
    m = jnp.maximum(m, jnp.max(s_n, axis=0, keepdims=True))
    m = jnp.maximum(m, sk)
    pieces = [(vc_ref[...], jnp.exp2(s_c - m).astype(BF16)), (vp_ref[...], jnp.exp2(s_p - m).astype(BF16)),
              (vo_ref[...], jnp.exp2(s_o - m).astype(BF16)), (vn_ref[...], jnp.exp2(s_n - m).astype(BF16))]
    l_sink = jnp.exp2(sk - m)
    outs = []
    for g in range(D_KV_HEADS):
        lanes = slice(rep * t * g, rep * t * (g + 1))
        o = None
        for v, p in pieces:
            va = jnp.concatenate([v[HEAD_DIM * g:HEAD_DIM * (g + 1)],
                                  jnp.ones((ONES_ROWS, v.shape[1]), BF16)], axis=0)
            d = jnp.dot(va, p[:, lanes], preferred_element_type=F32)
            o = d if o is None else o + d
        o = o[0:HEAD_DIM] / (o[HEAD_DIM:HEAD_DIM + 1] + l_sink[:, lanes])
        outs += [o[:, t * r:t * (r + 1)] for r in range(rep)]
    o_ref[...] = jnp.concatenate(outs, axis=0).T.astype(BF16)


def _attn_d(q, k, vt, sink, ctx_len):
    b, nt, wq = q.shape
    wk = k.shape[-1]
    n_blocks, wv = vt.shape[1], vt.shape[2]
    half = TQ // WINDOW
    n_half = nt // WINDOW
    assert ctx_len == TQ and TQ % WINDOW == 0 and half == 2

    in_specs = [
        pl.BlockSpec((None, TQ, wq), lambda bi, i: (bi, i, 0)),
        pl.BlockSpec((None, ctx_len, wk), lambda bi, i: (bi, 0, 0)),
        pl.BlockSpec((None, WINDOW, wk), lambda bi, i: (bi, jnp.clip(half * i - 1, 0, n_half - 1), 0)),
        pl.BlockSpec((None, TQ, wk), lambda bi, i: (bi, i, 0)),
        pl.BlockSpec((None, WINDOW, wk), lambda bi, i: (bi, jnp.clip(half * i + half, 0, n_half - 1), 0)),
        pl.BlockSpec((None, None, wv, ctx_len), lambda bi, i: (bi, 0, 0, 0)),
        pl.BlockSpec((None, None, wv, WINDOW), lambda bi, i: (bi, jnp.maximum(i - 1, 0), 0, half - 1)),
        pl.BlockSpec((None, None, wv, TQ), lambda bi, i: (bi, i, 0, 0)),
        pl.BlockSpec((None, None, wv, WINDOW), lambda bi, i: (bi, jnp.minimum(i + 1, n_blocks - 1), 0, 0)),
        pl.BlockSpec(sink.shape, lambda bi, i: (0, 0)),
    ]
    return pl.pallas_call(
        functools.partial(_attn_d_kernel, n_blocks=n_blocks),
        grid=(b, n_blocks),
        in_specs=in_specs,
        out_specs=pl.BlockSpec((None, TQ, GROUP_WIDTH), lambda bi, i: (bi, i, 0)),
        out_shape=jax.ShapeDtypeStruct((b, nt, GROUP_WIDTH), BF16),
        compiler_params=pltpu.CompilerParams(
            dimension_semantics=("arbitrary", "arbitrary"), vmem_limit_bytes=VMEM_LIMIT),
        name="attn_d",
    )(q, k, k, k, k, vt, vt, vt, vt, sink)


def _route(sel_rows, s_rows):
    epg = EXPERTS_PER_GROUP
    gscore = []
    for g in range(N_EXPERT_GROUPS):
        r = sel_rows[epg * g:epg * (g + 1)]
        pair = None
        for i in range(epg):
            for j in range(i + 1, epg):
                v = r[i] + r[j]
                pair = v if pair is None else jnp.maximum(pair, v)
        gscore.append(pair)
    best, best_g = gscore[0], jnp.zeros_like(gscore[0], dtype=jnp.int32)
    for g in range(1, N_EXPERT_GROUPS):
        better = gscore[g] > best
        best_g = jnp.where(better, g, best_g)
        best = jnp.where(better, gscore[g], best)
    w = []
    for e in range(N_EXPERTS):
        g = e // epg
        cnt = jnp.zeros_like(best_g)
        for e2 in range(epg * g, epg * (g + 1)):
            if e2 == e:
                continue
            beats = sel_rows[e2] > sel_rows[e]
            if e2 < e:
                beats = jnp.logical_or(beats, sel_rows[e2] == sel_rows[e])
            cnt = cnt + jnp.where(beats, 1, 0)
        chosen = jnp.logical_and(best_g == g, cnt < 2)
        w.append(jnp.where(chosen, s_rows[e], 0.0))
    tot = w[0]
    for e in range(1, N_EXPERTS):
        tot = tot + w[e]
    return [we / tot for we in w]


def _outproj_kernel(oa_ref, ob_ref, oc_ref, od_ref, x_ref, mod_ref, wout_ref, g_ref, b_ref,
                    rwt_ref, rb_ref, x1_ref, h2_ref, gates_ref, *, alpha):
    d = D_MODEL
    o = jnp.concatenate([oa_ref[...], ob_ref[...], oc_ref[...], od_ref[...]], axis=1)
    y = jnp.dot(o, wout_ref[...], preferred_element_type=F32)
    u = alpha * x_ref[...] + mod_ref[:, 2 * d:3 * d] * y
    x1 = _layer_norm(u) * g_ref[...] + b_ref[...]
    x1_ref[...] = x1
    h2 = _layer_norm(x1) * (1.0 + mod_ref[:, 4 * d:5 * d]) + mod_ref[:, 3 * d:4 * d]
    h2_ref[...] = h2.astype(BF16)

    h_hi, h_lo = _split_bf16(h2)
    w_hi, w_lo = _split_bf16(rwt_ref[...])
    logits = _nt_dot(w_hi, h_hi) + _nt_dot(w_hi, h_lo) + _nt_dot(w_lo, h_hi)
    s = jax.nn.sigmoid(logits)
    sel = s + rb_ref[...]
    s_rows = [s[e:e + 1] for e in range(N_EXPERTS)]
    sel_rows = [sel[e:e + 1] for e in range(N_EXPERTS)]
    gates = _route(sel_rows, s_rows)
    gt = jnp.concatenate(gates + [jnp.zeros((LANES - N_EXPERTS, TQ), F32)], axis=0)
    gates_ref[...] = gt.T


def _outproj(oa, ob, oc, od, xa, modsel, w_out, ln_g, ln_b, rwt, rb, alpha):
    b, nt, d = xa.shape
    nc = nt // TQ

    def rm(w):
        return pl.BlockSpec((None, TQ, w), lambda bi, i: (bi, i, 0))

    def full(a):
        nd = a.ndim
        return pl.BlockSpec(a.shape, lambda bi, i, _n=nd: (0,) * _n)

    in_specs = [rm(GROUP_WIDTH) for _ in range(4)] + [
        rm(d),
        pl.BlockSpec((None, None, 1, 6 * d), lambda bi, i: (bi, jnp.minimum(i, 1), 0, 0)),
        full(w_out), full(ln_g), full(ln_b), full(rwt), full(rb)]
    return pl.pallas_call(
        functools.partial(_outproj_kernel, alpha=alpha),
        grid=(b, nc),
        in_specs=in_specs,
        out_specs=[rm(d), rm(d), rm(LANES)],
        out_shape=[jax.ShapeDtypeStruct((b, nt, d), F32), jax.ShapeDtypeStruct((b, nt, d), BF16),
                   jax.ShapeDtypeStruct((b, nt, LANES), F32)],
        compiler_params=pltpu.CompilerParams(
            dimension_semantics=("arbitrary", "arbitrary"), vmem_limit_bytes=VMEM_LIMIT),
        name="outproj",
    )(oa, ob, oc, od, xa, modsel, w_out, ln_g, ln_b, rwt, rb)


def _moe_kernel(h_ref, gates_ref, x_ref, mod_ref, wg_ref, wu_ref, wd_ref, g_ref, b_ref, o_ref, *, alpha):
    d = D_MODEL
    h = h_ref[...]
    gates = gates_ref[...]
    y = jnp.zeros((TQ, d), F32)
    for g in range(N_EXPERT_GROUPS):
        acts = []
        for el in range(EXPERTS_PER_GROUP):
            e = EXPERTS_PER_GROUP * g + el
            hg = jnp.dot(h, wg_ref[e], preferred_element_type=F32)
            hu = jnp.dot(h, wu_ref[e], preferred_element_type=F32)
            a = hg * jax.nn.sigmoid(hg) * hu * gates[:, e:e + 1]
            acts.append(a.astype(BF16))
        y = y + jnp.dot(jnp.concatenate(acts, axis=1), wd_ref[g], preferred_element_type=F32)
    u = alpha * x_ref[...] + mod_ref[:, 5 * d:6 * d] * y
    o_ref[...] = _layer_norm(u) * g_ref[...] + b_ref[...]


def _moe(h2, gates, x1, modsel, wg, wu, wd, ln_g, ln_b, alpha, latents_only):
    b, nt, d = x1.shape
    nc = nt // TQ
    resident = pl.Buffered(1)
    if latents_only:
        out_spec = pl.BlockSpec((None, TQ, d), lambda bi, i: (bi, jnp.maximum(i - 1, 0), 0))
        out_rows = nt - TQ
    else:
        out_spec = pl.BlockSpec((None, TQ, d), lambda bi, i: (bi, i, 0))
        out_rows = nt

    def rm(w):
        return pl.BlockSpec((None, TQ, w), lambda bi, i: (bi, i, 0))

    def res(a):
        nd = a.ndim
        return pl.BlockSpec(a.shape, lambda bi, i, _n=nd: (0,) * _n, pipeline_mode=resident)

    in_specs = [rm(d), rm(LANES), rm(d),
                pl.BlockSpec((None, None, 1, 6 * d), lambda bi, i: (bi, jnp.minimum(i, 1), 0, 0)),
                res(wg), res(wu), res(wd), res(ln_g), res(ln_b)]
    return pl.pallas_call(
        functools.partial(_moe_kernel, alpha=alpha),
        grid=(b, nc),
        in_specs=in_specs,
        out_specs=out_spec,
        out_shape=jax.ShapeDtypeStruct((b, out_rows, d), F32),
        compiler_params=pltpu.CompilerParams(
            dimension_semantics=("arbitrary", "arbitrary"), vmem_limit_bytes=VMEM_LIMIT),
        name="moe",
    )(h2, gates, x1, modsel, wg, wu, wd, ln_g, ln_b)


def _rope_perm(rot_dim):
    n = rot_dim // 4
    j = np.arange(rot_dim)
    return j ^ n, np.where((j // n) % 2 == 0, -1.0, 1.0).astype(np.float32)


def _swapped(w, rot_dim):
    perm, sign = _rope_perm(rot_dim)
    cols = w.shape[-1]
    idx = (np.arange(cols) // rot_dim) * rot_dim + perm[np.arange(cols) % rot_dim]
    sgn = sign[np.arange(cols) % rot_dim]
    return w[..., idx] * sgn


def _rope_tables(n_lat, ctx_len):
    t = jnp.arange(n_lat, dtype=jnp.int32)
    row, col = (t // GRID_W).astype(F32), (t % GRID_W).astype(F32)

    def pattern(rot_dim):
        n = rot_dim // 4
        inv = ROPE_THETA ** (-jnp.arange(n, dtype=F32) / n)
        ar, ac = row[:, None] * inv, col[:, None] * inv
        ang = jnp.concatenate([ar, ar, ac, ac], axis=-1)
        return jnp.cos(ang), jnp.sin(ang)

    def with_ctx(a, fill):
        return jnp.concatenate([jnp.full((ctx_len, a.shape[1]), fill, F32), a], axis=0)

    c32, s32 = pattern(A_QK)
    c64, s64 = pattern(HEAD_DIM)
    ones = jnp.ones((n_lat, C_NOPE), F32)
    zeros = jnp.zeros((n_lat, C_NOPE), F32)
    pad1 = jnp.ones((n_lat, LANES - C_NOPE - C_ROPE), F32)
    pad0 = jnp.zeros((n_lat, LANES - C_NOPE - C_ROPE), F32)
    return {
        "cos32": with_ctx(jnp.tile(c32, (1, LANES // A_QK)), 1.0),
        "sin32": with_ctx(jnp.tile(s32, (1, LANES // A_QK)), 0.0),
        "cos64": with_ctx(jnp.tile(c64, (1, LANES // HEAD_DIM)), 1.0),
        "sin64": with_ctx(jnp.tile(s64, (1, LANES // HEAD_DIM)), 0.0),
        "cosc": with_ctx(jnp.concatenate([ones, c32, pad1], axis=-1), 1.0),
        "sinc": with_ctx(jnp.concatenate([zeros, s32, pad0], axis=-1), 0.0),
    }


def _prep_layer_weights(w_in, gq, gk, gcq, gckv, w_uq, w_ukv):
    d = w_in.shape[0]
    splits = np.cumsum([256, 256, 256, 256, 128, 128, C_Q_RANK, C_KV_RANK, C_ROPE, 256, 128, 128])[:-1]
    (a_q, a_k, a_v, b_q, b_k, b_v, c_q, c_kv, c_kr, d_q, d_k, d_v) = jnp.split(w_in, splits, axis=1)
    z = lambda n: jnp.zeros((d, n), F32)
    kr4 = jnp.concatenate([z(C_NOPE), c_kr, z(LANES - C_NOPE - C_ROPE)], axis=1)
    kr4s = jnp.concatenate([z(C_NOPE), _swapped(c_kr, C_ROPE), z(LANES - C_NOPE - C_ROPE)], axis=1)
    wrm = jnp.concatenate([
        a_q, _swapped(a_q, A_QK), a_k, _swapped(a_k, A_QK),
        b_q, _swapped(b_q, HEAD_DIM), b_k, _swapped(b_k, HEAD_DIM),
        d_q, _swapped(d_q, HEAD_DIM), d_k, _swapped(d_k, HEAD_DIM),
        c_q, z(256 - C_Q_RANK), c_kv, kr4, kr4s], axis=1)
    wt = jnp.concatenate([a_v, b_v, d_v], axis=1).T

    uq = w_uq.reshape(C_Q_RANK, C_HEADS, C_NOPE + C_ROPE)
    uq_n, uq_r = uq[..., :C_NOPE], uq[..., C_NOPE:]
    zq = lambda n: jnp.zeros((C_Q_RANK, C_HEADS, n), F32)
    pad = LANES - C_NOPE - C_ROPE
    wuq = jnp.concatenate([uq_n, uq_r, zq(pad)], axis=-1).reshape(C_Q_RANK, C_HEADS * LANES)
    wuqs = jnp.concatenate([zq(C_NOPE), _swapped(uq_r, C_ROPE), zq(pad)], axis=-1).reshape(C_Q_RANK, C_HEADS * LANES)
    zrows = jnp.zeros((256 - C_Q_RANK, C_HEADS * LANES), F32)
    wuq, wuqs = jnp.concatenate([wuq, zrows], axis=0), jnp.concatenate([wuqs, zrows], axis=0)
    ukv = w_ukv.reshape(C_KV_RANK, C_HEADS, C_NOPE + HEAD_DIM)
    uk_n, u_v = ukv[..., :C_NOPE], ukv[..., C_NOPE:]
    wukn = jnp.concatenate([uk_n, jnp.zeros((C_KV_RANK, C_HEADS, LANES - C_NOPE), F32)], axis=-1)
    wukn = wukn.reshape(C_KV_RANK, C_HEADS * LANES)
    wuvt = u_v.reshape(C_KV_RANK, C_HEADS * HEAD_DIM).T

    perm64, _ = _rope_perm(HEAD_DIM)
    return {
        "wrm": wrm.astype(BF16), "wt": wt.astype(BF16),
        "wuq": wuq.astype(BF16), "wuqs": wuqs.astype(BF16),
        "wukn": wukn.astype(BF16), "wuvt": wuvt.astype(BF16),
        "gqb": jnp.tile(gq, B_HEADS)[None, :], "gqbs": jnp.tile(gq[perm64], B_HEADS)[None, :],
        "gkb": jnp.tile(gk, B_KV_HEADS)[None, :], "gkbs": jnp.tile(gk[perm64], B_KV_HEADS)[None, :],
        "gcq": jnp.concatenate([gcq, jnp.zeros((256 - C_Q_RANK,), F32)])[None, :],
        "gckv": gckv[None, :],
    }


def kernel(x, c, ctx, c_ctx, w_ada, b_ada, w_in, w_out, diff_lambda_q1, diff_lambda_k1, diff_lambda_q2,
           diff_lambda_k2, diff_subln_g, gqa_q_norm_g, gqa_k_norm_g, mla_q_norm_g, mla_kv_norm_g, mla_w_uq,
           mla_w_ukv, swa_sink, ln1_g, ln1_b, ln2_g, ln2_b, router_w, router_bias,
           exp_w_gate, exp_w_up, exp_w_down):
    b, n_lat, d = x.shape
    ctx_len = ctx.shape[1]
    depth = w_ada.shape[0]
    assert d == D_MODEL and ctx_len == TQ and n_lat % TQ == 0 and b + 1 <= 8
    alpha = (2 * depth) ** 0.25

    xa = jnp.concatenate([ctx, x], axis=1)
    tabs = _rope_tables(n_lat, ctx_len)

    cc = jnp.concatenate([c, c_ctx[None, :], jnp.zeros((8 - b - 1, d), F32)], axis=0)
    mods = _ada(cc, w_ada, b_ada)
    rwt = router_w.T
    rb = router_bias[:, None]

    for l in range(depth):
        lat = mods[l, :b]
        cx = jnp.broadcast_to(mods[l, b], lat.shape)
        modsel = jnp.stack([cx, lat], axis=1)[:, :, None, :]
        lw = _prep_layer_weights(w_in[l], gqa_q_norm_g[l], gqa_k_norm_g[l], mla_q_norm_g[l],
                                 mla_kv_norm_g[l], mla_w_uq[l], mla_w_ukv[l])
        qa, ka, vat, qb, kb, vbt, qc, kc, vct, qd, kd, vdt = _proj(xa, modsel, lw, tabs)

        lam_init = 0.8 - 0.6 * math.exp(-0.3 * l)
        extras = (diff_lambda_q1[l][None, :], diff_lambda_k1[l][None, :], diff_lambda_q2[l][None, :],
                  diff_lambda_k2[l][None, :], diff_subln_g[l][:, None])
        oa = _attn_full("a", qa, ka, vat, extras, lam_init)
        ob = _attn_full("b", qb, kb, vbt)
        oc = _attn_full("c", qc, kc, vct)
        od = _attn_d(qd, kd, vdt, swa_sink[l][None, :], ctx_len)

        x1, h2, gates = _outproj(oa, ob, oc, od, xa, modsel, w_out[l].astype(BF16),
                                 ln1_g[l][None, :], ln1_b[l][None, :], rwt, rb, alpha)
        wg = exp_w_gate[l].astype(BF16)
        wu = exp_w_up[l].astype(BF16)
        wd = exp_w_down[l].astype(BF16).reshape(N_EXPERT_GROUPS, EXPERTS_PER_GROUP * D_EXPERT, d)
        xa = _moe(h2, gates, x1, modsel, wg, wu, wd, ln2_g[l][None, :], ln2_b[l][None, :], alpha,
                  latents_only=(l == depth - 1))
    return xa
```
